```python
import jax, jax.numpy as jnp
from jax import lax
import numpy as np

D_MODEL = 1024
BATCH = 1
SEQ = 16384
DEPTH = 4

GRID_W = 64
CTX_LEN = 256
N_EVEN = (DEPTH + 1) // 2
N_ODD = DEPTH // 2
EPS = 1e-6

FOURIER_GROUPS = 4
FOURIER_GROUP_DIM = 128
FOURIER_WIDTH = FOURIER_GROUPS * FOURIER_GROUP_DIM
CONV_GROUPS = 4
CONV_WIDTH = 512
CONV_K = 3
EVEN_IN = FOURIER_WIDTH + 3 * CONV_WIDTH
EVEN_OUT = FOURIER_WIDTH + CONV_WIDTH

N_HEADS = 16
N_KV_HEADS = 4
GQA_GROUP = N_HEADS // N_KV_HEADS
HEAD_DIM = 64
Q_DIM = N_HEADS * HEAD_DIM
KV_DIM = N_KV_HEADS * HEAD_DIM
QKV_OUT = Q_DIM + 2 * KV_DIM
WINDOW = 128
ATT_BLOCK = 128
ROPE_BASE = 10000.0
ROPE_PAIRS_PER_AXIS = HEAD_DIM // 4
NEG_INF = -1e30

N_GROUPS = 4
EXPERTS_PER_GROUP = 8
N_EXPERTS = N_GROUPS * EXPERTS_PER_GROUP
TOP_K = 2
D_FF_EXPERT = 512
MOE_BLOCK = 128

kernel_name = "hybrid_fourier_conv_swa_hmoe_dit"


def rms_norm(x, g):
    xf = x.astype(jnp.float32)
    y = xf * lax.rsqrt(jnp.mean(xf * xf, axis=-1, keepdims=True) + EPS)
    return (y * g.astype(jnp.float32)).astype(x.dtype)


def modulate(h, shift, scale):
    return h * (1 + scale) + shift


def fourier_mix(a):
    b_, l_, _ = a.shape
    a4 = a.reshape(b_, l_, FOURIER_GROUPS, FOURIER_GROUP_DIM).astype(jnp.float32)
    f = jnp.fft.fftn(a4, axes=(1, 3), norm="ortho").real
    return f.reshape(b_, l_, FOURIER_WIDTH).astype(a.dtype)


def short_conv_mix(b_gate, c_gate, h, w):
    u = c_gate * h
    up = jnp.pad(u, ((0, 0), (1, 1), (0, 0)))
    y = w[0] * up[:, :-2] + w[1] * up[:, 1:-1] + w[2] * up[:, 2:]
    return b_gate * y


def even_mixer(h, w_in, conv_w, w_out):
    p = h @ w_in
    a, b_gate, c_gate, hv = jnp.split(
        p, [FOURIER_WIDTH, FOURIER_WIDTH + CONV_WIDTH, FOURIER_WIDTH + 2 * CONV_WIDTH], axis=-1)
    y = jnp.concatenate([fourier_mix(a), short_conv_mix(b_gate, c_gate, hv, conv_w)], axis=-1)
    return y @ w_out


def axial_rope_tables(n_tok):
    t = jnp.arange(n_tok, dtype=jnp.int32)
    pos = jnp.stack([t // GRID_W, t % GRID_W], axis=-1).astype(jnp.float32)
    freqs = ROPE_BASE ** (-jnp.arange(ROPE_PAIRS_PER_AXIS, dtype=jnp.float32) / ROPE_PAIRS_PER_AXIS)
    ang = pos[:, :, None] * freqs
    return jnp.cos(ang), jnp.sin(ang)


def apply_axial_rope(x, cos, sin):
    b_, l_, h_, _ = x.shape
    xf = x.astype(jnp.float32).reshape(b_, l_, h_, 2, 2, ROPE_PAIRS_PER_AXIS)
    x1, x2 = xf[..., 0, :], xf[..., 1, :]
    cs, sn = cos[None, :, None], sin[None, :, None]
    out = jnp.stack([x1 * cs - x2 * sn, x1 * sn + x2 * cs], axis=-2)
    return out.reshape(x.shape).astype(x.dtype)


def project_q(h, w_qkv, q_g):
    b_, l_, _ = h.shape
    q = (h @ w_qkv[:, :Q_DIM]).reshape(b_, l_, N_HEADS, HEAD_DIM)
    return rms_norm(q, q_g)


def project_kv(h, w_qkv, k_g):
    b_, l_, _ = h.shape
    kv = h @ w_qkv[:, Q_DIM:]
    k = rms_norm(kv[..., :KV_DIM].reshape(b_, l_, N_KV_HEADS, HEAD_DIM), k_g)
    v = kv[..., KV_DIM:].reshape(b_, l_, N_KV_HEADS, HEAD_DIM)
    return k, v


def window_attention(hl, hc, w_qkv, q_g, k_g, sink, w_o, need_ctx_out):
    B, L, _ = hl.shape
    Lc = hc.shape[1]
    scale = HEAD_DIM ** -0.5
    cos, sin = axial_rope_tables(L)
    ql = apply_axial_rope(project_q(hl, w_qkv, q_g), cos, sin) * scale
    kl, vl = project_kv(hl, w_qkv, k_g)
    kl = apply_axial_rope(kl, cos, sin)
    kc, vc = project_kv(hc, w_qkv, k_g)
    sink_g = sink.reshape(N_KV_HEADS, GQA_GROUP).astype(jnp.float32)

    n_blocks = L // ATT_BLOCK
    qg = ql.reshape(B, L, N_KV_HEADS, GQA_GROUP, HEAD_DIM)
    pad = ((0, 0), (ATT_BLOCK, ATT_BLOCK), (0, 0), (0, 0))
    kp, vp = jnp.pad(kl, pad), jnp.pad(vl, pad)
    qi = jnp.arange(ATT_BLOCK)
    kj = jnp.arange(3 * ATT_BLOCK)
    band = jnp.abs(kj[None, :] - ATT_BLOCK - qi[:, None]) <= WINDOW

    def block(bidx):
        start = bidx * ATT_BLOCK
        qb = lax.dynamic_slice_in_dim(qg, start, ATT_BLOCK, axis=1)
        kb = lax.dynamic_slice_in_dim(kp, start, 3 * ATT_BLOCK, axis=1)
        vb = lax.dynamic_slice_in_dim(vp, start, 3 * ATT_BLOCK, axis=1)
        kpos = start - ATT_BLOCK + kj
        mask = band & ((kpos >= 0) & (kpos < L))[None, :]
        s_loc = jnp.einsum("bqkgd,bskd->bkgqs", qb, kb, preferred_element_type=jnp.float32)
        s_loc = jnp.where(mask, s_loc, NEG_INF)
        s_ctx = jnp.einsum("bqkgd,bskd->bkgqs", qb, kc, preferred_element_type=jnp.float32)
        s_sink = jnp.broadcast_to(sink_g[None, :, :, None, None], s_loc.shape[:-1] + (1,))
        p = jax.nn.softmax(jnp.concatenate([s_loc, s_ctx, s_sink], axis=-1), axis=-1)
        p_loc = p[..., :3 * ATT_BLOCK].astype(vb.dtype)
        p_ctx = p[..., 3 * ATT_BLOCK:3 * ATT_BLOCK + Lc].astype(vc.dtype)
        return (jnp.einsum("bkgqs,bskd->bqkgd", p_loc, vb)
                + jnp.einsum("bkgqs,bskd->bqkgd", p_ctx, vc))

    ob = lax.map(block, jnp.arange(n_blocks))
    yl = jnp.moveaxis(ob, 0, 1).reshape(B, L, Q_DIM) @ w_o

    if not need_ctx_out:
        return yl, None
    qc = project_q(hc, w_qkv, q_g).reshape(B, Lc, N_KV_HEADS, GQA_GROUP, HEAD_DIM) * scale
    s = jnp.einsum("bqkgd,bskd->bkgqs", qc, kc, preferred_element_type=jnp.float32)
    s_sink = jnp.broadcast_to(sink_g[None, :, :, None, None], s.shape[:-1] + (1,))
    p = jax.nn.softmax(jnp.concatenate([s, s_sink], axis=-1), axis=-1)[..., :Lc].astype(vc.dtype)
    yc = jnp.einsum("bkgqs,bskd->bqkgd", p, vc).reshape(B, Lc, Q_DIM) @ w_o
    return yl, yc


def hier_moe(xf, w_rg, b_rg, w_re, b_re, w1, w3, w2):
    T, D = xf.shape
    xr = xf.astype(jnp.float32)
    g_logits = xr @ w_rg.astype(jnp.float32) + b_rg.astype(jnp.float32)
    g_prob = jax.nn.softmax(g_logits, axis=-1)
    g_val, g_idx = lax.top_k(g_prob, 1)
    e_logits = (xr @ w_re.astype(jnp.float32) + b_re.astype(jnp.float32)).reshape(
        T, N_GROUPS, EXPERTS_PER_GROUP)
    e_in = jnp.take_along_axis(e_logits, g_idx[:, :, None], axis=1)[:, 0]
    top_v, top_i = lax.top_k(e_in, TOP_K)
    gate = jax.nn.softmax(top_v, axis=-1) * g_val
    expert = g_idx * EXPERTS_PER_GROUP + top_i

    A = T * TOP_K
    e_flat = expert.reshape(A)
    t_flat = jnp.repeat(jnp.arange(T, dtype=jnp.int32), TOP_K)
    g_flat = gate.reshape(A)
    counts = jnp.bincount(e_flat, length=N_EXPERTS)
    padded = (counts + MOE_BLOCK - 1) // MOE_BLOCK * MOE_BLOCK
    pad_end = jnp.cumsum(padded)
    pad_start = pad_end - padded
    raw_start = jnp.cumsum(counts) - counts
    order = jnp.argsort(e_flat)
    e_s = e_flat[order]
    dest = pad_start[e_s] + jnp.arange(A) - raw_start[e_s]
    n_blocks = -(-A // MOE_BLOCK) + N_EXPERTS
    R = n_blocks * MOE_BLOCK
    row_tok = jnp.full((R,), T, jnp.int32).at[dest].set(t_flat[order])
    row_gate = jnp.zeros((R,), jnp.float32).at[dest].set(g_flat[order])
    block_exp = jnp.clip(jnp.searchsorted(pad_end, jnp.arange(n_blocks) * MOE_BLOCK, side="right"),
                         0, N_EXPERTS - 1)

    x_pad = jnp.concatenate([xf, jnp.zeros((1, D), xf.dtype)], axis=0)
    xb = x_pad[row_tok].reshape(n_blocks, MOE_BLOCK, D)

    def expert_block(args):
        xblk, e = args
        return (jax.nn.silu(xblk @ w1[e]) * (xblk @ w3[e])) @ w2[e]

    yb = lax.map(expert_block, (xb, block_exp)).reshape(R, D)
    out = jnp.zeros((T + 1, D), yb.dtype).at[row_tok].add(yb * row_gate[:, None].astype(yb.dtype))
    return out[:T]


def setup_inputs(seed: int = 0) -> dict:
    key = jax.random.key(seed)
    ks = jax.random.split(key, 24)
    D = D_MODEL

    def nrm(k, shape, s):
        return jax.random.normal(k, shape, jnp.float32) * s

    return {
        "x": nrm(ks[0], (BATCH, SEQ, D), 1.0),
        "c": nrm(ks[1], (BATCH, D), 1.0),
        "ctx": nrm(ks[2], (BATCH, CTX_LEN, D), 1.0),
        "c_ctx": nrm(ks[3], (D,), 1.0),
        "w_mod": nrm(ks[4], (DEPTH, D, 6 * D), 0.5 * D ** -0.5),
        "b_mod": nrm(ks[5], (DEPTH, 6 * D), 0.01),
        "norm_mix_g": 1.0 + nrm(ks[6], (DEPTH, D), 0.02),
        "norm_ffn_g": 1.0 + nrm(ks[7], (DEPTH, D), 0.02),
        "w_in_even": nrm(ks[8], (N_EVEN, D, EVEN_IN), D ** -0.5),
        "conv_w": nrm(ks[9], (N_EVEN, CONV_K, CONV_WIDTH), CONV_K ** -0.5),
        "w_out_even": nrm(ks[10], (N_EVEN, EVEN_OUT, D), EVEN_OUT ** -0.5),
        "w_qkv": nrm(ks[11], (N_ODD, D, QKV_OUT), D ** -0.5),
        "q_norm_g": 1.0 + nrm(ks[12], (N_ODD, HEAD_DIM), 0.02),
        "k_norm_g": 1.0 + nrm(ks[13], (N_ODD, HEAD_DIM), 0.02),
        "sink_logit": nrm(ks[14], (N_ODD, N_HEADS), 1.0),
        "w_o": nrm(ks[15], (N_ODD, Q_DIM, D), Q_DIM ** -0.5),
        "w_router_g": nrm(ks[16], (DEPTH, D, N_GROUPS), D ** -0.5),
        "b_router_g": nrm(ks[17], (DEPTH, N_GROUPS), 0.01),
        "w_router_e": nrm(ks[18], (DEPTH, D, N_EXPERTS), D ** -0.5),
        "b_router_e": nrm(ks[19], (DEPTH, N_EXPERTS), 0.01),
        "w1": nrm(ks[20], (DEPTH, N_EXPERTS, D, D_FF_EXPERT), D ** -0.5),
        "w3": nrm(ks[21], (DEPTH, N_EXPERTS, D, D_FF_EXPERT), D ** -0.5),
        "w2": nrm(ks[22], (DEPTH, N_EXPERTS, D_FF_EXPERT, D), D_FF_EXPERT ** -0.5),
    }


def reference(x, c, ctx, c_ctx, w_mod, b_mod, norm_mix_g, norm_ffn_g, w_in_even, conv_w, w_out_even,
              w_qkv, q_norm_g, k_norm_g, sink_logit, w_o, w_router_g, b_router_g, w_router_e, b_router_e,
              w1, w3, w2):
    B, L, D = x.shape
    Lc = ctx.shape[1]
    xl, xc = x, ctx
    for layer in range(DEPTH):
        last = layer == DEPTH - 1
        is_even = layer % 2 == 0
        j = layer // 2
        mod_l = (jax.nn.silu(c) @ w_mod[layer] + b_mod[layer])[:, None, :]
        mod_c = jax.nn.silu(c_ctx) @ w_mod[layer] + b_mod[layer]
        sh1_l, sc1_l, g1_l, sh2_l, sc2_l, g2_l = jnp.split(mod_l, 6, axis=-1)
        sh1_c, sc1_c, g1_c, sh2_c, sc2_c, g2_c = jnp.split(mod_c, 6, axis=-1)

        hl = modulate(rms_norm(xl, norm_mix_g[layer]), sh1_l, sc1_l)
        if is_even:
            yl = even_mixer(hl, w_in_even[j], conv_w[j], w_out_even[j])
            yc = None
            if not last:
                hc = modulate(rms_norm(xc, norm_mix_g[layer]), sh1_c, sc1_c)
                yc = even_mixer(hc, w_in_even[j], conv_w[j], w_out_even[j])
        else:
            hc = modulate(rms_norm(xc, norm_mix_g[layer]), sh1_c, sc1_c)
            yl, yc = window_attention(hl, hc, w_qkv[j], q_norm_g[j], k_norm_g[j], sink_logit[j], w_o[j],
                                      not last)
        xl = xl + g1_l * yl

        hl2 = modulate(rms_norm(xl, norm_ffn_g[layer]), sh2_l, sc2_l).reshape(B * L, D)
        moe_args = (w_router_g[layer], b_router_g[layer], w_router_e[layer], b_router_e[layer],
                    w1[layer], w3[layer], w2[layer])
        if last:
            xl = xl + g2_l * hier_moe(hl2, *moe_args).reshape(B, L, D)
        else:
            xc = xc + g1_c * yc
            hc2 = modulate(rms_norm(xc, norm_ffn_g[layer]), sh2_c, sc2_c).reshape(B * Lc, D)
            f = hier_moe(jnp.concatenate([hl2, hc2], axis=0), *moe_args)
            xl = xl + g2_l * f[:B * L].reshape(B, L, D)
            xc = xc + g2_c * f[B * L:].reshape(B, Lc, D)
    return xl
```

```python
import functools
import math

import numpy as np
import jax
import jax.numpy as jnp
from jax import lax
from jax.experimental import pallas as pl
from jax.experimental.pallas import tpu as pltpu

F32 = jnp.float32
BF16 = jnp.bfloat16
I32 = jnp.int32

EPS = 1e-6
NEG_INF = -1e30

GRID_W = 64
FOURIER_GROUPS = 4
FOURIER_GROUP_DIM = 128
FOURIER_WIDTH = FOURIER_GROUPS * FOURIER_GROUP_DIM
CONV_WIDTH = 512
N_HEADS = 16
N_KV_HEADS = 4
GQA_GROUP = N_HEADS // N_KV_HEADS
HEAD_DIM = 64
WINDOW = 128
ROPE_BASE = 10000.0
ROPE_PAIRS = HEAD_DIM // 4
N_GROUPS = 4
EXPERTS_PER_GROUP = 8
N_EXPERTS = N_GROUPS * EXPERTS_PER_GROUP

LANES = 128
ROW_TILE = 256
ATT_BLOCK = 128
MOE_ROWS = 256
DFT_MINOR = 128
VMEM_LIMIT = 48 * 1024 * 1024

_HI = lax.Precision.HIGHEST


def _cparams(sem):
    return pltpu.CompilerParams(dimension_semantics=sem, vmem_limit_bytes=VMEM_LIMIT)


def _split_bf16(x):
    hi = x.astype(BF16)
    lo = (x - hi.astype(F32)).astype(BF16)
    return hi, lo


def _mod_kernel(ct_ref, w_ref, b_ref, o_ref):
    ct = ct_ref[...]
    s = ct * jax.nn.sigmoid(ct)
    w = w_ref[...]
    r0 = jnp.sum(w * s[:, 0:1], axis=0, keepdims=True)
    r1 = jnp.sum(w * s[:, 1:2], axis=0, keepdims=True)
    o_ref[...] = jnp.concatenate([r0, r1], axis=0) + b_ref[...]


def _modulation(c, c_ctx, w_mod, b_mod):
    depth, d, n = w_mod.shape
    tn = 512
    ct = jnp.stack([c.reshape(d), c_ctx.reshape(d)], axis=1)
    return pl.pallas_call(
        _mod_kernel,
        grid=(depth, n // tn),
        in_specs=[
            pl.BlockSpec((d, 2), lambda l, j: (0, 0)),
            pl.BlockSpec((None, d, tn), lambda l, j: (l, 0, j)),
            pl.BlockSpec((None, 1, tn), lambda l, j: (l, 0, j)),
        ],
        out_specs=pl.BlockSpec((None, 2, tn), lambda l, j: (l, 0, j)),
        out_shape=jax.ShapeDtypeStruct((depth, 2, n), F32),
        compiler_params=_cparams(("arbitrary", "arbitrary")),
        name="modulation",
    )(ct, w_mod, b_mod.reshape(depth, 1, n))


def _norm_mod(x, g, shift, scale):
    ms = jnp.mean(x * x, axis=-1, keepdims=True)
    y = x * lax.rsqrt(ms + EPS) * g
    return y * (1.0 + scale) + shift


def _stream_of(n_lat_tiles):
    return lambda i: (jnp.where(i >= n_lat_tiles, 1, 0), 0, 0)


def _even_in_kernel(x_ref, g_ref, mod_ref, w_ref, cs_ref, z_ref, bg_ref, u_ref):
    h = _norm_mod(x_ref[...], g_ref[...], mod_ref[0:1, :], mod_ref[1:2, :])
    p = jnp.dot(h.astype(BF16), w_ref[...], preferred_element_type=F32)
    cs = cs_ref[...]
    for grp in range(FOURIER_GROUPS):
        lo = grp * FOURIER_GROUP_DIM
        a = p[:, lo:lo + FOURIER_GROUP_DIM].astype(BF16)
        z = jnp.dot(a, cs, preferred_element_type=F32)
        z_ref[0, :, lo:lo + FOURIER_GROUP_DIM] = z[:, :FOURIER_GROUP_DIM].astype(BF16)
        z_ref[1, :, lo:lo + FOURIER_GROUP_DIM] = z[:, FOURIER_GROUP_DIM:].astype(BF16)
    o = FOURIER_WIDTH
    bg_ref[...] = p[:, o:o + CONV_WIDTH].astype(BF16)
    u_ref[...] = (p[:, o + CONV_WIDTH:o + 2 * CONV_WIDTH] * p[:, o + 2 * CONV_WIDTH:]).astype(BF16)


def _even_in(x, g, mod, w_in, cs, n_lat_tiles):
    t, d = x.shape
    n = w_in.shape[1]
    tm = ROW_TILE
    return pl.pallas_call(
        _even_in_kernel,
        grid=(t // tm,),
        in_specs=[
            pl.BlockSpec((tm, d), lambda i: (i, 0)),
            pl.BlockSpec((1, d), lambda i: (0, 0)),
            pl.BlockSpec((None, 6, d), _stream_of(n_lat_tiles)),
            pl.BlockSpec((d, n), lambda i: (0, 0)),
            pl.BlockSpec(cs.shape, lambda i: (0, 0)),
        ],
        out_specs=[
            pl.BlockSpec((2, tm, FOURIER_WIDTH), lambda i: (0, i, 0)),
            pl.BlockSpec((tm, CONV_WIDTH), lambda i: (i, 0)),
            pl.BlockSpec((tm, CONV_WIDTH), lambda i: (i, 0)),
        ],
        out_shape=[
            jax.ShapeDtypeStruct((2, t, FOURIER_WIDTH), BF16),
            jax.ShapeDtypeStruct((t, CONV_WIDTH), BF16),
            jax.ShapeDtypeStruct((t, CONV_WIDTH), BF16),
        ],
        compiler_params=_cparams(("arbitrary",)),
        name="even_in_proj",
    )(x, g, mod, w_in, cs)


def _mm_kernel(a_ref, b_ref, o_ref):
    o_ref[...] = jnp.dot(a_ref[...], b_ref[...], preferred_element_type=F32).astype(o_ref.dtype)


def _mm_lhs_resident(a, b, tn):
    m, k = a.shape
    n = b.shape[1]
    tn = min(tn, n)
    return pl.pallas_call(
        _mm_kernel,
        grid=(n // tn,),
        in_specs=[pl.BlockSpec((m, k), lambda j: (0, 0)), pl.BlockSpec((k, tn), lambda j: (0, j))],
        out_specs=pl.BlockSpec((m, tn), lambda j: (0, j)),
        out_shape=jax.ShapeDtypeStruct((m, n), BF16),
        compiler_params=_cparams(("arbitrary",)),
        name="dft_major",
    )(a, b)


def _dft_minor_kernel(a_ref, g_ref, o_ref, *, kb):
    for j in range(kb):
        gcat = jnp.concatenate([g_ref[0, j], g_ref[1, j]], axis=0)
        o_ref[j] = jnp.dot(a_ref[j], gcat, preferred_element_type=F32).astype(o_ref.dtype)


def _dft_minor(a_tab, g4):
    n1, m, k2 = a_tab.shape
    c = g4.shape[-1]
    kb = 8
    return pl.pallas_call(
        functools.partial(_dft_minor_kernel, kb=kb),
        grid=(n1 // kb,),
        in_specs=[
            pl.BlockSpec((kb, m, k2), lambda i: (i, 0, 0)),
            pl.BlockSpec((2, kb, DFT_MINOR, c), lambda i: (0, i, 0, 0)),
        ],
        out_specs=pl.BlockSpec((kb, m, c), lambda i: (i, 0, 0)),
        out_shape=jax.ShapeDtypeStruct((n1, m, c), BF16),
        compiler_params=_cparams(("arbitrary",)),
        name="dft_minor",
    )(a_tab, g4)


def _dft_tables(l, lc):
    gd = FOURIER_GROUP_DIM
    kk = np.arange(gd)
    ang = 2.0 * np.pi * ((kk[:, None] * kk[None, :]) % gd) / gd
    cs = np.concatenate([np.cos(ang), -np.sin(ang)], axis=1)
    n1 = l // DFT_MINOR
    k1 = np.arange(n1)
    ang1 = 2.0 * np.pi * ((k1[:, None] * k1[None, :]) % n1) / n1
    c1, s1 = np.cos(ang1), np.sin(ang1)
    m1 = np.block([[c1, s1], [-s1, c1]])
    l2 = np.arange(DFT_MINOR)
    kfull = k1[:, None, None] + n1 * l2[None, :, None]
    ang2 = 2.0 * np.pi * ((kfull * l2[None, None, :]) % l) / l
    sc = 1.0 / math.sqrt(l * gd)
    a_tab = np.concatenate([np.cos(ang2), np.sin(ang2)], axis=2) * sc
    kc = np.arange(lc)
    angc = 2.0 * np.pi * ((kc[:, None] * kc[None, :]) % lc) / lc
    mc = np.concatenate([np.cos(angc), np.sin(angc)], axis=1) / math.sqrt(lc * gd)
    as_bf16 = lambda v: jnp.asarray(v, F32).astype(BF16)
    return as_bf16(cs), as_bf16(m1), as_bf16(a_tab), as_bf16(mc)


def _fourier_seq(z, l, m1, a_tab, mc):
    t, c = z.shape[1], z.shape[2]
    lc = t - l
    n1 = l // DFT_MINOR
    zs = z[:, :l].reshape(2 * n1, DFT_MINOR * c)
    g = _mm_lhs_resident(m1, zs, 8192)
    g4 = g.reshape(2, n1, DFT_MINOR, c)
    fp = _dft_minor(a_tab, g4)
    f_lat = jnp.swapaxes(fp, 0, 1).reshape(l, c)
    zc = z[:, l:].reshape(2 * lc, c)
    f_ctx = _mm_lhs_resident(mc, zc, c)
    return jnp.concatenate([f_lat, f_ctx], axis=0)


def _even_out_kernel(f_ref, bg_ref, u_ref, up_ref, un_ref, cw_ref, w_ref, x_ref, mod_ref, o_ref,
                     *, n_lat_tiles, n_tiles):
    i = pl.program_id(0)
    tm = u_ref.shape[0]
    u = u_ref[...].astype(F32)
    row = lax.broadcasted_iota(I32, u.shape, 0)
    first = jnp.logical_or(i == 0, i == n_lat_tiles)
    last = jnp.logical_or(i == n_lat_tiles - 1, i == n_tiles - 1)
    hb = up_ref.shape[0]
    halo_p = up_ref[...].astype(F32)[hb - 1:hb, :] * jnp.where(first, 0.0, 1.0)
    halo_n = un_ref[...].astype(F32)[0:1, :] * jnp.where(last, 0.0, 1.0)
    u_prev = jnp.where(row == 0, halo_p, pltpu.roll(u, 1, axis=0))
    u_next = jnp.where(row == tm - 1, halo_n, pltpu.roll(u, tm - 1, axis=0))
    cw = cw_ref[...]
    y = bg_ref[...].astype(F32) * (cw[0:1, :] * u_prev + cw[1:2, :] * u + cw[2:3, :] * u_next)
    acc = jnp.dot(f_ref[...], w_ref[0:FOURIER_WIDTH, :], preferred_element_type=F32)
    acc += jnp.dot(y.astype(BF16), w_ref[FOURIER_WIDTH:, :], preferred_element_type=F32)
    o_ref[...] = x_ref[...] + mod_ref[2:3, :] * acc


def _even_out(f, bg, u, conv_w, w_out, x, mod, n_lat_tiles):
    t, d = x.shape
    tm = ROW_TILE
    hb = 16
    n_tiles = t // tm
    r = tm // hb
    return pl.pallas_call(
        functools.partial(_even_out_kernel, n_lat_tiles=n_lat_tiles, n_tiles=n_tiles),
        grid=(n_tiles,),
        in_specs=[
            pl.BlockSpec((tm, FOURIER_WIDTH), lambda i: (i, 0)),
            pl.BlockSpec((tm, CONV_WIDTH), lambda i: (i, 0)),
            pl.BlockSpec((tm, CONV_WIDTH), lambda i: (i, 0)),
            pl.BlockSpec((hb, CONV_WIDTH), lambda i: (jnp.maximum(i * r - 1, 0), 0)),
            pl.BlockSpec((hb, CONV_WIDTH), lambda i: (jnp.minimum((i + 1) * r, t // hb - 1), 0)),
            pl.BlockSpec(conv_w.shape, lambda i: (0, 0)),
            pl.BlockSpec(w_out.shape, lambda i: (0, 0)),
            pl.BlockSpec((tm, d), lambda i: (i, 0)),
            pl.BlockSpec((None, 6, d), _stream_of(n_lat_tiles)),
        ],
        out_specs=pl.BlockSpec((tm, d), lambda i: (i, 0)),
        out_shape=jax.ShapeDtypeStruct((t, d), F32),
        compiler_params=_cparams(("arbitrary",)),
        name="even_out_proj",
    )(f, bg, u, u, u, conv_w, w_out, x, mod)


def _seg_rms_scale(v, seg, seg_t):
    hi, lo = _split_bf16(v * v)
    ss = jnp.dot(hi, seg, preferred_element_type=F32) + jnp.dot(lo, seg, preferred_element_type=F32)
    inv = lax.rsqrt(ss * (1.0 / HEAD_DIM) + EPS)
    ihi, ilo = _split_bf16(inv)
    return (jnp.dot(ihi, seg_t, preferred_element_type=F32)
            + jnp.dot(ilo, seg_t, preferred_element_type=F32))


def _rope_cols(v, cos, sa, sb, scale):
    cols = []
    for j in range(v.shape[1] // LANES):
        c = v[:, j * LANES:(j + 1) * LANES]
        r = c * cos + pltpu.roll(c, LANES - ROPE_PAIRS, axis=1) * sa + pltpu.roll(c, ROPE_PAIRS, axis=1) * sb
        cols.append(r * scale if scale != 1.0 else r)
    return jnp.concatenate(cols, axis=1)


def _odd_in_kernel(x_ref, g_ref, mod_ref, w_ref, qg_ref, kg_ref, segq_ref, segqt_ref, segk_ref, segkt_ref,
                   rope_ref, q_ref, k_ref, v_ref):
    h = _norm_mod(x_ref[...], g_ref[...], mod_ref[0:1, :], mod_ref[1:2, :])
    p = jnp.dot(h.astype(BF16), w_ref[...], preferred_element_type=F32)
    qd = N_HEADS * HEAD_DIM
    kd = 2 * N_KV_HEADS * HEAD_DIM
    cos, sa, sb = rope_ref[0], rope_ref[1], rope_ref[2]
    q = p[:, :qd]
    q = q * _seg_rms_scale(q, segq_ref[...], segqt_ref[...]) * qg_ref[...]
    q_ref[...] = _rope_cols(q, cos, sa, sb, HEAD_DIM ** -0.5).astype(BF16)
    k = p[:, qd:qd + kd]
    k = k * _seg_rms_scale(k, segk_ref[...], segkt_ref[...]) * kg_ref[...]
    k_ref[...] = _rope_cols(k, cos, sa, sb, 1.0).astype(BF16)
    v_ref[...] = p[:, qd + kd:].astype(BF16)


def _odd_in(x, g, mod, w_qkv, qg, kg, segs, rope, n_lat_tiles, n_tiles):
    t, d = x.shape
    n = w_qkv.shape[1]
    tm = ROW_TILE
    rows = n_tiles * tm
    qd = N_HEADS * HEAD_DIM
    kd = 2 * N_KV_HEADS * HEAD_DIM
    segq, segqt, segk, segkt = segs
    const = lambda a: pl.BlockSpec(a.shape, lambda i: (0,) * a.ndim)
    return pl.pallas_call(
        _odd_in_kernel,
        grid=(n_tiles,),
        in_specs=[
            pl.BlockSpec((tm, d), lambda i: (i, 0)),
            const(g),
            pl.BlockSpec((None, 6, d), _stream_of(n_lat_tiles)),
            const(w_qkv), const(qg), const(kg), const(segq), const(segqt), const(segk), const(segkt),
            pl.BlockSpec((3, tm, LANES), lambda i: (0, i, 0)),
        ],
        out_specs=[
            pl.BlockSpec((tm, qd), lambda i: (i, 0)),
            pl.BlockSpec((tm, kd), lambda i: (i, 0)),
            pl.BlockSpec((tm, kd), lambda i: (i, 0)),
        ],
        out_shape=[
            jax.ShapeDtypeStruct((rows, qd), BF16),
            jax.ShapeDtypeStruct((rows, kd), BF16),
            jax.ShapeDtypeStruct((rows, kd), BF16),
        ],
        compiler_params=_cparams(("arbitrary",)),
        name="odd_in_proj",
    )(x, g, mod, w_qkv, qg, kg, segq, segqt, segk, segkt, rope)


def _rope_tables(l, t):
    pos = np.arange(l)
    freqs = ROPE_BASE ** (-np.arange(ROPE_PAIRS, dtype=np.float32) / ROPE_PAIRS)
    lane = np.arange(LANES) % HEAD_DIM
    axis = lane // (2 * ROPE_PAIRS)
    half = (lane % (2 * ROPE_PAIRS)) // ROPE_PAIRS
    pair = lane % ROPE_PAIRS
    p = np.where(axis[None, :] == 0, (pos // GRID_W)[:, None], (pos % GRID_W)[:, None]).astype(np.float32)
    ang = p * freqs[pair][None, :].astype(np.float32)
    cos, sin = np.cos(ang), np.sin(ang)
    sa = np.where(half[None, :] == 0, -sin, 0.0)
    sb = np.where(half[None, :] == 1, sin, 0.0)
    tab = np.zeros((3, t, LANES), np.float32)
    tab[0, :l], tab[1, :l], tab[2, :l] = cos, sa, sb
    tab[0, l:] = 1.0
    return jnp.asarray(tab)


def _segment_matrices():
    def seg(width):
        m = np.zeros((width, LANES), np.float32)
        m[np.arange(width), np.arange(width) // HEAD_DIM] = 1.0
        return m
    sq, sk = seg(N_HEADS * HEAD_DIM), seg(2 * N_KV_HEADS * HEAD_DIM)
    b = lambda v: jnp.asarray(v).astype(BF16)
    return b(sq), b(sq.T), b(sk), b(sk.T)


def _attn_kernel(sink_ref, q_ref, kp_ref, kc_ref, kn_ref, vp_ref, vc_ref, vn_ref, kx_ref, vx_ref, o_ref,
                 *, n_lat_blocks, seq_len):
    g = pl.program_id(0)
    b = pl.program_id(1)
    bq = q_ref.shape[0]
    lane = lax.broadcasted_iota(I32, (bq, LANES), 1)
    low = lane < HEAD_DIM
    parts, sinks = [], []
    for j in range(GQA_GROUP):
        c = q_ref[:, (j // 2) * LANES:(j // 2 + 1) * LANES]
        keep = low if j % 2 == 0 else jnp.logical_not(low)
        parts.append(jnp.where(keep, c, jnp.zeros_like(c)))
        sinks.append(jnp.full((bq, 1), sink_ref[g * GQA_GROUP + j], F32))
    qs = jnp.concatenate(parts, axis=0)
    sink = jnp.concatenate(sinks, axis=0)
    kwin = jnp.concatenate([kp_ref[...], kc_ref[...], kn_ref[...]], axis=0)
    vwin = jnp.concatenate([vp_ref[...], vc_ref[...], vn_ref[...]], axis=0)
    dn = (((1,), (1,)), ((), ()))
    s_loc = lax.dot_general(qs, kwin, dn, preferred_element_type=F32)
    s_ctx = lax.dot_general(qs, kx_ref[...], dn, preferred_element_type=F32)
    r = lax.broadcasted_iota(I32, s_loc.shape, 0) % bq
    col = lax.broadcasted_iota(I32, s_loc.shape, 1)
    kpos = (b - 1) * bq + col
    ok = (jnp.abs(col - bq - r) <= WINDOW) & (kpos >= 0) & (kpos < seq_len) & (b < n_lat_blocks)
    s_loc = jnp.where(ok, s_loc, NEG_INF)
    m = jnp.maximum(jnp.maximum(jnp.max(s_loc, axis=1, keepdims=True), jnp.max(s_ctx, axis=1, keepdims=True)), sink)
    p_loc = jnp.exp(s_loc - m)
    p_ctx = jnp.exp(s_ctx - m)
    den = jnp.sum(p_loc, axis=1, keepdims=True) + jnp.sum(p_ctx, axis=1, keepdims=True) + jnp.exp(sink - m)
    o = jnp.dot(p_loc.astype(BF16), vwin, preferred_element_type=F32)
    o += jnp.dot(p_ctx.astype(BF16), vx_ref[...], preferred_element_type=F32)
    o = o / den
    c0 = jnp.where(low, o[0:bq], o[bq:2 * bq])
    c1 = jnp.where(low, o[2 * bq:3 * bq], o[3 * bq:4 * bq])
    o_ref[...] = jnp.concatenate([c0, c1], axis=1).astype(o_ref.dtype)


def _attention(q, k, v, sink, l, n_q_blocks):
    t = q.shape[0]
    bq = ATT_BLOCK
    nlb = l // bq
    lc = t - l
    ctx_blk = l // lc
    gw = GQA_GROUP * HEAD_DIM
    kspec = lambda f: pl.BlockSpec((bq, LANES), f)
    prev = lambda g, b, s: (jnp.clip(b - 1, 0, nlb - 1), g)
    cur = lambda g, b, s: (jnp.minimum(b, nlb - 1), g)
    nxt = lambda g, b, s: (jnp.clip(b + 1, 0, nlb - 1), g)
    ctx = pl.BlockSpec((lc, LANES), lambda g, b, s: (ctx_blk, g))
    return pl.pallas_call(
        functools.partial(_attn_kernel, n_lat_blocks=nlb, seq_len=l),
        grid_spec=pltpu.PrefetchScalarGridSpec(
            num_scalar_prefetch=1,
            grid=(N_KV_HEADS, n_q_blocks),
            in_specs=[
                pl.BlockSpec((bq, gw), lambda g, b, s: (b, g)),
                kspec(prev), kspec(cur), kspec(nxt), kspec(prev), kspec(cur), kspec(nxt), ctx, ctx,
            ],
            out_specs=pl.BlockSpec((bq, gw), lambda g, b, s: (b, g)),
        ),
        out_shape=jax.ShapeDtypeStruct((n_q_blocks * bq, N_HEADS * HEAD_DIM), BF16),
        compiler_params=_cparams(("arbitrary", "arbitrary")),
        name="window_attention",
    )(sink, q, k, k, k, v, v, v, k, v)


def _odd_out_kernel(a_ref, w_ref, x_ref, mod_ref, o_ref):
    acc = jnp.dot(a_ref[...], w_ref[...], preferred_element_type=F32)
    o_ref[...] = x_ref[...] + mod_ref[2:3, :] * acc


def _odd_out(a, w_o, x, mod, n_lat_tiles, n_tiles):
    d = x.shape[1]
    tm = ROW_TILE
    return pl.pallas_call(
        _odd_out_kernel,
        grid=(n_tiles,),
        in_specs=[
            pl.BlockSpec((tm, a.shape[1]), lambda i: (i, 0)),
            pl.BlockSpec(w_o.shape, lambda i: (0, 0)),
            pl.BlockSpec((tm, d), lambda i: (i, 0)),
            pl.BlockSpec((None, 6, d), _stream_of(n_lat_tiles)),
        ],
        out_specs=pl.BlockSpec((tm, d), lambda i: (i, 0)),
        out_shape=jax.ShapeDtypeStruct((n_tiles * tm, d), F32),
        compiler_params=_cparams(("arbitrary",)),
        name="odd_out_proj",
    )(a, w_o, x, mod)


ROUTER_EXPERT_ROW0 = 8


def _router_kernel(x_ref, g_ref, mod_ref, wr_ref, br_ref, tri_ref, h_ref, ri_ref, rg_ref, cnt_ref, carry_ref):
    i = pl.program_id(0)

    @pl.when(i == 0)
    def _():
        carry_ref[...] = jnp.zeros_like(carry_ref)

    h = _norm_mod(x_ref[...], g_ref[...], mod_ref[3:4, :], mod_ref[4:5, :])
    h_ref[...] = h
    tm = h.shape[0]
    logits = lax.dot_general(wr_ref[...], h, (((1,), (1,)), ((), ())), precision=_HI,
                             preferred_element_type=F32) + br_ref[:, 0:1]
    gl = logits[0:N_GROUPS]
    gmax = jnp.max(gl, axis=0, keepdims=True)
    gi = lax.broadcasted_iota(I32, gl.shape, 0)
    g_idx = jnp.min(jnp.where(gl == gmax, gi, N_GROUPS), axis=0, keepdims=True)
    g_val = 1.0 / jnp.sum(jnp.exp(gl - gmax), axis=0, keepdims=True)
    e_in = logits[ROUTER_EXPERT_ROW0:ROUTER_EXPERT_ROW0 + EXPERTS_PER_GROUP]
    for grp in range(1, N_GROUPS):
        lo = ROUTER_EXPERT_ROW0 + grp * EXPERTS_PER_GROUP
        e_in = jnp.where(g_idx == grp, logits[lo:lo + EXPERTS_PER_GROUP], e_in)
    ei = lax.broadcasted_iota(I32, e_in.shape, 0)
    v1 = jnp.max(e_in, axis=0, keepdims=True)
    i1 = jnp.min(jnp.where(e_in == v1, ei, EXPERTS_PER_GROUP), axis=0, keepdims=True)
    rest = jnp.where(ei == i1, -jnp.inf, e_in)
    v2 = jnp.max(rest, axis=0, keepdims=True)
    i2 = jnp.min(jnp.where(rest == v2, ei, EXPERTS_PER_GROUP), axis=0, keepdims=True)
    w2 = jnp.exp(v2 - v1)
    gate1 = g_val / (1.0 + w2)
    gate2 = g_val * w2 / (1.0 + w2)
    e1 = g_idx * EXPERTS_PER_GROUP + i1
    e2 = g_idx * EXPERTS_PER_GROUP + i2
    xi = lax.broadcasted_iota(I32, (N_EXPERTS, tm), 0)
    oh1 = xi == e1
    oh2 = xi == e2
    oh = oh1.astype(F32) + oh2.astype(F32)
    before = jnp.dot(oh.astype(BF16), tri_ref[...], preferred_element_type=F32) + carry_ref[:, 0:1]
    rank1 = jnp.sum(jnp.where(oh1, before, 0.0), axis=0, keepdims=True)
    rank2 = jnp.sum(jnp.where(oh2, before, 0.0), axis=0, keepdims=True)
    carry_ref[...] = carry_ref[...] + jnp.sum(oh, axis=1, keepdims=True)
    cnt_ref[...] = carry_ref[...]
    orow = lax.broadcasted_iota(I32, (8, tm), 0)
    ri_ref[...] = jnp.where(orow == 0, e1, jnp.where(orow == 1, e2, jnp.where(
        orow == 2, rank1.astype(I32), jnp.where(orow == 3, rank2.astype(I32), 0))))
    rg_ref[...] = jnp.where(orow == 0, gate1, jnp.where(orow == 1, gate2, 0.0))


def _router(x, g, mod, wr, br, tri, n_lat_tiles, n_tiles):
    d = x.shape[1]
    tm = ROW_TILE
    rows = n_tiles * tm
    const = lambda a: pl.BlockSpec(a.shape, lambda i: (0,) * a.ndim)
    return pl.pallas_call(
        _router_kernel,
        grid=(n_tiles,),
        in_specs=[
            pl.BlockSpec((tm, d), lambda i: (i, 0)),
            const(g),
            pl.BlockSpec((None, 6, d), _stream_of(n_lat_tiles)),
            const(wr), const(br), const(tri),
        ],
        out_specs=[
            pl.BlockSpec((tm, d), lambda i: (i, 0)),
            pl.BlockSpec((8, tm), lambda i: (0, i)),
            pl.BlockSpec((8, tm), lambda i: (0, i)),
            pl.BlockSpec((N_EXPERTS, LANES), lambda i: (0, 0)),
        ],
        out_shape=[
            jax.ShapeDtypeStruct((rows, d), F32),
            jax.ShapeDtypeStruct((8, rows), I32),
            jax.ShapeDtypeStruct((8, rows), F32),
            jax.ShapeDtypeStruct((N_EXPERTS, LANES), F32),
        ],
        scratch_shapes=[pltpu.VMEM((N_EXPERTS, LANES), F32)],
        compiler_params=_cparams(("arbitrary",)),
        name="moe_router",
    )(x, g, mod, wr, br, tri)


def _dispatch_kernel(dest_ref, h_ref, xb_ref, sem):
    tm = h_ref.shape[0]

    def row_copy(k, tok):
        return pltpu.make_async_copy(h_ref.at[pl.ds(tok, 1), :], xb_ref.at[pl.ds(dest_ref[k, tok], 1), :], sem)

    def start(tok, c):
        row_copy(0, tok).start()
        row_copy(1, tok).start()
        return c

    def wait(tok, c):
        row_copy(0, tok).wait()
        row_copy(1, tok).wait()
        return c

    lax.fori_loop(0, tm, start, 0)
    lax.fori_loop(0, tm, wait, 0)


def _dispatch(h, dest, n_rows):
    t, d = h.shape
    tm = ROW_TILE
    return pl.pallas_call(
        _dispatch_kernel,
        grid=(t // tm,),
        in_specs=[
            pl.BlockSpec((None, 2, tm), lambda i: (i, 0, 0), memory_space=pltpu.SMEM),
            pl.BlockSpec((tm, d), lambda i: (i, 0)),
        ],
        out_specs=pl.BlockSpec(memory_space=pl.ANY),
        out_shape=jax.ShapeDtypeStruct((n_rows, d), F32),
        scratch_shapes=[pltpu.SemaphoreType.DMA(())],
        compiler_params=_cparams(("arbitrary",)),
        name="moe_dispatch",
    )(dest, h)


def _ffn_kernel(be_ref, nv_ref, xb_ref, w1_ref, w3_ref, w2_ref, yb_ref, w1s, w3s, w2s):
    i = pl.program_id(0)
    changed = jnp.logical_or(i == 0, be_ref[i] != be_ref[jnp.maximum(i - 1, 0)])

    @pl.when(changed)
    def _():
        w1s[...] = w1_ref[...].astype(BF16)
        w3s[...] = w3_ref[...].astype(BF16)
        w2s[...] = w2_ref[...].astype(BF16)

    nv = nv_ref[i]

    @pl.when(nv > 0)
    def _():
        x = xb_ref[...]
        row = lax.broadcasted_iota(I32, x.shape, 0)
        xb = jnp.where(row < nv, x, 0.0).astype(BF16)
        a = jnp.dot(xb, w1s[...], preferred_element_type=F32)
        b = jnp.dot(xb, w3s[...], preferred_element_type=F32)
        hid = (a * jax.nn.sigmoid(a) * b).astype(BF16)
        yb_ref[...] = jnp.dot(hid, w2s[...], preferred_element_type=F32)

    @pl.when(nv <= 0)
    def _():
        yb_ref[...] = jnp.zeros_like(yb_ref)


def _expert_ffn(xb, block_exp, n_valid, w1, w3, w2):
    r, d = xb.shape
    f = w1.shape[2]
    bm = MOE_ROWS
    return pl.pallas_call(
        _ffn_kernel,
        grid_spec=pltpu.PrefetchScalarGridSpec(
            num_scalar_prefetch=2,
            grid=(r // bm,),
            in_specs=[
                pl.BlockSpec((bm, d), lambda i, be, nv: (i, 0)),
                pl.BlockSpec((None, d, f), lambda i, be, nv: (be[i], 0, 0)),
                pl.BlockSpec((None, d, f), lambda i, be, nv: (be[i], 0, 0)),
                pl.BlockSpec((None, f, d), lambda i, be, nv: (be[i], 0, 0)),
            ],
            out_specs=pl.BlockSpec((bm, d), lambda i, be, nv: (i, 0)),
            scratch_shapes=[pltpu.VMEM((d, f), BF16), pltpu.VMEM((d, f), BF16), pltpu.VMEM((f, d), BF16)],
        ),
        out_shape=jax.ShapeDtypeStruct((r, d), F32),
        compiler_params=_cparams(("arbitrary",)),
        name="moe_expert_mlp",
    )(block_exp, n_valid, xb, w1, w3, w2)


def _combine_kernel(dest_ref, yb_ref, gate_ref, x_ref, mod_ref, o_ref, buf, sem):
    tm = x_ref.shape[0]

    def row_copy(k, tok):
        return pltpu.make_async_copy(yb_ref.at[pl.ds(dest_ref[k, tok], 1), :], buf.at[k, pl.ds(tok, 1), :], sem)

    def start(tok, c):
        row_copy(0, tok).start()
        row_copy(1, tok).start()
        return c

    def wait(tok, c):
        row_copy(0, tok).wait()
        row_copy(1, tok).wait()
        return c

    lax.fori_loop(0, tm, start, 0)
    lax.fori_loop(0, tm, wait, 0)
    gate = gate_ref[...]
    y = gate[:, 0:1] * buf[0] + gate[:, 1:2] * buf[1]
    o_ref[...] = x_ref[...] + mod_ref[5:6, :] * y


def _combine(yb, dest, gates, x, mod, n_lat_tiles, n_tiles):
    d = x.shape[1]
    tm = ROW_TILE
    return pl.pallas_call(
        _combine_kernel,
        grid=(n_tiles,),
        in_specs=[
            pl.BlockSpec((None, 2, tm), lambda i: (i, 0, 0), memory_space=pltpu.SMEM),
            pl.BlockSpec(memory_space=pl.ANY),
            pl.BlockSpec((tm, 8), lambda i: (i, 0)),
            pl.BlockSpec((tm, d), lambda i: (i, 0)),
            pl.BlockSpec((None, 6, d), _stream_of(n_lat_tiles)),
        ],
        out_specs=pl.BlockSpec((tm, d), lambda i: (i, 0)),
        out_shape=jax.ShapeDtypeStruct((n_tiles * tm, d), F32),
        scratch_shapes=[pltpu.VMEM((2, tm, d), F32), pltpu.SemaphoreType.DMA(())],
        compiler_params=_cparams(("arbitrary",)),
        name="moe_combine",
    )(dest, yb, gates, x, mod)


def _moe_layer(x, g, mod, wr, br, tri, w1, w3, w2, n_lat_tiles, n_tiles):
    tm = ROW_TILE
    rows = n_tiles * tm
    bm = MOE_ROWS
    h, ri, rg, cnt = _router(x, g, mod, wr, br, tri, n_lat_tiles, n_tiles)
    counts = cnt[:, 0].astype(I32)
    padded = (counts + bm - 1) // bm * bm
    pad_end = jnp.cumsum(padded)
    pad_start = pad_end - padded
    n_blocks = -(-(rows * 2) // bm) + N_EXPERTS
    ex = jnp.arange(N_EXPERTS, dtype=I32)
    start_of = lambda e: jnp.sum(jnp.where(e[:, None] == ex[None, :], pad_start[None, :], 0), axis=1)
    dest = jnp.stack([start_of(ri[0]) + ri[2], start_of(ri[1]) + ri[3]], axis=0)
    dest_t = dest.reshape(2, n_tiles, tm).transpose(1, 0, 2)
    blk0 = jnp.arange(n_blocks, dtype=I32) * bm
    block_exp = jnp.minimum(jnp.sum((pad_end[None, :] <= blk0[:, None]).astype(I32), axis=1), N_EXPERTS - 1)
    sel = block_exp[:, None] == ex[None, :]
    cnt_b = jnp.sum(jnp.where(sel, counts[None, :], 0), axis=1)
    st_b = jnp.sum(jnp.where(sel, pad_start[None, :], 0), axis=1)
    n_valid = jnp.clip(cnt_b - (blk0 - st_b), 0, bm).astype(I32)
    xb = _dispatch(h, dest_t, n_blocks * bm)
    yb = _expert_ffn(xb, block_exp.astype(I32), n_valid, w1, w3, w2)
    return _combine(yb, dest_t, rg[0:8].T, x, mod, n_lat_tiles, n_tiles)


def _router_matrix(w_rg, b_rg, w_re, b_re):
    d = w_rg.shape[0]
    wr = jnp.zeros((LANES, d), F32)
    wr = wr.at[0:N_GROUPS].set(w_rg.T.astype(F32))
    wr = wr.at[ROUTER_EXPERT_ROW0:ROUTER_EXPERT_ROW0 + N_EXPERTS].set(w_re.T.astype(F32))
    br = jnp.zeros((LANES,), F32)
    br = br.at[0:N_GROUPS].set(b_rg.astype(F32))
    br = br.at[ROUTER_EXPERT_ROW0:ROUTER_EXPERT_ROW0 + N_EXPERTS].set(b_re.astype(F32))
    return wr, jnp.broadcast_to(br[:, None], (LANES, LANES))


def _dup_heads(w):
    d = w.shape[0]
    w4 = w.reshape(d, N_KV_HEADS, 1, HEAD_DIM)
    return jnp.broadcast_to(w4, (d, N_KV_HEADS, 2, HEAD_DIM)).reshape(d, 2 * N_KV_HEADS * HEAD_DIM)


def kernel(x, c, ctx, c_ctx, w_mod, b_mod, norm_mix_g, norm_ffn_g, w_in_even, conv_w, w_out_even, w_qkv, q_norm_g,
           k_norm_g, sink_logit, w_o, w_router_g, b_router_g, w_router_e, b_router_e, w1, w3, w2):
    bsz, l, d = x.shape
    lc = ctx.shape[1]
    assert bsz == 1, "one sample per call"
    tm = ROW_TILE
    assert l % tm == 0 and lc % tm == 0 and l % lc == 0 and l % (DFT_MINOR * 8) == 0
    depth = w_mod.shape[0]
    t = l + lc
    nl, nt = l // tm, t // tm

    xs = jnp.concatenate([x.reshape(l, d), ctx.reshape(lc, d)], axis=0)
    mod_all = _modulation(c, c_ctx, w_mod, b_mod).reshape(depth, 2, 6, d)
    cs, m1, a_tab, mc = _dft_tables(l, lc)
    rope = _rope_tables(l, t)
    segs = _segment_matrices()
    tri = jnp.asarray(np.triu(np.ones((tm, tm), np.float32), 1)).astype(BF16)
    qd = N_HEADS * HEAD_DIM

    for layer in range(depth):
        last = layer == depth - 1
        j = layer // 2
        mod = mod_all[layer]
        g_mix = norm_mix_g[layer].reshape(1, d)
        g_ffn = norm_ffn_g[layer].reshape(1, d)
        if layer % 2 == 0:
            z, bg, u = _even_in(xs, g_mix, mod, w_in_even[j].astype(BF16), cs, nl)
            f = _fourier_seq(z, l, m1, a_tab, mc)
            xs = _even_out(f, bg, u, conv_w[j], w_out_even[j].astype(BF16), xs, mod, nl)
        else:
            wq = w_qkv[j]
            w_all = jnp.concatenate([wq[:, :qd], _dup_heads(wq[:, qd:qd + N_KV_HEADS * HEAD_DIM]),
                                     _dup_heads(wq[:, qd + N_KV_HEADS * HEAD_DIM:])], axis=1).astype(BF16)
            qg = jnp.tile(q_norm_g[j], N_HEADS).reshape(1, qd)
            kg = jnp.tile(k_norm_g[j], 2 * N_KV_HEADS).reshape(1, 2 * N_KV_HEADS * HEAD_DIM)
            q, k, v = _odd_in(xs, g_mix, mod, w_all, qg, kg, segs, rope, nl, nt)
            n_out = nl if last else nt
            att = _attention(q, k, v, sink_logit[j].astype(F32), l, n_out * (tm // ATT_BLOCK))
            xs = _odd_out(att, w_o[j].astype(BF16), xs, mod, nl, n_out)
        wr, br = _router_matrix(w_router_g[layer], b_router_g[layer], w_router_e[layer], b_router_e[layer])
        n_moe = nl if last else nt
        xs = _moe_layer(xs, g_ffn, mod, wr, br, tri, w1[layer], w3[layer], w2[layer], nl, n_moe)
    return xs[:l].reshape(bsz, l, d)
```

```python
import functools
import math

import numpy as np
import jax
import jax.numpy as jnp
from jax import lax
from jax.experimental import pallas as pl
from jax.experimental.pallas import tpu as pltpu

F32 = jnp.float32
BF16 = jnp.bfloat16
I32 = jnp.int32

EPS = 1e-6
NEG_INF = -1e30

GRID_W = 64
FOURIER_GROUPS = 4
FOURIER_GROUP_DIM = 128
FOURIER_WIDTH = FOURIER_GROUPS * FOURIER_GROUP_DIM
CONV_WIDTH = 512
N_HEADS = 16
N_KV_HEADS = 4
GQA_GROUP = N_HEADS // N_KV_HEADS
HEAD_DIM = 64
WINDOW = 128
ROPE_BASE = 10000.0
ROPE_PAIRS = HEAD_DIM // 4
N_GROUPS = 4
EXPERTS_PER_GROUP = 8
N_EXPERTS = N_GROUPS * EXPERTS_PER_GROUP

LANES = 128
ROW_TILE = 256
ATT_BLOCK = 128
MOE_ROWS = 256
DFT_MINOR = 128
VMEM_LIMIT = 48 * 1024 * 1024

_HI = lax.Precision.HIGHEST
LOG2E = math.log2(math.e)


def _cparams(sem):
    return pltpu.CompilerParams(dimension_semantics=sem, vmem_limit_bytes=VMEM_LIMIT)


def _split_bf16(x):
    hi = x.astype(BF16)
    lo = (x - hi.astype(F32)).astype(BF16)
    return hi, lo


def _mod_kernel(ct_ref, w_ref, b_ref, o_ref):
    ct = ct_ref[...]
    s = ct * jax.nn.sigmoid(ct)
    w = w_ref[...]
    r0 = jnp.sum(w * s[:, 0:1], axis=0, keepdims=True)
    r1 = jnp.sum(w * s[:, 1:2], axis=0, keepdims=True)
    o_ref[...] = jnp.concatenate([r0, r1], axis=0) + b_ref[...]


def _modulation(c, c_ctx, w_mod, b_mod):
    depth, d, n = w_mod.shape
    tn = 512
    ct = jnp.stack([c.reshape(d), c_ctx.reshape(d)], axis=1)
    return pl.pallas_call(
        _mod_kernel,
        grid=(depth, n // tn),
        in_specs=[
            pl.BlockSpec((d, 2), lambda l, j: (0, 0)),
            pl.BlockSpec((None, d, tn), lambda l, j: (l, 0, j)),
            pl.BlockSpec((None, 1, tn), lambda l, j: (l, 0, j)),
        ],
        out_specs=pl.BlockSpec((None, 2, tn), lambda l, j: (l, 0, j)),
        out_shape=jax.ShapeDtypeStruct((depth, 2, n), F32),
        compiler_params=_cparams(("arbitrary", "arbitrary")),
        name="modulation",
    )(ct, w_mod, b_mod.reshape(depth, 1, n))


def _norm_mod(x, g, shift, scale):
    ms = jnp.mean(x * x, axis=-1, keepdims=True)
    y = x * lax.rsqrt(ms + EPS) * g
    return y * (1.0 + scale) + shift


def _stream_of(n_lat_tiles):
    return lambda i: (jnp.where(i >= n_lat_tiles, 1, 0), 0, 0)


def _even_in_kernel(x_ref, g_ref, mod_ref, w_ref, cs_ref, z_ref, bg_ref, u_ref):
    h = _norm_mod(x_ref[...], g_ref[...], mod_ref[0:1, :], mod_ref[1:2, :])
    p = jnp.dot(h.astype(BF16), w_ref[...], preferred_element_type=F32)
    cs = cs_ref[...]
    for grp in range(FOURIER_GROUPS):
        lo = grp * FOURIER_GROUP_DIM
        a = p[:, lo:lo + FOURIER_GROUP_DIM].astype(BF16)
        z = jnp.dot(a, cs, preferred_element_type=F32)
        z_ref[0, :, lo:lo + FOURIER_GROUP_DIM] = z[:, :FOURIER_GROUP_DIM].astype(BF16)
        z_ref[1, :, lo:lo + FOURIER_GROUP_DIM] = z[:, FOURIER_GROUP_DIM:].astype(BF16)
    o = FOURIER_WIDTH
    bg_ref[...] = p[:, o:o + CONV_WIDTH].astype(BF16)
    u_ref[...] = (p[:, o + CONV_WIDTH:o + 2 * CONV_WIDTH] * p[:, o + 2 * CONV_WIDTH:]).astype(BF16)


def _even_in(x, g, mod, w_in, cs, n_lat_tiles):
    t, d = x.shape
    n = w_in.shape[1]
    tm = ROW_TILE
    return pl.pallas_call(
        _even_in_kernel,
        grid=(t // tm,),
        in_specs=[
            pl.BlockSpec((tm, d), lambda i: (i, 0)),
            pl.BlockSpec((1, d), lambda i: (0, 0)),
            pl.BlockSpec((None, 6, d), _stream_of(n_lat_tiles)),
            pl.BlockSpec((d, n), lambda i: (0, 0)),
            pl.BlockSpec(cs.shape, lambda i: (0, 0)),
        ],
        out_specs=[
            pl.BlockSpec((2, tm, FOURIER_WIDTH), lambda i: (0, i, 0)),
            pl.BlockSpec((tm, CONV_WIDTH), lambda i: (i, 0)),
            pl.BlockSpec((tm, CONV_WIDTH), lambda i: (i, 0)),
        ],
        out_shape=[
            jax.ShapeDtypeStruct((2, t, FOURIER_WIDTH), BF16),
            jax.ShapeDtypeStruct((t, CONV_WIDTH), BF16),
            jax.ShapeDtypeStruct((t, CONV_WIDTH), BF16),
        ],
        compiler_params=_cparams(("arbitrary",)),
        name="even_in_proj",
    )(x, g, mod, w_in, cs)


def _mm_kernel(a_ref, b_ref, o_ref):
    o_ref[...] = jnp.dot(a_ref[...], b_ref[...], preferred_element_type=F32).astype(o_ref.dtype)


def _mm_lhs_resident(a, b, tn):
    m, k = a.shape
    n = b.shape[1]
    tn = min(tn, n)
    return pl.pallas_call(
        _mm_kernel,
        grid=(n // tn,),
        in_specs=[pl.BlockSpec((m, k), lambda j: (0, 0)), pl.BlockSpec((k, tn), lambda j: (0, j))],
        out_specs=pl.BlockSpec((m, tn), lambda j: (0, j)),
        out_shape=jax.ShapeDtypeStruct((m, n), BF16),
        compiler_params=_cparams(("arbitrary",)),
        name="dft_major",
    )(a, b)


def _dft_minor_kernel(a_ref, g_ref, o_ref, *, kb):
    for j in range(kb):
        gcat = jnp.concatenate([g_ref[0, j], g_ref[1, j]], axis=0)
        o_ref[j] = jnp.dot(a_ref[j], gcat, preferred_element_type=F32).astype(o_ref.dtype)


def _dft_minor(a_tab, g4):
    n1, m, k2 = a_tab.shape
    c = g4.shape[-1]
    kb = 8
    return pl.pallas_call(
        functools.partial(_dft_minor_kernel, kb=kb),
        grid=(n1 // kb,),
        in_specs=[
            pl.BlockSpec((kb, m, k2), lambda i: (i, 0, 0)),
            pl.BlockSpec((2, kb, DFT_MINOR, c), lambda i: (0, i, 0, 0)),
        ],
        out_specs=pl.BlockSpec((kb, m, c), lambda i: (i, 0, 0)),
        out_shape=jax.ShapeDtypeStruct((n1, m, c), BF16),
        compiler_params=_cparams(("arbitrary",)),
        name="dft_minor",
    )(a_tab, g4)


def _dft_tables(l, lc):
    gd = FOURIER_GROUP_DIM
    kk = np.arange(gd)
    ang = 2.0 * np.pi * ((kk[:, None] * kk[None, :]) % gd) / gd
    cs = np.concatenate([np.cos(ang), -np.sin(ang)], axis=1)
    n1 = l // DFT_MINOR
    k1 = np.arange(n1)
    ang1 = 2.0 * np.pi * ((k1[:, None] * k1[None, :]) % n1) / n1
    c1, s1 = np.cos(ang1), np.sin(ang1)
    m1 = np.block([[c1, s1], [-s1, c1]])
    l2 = np.arange(DFT_MINOR)
    kfull = k1[:, None, None] + n1 * l2[None, :, None]
    ang2 = 2.0 * np.pi * ((kfull * l2[None, None, :]) % l) / l
    sc = 1.0 / math.sqrt(l * gd)
    a_tab = np.concatenate([np.cos(ang2), np.sin(ang2)], axis=2) * sc
    kc = np.arange(lc)
    angc = 2.0 * np.pi * ((kc[:, None] * kc[None, :]) % lc) / lc
    mc = np.concatenate([np.cos(angc), np.sin(angc)], axis=1) / math.sqrt(lc * gd)
    as_bf16 = lambda v: jnp.asarray(v, F32).astype(BF16)
    return as_bf16(cs), as_bf16(m1), as_bf16(a_tab), as_bf16(mc)


def _fourier_seq(z, l, m1, a_tab, mc):
    t, c = z.shape[1], z.shape[2]
    lc = t - l
    n1 = l // DFT_MINOR
    zs = z[:, :l].reshape(2 * n1, DFT_MINOR * c)
    g = _mm_lhs_resident(m1, zs, 8192)
    g4 = g.reshape(2, n1, DFT_MINOR, c)
    fp = _dft_minor(a_tab, g4)
    f_lat = jnp.swapaxes(fp, 0, 1).reshape(l, c)
    zc = z[:, l:].reshape(2 * lc, c)
    f_ctx = _mm_lhs_resident(mc, zc, c)
    return jnp.concatenate([f_lat, f_ctx], axis=0)


def _even_out_kernel(f_ref, bg_ref, u_ref, up_ref, un_ref, cw_ref, w_ref, x_ref, mod_ref, o_ref,
                     *, n_lat_tiles, n_tiles):
    i = pl.program_id(0)
    tm = u_ref.shape[0]
    u = u_ref[...].astype(F32)
    row = lax.broadcasted_iota(I32, u.shape, 0)
    first = jnp.logical_or(i == 0, i == n_lat_tiles)
    last = jnp.logical_or(i == n_lat_tiles - 1, i == n_tiles - 1)
    hb = up_ref.shape[0]
    halo_p = up_ref[...].astype(F32)[hb - 1:hb, :] * jnp.where(first, 0.0, 1.0)
    halo_n = un_ref[...].astype(F32)[0:1, :] * jnp.where(last, 0.0, 1.0)
    u_prev = jnp.where(row == 0, halo_p, pltpu.roll(u, 1, axis=0))
    u_next = jnp.where(row == tm - 1, halo_n, pltpu.roll(u, tm - 1, axis=0))
    cw = cw_ref[...]
    y = bg_ref[...].astype(F32) * (cw[0:1, :] * u_prev + cw[1:2, :] * u + cw[2:3, :] * u_next)
    acc = jnp.dot(f_ref[...], w_ref[0:FOURIER_WIDTH, :], preferred_element_type=F32)
    acc += jnp.dot(y.astype(BF16), w_ref[FOURIER_WIDTH:, :], preferred_element_type=F32)
    o_ref[...] = x_ref[...] + mod_ref[2:3, :] * acc


def _even_out(f, bg, u, conv_w, w_out, x, mod, n_lat_tiles):
    t, d = x.shape
    tm = ROW_TILE
    hb = 16
    n_tiles = t // tm
    r = tm // hb
    return pl.pallas_call(
        functools.partial(_even_out_kernel, n_lat_tiles=n_lat_tiles, n_tiles=n_tiles),
        grid=(n_tiles,),
        in_specs=[
            pl.BlockSpec((tm, FOURIER_WIDTH), lambda i: (i, 0)),
            pl.BlockSpec((tm, CONV_WIDTH), lambda i: (i, 0)),
            pl.BlockSpec((tm, CONV_WIDTH), lambda i: (i, 0)),
            pl.BlockSpec((hb, CONV_WIDTH), lambda i: (jnp.maximum(i * r - 1, 0), 0)),
            pl.BlockSpec((hb, CONV_WIDTH), lambda i: (jnp.minimum((i + 1) * r, t // hb - 1), 0)),
            pl.BlockSpec(conv_w.shape, lambda i: (0, 0)),
            pl.BlockSpec(w_out.shape, lambda i: (0, 0)),
            pl.BlockSpec((tm, d), lambda i: (i, 0)),
            pl.BlockSpec((None, 6, d), _stream_of(n_lat_tiles)),
        ],
        out_specs=pl.BlockSpec((tm, d), lambda i: (i, 0)),
        out_shape=jax.ShapeDtypeStruct((t, d), F32),
        compiler_params=_cparams(("arbitrary",)),
        name="even_out_proj",
    )(f, bg, u, u, u, conv_w, w_out, x, mod)


def _seg_rms_scale(v, seg, seg_t):
    hi, lo = _split_bf16(v * v)
    ss = jnp.dot(hi, seg, preferred_element_type=F32) + jnp.dot(lo, seg, preferred_element_type=F32)
    inv = lax.rsqrt(ss * (1.0 / HEAD_DIM) + EPS)
    ihi, ilo = _split_bf16(inv)
    return (jnp.dot(ihi, seg_t, preferred_element_type=F32)
            + jnp.dot(ilo, seg_t, preferred_element_type=F32))


def _rope_cols(v, cos, sa, sb, scale):
    cols = []
    for j in range(v.shape[1] // LANES):
        c = v[:, j * LANES:(j + 1) * LANES]
        r = c * cos + pltpu.roll(c, LANES - ROPE_PAIRS, axis=1) * sa + pltpu.roll(c, ROPE_PAIRS, axis=1) * sb
        cols.append(r * scale if scale != 1.0 else r)
    return jnp.concatenate(cols, axis=1)


def _odd_in_kernel(x_ref, g_ref, mod_ref, w_ref, qg_ref, kg_ref, segq_ref, segqt_ref, segk_ref, segkt_ref,
                   rope_ref, q_ref, k_ref, v_ref):
    h = _norm_mod(x_ref[...], g_ref[...], mod_ref[0:1, :], mod_ref[1:2, :])
    p = jnp.dot(h.astype(BF16), w_ref[...], preferred_element_type=F32)
    qd = N_HEADS * HEAD_DIM
    kd = 2 * N_KV_HEADS * HEAD_DIM
    cos, sa, sb = rope_ref[0], rope_ref[1], rope_ref[2]
    q = p[:, :qd]
    q = q * _seg_rms_scale(q, segq_ref[...], segqt_ref[...]) * qg_ref[...]
    q_ref[...] = _rope_cols(q, cos, sa, sb, HEAD_DIM ** -0.5 * LOG2E).astype(BF16)
    k = p[:, qd:qd + kd]
    k = k * _seg_rms_scale(k, segk_ref[...], segkt_ref[...]) * kg_ref[...]
    k_ref[...] = _rope_cols(k, cos, sa, sb, 1.0).astype(BF16)
    v_ref[...] = p[:, qd + kd:].astype(BF16)


def _odd_in(x, g, mod, w_qkv, qg, kg, segs, rope, n_lat_tiles, n_tiles):
    t, d = x.shape
    n = w_qkv.shape[1]
    tm = ROW_TILE
    rows = n_tiles * tm
    qd = N_HEADS * HEAD_DIM
    kd = 2 * N_KV_HEADS * HEAD_DIM
    segq, segqt, segk, segkt = segs
    const = lambda a: pl.BlockSpec(a.shape, lambda i: (0,) * a.ndim)
    return pl.pallas_call(
        _odd_in_kernel,
        grid=(n_tiles,),
        in_specs=[
            pl.BlockSpec((tm, d), lambda i: (i, 0)),
            const(g),
            pl.BlockSpec((None, 6, d), _stream_of(n_lat_tiles)),
            const(w_qkv), const(qg), const(kg), const(segq), const(segqt), const(segk), const(segkt),
            pl.BlockSpec((3, tm, LANES), lambda i: (0, i, 0)),
        ],
        out_specs=[
            pl.BlockSpec((tm, qd), lambda i: (i, 0)),
            pl.BlockSpec((tm, kd), lambda i: (i, 0)),
            pl.BlockSpec((tm, kd), lambda i: (i, 0)),
        ],
        out_shape=[
            jax.ShapeDtypeStruct((rows, qd), BF16),
            jax.ShapeDtypeStruct((rows, kd), BF16),
            jax.ShapeDtypeStruct((rows, kd), BF16),
        ],
        compiler_params=_cparams(("arbitrary",)),
        name="odd_in_proj",
    )(x, g, mod, w_qkv, qg, kg, segq, segqt, segk, segkt, rope)


def _rope_tables(l, t):
    pos = np.arange(l)
    freqs = ROPE_BASE ** (-np.arange(ROPE_PAIRS, dtype=np.float32) / ROPE_PAIRS)
    lane = np.arange(LANES) % HEAD_DIM
    axis = lane // (2 * ROPE_PAIRS)
    half = (lane % (2 * ROPE_PAIRS)) // ROPE_PAIRS
    pair = lane % ROPE_PAIRS
    p = np.where(axis[None, :] == 0, (pos // GRID_W)[:, None], (pos % GRID_W)[:, None]).astype(np.float32)
    ang = p * freqs[pair][None, :].astype(np.float32)
    cos, sin = np.cos(ang), np.sin(ang)
    sa = np.where(half[None, :] == 0, -sin, 0.0)
    sb = np.where(half[None, :] == 1, sin, 0.0)
    tab = np.zeros((3, t, LANES), np.float32)
    tab[0, :l], tab[1, :l], tab[2, :l] = cos, sa, sb
    tab[0, l:] = 1.0
    return jnp.asarray(tab)


def _segment_matrices():
    def seg(width):
        m = np.zeros((width, LANES), np.float32)
        m[np.arange(width), np.arange(width) // HEAD_DIM] = 1.0
        return m
    sq, sk = seg(N_HEADS * HEAD_DIM), seg(2 * N_KV_HEADS * HEAD_DIM)
    b = lambda v: jnp.asarray(v).astype(BF16)
    return b(sq), b(sq.T), b(sk), b(sk.T)


def _attn_kernel(sink_ref, q_ref, kp_ref, kc_ref, kn_ref, vp_ref, vc_ref, vn_ref, kx_ref, vx_ref, bias_ref, o_ref):
    bq = q_ref.shape[0]
    lane = lax.broadcasted_iota(I32, (bq, LANES), 1)
    low = lane < HEAD_DIM
    bias = bias_ref[...]
    dn = (((1,), (1,)), ((), ()))
    for g in range(N_KV_HEADS):
        ks = slice(g * LANES, (g + 1) * LANES)
        parts, sinks = [], []
        for j in range(GQA_GROUP):
            col = 2 * g + j // 2
            c = q_ref[:, col * LANES:(col + 1) * LANES]
            keep = low if j % 2 == 0 else jnp.logical_not(low)
            parts.append(jnp.where(keep, c, jnp.zeros_like(c)))
            sinks.append(jnp.full((bq, 1), sink_ref[g * GQA_GROUP + j], F32))
        qs = jnp.concatenate(parts, axis=0)
        sink = jnp.concatenate(sinks, axis=0)
        kwin = jnp.concatenate([kp_ref[:, ks], kc_ref[:, ks], kn_ref[:, ks]], axis=0)
        vwin = jnp.concatenate([vp_ref[:, ks], vc_ref[:, ks], vn_ref[:, ks]], axis=0)
        s_loc = lax.dot_general(qs, kwin, dn, preferred_element_type=F32) + bias
        s_ctx = lax.dot_general(qs, kx_ref[:, ks], dn, preferred_element_type=F32)
        m = jnp.maximum(jnp.maximum(jnp.max(s_loc, axis=1, keepdims=True),
                                    jnp.max(s_ctx, axis=1, keepdims=True)), sink)
        p_loc = jnp.exp2(s_loc - m)
        p_ctx = jnp.exp2(s_ctx - m)
        den = (jnp.sum(p_loc, axis=1, keepdims=True) + jnp.sum(p_ctx, axis=1, keepdims=True)
               + jnp.exp2(sink - m))
        o = jnp.dot(p_loc.astype(BF16), vwin, preferred_element_type=F32)
        o += jnp.dot(p_ctx.astype(BF16), vx_ref[:, ks], preferred_element_type=F32)
        o = o * (1.0 / den)
        c0 = jnp.where(low, o[0:bq], o[bq:2 * bq])
        c1 = jnp.where(low, o[2 * bq:3 * bq], o[3 * bq:4 * bq])
        o_ref[:, 2 * g * LANES:(2 * g + 1) * LANES] = c0.astype(o_ref.dtype)
        o_ref[:, (2 * g + 1) * LANES:(2 * g + 2) * LANES] = c1.astype(o_ref.dtype)


def _attn_bias(l):
    bq = ATT_BLOCK
    r = np.arange(GQA_GROUP * bq)[:, None] % bq
    col = np.arange(3 * bq)[None, :]
    band = np.abs(col - bq - r) <= WINDOW
    no_prev, no_next = col >= bq, col < 2 * bq
    masks = [band, band & no_prev, band & no_next, band & no_prev & no_next, np.zeros_like(band)]
    return jnp.asarray(np.where(np.stack(masks), 0.0, NEG_INF).astype(np.float32))


def _attention(q, k, v, sink, bias, l, n_q_blocks):
    t = q.shape[0]
    bq = ATT_BLOCK
    nlb = l // bq
    lc = t - l
    ctx_blk = l // lc
    kw = k.shape[1]
    kspec = lambda f: pl.BlockSpec((bq, kw), f)
    prev = lambda b, s: (jnp.clip(b - 1, 0, nlb - 1), 0)
    cur = lambda b, s: (jnp.minimum(b, nlb - 1), 0)
    nxt = lambda b, s: (jnp.clip(b + 1, 0, nlb - 1), 0)
    ctx = pl.BlockSpec((lc, kw), lambda b, s: (ctx_blk, 0))
    kind = lambda b, s: (jnp.where(b >= nlb, 4, (b == 0).astype(I32) + 2 * (b == nlb - 1).astype(I32)), 0, 0)
    return pl.pallas_call(
        _attn_kernel,
        grid_spec=pltpu.PrefetchScalarGridSpec(
            num_scalar_prefetch=1,
            grid=(n_q_blocks,),
            in_specs=[
                pl.BlockSpec((bq, q.shape[1]), lambda b, s: (b, 0)),
                kspec(prev), kspec(cur), kspec(nxt), kspec(prev), kspec(cur), kspec(nxt), ctx, ctx,
                pl.BlockSpec((None,) + bias.shape[1:], kind),
            ],
            out_specs=pl.BlockSpec((bq, q.shape[1]), lambda b, s: (b, 0)),
        ),
        out_shape=jax.ShapeDtypeStruct((n_q_blocks * bq, q.shape[1]), BF16),
        compiler_params=_cparams(("arbitrary",)),
        name="window_attention",
    )(sink, q, k, k, k, v, v, v, k, v, bias)


def _odd_out_kernel(a_ref, w_ref, x_ref, mod_ref, o_ref):
    acc = jnp.dot(a_ref[...], w_ref[...], preferred_element_type=F32)
    o_ref[...] = x_ref[...] + mod_ref[2:3, :] * acc


def _odd_out(a, w_o, x, mod, n_lat_tiles, n_tiles):
    d = x.shape[1]
    tm = ROW_TILE
    return pl.pallas_call(
        _odd_out_kernel,
        grid=(n_tiles,),
        in_specs=[
            pl.BlockSpec((tm, a.shape[1]), lambda i: (i, 0)),
            pl.BlockSpec(w_o.shape, lambda i: (0, 0)),
            pl.BlockSpec((tm, d), lambda i: (i, 0)),
            pl.BlockSpec((None, 6, d), _stream_of(n_lat_tiles)),
        ],
        out_specs=pl.BlockSpec((tm, d), lambda i: (i, 0)),
        out_shape=jax.ShapeDtypeStruct((n_tiles * tm, d), F32),
        compiler_params=_cparams(("arbitrary",)),
        name="odd_out_proj",
    )(a, w_o, x, mod)


ROUTER_EXPERT_ROW0 = 8


def _router_kernel(x_ref, g_ref, mod_ref, wr_ref, br_ref, tri_ref, h_ref, ri_ref, rg_ref, cnt_ref, carry_ref):
    i = pl.program_id(0)

    @pl.when(i == 0)
    def _():
        carry_ref[...] = jnp.zeros_like(carry_ref)

    h = _norm_mod(x_ref[...], g_ref[...], mod_ref[3:4, :], mod_ref[4:5, :])
    h_ref[...] = h
    tm = h.shape[0]
    logits = lax.dot_general(wr_ref[...], h, (((1,), (1,)), ((), ())), precision=_HI,
                             preferred_element_type=F32) + br_ref[:, 0:1]
    gl = logits[0:N_GROUPS]
    gmax = jnp.max(gl, axis=0, keepdims=True)
    gi = lax.broadcasted_iota(I32, gl.shape, 0)
    g_idx = jnp.min(jnp.where(gl == gmax, gi, N_GROUPS), axis=0, keepdims=True)
    g_val = 1.0 / jnp.sum(jnp.exp(gl - gmax), axis=0, keepdims=True)
    e_in = logits[ROUTER_EXPERT_ROW0:ROUTER_EXPERT_ROW0 + EXPERTS_PER_GROUP]
    for grp in range(1, N_GROUPS):
        lo = ROUTER_EXPERT_ROW0 + grp * EXPERTS_PER_GROUP
        e_in = jnp.where(g_idx == grp, logits[lo:lo + EXPERTS_PER_GROUP], e_in)
    ei = lax.broadcasted_iota(I32, e_in.shape, 0)
    v1 = jnp.max(e_in, axis=0, keepdims=True)
    i1 = jnp.min(jnp.where(e_in == v1, ei, EXPERTS_PER_GROUP), axis=0, keepdims=True)
    rest = jnp.where(ei == i1, -jnp.inf, e_in)
    v2 = jnp.max(rest, axis=0, keepdims=True)
    i2 = jnp.min(jnp.where(rest == v2, ei, EXPERTS_PER_GROUP), axis=0, keepdims=True)
    w2 = jnp.exp(v2 - v1)
    gate1 = g_val / (1.0 + w2)
    gate2 = g_val * w2 / (1.0 + w2)
    e1 = g_idx * EXPERTS_PER_GROUP + i1
    e2 = g_idx * EXPERTS_PER_GROUP + i2
    xi = lax.broadcasted_iota(I32, (N_EXPERTS, tm), 0)
    oh1 = xi == e1
    oh2 = xi == e2
    oh = oh1.astype(F32) + oh2.astype(F32)
    before = jnp.dot(oh.astype(BF16), tri_ref[...], preferred_element_type=F32) + carry_ref[:, 0:1]
    rank1 = jnp.sum(jnp.where(oh1, before, 0.0), axis=0, keepdims=True)
    rank2 = jnp.sum(jnp.where(oh2, before, 0.0), axis=0, keepdims=True)
    carry_ref[...] = carry_ref[...] + jnp.sum(oh, axis=1, keepdims=True)
    cnt_ref[...] = carry_ref[...]
    orow = lax.broadcasted_iota(I32, (8, tm), 0)
    ri_ref[...] = jnp.where(orow == 0, e1, jnp.where(orow == 1, e2, jnp.where(
        orow == 2, rank1.astype(I32), jnp.where(orow == 3, rank2.astype(I32), 0))))
    rg_ref[...] = jnp.where(orow == 0, gate1, jnp.where(orow == 1, gate2, 0.0))


def _router(x, g, mod, wr, br, tri, n_lat_tiles, n_tiles):
    d = x.shape[1]
    tm = ROW_TILE
    rows = n_tiles * tm
    const = lambda a: pl.BlockSpec(a.shape, lambda i: (0,) * a.ndim)
    return pl.pallas_call(
        _router_kernel,
        grid=(n_tiles,),
        in_specs=[
            pl.BlockSpec((tm, d), lambda i: (i, 0)),
            const(g),
            pl.BlockSpec((None, 6, d), _stream_of(n_lat_tiles)),
            const(wr), const(br), const(tri),
        ],
        out_specs=[
            pl.BlockSpec((tm, d), lambda i: (i, 0)),
            pl.BlockSpec((8, tm), lambda i: (0, i)),
            pl.BlockSpec((8, tm), lambda i: (0, i)),
            pl.BlockSpec((N_EXPERTS, LANES), lambda i: (0, 0)),
        ],
        out_shape=[
            jax.ShapeDtypeStruct((rows, d), F32),
            jax.ShapeDtypeStruct((8, rows), I32),
            jax.ShapeDtypeStruct((8, rows), F32),
            jax.ShapeDtypeStruct((N_EXPERTS, LANES), F32),
        ],
        scratch_shapes=[pltpu.VMEM((N_EXPERTS, LANES), F32)],
        compiler_params=_cparams(("arbitrary",)),
        name="moe_router",
    )(x, g, mod, wr, br, tri)


def _dispatch_kernel(dest_ref, h_ref, xb_ref, sem):
    tm = h_ref.shape[0]

    def row_copy(k, tok):
        return pltpu.make_async_copy(h_ref.at[pl.ds(tok, 1), :], xb_ref.at[pl.ds(dest_ref[k, tok], 1), :], sem)

    def start(tok, c):
        row_copy(0, tok).start()
        row_copy(1, tok).start()
        return c

    def wait(tok, c):
        row_copy(0, tok).wait()
        row_copy(1, tok).wait()
        return c

    lax.fori_loop(0, tm, start, 0)
    lax.fori_loop(0, tm, wait, 0)


def _dispatch(h, dest, n_rows):
    t, d = h.shape
    tm = ROW_TILE
    return pl.pallas_call(
        _dispatch_kernel,
        grid=(t // tm,),
        in_specs=[
            pl.BlockSpec((None, 2, tm), lambda i: (i, 0, 0), memory_space=pltpu.SMEM),
            pl.BlockSpec((tm, d), lambda i: (i, 0)),
        ],
        out_specs=pl.BlockSpec(memory_space=pl.ANY),
        out_shape=jax.ShapeDtypeStruct((n_rows, d), F32),
        scratch_shapes=[pltpu.SemaphoreType.DMA(())],
        compiler_params=_cparams(("arbitrary",)),
        name="moe_dispatch",
    )(dest, h)


def _ffn_kernel(be_ref, nv_ref, xb_ref, w1_ref, w3_ref, w2_ref, yb_ref, w1s, w3s, w2s):
    i = pl.program_id(0)
    changed = jnp.logical_or(i == 0, be_ref[i] != be_ref[jnp.maximum(i - 1, 0)])

    @pl.when(changed)
    def _():
        w1s[...] = w1_ref[...].astype(BF16)
        w3s[...] = w3_ref[...].astype(BF16)
        w2s[...] = w2_ref[...].astype(BF16)

    nv = nv_ref[i]

    @pl.when(nv > 0)
    def _():
        x = xb_ref[...]
        row = lax.broadcasted_iota(I32, x.shape, 0)
        xb = jnp.where(row < nv, x, 0.0).astype(BF16)
        a = jnp.dot(xb, w1s[...], preferred_element_type=F32)
        b = jnp.dot(xb, w3s[...], preferred_element_type=F32)
        hid = (a * jax.nn.sigmoid(a) * b).astype(BF16)
        yb_ref[...] = jnp.dot(hid, w2s[...], preferred_element_type=F32)

    @pl.when(nv <= 0)
    def _():
        yb_ref[...] = jnp.zeros_like(yb_ref)


def _expert_ffn(xb, block_exp, n_valid, w1, w3, w2):
    r, d = xb.shape
    f = w1.shape[2]
    bm = MOE_ROWS
    return pl.pallas_call(
        _ffn_kernel,
        grid_spec=pltpu.PrefetchScalarGridSpec(
            num_scalar_prefetch=2,
            grid=(r // bm,),
            in_specs=[
                pl.BlockSpec((bm, d), lambda i, be, nv: (i, 0)),
                pl.BlockSpec((None, d, f), lambda i, be, nv: (be[i], 0, 0)),
                pl.BlockSpec((None, d, f), lambda i, be, nv: (be[i], 0, 0)),
                pl.BlockSpec((None, f, d), lambda i, be, nv: (be[i], 0, 0)),
            ],
            out_specs=pl.BlockSpec((bm, d), lambda i, be, nv: (i, 0)),
            scratch_shapes=[pltpu.VMEM((d, f), BF16), pltpu.VMEM((d, f), BF16), pltpu.VMEM((f, d), BF16)],
        ),
        out_shape=jax.ShapeDtypeStruct((r, d), F32),
        compiler_params=_cparams(("arbitrary",)),
        name="moe_expert_mlp",
    )(block_exp, n_valid, xb, w1, w3, w2)


def _combine_kernel(dest_ref, yb_ref, gate_ref, x_ref, mod_ref, o_ref, buf, sem):
    tm = x_ref.shape[0]

    def row_copy(k, tok):
        return pltpu.make_async_copy(yb_ref.at[pl.ds(dest_ref[k, tok], 1), :], buf.at[k, pl.ds(tok, 1), :], sem)

    def start(tok, c):
        row_copy(0, tok).start()
        row_copy(1, tok).start()
        return c

    def wait(tok, c):
        row_copy(0, tok).wait()
        row_copy(1, tok).wait()
        return c

    lax.fori_loop(0, tm, start, 0)
    lax.fori_loop(0, tm, wait, 0)
    gate = gate_ref[...]
    y = gate[:, 0:1] * buf[0] + gate[:, 1:2] * buf[1]
    o_ref[...] = x_ref[...] + mod_ref[5:6, :] * y


def _combine(yb, dest, gates, x, mod, n_lat_tiles, n_tiles):
    d = x.shape[1]
    tm = ROW_TILE
    return pl.pallas_call(
        _combine_kernel,
        grid=(n_tiles,),
        in_specs=[
            pl.BlockSpec((None, 2, tm), lambda i: (i, 0, 0), memory_space=pltpu.SMEM),
            pl.BlockSpec(memory_space=pl.ANY),
            pl.BlockSpec((tm, 8), lambda i: (i, 0)),
            pl.BlockSpec((tm, d), lambda i: (i, 0)),
            pl.BlockSpec((None, 6, d), _stream_of(n_lat_tiles)),
        ],
        out_specs=pl.BlockSpec((tm, d), lambda i: (i, 0)),
        out_shape=jax.ShapeDtypeStruct((n_tiles * tm, d), F32),
        scratch_shapes=[pltpu.VMEM((2, tm, d), F32), pltpu.SemaphoreType.DMA(())],
        compiler_params=_cparams(("arbitrary",)),
        name="moe_combine",
    )(dest, yb, gates, x, mod)


def _moe_layer(x, g, mod, wr, br, tri, w1, w3, w2, n_lat_tiles, n_tiles):
    tm = ROW_TILE
    rows = n_tiles * tm
    bm = MOE_ROWS
    h, ri, rg, cnt = _router(x, g, mod, wr, br, tri, n_lat_tiles, n_tiles)
    counts = cnt[:, 0].astype(I32)
    padded = (counts + bm - 1) // bm * bm
    pad_end = jnp.cumsum(padded)
    pad_start = pad_end - padded
    n_blocks = -(-(rows * 2) // bm) + N_EXPERTS
    ex = jnp.arange(N_EXPERTS, dtype=I32)
    start_of = lambda e: jnp.sum(jnp.where(e[:, None] == ex[None, :], pad_start[None, :], 0), axis=1)
    dest = jnp.stack([start_of(ri[0]) + ri[2], start_of(ri[1]) + ri[3]], axis=0)
    dest_t = dest.reshape(2, n_tiles, tm).transpose(1, 0, 2)
    blk0 = jnp.arange(n_blocks, dtype=I32) * bm
    block_exp = jnp.minimum(jnp.sum((pad_end[None, :] <= blk0[:, None]).astype(I32), axis=1), N_EXPERTS - 1)
    sel = block_exp[:, None] == ex[None, :]
    cnt_b = jnp.sum(jnp.where(sel, counts[None, :], 0), axis=1)
    st_b = jnp.sum(jnp.where(sel, pad_start[None, :], 0), axis=1)
    n_valid = jnp.clip(cnt_b - (blk0 - st_b), 0, bm).astype(I32)
    xb = _dispatch(h, dest_t, n_blocks * bm)
    yb = _expert_ffn(xb, block_exp.astype(I32), n_valid, w1, w3, w2)
    return _combine(yb, dest_t, rg[0:8].T, x, mod, n_lat_tiles, n_tiles)


def _router_matrix(w_rg, b_rg, w_re, b_re):
    d = w_rg.shape[0]
    wr = jnp.zeros((LANES, d), F32)
    wr = wr.at[0:N_GROUPS].set(w_rg.T.astype(F32))
    wr = wr.at[ROUTER_EXPERT_ROW0:ROUTER_EXPERT_ROW0 + N_EXPERTS].set(w_re.T.astype(F32))
    br = jnp.zeros((LANES,), F32)
    br = br.at[0:N_GROUPS].set(b_rg.astype(F32))
    br = br.at[ROUTER_EXPERT_ROW0:ROUTER_EXPERT_ROW0 + N_EXPERTS].set(b_re.astype(F32))
    return wr, jnp.broadcast_to(br[:, None], (LANES, LANES))


def _dup_heads(w):
    d = w.shape[0]
    w4 = w.reshape(d, N_KV_HEADS, 1, HEAD_DIM)
    return jnp.broadcast_to(w4, (d, N_KV_HEADS, 2, HEAD_DIM)).reshape(d, 2 * N_KV_HEADS * HEAD_DIM)


def kernel(x, c, ctx, c_ctx, w_mod, b_mod, norm_mix_g, norm_ffn_g, w_in_even, conv_w, w_out_even, w_qkv, q_norm_g,
           k_norm_g, sink_logit, w_o, w_router_g, b_router_g, w_router_e, b_router_e, w1, w3, w2):
    bsz, l, d = x.shape
    lc = ctx.shape[1]
    assert bsz == 1, "one sample per call"
    tm = ROW_TILE
    assert l % tm == 0 and lc % tm == 0 and l % lc == 0 and l % (DFT_MINOR * 8) == 0
    depth = w_mod.shape[0]
    t = l + lc
    nl, nt = l // tm, t // tm

    xs = jnp.concatenate([x.reshape(l, d), ctx.reshape(lc, d)], axis=0)
    mod_all = _modulation(c, c_ctx, w_mod, b_mod).reshape(depth, 2, 6, d)
    cs, m1, a_tab, mc = _dft_tables(l, lc)
    rope = _rope_tables(l, t)
    segs = _segment_matrices()
    attn_bias = _attn_bias(l)
    tri = jnp.asarray(np.triu(np.ones((tm, tm), np.float32), 1)).astype(BF16)
    qd = N_HEADS * HEAD_DIM

    for layer in range(depth):
        last = layer == depth - 1
        j = layer // 2
        mod = mod_all[layer]
        g_mix = norm_mix_g[layer].reshape(1, d)
        g_ffn = norm_ffn_g[layer].reshape(1, d)
        if layer % 2 == 0:
            z, bg, u = _even_in(xs, g_mix, mod, w_in_even[j].astype(BF16), cs, nl)
            f = _fourier_seq(z, l, m1, a_tab, mc)
            xs = _even_out(f, bg, u, conv_w[j], w_out_even[j].astype(BF16), xs, mod, nl)
        else:
            wq = w_qkv[j]
            w_all = jnp.concatenate([wq[:, :qd], _dup_heads(wq[:, qd:qd + N_KV_HEADS * HEAD_DIM]),
                                     _dup_heads(wq[:, qd + N_KV_HEADS * HEAD_DIM:])], axis=1).astype(BF16)
            qg = jnp.tile(q_norm_g[j], N_HEADS).reshape(1, qd)
            kg = jnp.tile(k_norm_g[j], 2 * N_KV_HEADS).reshape(1, 2 * N_KV_HEADS * HEAD_DIM)
            q, k, v = _odd_in(xs, g_mix, mod, w_all, qg, kg, segs, rope, nl, nt)
            n_out = nl if last else nt
            att = _attention(q, k, v, sink_logit[j].astype(F32) * LOG2E, attn_bias, l,
                             n_out * (tm // ATT_BLOCK))
            xs = _odd_out(att, w_o[j].astype(BF16), xs, mod, nl, n_out)
        wr, br = _router_matrix(w_router_g[layer], b_router_g[layer], w_router_e[layer], b_router_e[layer])
        n_moe = nl if last else nt
        xs = _moe_layer(xs, g_ffn, mod, wr, br, tri, w1[layer], w3[layer], w2[layer], nl, n_moe)
    return xs[:l].reshape(bsz, l, d)
```

```python
import functools
import math

import numpy as np
import jax
import jax.numpy as jnp
from jax import lax
from jax.experimental import pallas as pl
from jax.experimental.pallas import tpu as pltpu

F32 = jnp.float32
BF16 = jnp.bfloat16
I32 = jnp.int32

EPS = 1e-6
NEG_INF = -1e30

GRID_W = 64
FOURIER_GROUPS = 4
FOURIER_GROUP_DIM = 128
FOURIER_WIDTH = FOURIER_GROUPS * FOURIER_GROUP_DIM
CONV_WIDTH = 512
N_HEADS = 16
N_KV_HEADS = 4
GQA_GROUP = N_HEADS // N_KV_HEADS
HEAD_DIM = 64
WINDOW = 128
ROPE_BASE = 10000.0
ROPE_PAIRS = HEAD_DIM // 4
N_GROUPS = 4
EXPERTS_PER_GROUP = 8
N_EXPERTS = N_GROUPS * EXPERTS_PER_GROUP

LANES = 128
ROW_TILE = 256
ATT_BLOCK = 128
MOE_ROWS = 256
DFT_MINOR = 128
VMEM_LIMIT = 48 * 1024 * 1024

_HI = lax.Precision.HIGHEST
LOG2E = math.log2(math.e)


def _cparams(sem):
    return pltpu.CompilerParams(dimension_semantics=sem, vmem_limit_bytes=VMEM_LIMIT)


def _split_bf16(x):
    hi = x.astype(BF16)
    lo = (x - hi.astype(F32)).astype(BF16)
    return hi, lo


def _mod_kernel(ct_ref, w_ref, b_ref, o_ref):
    ct = ct_ref[...]
    s = ct * jax.nn.sigmoid(ct)
    w = w_ref[...]
    r0 = jnp.sum(w * s[:, 0:1], axis=0, keepdims=True)
    r1 = jnp.sum(w * s[:, 1:2], axis=0, keepdims=True)
    o_ref[...] = jnp.concatenate([r0, r1], axis=0) + b_ref[...]


def _modulation(c, c_ctx, w_mod, b_mod):
    depth, d, n = w_mod.shape
    tn = 512
    ct = jnp.stack([c.reshape(d), c_ctx.reshape(d)], axis=1)
    return pl.pallas_call(
        _mod_kernel,
        grid=(depth, n // tn),
        in_specs=[
            pl.BlockSpec((d, 2), lambda l, j: (0, 0)),
            pl.BlockSpec((None, d, tn), lambda l, j: (l, 0, j)),
            pl.BlockSpec((None, 1, tn), lambda l, j: (l, 0, j)),
        ],
        out_specs=pl.BlockSpec((None, 2, tn), lambda l, j: (l, 0, j)),
        out_shape=jax.ShapeDtypeStruct((depth, 2, n), F32),
        compiler_params=_cparams(("arbitrary", "arbitrary")),
        name="modulation",
    )(ct, w_mod, b_mod.reshape(depth, 1, n))


def _norm_mod(x, g, shift, scale):
    ms = jnp.mean(x * x, axis=-1, keepdims=True)
    y = x * lax.rsqrt(ms + EPS) * g
    return y * (1.0 + scale) + shift


def _stream_of(n_lat_tiles):
    return lambda i: (jnp.where(i >= n_lat_tiles, 1, 0), 0, 0)


def _even_in_kernel(x_ref, g_ref, mod_ref, w_ref, cs_ref, z_ref, bg_ref, u_ref):
    h = _norm_mod(x_ref[...], g_ref[...], mod_ref[0:1, :], mod_ref[1:2, :])
    p = jnp.dot(h.astype(BF16), w_ref[...], preferred_element_type=F32)
    cs = cs_ref[...]
    for grp in range(FOURIER_GROUPS):
        lo = grp * FOURIER_GROUP_DIM
        a = p[:, lo:lo + FOURIER_GROUP_DIM].astype(BF16)
        z = jnp.dot(a, cs, preferred_element_type=F32)
        z_ref[0, :, lo:lo + FOURIER_GROUP_DIM] = z[:, :FOURIER_GROUP_DIM].astype(BF16)
        z_ref[1, :, lo:lo + FOURIER_GROUP_DIM] = z[:, FOURIER_GROUP_DIM:].astype(BF16)
    o = FOURIER_WIDTH
    bg_ref[...] = p[:, o:o + CONV_WIDTH].astype(BF16)
    u_ref[...] = (p[:, o + CONV_WIDTH:o + 2 * CONV_WIDTH] * p[:, o + 2 * CONV_WIDTH:]).astype(BF16)


def _even_in(x, g, mod, w_in, cs, n_lat_tiles):
    t, d = x.shape
    n = w_in.shape[1]
    tm = ROW_TILE
    return pl.pallas_call(
        _even_in_kernel,
        grid=(t // tm,),
        in_specs=[
            pl.BlockSpec((tm, d), lambda i: (i, 0)),
            pl.BlockSpec((1, d), lambda i: (0, 0)),
            pl.BlockSpec((None, 6, d), _stream_of(n_lat_tiles)),
            pl.BlockSpec((d, n), lambda i: (0, 0)),
            pl.BlockSpec(cs.shape, lambda i: (0, 0)),
        ],
        out_specs=[
            pl.BlockSpec((2, tm, FOURIER_WIDTH), lambda i: (0, i, 0)),
            pl.BlockSpec((tm, CONV_WIDTH), lambda i: (i, 0)),
            pl.BlockSpec((tm, CONV_WIDTH), lambda i: (i, 0)),
        ],
        out_shape=[
            jax.ShapeDtypeStruct((2, t, FOURIER_WIDTH), BF16),
            jax.ShapeDtypeStruct((t, CONV_WIDTH), BF16),
            jax.ShapeDtypeStruct((t, CONV_WIDTH), BF16),
        ],
        compiler_params=_cparams(("arbitrary",)),
        name="even_in_proj",
    )(x, g, mod, w_in, cs)


def _mm_kernel(a_ref, b_ref, o_ref):
    o_ref[...] = jnp.dot(a_ref[...], b_ref[...], preferred_element_type=F32).astype(o_ref.dtype)


def _mm_lhs_resident(a, b, tn):
    m, k = a.shape
    n = b.shape[1]
    tn = min(tn, n)
    return pl.pallas_call(
        _mm_kernel,
        grid=(n // tn,),
        in_specs=[pl.BlockSpec((m, k), lambda j: (0, 0)), pl.BlockSpec((k, tn), lambda j: (0, j))],
        out_specs=pl.BlockSpec((m, tn), lambda j: (0, j)),
        out_shape=jax.ShapeDtypeStruct((m, n), BF16),
        compiler_params=_cparams(("arbitrary",)),
        name="dft_major",
    )(a, b)


def _dft_minor_kernel(a_ref, g_ref, o_ref, *, kb):
    for j in range(kb):
        gcat = jnp.concatenate([g_ref[0, j], g_ref[1, j]], axis=0)
        o_ref[j] = jnp.dot(a_ref[j], gcat, preferred_element_type=F32).astype(o_ref.dtype)


def _dft_minor(a_tab, g4):
    n1, m, k2 = a_tab.shape
    c = g4.shape[-1]
    kb = 8
    return pl.pallas_call(
        functools.partial(_dft_minor_kernel, kb=kb),
        grid=(n1 // kb,),
        in_specs=[
            pl.BlockSpec((kb, m, k2), lambda i: (i, 0, 0)),
            pl.BlockSpec((2, kb, DFT_MINOR, c), lambda i: (0, i, 0, 0)),
        ],
        out_specs=pl.BlockSpec((kb, m, c), lambda i: (i, 0, 0)),
        out_shape=jax.ShapeDtypeStruct((n1, m, c), BF16),
        compiler_params=_cparams(("arbitrary",)),
        name="dft_minor",
    )(a_tab, g4)


def _dft_tables(l, lc):
    gd = FOURIER_GROUP_DIM
    kk = np.arange(gd)
    ang = 2.0 * np.pi * ((kk[:, None] * kk[None, :]) % gd) / gd
    cs = np.concatenate([np.cos(ang), -np.sin(ang)], axis=1)
    n1 = l // DFT_MINOR
    k1 = np.arange(n1)
    ang1 = 2.0 * np.pi * ((k1[:, None] * k1[None, :]) % n1) / n1
    c1, s1 = np.cos(ang1), np.sin(ang1)
    m1 = np.block([[c1, s1], [-s1, c1]])
    l2 = np.arange(DFT_MINOR)
    kfull = k1[:, None, None] + n1 * l2[None, :, None]
    ang2 = 2.0 * np.pi * ((kfull * l2[None, None, :]) % l) / l
    sc = 1.0 / math.sqrt(l * gd)
    a_tab = np.concatenate([np.cos(ang2), np.sin(ang2)], axis=2) * sc
    kc = np.arange(lc)
    angc = 2.0 * np.pi * ((kc[:, None] * kc[None, :]) % lc) / lc
    mc = np.concatenate([np.cos(angc), np.sin(angc)], axis=1) / math.sqrt(lc * gd)
    as_bf16 = lambda v: jnp.asarray(v, F32).astype(BF16)
    return as_bf16(cs), as_bf16(m1), as_bf16(a_tab), as_bf16(mc)


def _fourier_seq(z, l, m1, a_tab, mc):
    t, c = z.shape[1], z.shape[2]
    lc = t - l
    n1 = l // DFT_MINOR
    zs = z[:, :l].reshape(2 * n1, DFT_MINOR * c)
    g = _mm_lhs_resident(m1, zs, 8192)
    g4 = g.reshape(2, n1, DFT_MINOR, c)
    fp = _dft_minor(a_tab, g4)
    f_lat = jnp.swapaxes(fp, 0, 1).reshape(l, c)
    zc = z[:, l:].reshape(2 * lc, c)
    f_ctx = _mm_lhs_resident(mc, zc, c)
    return jnp.concatenate([f_lat, f_ctx], axis=0)


def _even_out_kernel(f_ref, bg_ref, u_ref, up_ref, un_ref, cw_ref, w_ref, x_ref, mod_ref, o_ref,
                     *, n_lat_tiles, n_tiles):
    i = pl.program_id(0)
    tm = u_ref.shape[0]
    u = u_ref[...].astype(F32)
    row = lax.broadcasted_iota(I32, u.shape, 0)
    first = jnp.logical_or(i == 0, i == n_lat_tiles)
    last = jnp.logical_or(i == n_lat_tiles - 1, i == n_tiles - 1)
    hb = up_ref.shape[0]
    halo_p = up_ref[...].astype(F32)[hb - 1:hb, :] * jnp.where(first, 0.0, 1.0)
    halo_n = un_ref[...].astype(F32)[0:1, :] * jnp.where(last, 0.0, 1.0)
    u_prev = jnp.where(row == 0, halo_p, pltpu.roll(u, 1, axis=0))
    u_next = jnp.where(row == tm - 1, halo_n, pltpu.roll(u, tm - 1, axis=0))
    cw = cw_ref[...]
    y = bg_ref[...].astype(F32) * (cw[0:1, :] * u_prev + cw[1:2, :] * u + cw[2:3, :] * u_next)
    acc = jnp.dot(f_ref[...], w_ref[0:FOURIER_WIDTH, :], preferred_element_type=F32)
    acc += jnp.dot(y.astype(BF16), w_ref[FOURIER_WIDTH:, :], preferred_element_type=F32)
    o_ref[...] = x_ref[...] + mod_ref[2:3, :] * acc


def _even_out(f, bg, u, conv_w, w_out, x, mod, n_lat_tiles):
    t, d = x.shape
    tm = ROW_TILE
    hb = 16
    n_tiles = t // tm
    r = tm // hb
    return pl.pallas_call(
        functools.partial(_even_out_kernel, n_lat_tiles=n_lat_tiles, n_tiles=n_tiles),
        grid=(n_tiles,),
        in_specs=[
            pl.BlockSpec((tm, FOURIER_WIDTH), lambda i: (i, 0)),
            pl.BlockSpec((tm, CONV_WIDTH), lambda i: (i, 0)),
            pl.BlockSpec((tm, CONV_WIDTH), lambda i: (i, 0)),
            pl.BlockSpec((hb, CONV_WIDTH), lambda i: (jnp.maximum(i * r - 1, 0), 0)),
            pl.BlockSpec((hb, CONV_WIDTH), lambda i: (jnp.minimum((i + 1) * r, t // hb - 1), 0)),
            pl.BlockSpec(conv_w.shape, lambda i: (0, 0)),
            pl.BlockSpec(w_out.shape, lambda i: (0, 0)),
            pl.BlockSpec((tm, d), lambda i: (i, 0)),
            pl.BlockSpec((None, 6, d), _stream_of(n_lat_tiles)),
        ],
        out_specs=pl.BlockSpec((tm, d), lambda i: (i, 0)),
        out_shape=jax.ShapeDtypeStruct((t, d), F32),
        compiler_params=_cparams(("arbitrary",)),
        name="even_out_proj",
    )(f, bg, u, u, u, conv_w, w_out, x, mod)


def _seg_rms_scale(v, seg, seg_t):
    hi, lo = _split_bf16(v * v)
    ss = jnp.dot(hi, seg, preferred_element_type=F32) + jnp.dot(lo, seg, preferred_element_type=F32)
    inv = lax.rsqrt(ss * (1.0 / HEAD_DIM) + EPS)
    ihi, ilo = _split_bf16(inv)
    return (jnp.dot(ihi, seg_t, preferred_element_type=F32)
            + jnp.dot(ilo, seg_t, preferred_element_type=F32))


def _rope_cols(v, cos, sa, sb, scale):
    cols = []
    for j in range(v.shape[1] // LANES):
        c = v[:, j * LANES:(j + 1) * LANES]
        r = c * cos + pltpu.roll(c, LANES - ROPE_PAIRS, axis=1) * sa + pltpu.roll(c, ROPE_PAIRS, axis=1) * sb
        cols.append(r * scale if scale != 1.0 else r)
    return jnp.concatenate(cols, axis=1)


def _odd_in_kernel(x_ref, g_ref, mod_ref, w_ref, qg_ref, kg_ref, segq_ref, segqt_ref, segk_ref, segkt_ref,
                   rope_ref, q_ref, k_ref, v_ref):
    h = _norm_mod(x_ref[...], g_ref[...], mod_ref[0:1, :], mod_ref[1:2, :])
    p = jnp.dot(h.astype(BF16), w_ref[...], preferred_element_type=F32)
    qd = N_HEADS * HEAD_DIM
    kd = 2 * N_KV_HEADS * HEAD_DIM
    cos, sa, sb = rope_ref[0], rope_ref[1], rope_ref[2]
    q = p[:, :qd]
    q = q * _seg_rms_scale(q, segq_ref[...], segqt_ref[...]) * qg_ref[...]
    q_ref[...] = _rope_cols(q, cos, sa, sb, HEAD_DIM ** -0.5 * LOG2E).astype(BF16)
    k = p[:, qd:qd + kd]
    k = k * _seg_rms_scale(k, segk_ref[...], segkt_ref[...]) * kg_ref[...]
    k_ref[...] = _rope_cols(k, cos, sa, sb, 1.0).astype(BF16)
    v_ref[...] = p[:, qd + kd:].astype(BF16)


def _odd_in(x, g, mod, w_qkv, qg, kg, segs, rope, n_lat_tiles, n_tiles):
    t, d = x.shape
    n = w_qkv.shape[1]
    tm = ROW_TILE
    rows = n_tiles * tm
    qd = N_HEADS * HEAD_DIM
    kd = 2 * N_KV_HEADS * HEAD_DIM
    segq, segqt, segk, segkt = segs
    const = lambda a: pl.BlockSpec(a.shape, lambda i: (0,) * a.ndim)
    return pl.pallas_call(
        _odd_in_kernel,
        grid=(n_tiles,),
        in_specs=[
            pl.BlockSpec((tm, d), lambda i: (i, 0)),
            const(g),
            pl.BlockSpec((None, 6, d), _stream_of(n_lat_tiles)),
            const(w_qkv), const(qg), const(kg), const(segq), const(segqt), const(segk), const(segkt),
            pl.BlockSpec((3, tm, LANES), lambda i: (0, i, 0)),
        ],
        out_specs=[
            pl.BlockSpec((tm, qd), lambda i: (i, 0)),
            pl.BlockSpec((tm, kd), lambda i: (i, 0)),
            pl.BlockSpec((tm, kd), lambda i: (i, 0)),
        ],
        out_shape=[
            jax.ShapeDtypeStruct((rows, qd), BF16),
            jax.ShapeDtypeStruct((rows, kd), BF16),
            jax.ShapeDtypeStruct((rows, kd), BF16),
        ],
        compiler_params=_cparams(("arbitrary",)),
        name="odd_in_proj",
    )(x, g, mod, w_qkv, qg, kg, segq, segqt, segk, segkt, rope)


def _rope_tables(l, t):
    pos = np.arange(l)
    freqs = ROPE_BASE ** (-np.arange(ROPE_PAIRS, dtype=np.float32) / ROPE_PAIRS)
    lane = np.arange(LANES) % HEAD_DIM
    axis = lane // (2 * ROPE_PAIRS)
    half = (lane % (2 * ROPE_PAIRS)) // ROPE_PAIRS
    pair = lane % ROPE_PAIRS
    p = np.where(axis[None, :] == 0, (pos // GRID_W)[:, None], (pos % GRID_W)[:, None]).astype(np.float32)
    ang = p * freqs[pair][None, :].astype(np.float32)
    cos, sin = np.cos(ang), np.sin(ang)
    sa = np.where(half[None, :] == 0, -sin, 0.0)
    sb = np.where(half[None, :] == 1, sin, 0.0)
    tab = np.zeros((3, t, LANES), np.float32)
    tab[0, :l], tab[1, :l], tab[2, :l] = cos, sa, sb
    tab[0, l:] = 1.0
    return jnp.asarray(tab)


def _segment_matrices():
    def seg(width):
        m = np.zeros((width, LANES), np.float32)
        m[np.arange(width), np.arange(width) // HEAD_DIM] = 1.0
        return m
    sq, sk = seg(N_HEADS * HEAD_DIM), seg(2 * N_KV_HEADS * HEAD_DIM)
    b = lambda v: jnp.asarray(v).astype(BF16)
    return b(sq), b(sq.T), b(sk), b(sk.T)


def _attn_kernel(sink_ref, q_ref, kp_ref, kc_ref, kn_ref, vp_ref, vc_ref, vn_ref, kx_ref, vx_ref, bias_ref, o_ref):
    bq = q_ref.shape[0]
    lane = lax.broadcasted_iota(I32, (bq, LANES), 1)
    low = lane < HEAD_DIM
    bias = bias_ref[...]
    dn = (((1,), (1,)), ((), ()))
    for g in range(N_KV_HEADS):
        ks = slice(g * LANES, (g + 1) * LANES)
        parts, sinks = [], []
        for j in range(GQA_GROUP):
            col = 2 * g + j // 2
            c = q_ref[:, col * LANES:(col + 1) * LANES]
            keep = low if j % 2 == 0 else jnp.logical_not(low)
            parts.append(jnp.where(keep, c, jnp.zeros_like(c)))
            sinks.append(jnp.full((bq, 1), sink_ref[g * GQA_GROUP + j], F32))
        qs = jnp.concatenate(parts, axis=0)
        sink = jnp.concatenate(sinks, axis=0)
        kwin = jnp.concatenate([kp_ref[:, ks], kc_ref[:, ks], kn_ref[:, ks]], axis=0)
        vwin = jnp.concatenate([vp_ref[:, ks], vc_ref[:, ks], vn_ref[:, ks]], axis=0)
        s_loc = lax.dot_general(qs, kwin, dn, preferred_element_type=F32) + bias
        s_ctx = lax.dot_general(qs, kx_ref[:, ks], dn, preferred_element_type=F32)
        m = jnp.maximum(jnp.maximum(jnp.max(s_loc, axis=1, keepdims=True),
                                    jnp.max(s_ctx, axis=1, keepdims=True)), sink)
        p_loc = jnp.exp2(s_loc - m)
        p_ctx = jnp.exp2(s_ctx - m)
        den = (jnp.sum(p_loc, axis=1, keepdims=True) + jnp.sum(p_ctx, axis=1, keepdims=True)
               + jnp.exp2(sink - m))
        o = jnp.dot(p_loc.astype(BF16), vwin, preferred_element_type=F32)
        o += jnp.dot(p_ctx.astype(BF16), vx_ref[:, ks], preferred_element_type=F32)
        o = o * (1.0 / den)
        c0 = jnp.where(low, o[0:bq], o[bq:2 * bq])
        c1 = jnp.where(low, o[2 * bq:3 * bq], o[3 * bq:4 * bq])
        o_ref[:, 2 * g * LANES:(2 * g + 1) * LANES] = c0.astype(o_ref.dtype)
        o_ref[:, (2 * g + 1) * LANES:(2 * g + 2) * LANES] = c1.astype(o_ref.dtype)


def _attn_bias(l):
    bq = ATT_BLOCK
    r = np.arange(GQA_GROUP * bq)[:, None] % bq
    col = np.arange(3 * bq)[None, :]
    band = np.abs(col - bq - r) <= WINDOW
    no_prev, no_next = col >= bq, col < 2 * bq
    masks = [band, band & no_prev, band & no_next, band & no_prev & no_next, np.zeros_like(band)]
    return jnp.asarray(np.where(np.stack(masks), 0.0, NEG_INF).astype(np.float32))


def _attention(q, k, v, sink, bias, l, n_q_blocks):
    t = q.shape[0]
    bq = ATT_BLOCK
    nlb = l // bq
    lc = t - l
    ctx_blk = l // lc
    kw = k.shape[1]
    kspec = lambda f: pl.BlockSpec((bq, kw), f)
    prev = lambda b, s: (jnp.clip(b - 1, 0, nlb - 1), 0)
    cur = lambda b, s: (jnp.minimum(b, nlb - 1), 0)
    nxt = lambda b, s: (jnp.clip(b + 1, 0, nlb - 1), 0)
    ctx = pl.BlockSpec((lc, kw), lambda b, s: (ctx_blk, 0))
    kind = lambda b, s: (jnp.where(b >= nlb, 4, (b == 0).astype(I32) + 2 * (b == nlb - 1).astype(I32)), 0, 0)
    return pl.pallas_call(
        _attn_kernel,
        grid_spec=pltpu.PrefetchScalarGridSpec(
            num_scalar_prefetch=1,
            grid=(n_q_blocks,),
            in_specs=[
                pl.BlockSpec((bq, q.shape[1]), lambda b, s: (b, 0)),
                kspec(prev), kspec(cur), kspec(nxt), kspec(prev), kspec(cur), kspec(nxt), ctx, ctx,
                pl.BlockSpec((None,) + bias.shape[1:], kind),
            ],
            out_specs=pl.BlockSpec((bq, q.shape[1]), lambda b, s: (b, 0)),
        ),
        out_shape=jax.ShapeDtypeStruct((n_q_blocks * bq, q.shape[1]), BF16),
        compiler_params=_cparams(("arbitrary",)),
        name="window_attention",
    )(sink, q, k, k, k, v, v, v, k, v, bias)


def _odd_out_kernel(a_ref, w_ref, x_ref, mod_ref, o_ref):
    acc = jnp.dot(a_ref[...], w_ref[...], preferred_element_type=F32)
    o_ref[...] = x_ref[...] + mod_ref[2:3, :] * acc


def _odd_out(a, w_o, x, mod, n_lat_tiles, n_tiles):
    d = x.shape[1]
    tm = ROW_TILE
    return pl.pallas_call(
        _odd_out_kernel,
        grid=(n_tiles,),
        in_specs=[
            pl.BlockSpec((tm, a.shape[1]), lambda i: (i, 0)),
            pl.BlockSpec(w_o.shape, lambda i: (0, 0)),
            pl.BlockSpec((tm, d), lambda i: (i, 0)),
            pl.BlockSpec((None, 6, d), _stream_of(n_lat_tiles)),
        ],
        out_specs=pl.BlockSpec((tm, d), lambda i: (i, 0)),
        out_shape=jax.ShapeDtypeStruct((n_tiles * tm, d), F32),
        compiler_params=_cparams(("arbitrary",)),
        name="odd_out_proj",
    )(a, w_o, x, mod)


ROUTER_EXPERT_ROW0 = 8
CHUNK = 8
TILE_BUF = 2 * ROW_TILE + N_EXPERTS * CHUNK
TAB_ROWS = 3


def _router_kernel(x_ref, g_ref, mod_ref, wr_ref, br_ref, tri_ref, h_ref, ri_ref, rg_ref, cnt_ref):
    h = _norm_mod(x_ref[...], g_ref[...], mod_ref[3:4, :], mod_ref[4:5, :])
    h_ref[...] = h.astype(BF16)
    tm = h.shape[0]
    logits = lax.dot_general(wr_ref[...], h, (((1,), (1,)), ((), ())), precision=_HI,
                             preferred_element_type=F32) + br_ref[:, 0:1]
    gl = logits[0:N_GROUPS]
    gmax = jnp.max(gl, axis=0, keepdims=True)
    gi = lax.broadcasted_iota(I32, gl.shape, 0)
    g_idx = jnp.min(jnp.where(gl == gmax, gi, N_GROUPS), axis=0, keepdims=True)
    g_val = 1.0 / jnp.sum(jnp.exp(gl - gmax), axis=0, keepdims=True)
    e_in = logits[ROUTER_EXPERT_ROW0:ROUTER_EXPERT_ROW0 + EXPERTS_PER_GROUP]
    for grp in range(1, N_GROUPS):
        lo = ROUTER_EXPERT_ROW0 + grp * EXPERTS_PER_GROUP
        e_in = jnp.where(g_idx == grp, logits[lo:lo + EXPERTS_PER_GROUP], e_in)
    ei = lax.broadcasted_iota(I32, e_in.shape, 0)
    v1 = jnp.max(e_in, axis=0, keepdims=True)
    i1 = jnp.min(jnp.where(e_in == v1, ei, EXPERTS_PER_GROUP), axis=0, keepdims=True)
    rest = jnp.where(ei == i1, -jnp.inf, e_in)
    v2 = jnp.max(rest, axis=0, keepdims=True)
    i2 = jnp.min(jnp.where(rest == v2, ei, EXPERTS_PER_GROUP), axis=0, keepdims=True)
    w2 = jnp.exp(v2 - v1)
    gate1 = g_val / (1.0 + w2)
    gate2 = g_val * w2 / (1.0 + w2)
    e1 = g_idx * EXPERTS_PER_GROUP + i1
    e2 = g_idx * EXPERTS_PER_GROUP + i2
    xi = lax.broadcasted_iota(I32, (N_EXPERTS, tm), 0)
    oh1 = xi == e1
    oh2 = xi == e2
    oh = oh1.astype(F32) + oh2.astype(F32)
    before = jnp.dot(oh.astype(BF16), tri_ref[...], preferred_element_type=F32)
    rank1 = jnp.sum(jnp.where(oh1, before, 0.0), axis=0, keepdims=True)
    rank2 = jnp.sum(jnp.where(oh2, before, 0.0), axis=0, keepdims=True)
    cnt_ref[...] = jnp.broadcast_to(jnp.sum(oh, axis=1, keepdims=True), cnt_ref.shape)
    orow = lax.broadcasted_iota(I32, (8, tm), 0)
    ri_ref[...] = jnp.where(orow == 0, e1, jnp.where(orow == 1, e2, jnp.where(
        orow == 2, rank1.astype(I32), jnp.where(orow == 3, rank2.astype(I32), 0))))
    rg_ref[...] = jnp.where(orow == 0, gate1, jnp.where(orow == 1, gate2, 0.0))


def _router(x, g, mod, wr, br, tri, n_lat_tiles, n_tiles):
    d = x.shape[1]
    tm = ROW_TILE
    rows = n_tiles * tm
    const = lambda a: pl.BlockSpec(a.shape, lambda i: (0,) * a.ndim)
    return pl.pallas_call(
        _router_kernel,
        grid=(n_tiles,),
        in_specs=[
            pl.BlockSpec((tm, d), lambda i: (i, 0)),
            const(g),
            pl.BlockSpec((None, 6, d), _stream_of(n_lat_tiles)),
            const(wr), const(br), const(tri),
        ],
        out_specs=[
            pl.BlockSpec((tm, d), lambda i: (i, 0)),
            pl.BlockSpec((8, tm), lambda i: (0, i)),
            pl.BlockSpec((8, tm), lambda i: (0, i)),
            pl.BlockSpec((None, N_EXPERTS, LANES), lambda i: (i, 0, 0)),
        ],
        out_shape=[
            jax.ShapeDtypeStruct((rows, d), BF16),
            jax.ShapeDtypeStruct((8, rows), I32),
            jax.ShapeDtypeStruct((8, rows), F32),
            jax.ShapeDtypeStruct((n_tiles, N_EXPERTS, LANES), F32),
        ],
        compiler_params=_cparams(("arbitrary",)),
        name="moe_router",
    )(x, g, mod, wr, br, tri)


def _chunk_rows(c):
    if isinstance(c, int):
        return pl.ds(c * CHUNK, CHUNK)
    return pl.ds(pl.multiple_of(c * CHUNK, CHUNK), CHUNK)


def _dispatch_kernel(tab_ref, lused_ref, fill_ref, h_ref, pos_ref, xb_ref, hs, zbuf, sem, fsem, *, n_tiles, n_blocks):
    i = pl.program_id(0)
    slot = i % 2
    tm = h_ref.shape[0]

    def tail_copy(c):
        return pltpu.make_async_copy(zbuf.at[pl.ds(0, CHUNK), :], xb_ref.at[_chunk_rows(c), :], fsem)

    def block_copy(b):
        return pltpu.make_async_copy(zbuf, xb_ref.at[pl.ds(pl.multiple_of(b * MOE_ROWS, MOE_ROWS), MOE_ROWS), :], fsem)

    def fill(start):
        def tail(e, c):
            st, n = fill_ref[e], fill_ref[N_EXPERTS + e]

            def one(c2, cc):
                cp = tail_copy(st + c2)
                cp.start() if start else cp.wait()
                return cc
            return lax.fori_loop(0, n, one, c)
        lax.fori_loop(0, N_EXPERTS, tail, 0)

        def blk(b, c):
            cp = block_copy(b)
            cp.start() if start else cp.wait()
            return c
        lax.fori_loop(fill_ref[2 * N_EXPERTS], n_blocks, blk, 0)

    @pl.when(i == 0)
    def _():
        zbuf[...] = jnp.zeros_like(zbuf)
        fill(True)

    pos = pos_ref[...]
    r = lax.broadcasted_iota(I32, (TILE_BUF, tm), 0)
    onehot = jnp.where(jnp.logical_or(r == pos[0:1, :], r == pos[1:2, :]), 1.0, 0.0).astype(BF16)
    hs[slot] = jnp.dot(onehot, h_ref[...], preferred_element_type=F32)

    def chunk_copy(sl, src, dst):
        return pltpu.make_async_copy(hs.at[sl, _chunk_rows(src), :], xb_ref.at[_chunk_rows(dst), :], sem.at[sl])

    base = i * (TAB_ROWS * N_EXPERTS)

    def per_expert(e, c):
        src0, n, dst0 = tab_ref[base + e], tab_ref[base + N_EXPERTS + e], tab_ref[base + 2 * N_EXPERTS + e]

        def per_chunk(c2, cc):
            chunk_copy(slot, src0 + c2, dst0 + c2).start()
            return cc
        return lax.fori_loop(0, n, per_chunk, c)
    lax.fori_loop(0, N_EXPERTS, per_expert, 0)

    def wait_chunks(sl, n):
        def w(c, cc):
            chunk_copy(sl, 0, 0).wait()
            return cc
        lax.fori_loop(0, n, w, 0)

    @pl.when(i > 0)
    def _():
        wait_chunks(1 - slot, lused_ref[jnp.maximum(i - 1, 0)])

    @pl.when(i == n_tiles - 1)
    def _():
        wait_chunks(slot, lused_ref[i])
        fill(False)


def _dispatch(h, pos_rows, tab, lused, fill, n_blocks):
    t, d = h.shape
    tm = ROW_TILE
    n_tiles = t // tm
    return pl.pallas_call(
        functools.partial(_dispatch_kernel, n_tiles=n_tiles, n_blocks=n_blocks),
        grid_spec=pltpu.PrefetchScalarGridSpec(
            num_scalar_prefetch=3,
            grid=(n_tiles,),
            in_specs=[
                pl.BlockSpec((tm, d), lambda i, *_: (i, 0)),
                pl.BlockSpec((8, tm), lambda i, *_: (0, i)),
            ],
            out_specs=pl.BlockSpec(memory_space=pl.ANY),
            scratch_shapes=[
                pltpu.VMEM((2, TILE_BUF, d), F32),
                pltpu.VMEM((MOE_ROWS, d), F32),
                pltpu.SemaphoreType.DMA((2,)),
                pltpu.SemaphoreType.DMA(()),
            ],
        ),
        out_shape=jax.ShapeDtypeStruct((n_blocks * MOE_ROWS, d), F32),
        compiler_params=_cparams(("arbitrary",)),
        name="moe_dispatch",
    )(tab, lused, fill, h, pos_rows)


def _ffn_kernel(be_ref, nv_ref, xb_ref, w1_ref, w3_ref, w2_ref, yb_ref, w1s, w3s, w2s):
    i = pl.program_id(0)
    changed = jnp.logical_or(i == 0, be_ref[i] != be_ref[jnp.maximum(i - 1, 0)])

    @pl.when(changed)
    def _():
        w1s[...] = w1_ref[...].astype(BF16)
        w3s[...] = w3_ref[...].astype(BF16)
        w2s[...] = w2_ref[...].astype(BF16)

    nv = nv_ref[i]

    @pl.when(nv > 0)
    def _():
        x = xb_ref[...]
        row = lax.broadcasted_iota(I32, x.shape, 0)
        xb = jnp.where(row < nv, x, 0.0).astype(BF16)
        a = jnp.dot(xb, w1s[...], preferred_element_type=F32)
        b = jnp.dot(xb, w3s[...], preferred_element_type=F32)
        hid = (a * jax.nn.sigmoid(a) * b).astype(BF16)
        yb_ref[...] = jnp.dot(hid, w2s[...], preferred_element_type=F32)

    @pl.when(nv <= 0)
    def _():
        yb_ref[...] = jnp.zeros_like(yb_ref)


def _expert_ffn(xb, block_exp, n_valid, w1, w3, w2):
    r, d = xb.shape
    f = w1.shape[2]
    bm = MOE_ROWS
    return pl.pallas_call(
        _ffn_kernel,
        grid_spec=pltpu.PrefetchScalarGridSpec(
            num_scalar_prefetch=2,
            grid=(r // bm,),
            in_specs=[
                pl.BlockSpec((bm, d), lambda i, be, nv: (i, 0)),
                pl.BlockSpec((None, d, f), lambda i, be, nv: (be[i], 0, 0)),
                pl.BlockSpec((None, d, f), lambda i, be, nv: (be[i], 0, 0)),
                pl.BlockSpec((None, f, d), lambda i, be, nv: (be[i], 0, 0)),
            ],
            out_specs=pl.BlockSpec((bm, d), lambda i, be, nv: (i, 0)),
            scratch_shapes=[pltpu.VMEM((d, f), BF16), pltpu.VMEM((d, f), BF16), pltpu.VMEM((f, d), BF16)],
        ),
        out_shape=jax.ShapeDtypeStruct((r, d), F32),
        compiler_params=_cparams(("arbitrary",)),
        name="moe_expert_mlp",
    )(block_exp, n_valid, xb, w1, w3, w2)


def _combine_kernel(tab_ref, lused_ref, yb_ref, pos_ref, gate_ref, x_ref, mod_ref, o_ref, ys, sem, *, n_tiles):
    i = pl.program_id(0)
    slot = i % 2
    tm = x_ref.shape[0]

    def chunk_copy(sl, src, dst):
        return pltpu.make_async_copy(yb_ref.at[_chunk_rows(src), :], ys.at[sl, _chunk_rows(dst), :], sem.at[sl])

    def fetch(tile, sl):
        base = tile * (TAB_ROWS * N_EXPERTS)

        def per_expert(e, c):
            dst0, n, src0 = tab_ref[base + e], tab_ref[base + N_EXPERTS + e], tab_ref[base + 2 * N_EXPERTS + e]

            def per_chunk(c2, cc):
                chunk_copy(sl, src0 + c2, dst0 + c2).start()
                return cc
            return lax.fori_loop(0, n, per_chunk, c)
        lax.fori_loop(0, N_EXPERTS, per_expert, 0)

    @pl.when(i == 0)
    def _():
        fetch(0, 0)

    @pl.when(i + 1 < n_tiles)
    def _():
        fetch(jnp.minimum(i + 1, n_tiles - 1), 1 - slot)

    def w(c, cc):
        chunk_copy(slot, 0, 0).wait()
        return cc
    lax.fori_loop(0, lused_ref[i], w, 0)

    y = ys[slot]
    used = lused_ref[i] * CHUNK
    rowi = lax.broadcasted_iota(I32, y.shape, 0)
    y16 = jnp.where(rowi < used, y, 0.0).astype(BF16)
    pos = pos_ref[...]
    gate = gate_ref[...]
    lane = lax.broadcasted_iota(I32, (tm, TILE_BUF), 1)
    gm = (jnp.where(lane == pos[:, 0:1], gate[:, 0:1], 0.0) + jnp.where(lane == pos[:, 1:2], gate[:, 1:2], 0.0))
    ghi, glo = _split_bf16(gm)
    mix = jnp.dot(ghi, y16, preferred_element_type=F32) + jnp.dot(glo, y16, preferred_element_type=F32)
    o_ref[...] = x_ref[...] + mod_ref[5:6, :] * mix


def _combine(yb, pos_cols, gates, tab, lused, x, mod, n_lat_tiles, n_tiles):
    d = x.shape[1]
    tm = ROW_TILE
    return pl.pallas_call(
        functools.partial(_combine_kernel, n_tiles=n_tiles),
        grid_spec=pltpu.PrefetchScalarGridSpec(
            num_scalar_prefetch=2,
            grid=(n_tiles,),
            in_specs=[
                pl.BlockSpec(memory_space=pl.ANY),
                pl.BlockSpec((tm, 8), lambda i, *_: (i, 0)),
                pl.BlockSpec((tm, 8), lambda i, *_: (i, 0)),
                pl.BlockSpec((tm, d), lambda i, *_: (i, 0)),
                pl.BlockSpec((None, 6, d), lambda i, *_: (jnp.where(i >= n_lat_tiles, 1, 0), 0, 0)),
            ],
            out_specs=pl.BlockSpec((tm, d), lambda i, *_: (i, 0)),
            scratch_shapes=[pltpu.VMEM((2, TILE_BUF, d), F32), pltpu.SemaphoreType.DMA((2,))],
        ),
        out_shape=jax.ShapeDtypeStruct((n_tiles * tm, d), F32),
        compiler_params=_cparams(("arbitrary",)),
        name="moe_combine",
    )(tab, lused, yb, pos_cols, gates, x, mod)


def _moe_layer(x, g, mod, wr, br, tri, w1, w3, w2, n_lat_tiles, n_tiles):
    tm = ROW_TILE
    rows = n_tiles * tm
    cpb = MOE_ROWS // CHUNK
    h, ri, rg, cnt3 = _router(x, g, mod, wr, br, tri, n_lat_tiles, n_tiles)
    cnt = cnt3[:, :, 0].astype(I32)
    nch = (cnt + CHUNK - 1) // CHUNK
    lbase = jnp.cumsum(nch, axis=1) - nch
    lused = jnp.sum(nch, axis=1).astype(I32)
    tot = jnp.sum(nch, axis=0)
    reg = (tot + cpb - 1) // cpb * cpb
    gend = jnp.cumsum(reg)
    gstart = gend - reg
    gpos = gstart[None, :] + jnp.cumsum(nch, axis=0) - nch
    rows_max = 2 * rows + n_tiles * N_EXPERTS * (CHUNK - 1) + N_EXPERTS * (MOE_ROWS - CHUNK)
    n_blocks = -(-rows_max // MOE_ROWS)
    ex = jnp.arange(N_EXPERTS, dtype=I32)
    blk0 = jnp.arange(n_blocks, dtype=I32) * cpb
    block_exp = jnp.minimum(jnp.sum((gend[None, :] <= blk0[:, None]).astype(I32), axis=1), N_EXPERTS - 1)
    sel = block_exp[:, None] == ex[None, :]
    tot_b = jnp.sum(jnp.where(sel, tot[None, :], 0), axis=1)
    st_b = jnp.sum(jnp.where(sel, gstart[None, :], 0), axis=1)
    n_valid = jnp.clip((tot_b - (blk0 - st_b)) * CHUNK, 0, MOE_ROWS).astype(I32)
    lb_tok = jnp.repeat(lbase, tm, axis=0)
    at = lambda e: jnp.sum(jnp.where(e[:, None] == ex[None, :], lb_tok, 0), axis=1)
    pos1 = CHUNK * at(ri[0]) + ri[2]
    pos2 = CHUNK * at(ri[1]) + ri[3]
    zero = jnp.zeros_like(pos1)
    pos_rows = jnp.stack([pos1, pos2] + [zero] * 6, axis=0).astype(I32)
    tab = jnp.stack([lbase, nch, gpos], axis=1).astype(I32).reshape(-1)
    fill = jnp.concatenate([gstart + tot, reg - tot, gend[-1:] // cpb]).astype(I32)
    xb = _dispatch(h, pos_rows, tab, lused, fill, n_blocks)
    yb = _expert_ffn(xb, block_exp.astype(I32), n_valid, w1, w3, w2)
    return _combine(yb, pos_rows.T, rg.T, tab, lused, x, mod, n_lat_tiles, n_tiles)


def _router_matrix(w_rg, b_rg, w_re, b_re):
    d = w_rg.shape[0]
    wr = jnp.zeros((LANES, d), F32)
    wr = wr.at[0:N_GROUPS].set(w_rg.T.astype(F32))
    wr = wr.at[ROUTER_EXPERT_ROW0:ROUTER_EXPERT_ROW0 + N_EXPERTS].set(w_re.T.astype(F32))
    br = jnp.zeros((LANES,), F32)
    br = br.at[0:N_GROUPS].set(b_rg.astype(F32))
    br = br.at[ROUTER_EXPERT_ROW0:ROUTER_EXPERT_ROW0 + N_EXPERTS].set(b_re.astype(F32))
    return wr, jnp.broadcast_to(br[:, None], (LANES, LANES))


def _dup_heads(w):
    d = w.shape[0]
    w4 = w.reshape(d, N_KV_HEADS, 1, HEAD_DIM)
    return jnp.broadcast_to(w4, (d, N_KV_HEADS, 2, HEAD_DIM)).reshape(d, 2 * N_KV_HEADS * HEAD_DIM)


def kernel(x, c, ctx, c_ctx, w_mod, b_mod, norm_mix_g, norm_ffn_g, w_in_even, conv_w, w_out_even, w_qkv, q_norm_g,
           k_norm_g, sink_logit, w_o, w_router_g, b_router_g, w_router_e, b_router_e, w1, w3, w2):
    bsz, l, d = x.shape
    lc = ctx.shape[1]
    assert bsz == 1, "one sample per call"
    tm = ROW_TILE
    assert l % tm == 0 and lc % tm == 0 and l % lc == 0 and l % (DFT_MINOR * 8) == 0
    depth = w_mod.shape[0]
    t = l + lc
    nl, nt = l // tm, t // tm

    xs = jnp.concatenate([x.reshape(l, d), ctx.reshape(lc, d)], axis=0)
    mod_all = _modulation(c, c_ctx, w_mod, b_mod).reshape(depth, 2, 6, d)
    cs, m1, a_tab, mc = _dft_tables(l, lc)
    rope = _rope_tables(l, t)
    segs = _segment_matrices()
    attn_bias = _attn_bias(l)
    tri = jnp.asarray(np.triu(np.ones((tm, tm), np.float32), 1)).astype(BF16)
    qd = N_HEADS * HEAD_DIM

    for layer in range(depth):
        last = layer == depth - 1
        j = layer // 2
        mod = mod_all[layer]
        g_mix = norm_mix_g[layer].reshape(1, d)
        g_ffn = norm_ffn_g[layer].reshape(1, d)
        if layer % 2 == 0:
            z, bg, u = _even_in(xs, g_mix, mod, w_in_even[j].astype(BF16), cs, nl)
            f = _fourier_seq(z, l, m1, a_tab, mc)
            xs = _even_out(f, bg, u, conv_w[j], w_out_even[j].astype(BF16), xs, mod, nl)
        else:
            wq = w_qkv[j]
            w_all = jnp.concatenate([wq[:, :qd], _dup_heads(wq[:, qd:qd + N_KV_HEADS * HEAD_DIM]),
                                     _dup_heads(wq[:, qd + N_KV_HEADS * HEAD_DIM:])], axis=1).astype(BF16)
            qg = jnp.tile(q_norm_g[j], N_HEADS).reshape(1, qd)
            kg = jnp.tile(k_norm_g[j], 2 * N_KV_HEADS).reshape(1, 2 * N_KV_HEADS * HEAD_DIM)
            q, k, v = _odd_in(xs, g_mix, mod, w_all, qg, kg, segs, rope, nl, nt)
            n_out = nl if last else nt
            att = _attention(q, k, v, sink_logit[j].astype(F32) * LOG2E, attn_bias, l,
                             n_out * (tm // ATT_BLOCK))
            xs = _odd_out(att, w_o[j].astype(BF16), xs, mod, nl, n_out)
        wr, br = _router_matrix(w_router_g[layer], b_router_g[layer], w_router_e[layer], b_router_e[layer])
        n_moe = nl if last else nt
        xs = _moe_layer(xs, g_ffn, mod, wr, br, tri, w1[layer], w3[layer], w2[layer], nl, n_moe)
    return xs[:l].reshape(bsz, l, d)
```

```python
import functools
import math

import numpy as np
import jax
import jax.numpy as jnp
from jax import lax
from jax.experimental import pallas as pl
from jax.experimental.pallas import tpu as pltpu

F32 = jnp.float32
BF16 = jnp.bfloat16
I32 = jnp.int32

EPS = 1e-6
NEG_INF = -1e30

GRID_W = 64
FOURIER_GROUPS = 4
FOURIER_GROUP_DIM = 128
FOURIER_WIDTH = FOURIER_GROUPS * FOURIER_GROUP_DIM
CONV_WIDTH = 512
N_HEADS = 16
N_KV_HEADS = 4
GQA_GROUP = N_HEADS // N_KV_HEADS
HEAD_DIM = 64
WINDOW = 128
ROPE_BASE = 10000.0
ROPE_PAIRS = HEAD_DIM // 4
N_GROUPS = 4
EXPERTS_PER_GROUP = 8
N_EXPERTS = N_GROUPS * EXPERTS_PER_GROUP

LANES = 128
ROW_TILE = 256
ATT_BLOCK = 128
MOE_ROWS = 256
DFT_MINOR = 128
VMEM_LIMIT = 48 * 1024 * 1024

_HI = lax.Precision.HIGHEST
LOG2E = math.log2(math.e)


def _cparams(sem):
    return pltpu.CompilerParams(dimension_semantics=sem, vmem_limit_bytes=VMEM_LIMIT)


def _split_bf16(x):
    hi = x.astype(BF16)
    lo = (x - hi.astype(F32)).astype(BF16)
    return hi, lo


def _mod_kernel(ct_ref, w_ref, b_ref, o_ref):
    ct = ct_ref[...]
    s = ct * jax.nn.sigmoid(ct)
    w = w_ref[...]
    r0 = jnp.sum(w * s[:, 0:1], axis=0, keepdims=True)
    r1 = jnp.sum(w * s[:, 1:2], axis=0, keepdims=True)
    o_ref[...] = jnp.concatenate([r0, r1], axis=0) + b_ref[...]


def _modulation(c, c_ctx, w_mod, b_mod):
    depth, d, n = w_mod.shape
    tn = 512
    ct = jnp.stack([c.reshape(d), c_ctx.reshape(d)], axis=1)
    return pl.pallas_call(
        _mod_kernel,
        grid=(depth, n // tn),
        in_specs=[
            pl.BlockSpec((d, 2), lambda l, j: (0, 0)),
            pl.BlockSpec((None, d, tn), lambda l, j: (l, 0, j)),
            pl.BlockSpec((None, 1, tn), lambda l, j: (l, 0, j)),
        ],
        out_specs=pl.BlockSpec((None, 2, tn), lambda l, j: (l, 0, j)),
        out_shape=jax.ShapeDtypeStruct((depth, 2, n), F32),
        compiler_params=_cparams(("arbitrary", "arbitrary")),
        name="modulation",
    )(ct, w_mod, b_mod.reshape(depth, 1, n))


def _norm_mod(x, g, shift, scale):
    ms = jnp.mean(x * x, axis=-1, keepdims=True)
    y = x * lax.rsqrt(ms + EPS) * g
    return y * (1.0 + scale) + shift


def _stream_of(n_lat_tiles):
    return lambda i: (jnp.where(i >= n_lat_tiles, 1, 0), 0, 0)


def _even_in_kernel(x_ref, g_ref, mod_ref, w_ref, cs_ref, z_ref, bg_ref, u_ref):
    h = _norm_mod(x_ref[...], g_ref[...], mod_ref[0:1, :], mod_ref[1:2, :])
    p = jnp.dot(h.astype(BF16), w_ref[...], preferred_element_type=F32)
    cs = cs_ref[...]
    for grp in range(FOURIER_GROUPS):
        lo = grp * FOURIER_GROUP_DIM
        a = p[:, lo:lo + FOURIER_GROUP_DIM].astype(BF16)
        z = jnp.dot(a, cs, preferred_element_type=F32)
        z_ref[0, :, lo:lo + FOURIER_GROUP_DIM] = z[:, :FOURIER_GROUP_DIM]
        z_ref[1, :, lo:lo + FOURIER_GROUP_DIM] = z[:, FOURIER_GROUP_DIM:]
    o = FOURIER_WIDTH
    bg_ref[...] = p[:, o:o + CONV_WIDTH].astype(BF16)
    u_ref[...] = (p[:, o + CONV_WIDTH:o + 2 * CONV_WIDTH] * p[:, o + 2 * CONV_WIDTH:]).astype(BF16)


def _even_in(x, g, mod, w_in, cs, n_lat_tiles):
    t, d = x.shape
    n = w_in.shape[1]
    tm = ROW_TILE
    return pl.pallas_call(
        _even_in_kernel,
        grid=(t // tm,),
        in_specs=[
            pl.BlockSpec((tm, d), lambda i: (i, 0)),
            pl.BlockSpec((1, d), lambda i: (0, 0)),
            pl.BlockSpec((None, 6, d), _stream_of(n_lat_tiles)),
            pl.BlockSpec((d, n), lambda i: (0, 0)),
            pl.BlockSpec(cs.shape, lambda i: (0, 0)),
        ],
        out_specs=[
            pl.BlockSpec((2, tm, FOURIER_WIDTH), lambda i: (0, i, 0)),
            pl.BlockSpec((tm, CONV_WIDTH), lambda i: (i, 0)),
            pl.BlockSpec((tm, CONV_WIDTH), lambda i: (i, 0)),
        ],
        out_shape=[
            jax.ShapeDtypeStruct((2, t, FOURIER_WIDTH), F32),
            jax.ShapeDtypeStruct((t, CONV_WIDTH), BF16),
            jax.ShapeDtypeStruct((t, CONV_WIDTH), BF16),
        ],
        compiler_params=_cparams(("arbitrary",)),
        name="even_in_proj",
    )(x, g, mod, w_in, cs)


def _dft_major_kernel(m_ref, z_ref, o_ref):
    m = m_ref[...]
    n1, c = z_ref.shape[1], z_ref.shape[3]
    for j in range(z_ref.shape[2]):
        zc = z_ref[:, :, j, :].reshape(2 * n1, c).astype(BF16)
        g = jnp.dot(m, zc, preferred_element_type=F32)
        o_ref[:, :, j, :] = g.reshape(2, n1, c)


def _dft_major(m1, z4, n1):
    c = z4.shape[-1]
    cb = 8
    return pl.pallas_call(
        _dft_major_kernel,
        grid=(DFT_MINOR // cb,),
        in_specs=[pl.BlockSpec(m1.shape, lambda j: (0, 0)), pl.BlockSpec((2, n1, cb, c), lambda j: (0, 0, j, 0))],
        out_specs=pl.BlockSpec((2, n1, cb, c), lambda j: (0, 0, j, 0)),
        out_shape=jax.ShapeDtypeStruct((2, n1, DFT_MINOR, c), F32),
        compiler_params=_cparams(("arbitrary",)),
        name="dft_major",
    )(m1, z4)


def _dft_minor_kernel(a_ref, g_ref, o_ref):
    for j in range(a_ref.shape[0]):
        gcat = jnp.concatenate([g_ref[0, j], g_ref[1, j]], axis=0).astype(BF16)
        o_ref[:, j, :] = jnp.dot(a_ref[j], gcat, preferred_element_type=F32)


def _dft_minor(a_tab, g4):
    n1, m, k2 = a_tab.shape
    c = g4.shape[-1]
    kb = 8
    return pl.pallas_call(
        _dft_minor_kernel,
        grid=(n1 // kb,),
        in_specs=[
            pl.BlockSpec((kb, m, k2), lambda i: (i, 0, 0)),
            pl.BlockSpec((2, kb, DFT_MINOR, c), lambda i: (0, i, 0, 0)),
        ],
        out_specs=pl.BlockSpec((DFT_MINOR, kb, c), lambda i: (0, i, 0)),
        out_shape=jax.ShapeDtypeStruct((DFT_MINOR, n1, c), F32),
        compiler_params=_cparams(("arbitrary",)),
        name="dft_minor",
    )(a_tab, g4)


def _dft_ctx_kernel(a_ref, z_ref, o_ref):
    o_ref[...] = jnp.dot(a_ref[...], z_ref[...].astype(BF16), preferred_element_type=F32)


def _dft_ctx(mc, zc):
    lc, c = mc.shape[0], zc.shape[1]
    return pl.pallas_call(
        _dft_ctx_kernel,
        grid=(1,),
        in_specs=[pl.BlockSpec(mc.shape, lambda i: (0, 0)), pl.BlockSpec(zc.shape, lambda i: (0, 0))],
        out_specs=pl.BlockSpec((lc, c), lambda i: (0, 0)),
        out_shape=jax.ShapeDtypeStruct((lc, c), F32),
        compiler_params=_cparams(("arbitrary",)),
        name="dft_context",
    )(mc, zc)


def _dft_tables(l, lc):
    gd = FOURIER_GROUP_DIM
    kk = np.arange(gd)
    ang = 2.0 * np.pi * ((kk[:, None] * kk[None, :]) % gd) / gd
    cs = np.concatenate([np.cos(ang), -np.sin(ang)], axis=1)
    n1 = l // DFT_MINOR
    k1 = np.arange(n1)
    ang1 = 2.0 * np.pi * ((k1[:, None] * k1[None, :]) % n1) / n1
    c1, s1 = np.cos(ang1), np.sin(ang1)
    m1 = np.block([[c1, s1], [-s1, c1]])
    l2 = np.arange(DFT_MINOR)
    kfull = k1[:, None, None] + n1 * l2[None, :, None]
    ang2 = 2.0 * np.pi * ((kfull * l2[None, None, :]) % l) / l
    sc = 1.0 / math.sqrt(l * gd)
    a_tab = np.concatenate([np.cos(ang2), np.sin(ang2)], axis=2) * sc
    kc = np.arange(lc)
    angc = 2.0 * np.pi * ((kc[:, None] * kc[None, :]) % lc) / lc
    mc = np.concatenate([np.cos(angc), np.sin(angc)], axis=1) / math.sqrt(lc * gd)
    as_bf16 = lambda v: jnp.asarray(v, F32).astype(BF16)
    return as_bf16(cs), as_bf16(m1), as_bf16(a_tab), as_bf16(mc)


def _fourier_seq(z, l, m1, a_tab, mc):
    t, c = z.shape[1], z.shape[2]
    lc = t - l
    n1 = l // DFT_MINOR
    g4 = _dft_major(m1, z.reshape(2, t // DFT_MINOR, DFT_MINOR, c), n1)
    f_lat = _dft_minor(a_tab, g4).reshape(l, c)
    return f_lat, _dft_ctx(mc, z[:, l:].reshape(2 * lc, c))


def _even_out_kernel(fl_ref, fc_ref, bg_ref, u_ref, up_ref, un_ref, cw_ref, w_ref, x_ref, mod_ref, o_ref,
                     *, n_lat_tiles, n_tiles):
    i = pl.program_id(0)
    tm = u_ref.shape[0]
    u = u_ref[...].astype(F32)
    row = lax.broadcasted_iota(I32, u.shape, 0)
    first = jnp.logical_or(i == 0, i == n_lat_tiles)
    last = jnp.logical_or(i == n_lat_tiles - 1, i == n_tiles - 1)
    hb = up_ref.shape[0]
    halo_p = up_ref[...].astype(F32)[hb - 1:hb, :] * jnp.where(first, 0.0, 1.0)
    halo_n = un_ref[...].astype(F32)[0:1, :] * jnp.where(last, 0.0, 1.0)
    u_prev = jnp.where(row == 0, halo_p, pltpu.roll(u, 1, axis=0))
    u_next = jnp.where(row == tm - 1, halo_n, pltpu.roll(u, tm - 1, axis=0))
    cw = cw_ref[...]
    y = bg_ref[...].astype(F32) * (cw[0:1, :] * u_prev + cw[1:2, :] * u + cw[2:3, :] * u_next)
    is_ctx = jnp.full(fl_ref.shape, i, I32) >= n_lat_tiles
    f = jnp.where(is_ctx, fc_ref[...], fl_ref[...])
    acc = jnp.dot(f.astype(BF16), w_ref[0:FOURIER_WIDTH, :], preferred_element_type=F32)
    acc += jnp.dot(y.astype(BF16), w_ref[FOURIER_WIDTH:, :], preferred_element_type=F32)
    o_ref[...] = x_ref[...] + mod_ref[2:3, :] * acc


def _even_out(f_lat, f_ctx, bg, u, conv_w, w_out, x, mod, n_lat_tiles):
    t, d = x.shape
    tm = ROW_TILE
    hb = 16
    n_tiles = t // tm
    r = tm // hb
    return pl.pallas_call(
        functools.partial(_even_out_kernel, n_lat_tiles=n_lat_tiles, n_tiles=n_tiles),
        grid=(n_tiles,),
        in_specs=[
            pl.BlockSpec((tm, FOURIER_WIDTH), lambda i: (jnp.minimum(i, n_lat_tiles - 1), 0)),
            pl.BlockSpec((tm, FOURIER_WIDTH), lambda i: (jnp.clip(i - n_lat_tiles, 0, n_tiles - n_lat_tiles - 1), 0)),
            pl.BlockSpec((tm, CONV_WIDTH), lambda i: (i, 0)),
            pl.BlockSpec((tm, CONV_WIDTH), lambda i: (i, 0)),
            pl.BlockSpec((hb, CONV_WIDTH), lambda i: (jnp.maximum(i * r - 1, 0), 0)),
            pl.BlockSpec((hb, CONV_WIDTH), lambda i: (jnp.minimum((i + 1) * r, t // hb - 1), 0)),
            pl.BlockSpec(conv_w.shape, lambda i: (0, 0)),
            pl.BlockSpec(w_out.shape, lambda i: (0, 0)),
            pl.BlockSpec((tm, d), lambda i: (i, 0)),
            pl.BlockSpec((None, 6, d), _stream_of(n_lat_tiles)),
        ],
        out_specs=pl.BlockSpec((tm, d), lambda i: (i, 0)),
        out_shape=jax.ShapeDtypeStruct((t, d), F32),
        compiler_params=_cparams(("arbitrary",)),
        name="even_out_proj",
    )(f_lat, f_ctx, bg, u, u, u, conv_w, w_out, x, mod)


def _seg_rms_scale(v, seg, seg_t):
    hi, lo = _split_bf16(v * v)
    ss = jnp.dot(hi, seg, preferred_element_type=F32) + jnp.dot(lo, seg, preferred_element_type=F32)
    inv = lax.rsqrt(ss * (1.0 / HEAD_DIM) + EPS)
    ihi, ilo = _split_bf16(inv)
    return (jnp.dot(ihi, seg_t, preferred_element_type=F32)
            + jnp.dot(ilo, seg_t, preferred_element_type=F32))


def _rope_cols(v, cos, sa, sb, scale):
    cols = []
    for j in range(v.shape[1] // LANES):
        c = v[:, j * LANES:(j + 1) * LANES]
        r = c * cos + pltpu.roll(c, LANES - ROPE_PAIRS, axis=1) * sa + pltpu.roll(c, ROPE_PAIRS, axis=1) * sb
        cols.append(r * scale if scale != 1.0 else r)
    return jnp.concatenate(cols, axis=1)


def _odd_in_kernel(x_ref, g_ref, mod_ref, w_ref, qg_ref, kg_ref, segq_ref, segqt_ref, segk_ref, segkt_ref,
                   rope_ref, q_ref, k_ref, v_ref):
    h = _norm_mod(x_ref[...], g_ref[...], mod_ref[0:1, :], mod_ref[1:2, :])
    p = jnp.dot(h.astype(BF16), w_ref[...], preferred_element_type=F32)
    qd = N_HEADS * HEAD_DIM
    kd = 2 * N_KV_HEADS * HEAD_DIM
    cos, sa, sb = rope_ref[0], rope_ref[1], rope_ref[2]
    q = p[:, :qd]
    q = q * _seg_rms_scale(q, segq_ref[...], segqt_ref[...]) * qg_ref[...]
    q_ref[...] = _rope_cols(q, cos, sa, sb, HEAD_DIM ** -0.5 * LOG2E).astype(BF16)
    k = p[:, qd:qd + kd]
    k = k * _seg_rms_scale(k, segk_ref[...], segkt_ref[...]) * kg_ref[...]
    k_ref[...] = _rope_cols(k, cos, sa, sb, 1.0).astype(BF16)
    v_ref[...] = p[:, qd + kd:].astype(BF16)


def _odd_in(x, g, mod, w_qkv, qg, kg, segs, rope, n_lat_tiles, n_tiles):
    t, d = x.shape
    n = w_qkv.shape[1]
    tm = ROW_TILE
    rows = n_tiles * tm
    qd = N_HEADS * HEAD_DIM
    kd = 2 * N_KV_HEADS * HEAD_DIM
    segq, segqt, segk, segkt = segs
    const = lambda a: pl.BlockSpec(a.shape, lambda i: (0,) * a.ndim)
    return pl.pallas_call(
        _odd_in_kernel,
        grid=(n_tiles,),
        in_specs=[
            pl.BlockSpec((tm, d), lambda i: (i, 0)),
            const(g),
            pl.BlockSpec((None, 6, d), _stream_of(n_lat_tiles)),
            const(w_qkv), const(qg), const(kg), const(segq), const(segqt), const(segk), const(segkt),
            pl.BlockSpec((3, tm, LANES), lambda i: (0, i, 0)),
        ],
        out_specs=[
            pl.BlockSpec((tm, qd), lambda i: (i, 0)),
            pl.BlockSpec((tm, kd), lambda i: (i, 0)),
            pl.BlockSpec((tm, kd), lambda i: (i, 0)),
        ],
        out_shape=[
            jax.ShapeDtypeStruct((rows, qd), BF16),
            jax.ShapeDtypeStruct((rows, kd), BF16),
            jax.ShapeDtypeStruct((rows, kd), BF16),
        ],
        compiler_params=_cparams(("arbitrary",)),
        name="odd_in_proj",
    )(x, g, mod, w_qkv, qg, kg, segq, segqt, segk, segkt, rope)


def _rope_tables(l, t):
    pos = np.arange(l)
    freqs = ROPE_BASE ** (-np.arange(ROPE_PAIRS, dtype=np.float32) / ROPE_PAIRS)
    lane = np.arange(LANES) % HEAD_DIM
    axis = lane // (2 * ROPE_PAIRS)
    half = (lane % (2 * ROPE_PAIRS)) // ROPE_PAIRS
    pair = lane % ROPE_PAIRS
    p = np.where(axis[None, :] == 0, (pos // GRID_W)[:, None], (pos % GRID_W)[:, None]).astype(np.float32)
    ang = p * freqs[pair][None, :].astype(np.float32)
    cos, sin = np.cos(ang), np.sin(ang)
    sa = np.where(half[None, :] == 0, -sin, 0.0)
    sb = np.where(half[None, :] == 1, sin, 0.0)
    tab = np.zeros((3, t, LANES), np.float32)
    tab[0, :l], tab[1, :l], tab[2, :l] = cos, sa, sb
    tab[0, l:] = 1.0
    return jnp.asarray(tab)


def _segment_matrices():
    def seg(width):
        m = np.zeros((width, LANES), np.float32)
        m[np.arange(width), np.arange(width) // HEAD_DIM] = 1.0
        return m
    sq, sk = seg(N_HEADS * HEAD_DIM), seg(2 * N_KV_HEADS * HEAD_DIM)
    b = lambda v: jnp.asarray(v).astype(BF16)
    return b(sq), b(sq.T), b(sk), b(sk.T)


def _attn_kernel(sink_ref, q_ref, kp_ref, kc_ref, kn_ref, vp_ref, vc_ref, vn_ref, kx_ref, vx_ref, bias_ref, o_ref):
    bq = q_ref.shape[0]
    low = lax.broadcasted_iota(I32, (bq, LANES), 1) < HEAD_DIM
    top = lax.broadcasted_iota(I32, (LANES, bq), 0) < HEAD_DIM
    bias = bias_ref[...]
    nt = (((1,), (1,)), ((), ()))
    tn = (((0,), (0,)), ((), ()))
    for g in range(N_KV_HEADS):
        ks = slice(g * LANES, (g + 1) * LANES)
        parts, sinks = [], []
        for j in range(GQA_GROUP):
            col = 2 * g + j // 2
            c = q_ref[:, col * LANES:(col + 1) * LANES]
            keep = low if j % 2 == 0 else jnp.logical_not(low)
            parts.append(jnp.where(keep, c, jnp.zeros_like(c)))
            sinks.append(jnp.full((1, bq), sink_ref[g * GQA_GROUP + j], F32))
        qs = jnp.concatenate(parts, axis=0)
        sink = jnp.concatenate(sinks, axis=1)
        kwin = jnp.concatenate([kp_ref[:, ks], kc_ref[:, ks], kn_ref[:, ks]], axis=0)
        vwin = jnp.concatenate([vp_ref[:, ks], vc_ref[:, ks], vn_ref[:, ks]], axis=0)
        s_loc = lax.dot_general(kwin, qs, nt, preferred_element_type=F32) + bias
        s_ctx = lax.dot_general(kx_ref[:, ks], qs, nt, preferred_element_type=F32)
        m = jnp.maximum(jnp.maximum(jnp.max(s_loc, axis=0, keepdims=True),
                                    jnp.max(s_ctx, axis=0, keepdims=True)), sink)
        p_loc = jnp.exp2(s_loc - m)
        p_ctx = jnp.exp2(s_ctx - m)
        den = (jnp.sum(p_loc, axis=0, keepdims=True) + jnp.sum(p_ctx, axis=0, keepdims=True)
               + jnp.exp2(sink - m))
        ot = lax.dot_general(vwin, p_loc.astype(BF16), tn, preferred_element_type=F32)
        ot += lax.dot_general(vx_ref[:, ks], p_ctx.astype(BF16), tn, preferred_element_type=F32)
        ot = ot * (1.0 / den)
        t0 = jnp.where(top, ot[:, 0:bq], ot[:, bq:2 * bq])
        t1 = jnp.where(top, ot[:, 2 * bq:3 * bq], ot[:, 3 * bq:4 * bq])
        o_ref[:, 2 * g * LANES:(2 * g + 1) * LANES] = t0.T.astype(o_ref.dtype)
        o_ref[:, (2 * g + 1) * LANES:(2 * g + 2) * LANES] = t1.T.astype(o_ref.dtype)


def _attn_bias(l):
    bq = ATT_BLOCK
    r = np.arange(GQA_GROUP * bq)[None, :] % bq
    col = np.arange(3 * bq)[:, None]
    band = np.abs(col - bq - r) <= WINDOW
    no_prev, no_next = col >= bq, col < 2 * bq
    masks = [band, band & no_prev, band & no_next, band & no_prev & no_next, np.zeros_like(band)]
    return jnp.asarray(np.where(np.stack(masks), 0.0, NEG_INF).astype(np.float32))


def _attention(q, k, v, sink, bias, l, n_q_blocks):
    t = q.shape[0]
    bq = ATT_BLOCK
    nlb = l // bq
    lc = t - l
    ctx_blk = l // lc
    kw = k.shape[1]
    kspec = lambda f: pl.BlockSpec((bq, kw), f)
    prev = lambda b, s: (jnp.clip(b - 1, 0, nlb - 1), 0)
    cur = lambda b, s: (jnp.minimum(b, nlb - 1), 0)
    nxt = lambda b, s: (jnp.clip(b + 1, 0, nlb - 1), 0)
    ctx = pl.BlockSpec((lc, kw), lambda b, s: (ctx_blk, 0))
    kind = lambda b, s: (jnp.where(b >= nlb, 4, (b == 0).astype(I32) + 2 * (b == nlb - 1).astype(I32)), 0, 0)
    return pl.pallas_call(
        _attn_kernel,
        grid_spec=pltpu.PrefetchScalarGridSpec(
            num_scalar_prefetch=1,
            grid=(n_q_blocks,),
            in_specs=[
                pl.BlockSpec((bq, q.shape[1]), lambda b, s: (b, 0)),
                kspec(prev), kspec(cur), kspec(nxt), kspec(prev), kspec(cur), kspec(nxt), ctx, ctx,
                pl.BlockSpec((None,) + bias.shape[1:], kind),
            ],
            out_specs=pl.BlockSpec((bq, q.shape[1]), lambda b, s: (b, 0)),
        ),
        out_shape=jax.ShapeDtypeStruct((n_q_blocks * bq, q.shape[1]), BF16),
        compiler_params=_cparams(("arbitrary",)),
        name="window_attention",
    )(sink, q, k, k, k, v, v, v, k, v, bias)


def _odd_out_kernel(a_ref, w_ref, x_ref, mod_ref, o_ref):
    acc = jnp.dot(a_ref[...], w_ref[...], preferred_element_type=F32)
    o_ref[...] = x_ref[...] + mod_ref[2:3, :] * acc


def _odd_out(a, w_o, x, mod, n_lat_tiles, n_tiles):
    d = x.shape[1]
    tm = ROW_TILE
    return pl.pallas_call(
        _odd_out_kernel,
        grid=(n_tiles,),
        in_specs=[
            pl.BlockSpec((tm, a.shape[1]), lambda i: (i, 0)),
            pl.BlockSpec(w_o.shape, lambda i: (0, 0)),
            pl.BlockSpec((tm, d), lambda i: (i, 0)),
            pl.BlockSpec((None, 6, d), _stream_of(n_lat_tiles)),
        ],
        out_specs=pl.BlockSpec((tm, d), lambda i: (i, 0)),
        out_shape=jax.ShapeDtypeStruct((n_tiles * tm, d), F32),
        compiler_params=_cparams(("arbitrary",)),
        name="odd_out_proj",
    )(a, w_o, x, mod)


ROUTER_EXPERT_ROW0 = 8
CHUNK = 8
TILE_BUF = 2 * ROW_TILE + N_EXPERTS * CHUNK
TAB_ROWS = 3


def _router_kernel(x_ref, g_ref, mod_ref, wr_ref, br_ref, tri_ref, h_ref, ri_ref, rg_ref, cnt_ref):
    h = _norm_mod(x_ref[...], g_ref[...], mod_ref[3:4, :], mod_ref[4:5, :])
    h_ref[...] = h.astype(BF16)
    tm = h.shape[0]
    logits = lax.dot_general(wr_ref[...], h, (((1,), (1,)), ((), ())), precision=_HI,
                             preferred_element_type=F32) + br_ref[:, 0:1]
    gl = logits[0:N_GROUPS]
    gmax = jnp.max(gl, axis=0, keepdims=True)
    gi = lax.broadcasted_iota(I32, gl.shape, 0)
    g_idx = jnp.min(jnp.where(gl == gmax, gi, N_GROUPS), axis=0, keepdims=True)
    g_val = 1.0 / jnp.sum(jnp.exp(gl - gmax), axis=0, keepdims=True)
    e_in = logits[ROUTER_EXPERT_ROW0:ROUTER_EXPERT_ROW0 + EXPERTS_PER_GROUP]
    for grp in range(1, N_GROUPS):
        lo = ROUTER_EXPERT_ROW0 + grp * EXPERTS_PER_GROUP
        e_in = jnp.where(g_idx == grp, logits[lo:lo + EXPERTS_PER_GROUP], e_in)
    ei = lax.broadcasted_iota(I32, e_in.shape, 0)
    v1 = jnp.max(e_in, axis=0, keepdims=True)
    i1 = jnp.min(jnp.where(e_in == v1, ei, EXPERTS_PER_GROUP), axis=0, keepdims=True)
    rest = jnp.where(ei == i1, -jnp.inf, e_in)
    v2 = jnp.max(rest, axis=0, keepdims=True)
    i2 = jnp.min(jnp.where(rest == v2, ei, EXPERTS_PER_GROUP), axis=0, keepdims=True)
    w2 = jnp.exp(v2 - v1)
    gate1 = g_val / (1.0 + w2)
    gate2 = g_val * w2 / (1.0 + w2)
    e1 = g_idx * EXPERTS_PER_GROUP + i1
    e2 = g_idx * EXPERTS_PER_GROUP + i2
    xi = lax.broadcasted_iota(I32, (N_EXPERTS, tm), 0)
    oh1 = xi == e1
    oh2 = xi == e2
    oh = oh1.astype(F32) + oh2.astype(F32)
    before = jnp.dot(oh.astype(BF16), tri_ref[...], preferred_element_type=F32)
    rank1 = jnp.sum(jnp.where(oh1, before, 0.0), axis=0, keepdims=True)
    rank2 = jnp.sum(jnp.where(oh2, before, 0.0), axis=0, keepdims=True)
    cnt_ref[...] = jnp.broadcast_to(jnp.sum(oh, axis=1, keepdims=True), cnt_ref.shape)
    orow = lax.broadcasted_iota(I32, (8, tm), 0)
    ri_ref[...] = jnp.where(orow == 0, e1, jnp.where(orow == 1, e2, jnp.where(
        orow == 2, rank1.astype(I32), jnp.where(orow == 3, rank2.astype(I32), 0))))
    rg_ref[...] = jnp.where(orow == 0, gate1, jnp.where(orow == 1, gate2, 0.0))


def _router(x, g, mod, wr, br, tri, n_lat_tiles, n_tiles):
    d = x.shape[1]
    tm = ROW_TILE
    rows = n_tiles * tm
    const = lambda a: pl.BlockSpec(a.shape, lambda i: (0,) * a.ndim)
    return pl.pallas_call(
        _router_kernel,
        grid=(n_tiles,),
        in_specs=[
            pl.BlockSpec((tm, d), lambda i: (i, 0)),
            const(g),
            pl.BlockSpec((None, 6, d), _stream_of(n_lat_tiles)),
            const(wr), const(br), const(tri),
        ],
        out_specs=[
            pl.BlockSpec((tm, d), lambda i: (i, 0)),
            pl.BlockSpec((8, tm), lambda i: (0, i)),
            pl.BlockSpec((8, tm), lambda i: (0, i)),
            pl.BlockSpec((None, N_EXPERTS, LANES), lambda i: (i, 0, 0)),
        ],
        out_shape=[
            jax.ShapeDtypeStruct((rows, d), BF16),
            jax.ShapeDtypeStruct((8, rows), I32),
            jax.ShapeDtypeStruct((8, rows), F32),
            jax.ShapeDtypeStruct((n_tiles, N_EXPERTS, LANES), F32),
        ],
        compiler_params=_cparams(("arbitrary",)),
        name="moe_router",
    )(x, g, mod, wr, br, tri)


def _chunk_rows(c):
    if isinstance(c, int):
        return pl.ds(c * CHUNK, CHUNK)
    return pl.ds(pl.multiple_of(c * CHUNK, CHUNK), CHUNK)


def _dispatch_kernel(tab_ref, lused_ref, fill_ref, h_ref, pos_ref, xb_ref, hs, zbuf, sem, fsem, *, n_tiles, n_blocks):
    i = pl.program_id(0)
    slot = i % 2
    tm = h_ref.shape[0]

    def tail_copy(c):
        return pltpu.make_async_copy(zbuf.at[pl.ds(0, CHUNK), :], xb_ref.at[_chunk_rows(c), :], fsem)

    def block_copy(b):
        return pltpu.make_async_copy(zbuf, xb_ref.at[pl.ds(pl.multiple_of(b * MOE_ROWS, MOE_ROWS), MOE_ROWS), :], fsem)

    def fill(start):
        def tail(e, c):
            st, n = fill_ref[e], fill_ref[N_EXPERTS + e]

            def one(c2, cc):
                cp = tail_copy(st + c2)
                cp.start() if start else cp.wait()
                return cc
            return lax.fori_loop(0, n, one, c)
        lax.fori_loop(0, N_EXPERTS, tail, 0)

        def blk(b, c):
            cp = block_copy(b)
            cp.start() if start else cp.wait()
            return c
        lax.fori_loop(fill_ref[2 * N_EXPERTS], n_blocks, blk, 0)

    @pl.when(i == 0)
    def _():
        zbuf[...] = jnp.zeros_like(zbuf)
        fill(True)

    pos = pos_ref[...]
    r = lax.broadcasted_iota(I32, (TILE_BUF, tm), 0)
    onehot = jnp.where(jnp.logical_or(r == pos[0:1, :], r == pos[1:2, :]), 1.0, 0.0).astype(BF16)
    hs[slot] = jnp.dot(onehot, h_ref[...], preferred_element_type=F32)

    def chunk_copy(sl, src, dst):
        return pltpu.make_async_copy(hs.at[sl, _chunk_rows(src), :], xb_ref.at[_chunk_rows(dst), :], sem.at[sl])

    base = i * (TAB_ROWS * N_EXPERTS)

    def per_expert(e, c):
        src0, n, dst0 = tab_ref[base + e], tab_ref[base + N_EXPERTS + e], tab_ref[base + 2 * N_EXPERTS + e]

        def per_chunk(c2, cc):
            chunk_copy(slot, src0 + c2, dst0 + c2).start()
            return cc
        return lax.fori_loop(0, n, per_chunk, c)
    lax.fori_loop(0, N_EXPERTS, per_expert, 0)

    def wait_chunks(sl, n):
        def w(c, cc):
            chunk_copy(sl, 0, 0).wait()
            return cc
        lax.fori_loop(0, n, w, 0)

    @pl.when(i > 0)
    def _():
        wait_chunks(1 - slot, lused_ref[jnp.maximum(i - 1, 0)])

    @pl.when(i == n_tiles - 1)
    def _():
        wait_chunks(slot, lused_ref[i])
        fill(False)


def _dispatch(h, pos_rows, tab, lused, fill, n_blocks):
    t, d = h.shape
    tm = ROW_TILE
    n_tiles = t // tm
    return pl.pallas_call(
        functools.partial(_dispatch_kernel, n_tiles=n_tiles, n_blocks=n_blocks),
        grid_spec=pltpu.PrefetchScalarGridSpec(
            num_scalar_prefetch=3,
            grid=(n_tiles,),
            in_specs=[
                pl.BlockSpec((tm, d), lambda i, *_: (i, 0)),
                pl.BlockSpec((8, tm), lambda i, *_: (0, i)),
            ],
            out_specs=pl.BlockSpec(memory_space=pl.ANY),
            scratch_shapes=[
                pltpu.VMEM((2, TILE_BUF, d), F32),
                pltpu.VMEM((MOE_ROWS, d), F32),
                pltpu.SemaphoreType.DMA((2,)),
                pltpu.SemaphoreType.DMA(()),
            ],
        ),
        out_shape=jax.ShapeDtypeStruct((n_blocks * MOE_ROWS, d), F32),
        compiler_params=_cparams(("arbitrary",)),
        name="moe_dispatch",
    )(tab, lused, fill, h, pos_rows)


def _ffn_kernel(be_ref, nv_ref, xb_ref, w1_ref, w3_ref, w2_ref, yb_ref, w1s, w3s, w2s):
    i = pl.program_id(0)
    changed = jnp.logical_or(i == 0, be_ref[i] != be_ref[jnp.maximum(i - 1, 0)])

    @pl.when(changed)
    def _():
        w1s[...] = w1_ref[...].astype(BF16)
        w3s[...] = w3_ref[...].astype(BF16)
        w2s[...] = w2_ref[...].astype(BF16)

    nv = nv_ref[i]

    @pl.when(nv > 0)
    def _():
        x = xb_ref[...]
        row = lax.broadcasted_iota(I32, x.shape, 0)
        xb = jnp.where(row < nv, x, 0.0).astype(BF16)
        a = jnp.dot(xb, w1s[...], preferred_element_type=F32)
        b = jnp.dot(xb, w3s[...], preferred_element_type=F32)
        hid = (a * jax.nn.sigmoid(a) * b).astype(BF16)
        yb_ref[...] = jnp.dot(hid, w2s[...], preferred_element_type=F32)

    @pl.when(nv <= 0)
    def _():
        yb_ref[...] = jnp.zeros_like(yb_ref)


def _expert_ffn(xb, block_exp, n_valid, w1, w3, w2, layer):
    r = xb.shape[0]
    d, f = w1.shape[2], w1.shape[3]
    bm = MOE_ROWS
    return pl.pallas_call(
        _ffn_kernel,
        grid_spec=pltpu.PrefetchScalarGridSpec(
            num_scalar_prefetch=2,
            grid=(r // bm,),
            in_specs=[
                pl.BlockSpec((bm, d), lambda i, be, nv: (i, 0)),
                pl.BlockSpec((None, None, d, f), lambda i, be, nv: (layer, be[i], 0, 0)),
                pl.BlockSpec((None, None, d, f), lambda i, be, nv: (layer, be[i], 0, 0)),
                pl.BlockSpec((None, None, f, d), lambda i, be, nv: (layer, be[i], 0, 0)),
            ],
            out_specs=pl.BlockSpec((bm, d), lambda i, be, nv: (i, 0)),
            scratch_shapes=[pltpu.VMEM((d, f), BF16), pltpu.VMEM((d, f), BF16), pltpu.VMEM((f, d), BF16)],
        ),
        out_shape=jax.ShapeDtypeStruct((r, d), F32),
        compiler_params=_cparams(("arbitrary",)),
        name="moe_expert_mlp",
    )(block_exp, n_valid, xb, w1, w3, w2)


def _combine_kernel(tab_ref, lused_ref, yb_ref, pos_ref, gate_ref, x_ref, mod_ref, o_ref, ys, sem, *, n_tiles):
    i = pl.program_id(0)
    slot = i % 2
    tm = x_ref.shape[0]

    def chunk_copy(sl, src, dst):
        return pltpu.make_async_copy(yb_ref.at[_chunk_rows(src), :], ys.at[sl, _chunk_rows(dst), :], sem.at[sl])

    def fetch(tile, sl):
        base = tile * (TAB_ROWS * N_EXPERTS)

        def per_expert(e, c):
            dst0, n, src0 = tab_ref[base + e], tab_ref[base + N_EXPERTS + e], tab_ref[base + 2 * N_EXPERTS + e]

            def per_chunk(c2, cc):
                chunk_copy(sl, src0 + c2, dst0 + c2).start()
                return cc
            return lax.fori_loop(0, n, per_chunk, c)
        lax.fori_loop(0, N_EXPERTS, per_expert, 0)

    @pl.when(i == 0)
    def _():
        fetch(0, 0)

    @pl.when(i + 1 < n_tiles)
    def _():
        fetch(jnp.minimum(i + 1, n_tiles - 1), 1 - slot)

    def w(c, cc):
        chunk_copy(slot, 0, 0).wait()
        return cc
    lax.fori_loop(0, lused_ref[i], w, 0)

    y = ys[slot]
    used = lused_ref[i] * CHUNK
    rowi = lax.broadcasted_iota(I32, y.shape, 0)
    y16 = jnp.where(rowi < used, y, 0.0).astype(BF16)
    pos = pos_ref[...]
    gate = gate_ref[...]
    lane = lax.broadcasted_iota(I32, (tm, TILE_BUF), 1)
    gm = (jnp.where(lane == pos[:, 0:1], gate[:, 0:1], 0.0) + jnp.where(lane == pos[:, 1:2], gate[:, 1:2], 0.0))
    ghi, glo = _split_bf16(gm)
    mix = jnp.dot(ghi, y16, preferred_element_type=F32) + jnp.dot(glo, y16, preferred_element_type=F32)
    o_ref[...] = x_ref[...] + mod_ref[5:6, :] * mix


def _combine(yb, pos_cols, gates, tab, lused, x, mod, n_lat_tiles, n_tiles):
    d = x.shape[1]
    tm = ROW_TILE
    return pl.pallas_call(
        functools.partial(_combine_kernel, n_tiles=n_tiles),
        grid_spec=pltpu.PrefetchScalarGridSpec(
            num_scalar_prefetch=2,
            grid=(n_tiles,),
            in_specs=[
                pl.BlockSpec(memory_space=pl.ANY),
                pl.BlockSpec((tm, 8), lambda i, *_: (i, 0)),
                pl.BlockSpec((tm, 8), lambda i, *_: (i, 0)),
                pl.BlockSpec((tm, d), lambda i, *_: (i, 0)),
                pl.BlockSpec((None, 6, d), lambda i, *_: (jnp.where(i >= n_lat_tiles, 1, 0), 0, 0)),
            ],
            out_specs=pl.BlockSpec((tm, d), lambda i, *_: (i, 0)),
            scratch_shapes=[pltpu.VMEM((2, TILE_BUF, d), F32), pltpu.SemaphoreType.DMA((2,))],
        ),
        out_shape=jax.ShapeDtypeStruct((n_tiles * tm, d), F32),
        compiler_params=_cparams(("arbitrary",)),
        name="moe_combine",
    )(tab, lused, yb, pos_cols, gates, x, mod)


def _moe_layer(x, g, mod, wr, br, tri, w1, w3, w2, layer, n_lat_tiles, n_tiles):
    tm = ROW_TILE
    rows = n_tiles * tm
    cpb = MOE_ROWS // CHUNK
    h, ri, rg, cnt3 = _router(x, g, mod, wr, br, tri, n_lat_tiles, n_tiles)
    cnt = cnt3[:, :, 0].astype(I32)
    nch = (cnt + CHUNK - 1) // CHUNK
    lbase = jnp.cumsum(nch, axis=1) - nch
    lused = jnp.sum(nch, axis=1).astype(I32)
    tot = jnp.sum(nch, axis=0)
    reg = (tot + cpb - 1) // cpb * cpb
    gend = jnp.cumsum(reg)
    gstart = gend - reg
    gpos = gstart[None, :] + jnp.cumsum(nch, axis=0) - nch
    rows_max = 2 * rows + n_tiles * N_EXPERTS * (CHUNK - 1) + N_EXPERTS * (MOE_ROWS - CHUNK)
    n_blocks = -(-rows_max // MOE_ROWS)
    ex = jnp.arange(N_EXPERTS, dtype=I32)
    blk0 = jnp.arange(n_blocks, dtype=I32) * cpb
    block_exp = jnp.minimum(jnp.sum((gend[None, :] <= blk0[:, None]).astype(I32), axis=1), N_EXPERTS - 1)
    sel = block_exp[:, None] == ex[None, :]
    tot_b = jnp.sum(jnp.where(sel, tot[None, :], 0), axis=1)
    st_b = jnp.sum(jnp.where(sel, gstart[None, :], 0), axis=1)
    n_valid = jnp.clip((tot_b - (blk0 - st_b)) * CHUNK, 0, MOE_ROWS).astype(I32)
    lb_tok = jnp.repeat(lbase, tm, axis=0)
    at = lambda e: jnp.sum(jnp.where(e[:, None] == ex[None, :], lb_tok, 0), axis=1)
    pos1 = CHUNK * at(ri[0]) + ri[2]
    pos2 = CHUNK * at(ri[1]) + ri[3]
    zero = jnp.zeros_like(pos1)
    pos_rows = jnp.stack([pos1, pos2] + [zero] * 6, axis=0).astype(I32)
    tab = jnp.stack([lbase, nch, gpos], axis=1).astype(I32).reshape(-1)
    fill = jnp.concatenate([gstart + tot, reg - tot, gend[-1:] // cpb]).astype(I32)
    xb = _dispatch(h, pos_rows, tab, lused, fill, n_blocks)
    yb = _expert_ffn(xb, block_exp.astype(I32), n_valid, w1, w3, w2, layer)
    return _combine(yb, pos_rows.T, rg.T, tab, lused, x, mod, n_lat_tiles, n_tiles)


def _router_matrix(w_rg, b_rg, w_re, b_re):
    d = w_rg.shape[0]
    wr = jnp.zeros((LANES, d), F32)
    wr = wr.at[0:N_GROUPS].set(w_rg.T.astype(F32))
    wr = wr.at[ROUTER_EXPERT_ROW0:ROUTER_EXPERT_ROW0 + N_EXPERTS].set(w_re.T.astype(F32))
    br = jnp.zeros((LANES,), F32)
    br = br.at[0:N_GROUPS].set(b_rg.astype(F32))
    br = br.at[ROUTER_EXPERT_ROW0:ROUTER_EXPERT_ROW0 + N_EXPERTS].set(b_re.astype(F32))
    return wr, jnp.broadcast_to(br[:, None], (LANES, LANES))


def _dup_heads(w):
    d = w.shape[0]
    w4 = w.reshape(d, N_KV_HEADS, 1, HEAD_DIM)
    return jnp.broadcast_to(w4, (d, N_KV_HEADS, 2, HEAD_DIM)).reshape(d, 2 * N_KV_HEADS * HEAD_DIM)


def kernel(x, c, ctx, c_ctx, w_mod, b_mod, norm_mix_g, norm_ffn_g, w_in_even, conv_w, w_out_even, w_qkv, q_norm_g,
           k_norm_g, sink_logit, w_o, w_router_g, b_router_g, w_router_e, b_router_e, w1, w3, w2):
    bsz, l, d = x.shape
    lc = ctx.shape[1]
    assert bsz == 1, "one sample per call"
    tm = ROW_TILE
    assert l % tm == 0 and lc % tm == 0 and l % lc == 0 and l % (DFT_MINOR * 8) == 0
    depth = w_mod.shape[0]
    t = l + lc
    assert t % DFT_MINOR == 0
    nl, nt = l // tm, t // tm

    xs = jnp.concatenate([x.reshape(l, d), ctx.reshape(lc, d)], axis=0)
    mod_all = _modulation(c, c_ctx, w_mod, b_mod).reshape(depth, 2, 6, d)
    cs, m1, a_tab, mc = _dft_tables(l, lc)
    rope = _rope_tables(l, t)
    segs = _segment_matrices()
    attn_bias = _attn_bias(l)
    tri = jnp.asarray(np.triu(np.ones((tm, tm), np.float32), 1)).astype(BF16)
    qd = N_HEADS * HEAD_DIM

    for layer in range(depth):
        last = layer == depth - 1
        j = layer // 2
        mod = mod_all[layer]
        g_mix = norm_mix_g[layer].reshape(1, d)
        g_ffn = norm_ffn_g[layer].reshape(1, d)
        if layer % 2 == 0:
            z, bg, u = _even_in(xs, g_mix, mod, w_in_even[j].astype(BF16), cs, nl)
            f_lat, f_ctx = _fourier_seq(z, l, m1, a_tab, mc)
            xs = _even_out(f_lat, f_ctx, bg, u, conv_w[j], w_out_even[j].astype(BF16), xs, mod, nl)
        else:
            wq = w_qkv[j]
            w_all = jnp.concatenate([wq[:, :qd], _dup_heads(wq[:, qd:qd + N_KV_HEADS * HEAD_DIM]),
                                     _dup_heads(wq[:, qd + N_KV_HEADS * HEAD_DIM:])], axis=1).astype(BF16)
            qg = jnp.tile(q_norm_g[j], N_HEADS).reshape(1, qd)
            kg = jnp.tile(k_norm_g[j], 2 * N_KV_HEADS).reshape(1, 2 * N_KV_HEADS * HEAD_DIM)
            q, k, v = _odd_in(xs, g_mix, mod, w_all, qg, kg, segs, rope, nl, nt)
            n_out = nl if last else nt
            att = _attention(q, k, v, sink_logit[j].astype(F32) * LOG2E, attn_bias, l,
                             n_out * (tm // ATT_BLOCK))
            xs = _odd_out(att, w_o[j].astype(BF16), xs, mod, nl, n_out)
        wr, br = _router_matrix(w_router_g[layer], b_router_g[layer], w_router_e[layer], b_router_e[layer])
        n_moe = nl if last else nt
        xs = _moe_layer(xs, g_ffn, mod, wr, br, tri, w1, w3, w2, layer, nl, n_moe)
    return xs[:l].reshape(bsz, l, d)
```

```python
import functools
import math

import numpy as np
import jax
import jax.numpy as jnp
from jax import lax
from jax.experimental import pallas as pl
from jax.experimental.pallas import tpu as pltpu

F32 = jnp.float32
BF16 = jnp.bfloat16
I32 = jnp.int32

EPS = 1e-6
NEG_INF = -1e30

GRID_W = 64
FOURIER_GROUPS = 4
FOURIER_GROUP_DIM = 128
FOURIER_WIDTH = FOURIER_GROUPS * FOURIER_GROUP_DIM
CONV_WIDTH = 512
N_HEADS = 16
N_KV_HEADS = 4
GQA_GROUP = N_HEADS // N_KV_HEADS
HEAD_DIM = 64
WINDOW = 128
ROPE_BASE = 10000.0
ROPE_PAIRS = HEAD_DIM // 4
N_GROUPS = 4
EXPERTS_PER_GROUP = 8
N_EXPERTS = N_GROUPS * EXPERTS_PER_GROUP

LANES = 128
ROW_TILE = 256
ATT_BLOCK = 128
MOE_ROWS = 256
DFT_MINOR = 128
VMEM_LIMIT = 48 * 1024 * 1024

_HI = lax.Precision.HIGHEST
LOG2E = math.log2(math.e)


def _cparams(sem):
    return pltpu.CompilerParams(dimension_semantics=sem, vmem_limit_bytes=VMEM_LIMIT)


def _split_bf16(x):
    hi = x.astype(BF16)
    lo = (x - hi.astype(F32)).astype(BF16)
    return hi, lo


def _mod_kernel(ct_ref, w_ref, b_ref, o_ref):
    ct = ct_ref[...]
    s = ct * jax.nn.sigmoid(ct)
    w = w_ref[...]
    r0 = jnp.sum(w * s[:, 0:1], axis=0, keepdims=True)
    r1 = jnp.sum(w * s[:, 1:2], axis=0, keepdims=True)
    o_ref[...] = jnp.concatenate([r0, r1], axis=0) + b_ref[...]


def _modulation(c, c_ctx, w_mod, b_mod):
    depth, d, n = w_mod.shape
    tn = 512
    ct = jnp.stack([c.reshape(d), c_ctx.reshape(d)], axis=1)
    return pl.pallas_call(
        _mod_kernel,
        grid=(depth, n // tn),
        in_specs=[
            pl.BlockSpec((d, 2), lambda l, j: (0, 0)),
            pl.BlockSpec((None, d, tn), lambda l, j: (l, 0, j)),
            pl.BlockSpec((None, 1, tn), lambda l, j: (l, 0, j)),
        ],
        out_specs=pl.BlockSpec((None, 2, tn), lambda l, j: (l, 0, j)),
        out_shape=jax.ShapeDtypeStruct((depth, 2, n), F32),
        compiler_params=_cparams(("arbitrary", "arbitrary")),
        name="modulation",
    )(ct, w_mod, b_mod.reshape(depth, 1, n))


def _norm_mod(x, g, shift, scale):
    ms = jnp.mean(x * x, axis=-1, keepdims=True)
    y = x * lax.rsqrt(ms + EPS) * g
    return y * (1.0 + scale) + shift


def _stream_of(n_lat_tiles):
    return lambda i: (jnp.where(i >= n_lat_tiles, 1, 0), 0, 0)


def _even_in_kernel(x_ref, g_ref, mod_ref, w_ref, cs_ref, z_ref, bg_ref, u_ref):
    h = _norm_mod(x_ref[...], g_ref[...], mod_ref[0:1, :], mod_ref[1:2, :])
    p = jnp.dot(h.astype(BF16), w_ref[...], preferred_element_type=F32)
    cs = cs_ref[...]
    for grp in range(FOURIER_GROUPS):
        lo = grp * FOURIER_GROUP_DIM
        a = p[:, lo:lo + FOURIER_GROUP_DIM].astype(BF16)
        z = jnp.dot(a, cs, preferred_element_type=F32)
        z_ref[0, :, lo:lo + FOURIER_GROUP_DIM] = z[:, :FOURIER_GROUP_DIM]
        z_ref[1, :, lo:lo + FOURIER_GROUP_DIM] = z[:, FOURIER_GROUP_DIM:]
    o = FOURIER_WIDTH
    bg_ref[...] = p[:, o:o + CONV_WIDTH].astype(BF16)
    u_ref[...] = (p[:, o + CONV_WIDTH:o + 2 * CONV_WIDTH] * p[:, o + 2 * CONV_WIDTH:]).astype(BF16)


def _even_in(x, g, mod, w_in, cs, n_lat_tiles):
    t, d = x.shape
    n = w_in.shape[1]
    tm = ROW_TILE
    return pl.pallas_call(
        _even_in_kernel,
        grid=(t // tm,),
        in_specs=[
            pl.BlockSpec((tm, d), lambda i: (i, 0)),
            pl.BlockSpec((1, d), lambda i: (0, 0)),
            pl.BlockSpec((None, 6, d), _stream_of(n_lat_tiles)),
            pl.BlockSpec((d, n), lambda i: (0, 0)),
            pl.BlockSpec(cs.shape, lambda i: (0, 0)),
        ],
        out_specs=[
            pl.BlockSpec((2, tm, FOURIER_WIDTH), lambda i: (0, i, 0)),
            pl.BlockSpec((tm, CONV_WIDTH), lambda i: (i, 0)),
            pl.BlockSpec((tm, CONV_WIDTH), lambda i: (i, 0)),
        ],
        out_shape=[
            jax.ShapeDtypeStruct((2, t, FOURIER_WIDTH), F32),
            jax.ShapeDtypeStruct((t, CONV_WIDTH), BF16),
            jax.ShapeDtypeStruct((t, CONV_WIDTH), BF16),
        ],
        compiler_params=_cparams(("arbitrary",)),
        name="even_in_proj",
    )(x, g, mod, w_in, cs)


def _dft_major_kernel(m_ref, z_ref, o_ref):
    m = m_ref[...]
    n1, c = z_ref.shape[1], z_ref.shape[3]
    for j in range(z_ref.shape[2]):
        zc = z_ref[:, :, j, :].reshape(2 * n1, c).astype(BF16)
        g = jnp.dot(m, zc, preferred_element_type=F32)
        o_ref[:, :, j, :] = g.reshape(2, n1, c)


def _dft_major(m1, z4, n1):
    c = z4.shape[-1]
    cb = 8
    return pl.pallas_call(
        _dft_major_kernel,
        grid=(DFT_MINOR // cb,),
        in_specs=[pl.BlockSpec(m1.shape, lambda j: (0, 0)), pl.BlockSpec((2, n1, cb, c), lambda j: (0, 0, j, 0))],
        out_specs=pl.BlockSpec((2, n1, cb, c), lambda j: (0, 0, j, 0)),
        out_shape=jax.ShapeDtypeStruct((2, n1, DFT_MINOR, c), F32),
        compiler_params=_cparams(("arbitrary",)),
        name="dft_major",
    )(m1, z4)


def _dft_minor_kernel(a_ref, g_ref, o_ref):
    for j in range(a_ref.shape[0]):
        gcat = jnp.concatenate([g_ref[0, j], g_ref[1, j]], axis=0).astype(BF16)
        o_ref[:, j, :] = jnp.dot(a_ref[j], gcat, preferred_element_type=F32)


def _dft_minor(a_tab, g4):
    n1, m, k2 = a_tab.shape
    c = g4.shape[-1]
    kb = 8
    return pl.pallas_call(
        _dft_minor_kernel,
        grid=(n1 // kb,),
        in_specs=[
            pl.BlockSpec((kb, m, k2), lambda i: (i, 0, 0)),
            pl.BlockSpec((2, kb, DFT_MINOR, c), lambda i: (0, i, 0, 0)),
        ],
        out_specs=pl.BlockSpec((DFT_MINOR, kb, c), lambda i: (0, i, 0)),
        out_shape=jax.ShapeDtypeStruct((DFT_MINOR, n1, c), F32),
        compiler_params=_cparams(("arbitrary",)),
        name="dft_minor",
    )(a_tab, g4)


def _dft_ctx_kernel(a_ref, z_ref, o_ref):
    o_ref[...] = jnp.dot(a_ref[...], z_ref[...].astype(BF16), preferred_element_type=F32)


def _dft_ctx(mc, zc):
    lc, c = mc.shape[0], zc.shape[1]
    return pl.pallas_call(
        _dft_ctx_kernel,
        grid=(1,),
        in_specs=[pl.BlockSpec(mc.shape, lambda i: (0, 0)), pl.BlockSpec(zc.shape, lambda i: (0, 0))],
        out_specs=pl.BlockSpec((lc, c), lambda i: (0, 0)),
        out_shape=jax.ShapeDtypeStruct((lc, c), F32),
        compiler_params=_cparams(("arbitrary",)),
        name="dft_context",
    )(mc, zc)


def _dft_tables(l, lc):
    gd = FOURIER_GROUP_DIM
    kk = np.arange(gd)
    ang = 2.0 * np.pi * ((kk[:, None] * kk[None, :]) % gd) / gd
    cs = np.concatenate([np.cos(ang), -np.sin(ang)], axis=1)
    n1 = l // DFT_MINOR
    k1 = np.arange(n1)
    ang1 = 2.0 * np.pi * ((k1[:, None] * k1[None, :]) % n1) / n1
    c1, s1 = np.cos(ang1), np.sin(ang1)
    m1 = np.block([[c1, s1], [-s1, c1]])
    l2 = np.arange(DFT_MINOR)
    kfull = k1[:, None, None] + n1 * l2[None, :, None]
    ang2 = 2.0 * np.pi * ((kfull * l2[None, None, :]) % l) / l
    sc = 1.0 / math.sqrt(l * gd)
    a_tab = np.concatenate([np.cos(ang2), np.sin(ang2)], axis=2) * sc
    kc = np.arange(lc)
    angc = 2.0 * np.pi * ((kc[:, None] * kc[None, :]) % lc) / lc
    mc = np.concatenate([np.cos(angc), np.sin(angc)], axis=1) / math.sqrt(lc * gd)
    as_bf16 = lambda v: jnp.asarray(v, F32).astype(BF16)
    return as_bf16(cs), as_bf16(m1), as_bf16(a_tab), as_bf16(mc)


def _fourier_seq(z, l, m1, a_tab, mc):
    t, c = z.shape[1], z.shape[2]
    lc = t - l
    n1 = l // DFT_MINOR
    g4 = _dft_major(m1, z.reshape(2, t // DFT_MINOR, DFT_MINOR, c), n1)
    f_lat = _dft_minor(a_tab, g4).reshape(l, c)
    return f_lat, _dft_ctx(mc, z[:, l:].reshape(2 * lc, c))


def _even_out_kernel(fl_ref, fc_ref, bg_ref, u_ref, up_ref, un_ref, cw_ref, w_ref, x_ref, mod_ref, o_ref,
                     *, n_lat_tiles, n_tiles):
    i = pl.program_id(0)
    tm = u_ref.shape[0]
    u = u_ref[...].astype(F32)
    row = lax.broadcasted_iota(I32, u.shape, 0)
    first = jnp.logical_or(i == 0, i == n_lat_tiles)
    last = jnp.logical_or(i == n_lat_tiles - 1, i == n_tiles - 1)
    hb = up_ref.shape[0]
    halo_p = up_ref[...].astype(F32)[hb - 1:hb, :] * jnp.where(first, 0.0, 1.0)
    halo_n = un_ref[...].astype(F32)[0:1, :] * jnp.where(last, 0.0, 1.0)
    u_prev = jnp.where(row == 0, halo_p, pltpu.roll(u, 1, axis=0))
    u_next = jnp.where(row == tm - 1, halo_n, pltpu.roll(u, tm - 1, axis=0))
    cw = cw_ref[...]
    y = bg_ref[...].astype(F32) * (cw[0:1, :] * u_prev + cw[1:2, :] * u + cw[2:3, :] * u_next)
    is_ctx = jnp.full(fl_ref.shape, i, I32) >= n_lat_tiles
    f = jnp.where(is_ctx, fc_ref[...], fl_ref[...])
    acc = jnp.dot(f.astype(BF16), w_ref[0:FOURIER_WIDTH, :], preferred_element_type=F32)
    acc += jnp.dot(y.astype(BF16), w_ref[FOURIER_WIDTH:, :], preferred_element_type=F32)
    o_ref[...] = x_ref[...] + mod_ref[2:3, :] * acc


def _even_out(f_lat, f_ctx, bg, u, conv_w, w_out, x, mod, n_lat_tiles):
    t, d = x.shape
    tm = ROW_TILE
    hb = 16
    n_tiles = t // tm
    r = tm // hb
    return pl.pallas_call(
        functools.partial(_even_out_kernel, n_lat_tiles=n_lat_tiles, n_tiles=n_tiles),
        grid=(n_tiles,),
        in_specs=[
            pl.BlockSpec((tm, FOURIER_WIDTH), lambda i: (jnp.minimum(i, n_lat_tiles - 1), 0)),
            pl.BlockSpec((tm, FOURIER_WIDTH), lambda i: (jnp.clip(i - n_lat_tiles, 0, n_tiles - n_lat_tiles - 1), 0)),
            pl.BlockSpec((tm, CONV_WIDTH), lambda i: (i, 0)),
            pl.BlockSpec((tm, CONV_WIDTH), lambda i: (i, 0)),
            pl.BlockSpec((hb, CONV_WIDTH), lambda i: (jnp.maximum(i * r - 1, 0), 0)),
            pl.BlockSpec((hb, CONV_WIDTH), lambda i: (jnp.minimum((i + 1) * r, t // hb - 1), 0)),
            pl.BlockSpec(conv_w.shape, lambda i: (0, 0)),
            pl.BlockSpec(w_out.shape, lambda i: (0, 0)),
            pl.BlockSpec((tm, d), lambda i: (i, 0)),
            pl.BlockSpec((None, 6, d), _stream_of(n_lat_tiles)),
        ],
        out_specs=pl.BlockSpec((tm, d), lambda i: (i, 0)),
        out_shape=jax.ShapeDtypeStruct((t, d), F32),
        compiler_params=_cparams(("arbitrary",)),
        name="even_out_proj",
    )(f_lat, f_ctx, bg, u, u, u, conv_w, w_out, x, mod)


def _seg_rms_scale(v, seg, seg_t):
    hi, lo = _split_bf16(v * v)
    ss = jnp.dot(hi, seg, preferred_element_type=F32) + jnp.dot(lo, seg, preferred_element_type=F32)
    inv = lax.rsqrt(ss * (1.0 / HEAD_DIM) + EPS)
    ihi, ilo = _split_bf16(inv)
    return (jnp.dot(ihi, seg_t, preferred_element_type=F32)
            + jnp.dot(ilo, seg_t, preferred_element_type=F32))


def _rope_cols(v, cos, sa, sb, scale):
    cols = []
    for j in range(v.shape[1] // LANES):
        c = v[:, j * LANES:(j + 1) * LANES]
        r = c * cos + pltpu.roll(c, LANES - ROPE_PAIRS, axis=1) * sa + pltpu.roll(c, ROPE_PAIRS, axis=1) * sb
        cols.append(r * scale if scale != 1.0 else r)
    return jnp.concatenate(cols, axis=1)


def _odd_in_kernel(x_ref, g_ref, mod_ref, w_ref, qg_ref, kg_ref, segq_ref, segqt_ref, segk_ref, segkt_ref,
                   rope_ref, q_ref, k_ref, v_ref):
    h = _norm_mod(x_ref[...], g_ref[...], mod_ref[0:1, :], mod_ref[1:2, :])
    p = jnp.dot(h.astype(BF16), w_ref[...], preferred_element_type=F32)
    qd = N_HEADS * HEAD_DIM
    kd = 2 * N_KV_HEADS * HEAD_DIM
    cos, sa, sb = rope_ref[0], rope_ref[1], rope_ref[2]
    q = p[:, :qd]
    q = q * _seg_rms_scale(q, segq_ref[...], segqt_ref[...]) * qg_ref[...]
    q_ref[...] = _rope_cols(q, cos, sa, sb, HEAD_DIM ** -0.5 * LOG2E).astype(BF16)
    k = p[:, qd:qd + kd]
    k = k * _seg_rms_scale(k, segk_ref[...], segkt_ref[...]) * kg_ref[...]
    k_ref[...] = _rope_cols(k, cos, sa, sb, 1.0).astype(BF16)
    v_ref[...] = p[:, qd + kd:].astype(BF16)


def _odd_in(x, g, mod, w_qkv, qg, kg, segs, rope, n_lat_tiles, n_tiles):
    t, d = x.shape
    n = w_qkv.shape[1]
    tm = ROW_TILE
    rows = n_tiles * tm
    qd = N_HEADS * HEAD_DIM
    kd = 2 * N_KV_HEADS * HEAD_DIM
    segq, segqt, segk, segkt = segs
    const = lambda a: pl.BlockSpec(a.shape, lambda i: (0,) * a.ndim)
    return pl.pallas_call(
        _odd_in_kernel,
        grid=(n_tiles,),
        in_specs=[
            pl.BlockSpec((tm, d), lambda i: (i, 0)),
            const(g),
            pl.BlockSpec((None, 6, d), _stream_of(n_lat_tiles)),
            const(w_qkv), const(qg), const(kg), const(segq), const(segqt), const(segk), const(segkt),
            pl.BlockSpec((3, tm, LANES), lambda i: (0, i, 0)),
        ],
        out_specs=[
            pl.BlockSpec((tm, qd), lambda i: (i, 0)),
            pl.BlockSpec((tm, kd), lambda i: (i, 0)),
            pl.BlockSpec((tm, kd), lambda i: (i, 0)),
        ],
        out_shape=[
            jax.ShapeDtypeStruct((rows, qd), BF16),
            jax.ShapeDtypeStruct((rows, kd), BF16),
            jax.ShapeDtypeStruct((rows, kd), BF16),
        ],
        compiler_params=_cparams(("arbitrary",)),
        name="odd_in_proj",
    )(x, g, mod, w_qkv, qg, kg, segq, segqt, segk, segkt, rope)


def _rope_tables(l, t):
    pos = np.arange(l)
    freqs = ROPE_BASE ** (-np.arange(ROPE_PAIRS, dtype=np.float32) / ROPE_PAIRS)
    lane = np.arange(LANES) % HEAD_DIM
    axis = lane // (2 * ROPE_PAIRS)
    half = (lane % (2 * ROPE_PAIRS)) // ROPE_PAIRS
    pair = lane % ROPE_PAIRS
    p = np.where(axis[None, :] == 0, (pos // GRID_W)[:, None], (pos % GRID_W)[:, None]).astype(np.float32)
    ang = p * freqs[pair][None, :].astype(np.float32)
    cos, sin = np.cos(ang), np.sin(ang)
    sa = np.where(half[None, :] == 0, -sin, 0.0)
    sb = np.where(half[None, :] == 1, sin, 0.0)
    tab = np.zeros((3, t, LANES), np.float32)
    tab[0, :l], tab[1, :l], tab[2, :l] = cos, sa, sb
    tab[0, l:] = 1.0
    return jnp.asarray(tab)


def _segment_matrices():
    def seg(width):
        m = np.zeros((width, LANES), np.float32)
        m[np.arange(width), np.arange(width) // HEAD_DIM] = 1.0
        return m
    sq, sk = seg(N_HEADS * HEAD_DIM), seg(2 * N_KV_HEADS * HEAD_DIM)
    b = lambda v: jnp.asarray(v).astype(BF16)
    return b(sq), b(sq.T), b(sk), b(sk.T)


def _attn_kernel(sink_ref, q_ref, kp_ref, kc_ref, kn_ref, vp_ref, vc_ref, vn_ref, kx_ref, vx_ref, bias_ref, o_ref):
    bq = q_ref.shape[0]
    low = lax.broadcasted_iota(I32, (bq, LANES), 1) < HEAD_DIM
    top = lax.broadcasted_iota(I32, (LANES, bq), 0) < HEAD_DIM
    bias = bias_ref[...]
    nt = (((1,), (1,)), ((), ()))
    tn = (((0,), (0,)), ((), ()))
    for g in range(N_KV_HEADS):
        ks = slice(g * LANES, (g + 1) * LANES)
        parts, sinks = [], []
        for j in range(GQA_GROUP):
            col = 2 * g + j // 2
            c = q_ref[:, col * LANES:(col + 1) * LANES]
            keep = low if j % 2 == 0 else jnp.logical_not(low)
            parts.append(jnp.where(keep, c, jnp.zeros_like(c)))
            sinks.append(jnp.full((1, bq), sink_ref[g * GQA_GROUP + j], F32))
        qs = jnp.concatenate(parts, axis=0)
        sink = jnp.concatenate(sinks, axis=1)
        kwin = jnp.concatenate([kp_ref[:, ks], kc_ref[:, ks], kn_ref[:, ks]], axis=0)
        vwin = jnp.concatenate([vp_ref[:, ks], vc_ref[:, ks], vn_ref[:, ks]], axis=0)
        s_loc = lax.dot_general(kwin, qs, nt, preferred_element_type=F32) + bias
        s_ctx = lax.dot_general(kx_ref[:, ks], qs, nt, preferred_element_type=F32)
        m = jnp.maximum(jnp.maximum(jnp.max(s_loc, axis=0, keepdims=True),
                                    jnp.max(s_ctx, axis=0, keepdims=True)), sink)
        p_loc = jnp.exp2(s_loc - m)
        p_ctx = jnp.exp2(s_ctx - m)
        den = (jnp.sum(p_loc, axis=0, keepdims=True) + jnp.sum(p_ctx, axis=0, keepdims=True)
               + jnp.exp2(sink - m))
        ot = lax.dot_general(vwin, p_loc.astype(BF16), tn, preferred_element_type=F32)
        ot += lax.dot_general(vx_ref[:, ks], p_ctx.astype(BF16), tn, preferred_element_type=F32)
        ot = ot * (1.0 / den)
        t0 = jnp.where(top, ot[:, 0:bq], ot[:, bq:2 * bq])
        t1 = jnp.where(top, ot[:, 2 * bq:3 * bq], ot[:, 3 * bq:4 * bq])
        o_ref[:, 2 * g * LANES:(2 * g + 1) * LANES] = t0.T.astype(o_ref.dtype)
        o_ref[:, (2 * g + 1) * LANES:(2 * g + 2) * LANES] = t1.T.astype(o_ref.dtype)


def _attn_bias(l):
    bq = ATT_BLOCK
    r = np.arange(GQA_GROUP * bq)[None, :] % bq
    col = np.arange(3 * bq)[:, None]
    band = np.abs(col - bq - r) <= WINDOW
    no_prev, no_next = col >= bq, col < 2 * bq
    masks = [band, band & no_prev, band & no_next, band & no_prev & no_next, np.zeros_like(band)]
    return jnp.asarray(np.where(np.stack(masks), 0.0, NEG_INF).astype(np.float32))


def _attention(q, k, v, sink, bias, l, n_q_blocks):
    t = q.shape[0]
    bq = ATT_BLOCK
    nlb = l // bq
    lc = t - l
    ctx_blk = l // lc
    kw = k.shape[1]
    kspec = lambda f: pl.BlockSpec((bq, kw), f)
    prev = lambda b, s: (jnp.clip(b - 1, 0, nlb - 1), 0)
    cur = lambda b, s: (jnp.minimum(b, nlb - 1), 0)
    nxt = lambda b, s: (jnp.clip(b + 1, 0, nlb - 1), 0)
    ctx = pl.BlockSpec((lc, kw), lambda b, s: (ctx_blk, 0))
    kind = lambda b, s: (jnp.where(b >= nlb, 4, (b == 0).astype(I32) + 2 * (b == nlb - 1).astype(I32)), 0, 0)
    return pl.pallas_call(
        _attn_kernel,
        grid_spec=pltpu.PrefetchScalarGridSpec(
            num_scalar_prefetch=1,
            grid=(n_q_blocks,),
            in_specs=[
                pl.BlockSpec((bq, q.shape[1]), lambda b, s: (b, 0)),
                kspec(prev), kspec(cur), kspec(nxt), kspec(prev), kspec(cur), kspec(nxt), ctx, ctx,
                pl.BlockSpec((None,) + bias.shape[1:], kind),
            ],
            out_specs=pl.BlockSpec((bq, q.shape[1]), lambda b, s: (b, 0)),
        ),
        out_shape=jax.ShapeDtypeStruct((n_q_blocks * bq, q.shape[1]), BF16),
        compiler_params=_cparams(("arbitrary",)),
        name="window_attention",
    )(sink, q, k, k, k, v, v, v, k, v, bias)


def _odd_out_kernel(a_ref, w_ref, x_ref, mod_ref, o_ref):
    acc = jnp.dot(a_ref[...], w_ref[...], preferred_element_type=F32)
    o_ref[...] = x_ref[...] + mod_ref[2:3, :] * acc


def _odd_out(a, w_o, x, mod, n_lat_tiles, n_tiles):
    d = x.shape[1]
    tm = ROW_TILE
    return pl.pallas_call(
        _odd_out_kernel,
        grid=(n_tiles,),
        in_specs=[
            pl.BlockSpec((tm, a.shape[1]), lambda i: (i, 0)),
            pl.BlockSpec(w_o.shape, lambda i: (0, 0)),
            pl.BlockSpec((tm, d), lambda i: (i, 0)),
            pl.BlockSpec((None, 6, d), _stream_of(n_lat_tiles)),
        ],
        out_specs=pl.BlockSpec((tm, d), lambda i: (i, 0)),
        out_shape=jax.ShapeDtypeStruct((n_tiles * tm, d), F32),
        compiler_params=_cparams(("arbitrary",)),
        name="odd_out_proj",
    )(a, w_o, x, mod)


ROUTER_EXPERT_ROW0 = 8
CHUNK = 8
TILE_BUF = 2 * ROW_TILE + N_EXPERTS * CHUNK
MAX_TILE_CHUNKS = TILE_BUF // CHUNK


def _router_kernel(x_ref, g_ref, mod_ref, wr_ref, br_ref, tri_ref, h_ref, ri_ref, rg_ref, cnt_ref):
    h = _norm_mod(x_ref[...], g_ref[...], mod_ref[3:4, :], mod_ref[4:5, :])
    h_ref[...] = h.astype(BF16)
    tm = h.shape[0]
    logits = lax.dot_general(wr_ref[...], h, (((1,), (1,)), ((), ())), precision=_HI,
                             preferred_element_type=F32) + br_ref[:, 0:1]
    gl = logits[0:N_GROUPS]
    gmax = jnp.max(gl, axis=0, keepdims=True)
    gi = lax.broadcasted_iota(I32, gl.shape, 0)
    g_idx = jnp.min(jnp.where(gl == gmax, gi, N_GROUPS), axis=0, keepdims=True)
    g_val = 1.0 / jnp.sum(jnp.exp(gl - gmax), axis=0, keepdims=True)
    e_in = logits[ROUTER_EXPERT_ROW0:ROUTER_EXPERT_ROW0 + EXPERTS_PER_GROUP]
    for grp in range(1, N_GROUPS):
        lo = ROUTER_EXPERT_ROW0 + grp * EXPERTS_PER_GROUP
        e_in = jnp.where(g_idx == grp, logits[lo:lo + EXPERTS_PER_GROUP], e_in)
    ei = lax.broadcasted_iota(I32, e_in.shape, 0)
    v1 = jnp.max(e_in, axis=0, keepdims=True)
    i1 = jnp.min(jnp.where(e_in == v1, ei, EXPERTS_PER_GROUP), axis=0, keepdims=True)
    rest = jnp.where(ei == i1, -jnp.inf, e_in)
    v2 = jnp.max(rest, axis=0, keepdims=True)
    i2 = jnp.min(jnp.where(rest == v2, ei, EXPERTS_PER_GROUP), axis=0, keepdims=True)
    w2 = jnp.exp(v2 - v1)
    gate1 = g_val / (1.0 + w2)
    gate2 = g_val * w2 / (1.0 + w2)
    e1 = g_idx * EXPERTS_PER_GROUP + i1
    e2 = g_idx * EXPERTS_PER_GROUP + i2
    xi = lax.broadcasted_iota(I32, (N_EXPERTS, tm), 0)
    oh1 = xi == e1
    oh2 = xi == e2
    oh = oh1.astype(F32) + oh2.astype(F32)
    before = jnp.dot(oh.astype(BF16), tri_ref[...], preferred_element_type=F32)
    rank1 = jnp.sum(jnp.where(oh1, before, 0.0), axis=0, keepdims=True)
    rank2 = jnp.sum(jnp.where(oh2, before, 0.0), axis=0, keepdims=True)
    cnt_ref[...] = jnp.broadcast_to(jnp.sum(oh, axis=1, keepdims=True), cnt_ref.shape)
    orow = lax.broadcasted_iota(I32, (8, tm), 0)
    ri_ref[...] = jnp.where(orow == 0, e1, jnp.where(orow == 1, e2, jnp.where(
        orow == 2, rank1.astype(I32), jnp.where(orow == 3, rank2.astype(I32), 0))))
    rg_ref[...] = jnp.where(orow == 0, gate1, jnp.where(orow == 1, gate2, 0.0))


def _router(x, g, mod, wr, br, tri, n_lat_tiles, n_tiles):
    d = x.shape[1]
    tm = ROW_TILE
    rows = n_tiles * tm
    const = lambda a: pl.BlockSpec(a.shape, lambda i: (0,) * a.ndim)
    return pl.pallas_call(
        _router_kernel,
        grid=(n_tiles,),
        in_specs=[
            pl.BlockSpec((tm, d), lambda i: (i, 0)),
            const(g),
            pl.BlockSpec((None, 6, d), _stream_of(n_lat_tiles)),
            const(wr), const(br), const(tri),
        ],
        out_specs=[
            pl.BlockSpec((tm, d), lambda i: (i, 0)),
            pl.BlockSpec((8, tm), lambda i: (0, i)),
            pl.BlockSpec((8, tm), lambda i: (0, i)),
            pl.BlockSpec((None, N_EXPERTS, LANES), lambda i: (i, 0, 0)),
        ],
        out_shape=[
            jax.ShapeDtypeStruct((rows, d), BF16),
            jax.ShapeDtypeStruct((8, rows), I32),
            jax.ShapeDtypeStruct((8, rows), F32),
            jax.ShapeDtypeStruct((n_tiles, N_EXPERTS, LANES), F32),
        ],
        compiler_params=_cparams(("arbitrary",)),
        name="moe_router",
    )(x, g, mod, wr, br, tri)


def _chunk_rows(c):
    return pl.ds(pl.multiple_of(c * CHUNK, CHUNK), CHUNK)


def _wait_rows(copy_of_rows, n_chunks):
    bit = 1
    while bit < MAX_TILE_CHUNKS:
        @pl.when((n_chunks & bit) != 0)
        def _(bit=bit):
            copy_of_rows(bit * CHUNK).wait()
        bit *= 2


def _dispatch_kernel(tab_ref, lused_ref, fill_ref, h_ref, pos_ref, xb_ref, hs, zbuf, sem, fsem, *, n_tiles, n_blocks):
    i = pl.program_id(0)
    slot = i % 2
    tm = h_ref.shape[0]

    def tail_copy(c):
        return pltpu.make_async_copy(zbuf.at[pl.ds(0, CHUNK), :], xb_ref.at[_chunk_rows(c), :], fsem)

    def block_copy(b):
        return pltpu.make_async_copy(zbuf, xb_ref.at[pl.ds(pl.multiple_of(b * MOE_ROWS, MOE_ROWS), MOE_ROWS), :], fsem)

    def fill(start):
        def tail(e, c):
            st, n = fill_ref[e], fill_ref[N_EXPERTS + e]

            def one(c2, cc):
                cp = tail_copy(st + c2)
                cp.start() if start else cp.wait()
                return cc
            return lax.fori_loop(0, n, one, c)
        lax.fori_loop(0, N_EXPERTS, tail, 0)

        def blk(b, c):
            cp = block_copy(b)
            cp.start() if start else cp.wait()
            return c
        lax.fori_loop(fill_ref[2 * N_EXPERTS], n_blocks, blk, 0)

    @pl.when(i == 0)
    def _():
        zbuf[...] = jnp.zeros_like(zbuf)
        fill(True)

    pos = pos_ref[...]
    r = lax.broadcasted_iota(I32, (TILE_BUF, tm), 0)
    onehot = jnp.where(jnp.logical_or(r == pos[0:1, :], r == pos[1:2, :]), 1.0, 0.0).astype(BF16)
    hs[slot] = jnp.dot(onehot, h_ref[...], preferred_element_type=F32)

    def chunk_copy(sl, src, dst):
        return pltpu.make_async_copy(hs.at[sl, _chunk_rows(src), :], xb_ref.at[_chunk_rows(dst), :], sem.at[sl])

    base = i * MAX_TILE_CHUNKS

    def per_chunk(c, cc):
        chunk_copy(slot, c, tab_ref[base + c]).start()
        return cc
    lax.fori_loop(0, lused_ref[i], per_chunk, 0)

    def wait_chunks(sl, n):
        _wait_rows(lambda rows: pltpu.make_async_copy(hs.at[sl, pl.ds(0, rows), :], xb_ref.at[pl.ds(0, rows), :],
                                                      sem.at[sl]), n)

    @pl.when(i > 0)
    def _():
        wait_chunks(1 - slot, lused_ref[jnp.maximum(i - 1, 0)])

    @pl.when(i == n_tiles - 1)
    def _():
        wait_chunks(slot, lused_ref[i])
        fill(False)


def _dispatch(h, pos_rows, tab, lused, fill, n_blocks):
    t, d = h.shape
    tm = ROW_TILE
    n_tiles = t // tm
    return pl.pallas_call(
        functools.partial(_dispatch_kernel, n_tiles=n_tiles, n_blocks=n_blocks),
        grid_spec=pltpu.PrefetchScalarGridSpec(
            num_scalar_prefetch=3,
            grid=(n_tiles,),
            in_specs=[
                pl.BlockSpec((tm, d), lambda i, *_: (i, 0)),
                pl.BlockSpec((8, tm), lambda i, *_: (0, i)),
            ],
            out_specs=pl.BlockSpec(memory_space=pl.ANY),
            scratch_shapes=[
                pltpu.VMEM((2, TILE_BUF, d), F32),
                pltpu.VMEM((MOE_ROWS, d), F32),
                pltpu.SemaphoreType.DMA((2,)),
                pltpu.SemaphoreType.DMA(()),
            ],
        ),
        out_shape=jax.ShapeDtypeStruct((n_blocks * MOE_ROWS, d), F32),
        compiler_params=_cparams(("arbitrary",)),
        name="moe_dispatch",
    )(tab, lused, fill, h, pos_rows)


def _ffn_kernel(b0_ref, nb_ref, nv_ref, fill_ref, xb_ref, w1_ref, w3_ref, w2_ref, yb_ref,
                w1s, w3s, w2s, xbuf, ybuf, zbuf, sem_in, sem_out, fsem, *, n_blocks):
    e = pl.program_id(0)
    w1s[...] = w1_ref[...].astype(BF16)
    w3s[...] = w3_ref[...].astype(BF16)
    w2s[...] = w2_ref[...].astype(BF16)
    b0, nb = b0_ref[e], nb_ref[e]

    def rows(b):
        return pl.ds(pl.multiple_of(b * MOE_ROWS, MOE_ROWS), MOE_ROWS)

    def x_copy(b, sl):
        return pltpu.make_async_copy(xb_ref.at[rows(b), :], xbuf.at[sl], sem_in.at[sl])

    def y_copy(b, sl):
        return pltpu.make_async_copy(ybuf.at[sl], yb_ref.at[rows(b), :], sem_out.at[sl])

    @pl.when(nb > 0)
    def _():
        x_copy(b0, 0).start()

    def block(j, c):
        sl = j % 2
        x_copy(b0 + j, sl).wait()

        @pl.when(j + 1 < nb)
        def _():
            x_copy(b0 + j + 1, 1 - sl).start()

        @pl.when(j >= 2)
        def _():
            y_copy(b0 + j - 2, sl).wait()

        x = xbuf[sl]
        row = lax.broadcasted_iota(I32, x.shape, 0)
        xb = jnp.where(row < nv_ref[b0 + j], x, 0.0).astype(BF16)
        a = jnp.dot(xb, w1s[...], preferred_element_type=F32)
        b = jnp.dot(xb, w3s[...], preferred_element_type=F32)
        hid = (a * jax.nn.sigmoid(a) * b).astype(BF16)
        ybuf[sl] = jnp.dot(hid, w2s[...], preferred_element_type=F32)
        y_copy(b0 + j, sl).start()
        return c
    lax.fori_loop(0, nb, block, 0)

    @pl.when(nb >= 2)
    def _():
        y_copy(b0 + nb - 2, nb % 2).wait()

    @pl.when(nb >= 1)
    def _():
        y_copy(b0 + nb - 1, (nb - 1) % 2).wait()

    @pl.when(e == N_EXPERTS - 1)
    def _():
        zbuf[...] = jnp.zeros_like(zbuf)

        def z_copy(b):
            return pltpu.make_async_copy(zbuf, yb_ref.at[rows(b), :], fsem)

        def start(b, c):
            z_copy(b).start()
            return c

        def wait(b, c):
            z_copy(b).wait()
            return c
        lax.fori_loop(fill_ref[0], n_blocks, start, 0)
        lax.fori_loop(fill_ref[0], n_blocks, wait, 0)


def _expert_ffn(xb, blk_start, blk_count, n_valid, used_blocks, w1, w3, w2, layer):
    r = xb.shape[0]
    d, f = w1.shape[2], w1.shape[3]
    bm = MOE_ROWS
    return pl.pallas_call(
        functools.partial(_ffn_kernel, n_blocks=r // bm),
        grid_spec=pltpu.PrefetchScalarGridSpec(
            num_scalar_prefetch=4,
            grid=(N_EXPERTS,),
            in_specs=[
                pl.BlockSpec(memory_space=pl.ANY),
                pl.BlockSpec((None, None, d, f), lambda e, *_: (layer, e, 0, 0)),
                pl.BlockSpec((None, None, d, f), lambda e, *_: (layer, e, 0, 0)),
                pl.BlockSpec((None, None, f, d), lambda e, *_: (layer, e, 0, 0)),
            ],
            out_specs=pl.BlockSpec(memory_space=pl.ANY),
            scratch_shapes=[
                pltpu.VMEM((d, f), BF16), pltpu.VMEM((d, f), BF16), pltpu.VMEM((f, d), BF16),
                pltpu.VMEM((2, bm, d), F32), pltpu.VMEM((2, bm, d), F32), pltpu.VMEM((bm, d), F32),
                pltpu.SemaphoreType.DMA((2,)), pltpu.SemaphoreType.DMA((2,)), pltpu.SemaphoreType.DMA(()),
            ],
        ),
        out_shape=jax.ShapeDtypeStruct((r, d), F32),
        compiler_params=_cparams(("arbitrary",)),
        name="moe_expert_mlp",
    )(blk_start, blk_count, n_valid, used_blocks, xb, w1, w3, w2)


def _combine_kernel(tab_ref, lused_ref, yb_ref, pos_ref, gate_ref, x_ref, mod_ref, o_ref, ys, sem, *, n_tiles):
    i = pl.program_id(0)
    slot = i % 2
    tm = x_ref.shape[0]

    def chunk_copy(sl, src, dst):
        return pltpu.make_async_copy(yb_ref.at[_chunk_rows(src), :], ys.at[sl, _chunk_rows(dst), :], sem.at[sl])

    def fetch(tile, sl):
        base = tile * MAX_TILE_CHUNKS

        def per_chunk(c, cc):
            chunk_copy(sl, tab_ref[base + c], c).start()
            return cc
        lax.fori_loop(0, lused_ref[tile], per_chunk, 0)

    @pl.when(i == 0)
    def _():
        fetch(0, 0)

    @pl.when(i + 1 < n_tiles)
    def _():
        fetch(jnp.minimum(i + 1, n_tiles - 1), 1 - slot)

    _wait_rows(lambda rows: pltpu.make_async_copy(yb_ref.at[pl.ds(0, rows), :], ys.at[slot, pl.ds(0, rows), :],
                                                  sem.at[slot]), lused_ref[i])

    y = ys[slot]
    used = lused_ref[i] * CHUNK
    rowi = lax.broadcasted_iota(I32, y.shape, 0)
    y16 = jnp.where(rowi < used, y, 0.0).astype(BF16)
    pos = pos_ref[...]
    gate = gate_ref[...]
    lane = lax.broadcasted_iota(I32, (tm, TILE_BUF), 1)
    gm = (jnp.where(lane == pos[:, 0:1], gate[:, 0:1], 0.0) + jnp.where(lane == pos[:, 1:2], gate[:, 1:2], 0.0))
    ghi, glo = _split_bf16(gm)
    mix = jnp.dot(ghi, y16, preferred_element_type=F32) + jnp.dot(glo, y16, preferred_element_type=F32)
    o_ref[...] = x_ref[...] + mod_ref[5:6, :] * mix


def _combine(yb, pos_cols, gates, tab, lused, x, mod, n_lat_tiles, n_tiles):
    d = x.shape[1]
    tm = ROW_TILE
    return pl.pallas_call(
        functools.partial(_combine_kernel, n_tiles=n_tiles),
        grid_spec=pltpu.PrefetchScalarGridSpec(
            num_scalar_prefetch=2,
            grid=(n_tiles,),
            in_specs=[
                pl.BlockSpec(memory_space=pl.ANY),
                pl.BlockSpec((tm, 8), lambda i, *_: (i, 0)),
                pl.BlockSpec((tm, 8), lambda i, *_: (i, 0)),
                pl.BlockSpec((tm, d), lambda i, *_: (i, 0)),
                pl.BlockSpec((None, 6, d), lambda i, *_: (jnp.where(i >= n_lat_tiles, 1, 0), 0, 0)),
            ],
            out_specs=pl.BlockSpec((tm, d), lambda i, *_: (i, 0)),
            scratch_shapes=[pltpu.VMEM((2, TILE_BUF, d), F32), pltpu.SemaphoreType.DMA((2,))],
        ),
        out_shape=jax.ShapeDtypeStruct((n_tiles * tm, d), F32),
        compiler_params=_cparams(("arbitrary",)),
        name="moe_combine",
    )(tab, lused, yb, pos_cols, gates, x, mod)


def _moe_layer(x, g, mod, wr, br, tri, w1, w3, w2, layer, n_lat_tiles, n_tiles):
    tm = ROW_TILE
    rows = n_tiles * tm
    cpb = MOE_ROWS // CHUNK
    h, ri, rg, cnt3 = _router(x, g, mod, wr, br, tri, n_lat_tiles, n_tiles)
    cnt = cnt3[:, :, 0].astype(I32)
    nch = (cnt + CHUNK - 1) // CHUNK
    lbase = jnp.cumsum(nch, axis=1) - nch
    lused = jnp.sum(nch, axis=1).astype(I32)
    tot = jnp.sum(nch, axis=0)
    reg = (tot + cpb - 1) // cpb * cpb
    gend = jnp.cumsum(reg)
    gstart = gend - reg
    gpos = gstart[None, :] + jnp.cumsum(nch, axis=0) - nch
    rows_max = 2 * rows + n_tiles * N_EXPERTS * (CHUNK - 1) + N_EXPERTS * (MOE_ROWS - CHUNK)
    n_blocks = -(-rows_max // MOE_ROWS)
    ex = jnp.arange(N_EXPERTS, dtype=I32)
    blk0 = jnp.arange(n_blocks, dtype=I32) * cpb
    block_exp = jnp.minimum(jnp.sum((gend[None, :] <= blk0[:, None]).astype(I32), axis=1), N_EXPERTS - 1)
    sel = block_exp[:, None] == ex[None, :]
    tot_b = jnp.sum(jnp.where(sel, tot[None, :], 0), axis=1)
    st_b = jnp.sum(jnp.where(sel, gstart[None, :], 0), axis=1)
    n_valid = jnp.clip((tot_b - (blk0 - st_b)) * CHUNK, 0, MOE_ROWS).astype(I32)
    lb_tok = jnp.repeat(lbase, tm, axis=0)
    at = lambda e: jnp.sum(jnp.where(e[:, None] == ex[None, :], lb_tok, 0), axis=1)
    pos1 = CHUNK * at(ri[0]) + ri[2]
    pos2 = CHUNK * at(ri[1]) + ri[3]
    zero = jnp.zeros_like(pos1)
    pos_rows = jnp.stack([pos1, pos2] + [zero] * 6, axis=0).astype(I32)
    slot_id = jnp.arange(MAX_TILE_CHUNKS, dtype=I32)
    owner = jnp.sum((lbase + nch)[:, None, :] <= slot_id[None, :, None], axis=2)
    own = jnp.minimum(owner, N_EXPERTS - 1)[:, :, None] == ex[None, None, :]
    tab = jnp.sum(jnp.where(own, (gpos - lbase)[:, None, :], 0), axis=2) + slot_id[None, :]
    tab = tab.astype(I32).reshape(-1)
    fill = jnp.concatenate([gstart + tot, reg - tot, gend[-1:] // cpb]).astype(I32)
    xb = _dispatch(h, pos_rows, tab, lused, fill, n_blocks)
    yb = _expert_ffn(xb, (gstart // cpb).astype(I32), (reg // cpb).astype(I32), n_valid,
                     (gend[-1:] // cpb).astype(I32), w1, w3, w2, layer)
    return _combine(yb, pos_rows.T, rg.T, tab, lused, x, mod, n_lat_tiles, n_tiles)


def _router_matrix(w_rg, b_rg, w_re, b_re):
    d = w_rg.shape[0]
    wr = jnp.zeros((LANES, d), F32)
    wr = wr.at[0:N_GROUPS].set(w_rg.T.astype(F32))
    wr = wr.at[ROUTER_EXPERT_ROW0:ROUTER_EXPERT_ROW0 + N_EXPERTS].set(w_re.T.astype(F32))
    br = jnp.zeros((LANES,), F32)
    br = br.at[0:N_GROUPS].set(b_rg.astype(F32))
    br = br.at[ROUTER_EXPERT_ROW0:ROUTER_EXPERT_ROW0 + N_EXPERTS].set(b_re.astype(F32))
    return wr, jnp.broadcast_to(br[:, None], (LANES, LANES))


def _dup_heads(w):
    d = w.shape[0]
    w4 = w.reshape(d, N_KV_HEADS, 1, HEAD_DIM)
    return jnp.broadcast_to(w4, (d, N_KV_HEADS, 2, HEAD_DIM)).reshape(d, 2 * N_KV_HEADS * HEAD_DIM)


def kernel(x, c, ctx, c_ctx, w_mod, b_mod, norm_mix_g, norm_ffn_g, w_in_even, conv_w, w_out_even, w_qkv, q_norm_g,
           k_norm_g, sink_logit, w_o, w_router_g, b_router_g, w_router_e, b_router_e, w1, w3, w2):
    bsz, l, d = x.shape
    lc = ctx.shape[1]
    assert bsz == 1, "one sample per call"
    tm = ROW_TILE
    assert l % tm == 0 and lc % tm == 0 and l % lc == 0 and l % (DFT_MINOR * 8) == 0
    depth = w_mod.shape[0]
    t = l + lc
    assert t % DFT_MINOR == 0
    nl, nt = l // tm, t // tm

    xs = jnp.concatenate([x.reshape(l, d), ctx.reshape(lc, d)], axis=0)
    mod_all = _modulation(c, c_ctx, w_mod, b_mod).reshape(depth, 2, 6, d)
    cs, m1, a_tab, mc = _dft_tables(l, lc)
    rope = _rope_tables(l, t)
    segs = _segment_matrices()
    attn_bias = _attn_bias(l)
    tri = jnp.asarray(np.triu(np.ones((tm, tm), np.float32), 1)).astype(BF16)
    qd = N_HEADS * HEAD_DIM

    for layer in range(depth):
        last = layer == depth - 1
        j = layer // 2
        mod = mod_all[layer]
        g_mix = norm_mix_g[layer].reshape(1, d)
        g_ffn = norm_ffn_g[layer].reshape(1, d)
        if layer % 2 == 0:
            z, bg, u = _even_in(xs, g_mix, mod, w_in_even[j].astype(BF16), cs, nl)
            f_lat, f_ctx = _fourier_seq(z, l, m1, a_tab, mc)
            xs = _even_out(f_lat, f_ctx, bg, u, conv_w[j], w_out_even[j].astype(BF16), xs, mod, nl)
        else:
            wq = w_qkv[j]
            w_all = jnp.concatenate([wq[:, :qd], _dup_heads(wq[:, qd:qd + N_KV_HEADS * HEAD_DIM]),
                                     _dup_heads(wq[:, qd + N_KV_HEADS * HEAD_DIM:])], axis=1).astype(BF16)
            qg = jnp.tile(q_norm_g[j], N_HEADS).reshape(1, qd)
            kg = jnp.tile(k_norm_g[j], 2 * N_KV_HEADS).reshape(1, 2 * N_KV_HEADS * HEAD_DIM)
            q, k, v = _odd_in(xs, g_mix, mod, w_all, qg, kg, segs, rope, nl, nt)
            n_out = nl if last else nt
            att = _attention(q, k, v, sink_logit[j].astype(F32) * LOG2E, attn_bias, l,
                             n_out * (tm // ATT_BLOCK))
            xs = _odd_out(att, w_o[j].astype(BF16), xs, mod, nl, n_out)
        wr, br = _router_matrix(w_router_g[layer], b_router_g[layer], w_router_e[layer], b_router_e[layer])
        n_moe = nl if last else nt
        xs = _moe_layer(xs, g_ffn, mod, wr, br, tri, w1, w3, w2, layer, nl, n_moe)
    return xs[:l].reshape(bsz, l, d)
```

```python
import functools
import math

import numpy as np
import jax
import jax.numpy as jnp
from jax import lax
from jax.experimental import pallas as pl
from jax.experimental.pallas import tpu as pltpu

F32 = jnp.float32
BF16 = jnp.bfloat16
I32 = jnp.int32

EPS = 1e-6
NEG_INF = -1e30

GRID_W = 64
FOURIER_GROUPS = 4
FOURIER_GROUP_DIM = 128
FOURIER_WIDTH = FOURIER_GROUPS * FOURIER_GROUP_DIM
CONV_WIDTH = 512
N_HEADS = 16
N_KV_HEADS = 4
GQA_GROUP = N_HEADS // N_KV_HEADS
HEAD_DIM = 64
WINDOW = 128
ROPE_BASE = 10000.0
ROPE_PAIRS = HEAD_DIM // 4
N_GROUPS = 4
EXPERTS_PER_GROUP = 8
N_EXPERTS = N_GROUPS * EXPERTS_PER_GROUP

LANES = 128
ROW_TILE = 256
ATT_BLOCK = 128
MOE_ROWS = 256
DFT_MINOR = 128
VMEM_LIMIT = 48 * 1024 * 1024

_HI = lax.Precision.HIGHEST
LOG2E = math.log2(math.e)


def _cparams(sem):
    return pltpu.CompilerParams(dimension_semantics=sem, vmem_limit_bytes=VMEM_LIMIT)


def _split_bf16(x):
    hi = x.astype(BF16)
    lo = (x - hi.astype(F32)).astype(BF16)
    return hi, lo


def _mod_kernel(ct_ref, w_ref, b_ref, o_ref):
    ct = ct_ref[...]
    s = ct * jax.nn.sigmoid(ct)
    w = w_ref[...]
    r0 = jnp.sum(w * s[:, 0:1], axis=0, keepdims=True)
    r1 = jnp.sum(w * s[:, 1:2], axis=0, keepdims=True)
    o_ref[...] = jnp.concatenate([r0, r1], axis=0) + b_ref[...]


def _modulation(c, c_ctx, w_mod, b_mod):
    depth, d, n = w_mod.shape
    tn = 512
    ct = jnp.stack([c.reshape(d), c_ctx.reshape(d)], axis=1)
    return pl.pallas_call(
        _mod_kernel,
        grid=(depth, n // tn),
        in_specs=[
            pl.BlockSpec((d, 2), lambda l, j: (0, 0)),
            pl.BlockSpec((None, d, tn), lambda l, j: (l, 0, j)),
            pl.BlockSpec((None, 1, tn), lambda l, j: (l, 0, j)),
        ],
        out_specs=pl.BlockSpec((None, 2, tn), lambda l, j: (l, 0, j)),
        out_shape=jax.ShapeDtypeStruct((depth, 2, n), F32),
        compiler_params=_cparams(("arbitrary", "arbitrary")),
        name="modulation",
    )(ct, w_mod, b_mod.reshape(depth, 1, n))


def _norm_mod(x, g, shift, scale):
    ms = jnp.mean(x * x, axis=-1, keepdims=True)
    y = x * lax.rsqrt(ms + EPS) * g
    return y * (1.0 + scale) + shift


def _stream_of(n_lat_tiles):
    return lambda i: (jnp.where(i >= n_lat_tiles, 1, 0), 0, 0)


def _even_in_kernel(x_ref, g_ref, mod_ref, w_ref, cs_ref, z_ref, bg_ref, u_ref):
    h = _norm_mod(x_ref[...], g_ref[...], mod_ref[0:1, :], mod_ref[1:2, :])
    p = jnp.dot(h.astype(BF16), w_ref[...], preferred_element_type=F32)
    cs = cs_ref[...]
    for grp in range(FOURIER_GROUPS):
        lo = grp * FOURIER_GROUP_DIM
        a = p[:, lo:lo + FOURIER_GROUP_DIM].astype(BF16)
        z = jnp.dot(a, cs, preferred_element_type=F32)
        z_ref[0, :, lo:lo + FOURIER_GROUP_DIM] = z[:, :FOURIER_GROUP_DIM]
        z_ref[1, :, lo:lo + FOURIER_GROUP_DIM] = z[:, FOURIER_GROUP_DIM:]
    o = FOURIER_WIDTH
    bg_ref[...] = p[:, o:o + CONV_WIDTH].astype(BF16)
    u_ref[...] = (p[:, o + CONV_WIDTH:o + 2 * CONV_WIDTH] * p[:, o + 2 * CONV_WIDTH:]).astype(BF16)


def _even_in(x, g, mod, w_in, cs, n_lat_tiles):
    t, d = x.shape
    n = w_in.shape[1]
    tm = ROW_TILE
    return pl.pallas_call(
        _even_in_kernel,
        grid=(t // tm,),
        in_specs=[
            pl.BlockSpec((tm, d), lambda i: (i, 0)),
            pl.BlockSpec((1, d), lambda i: (0, 0)),
            pl.BlockSpec((None, 6, d), _stream_of(n_lat_tiles)),
            pl.BlockSpec((d, n), lambda i: (0, 0)),
            pl.BlockSpec(cs.shape, lambda i: (0, 0)),
        ],
        out_specs=[
            pl.BlockSpec((2, tm, FOURIER_WIDTH), lambda i: (0, i, 0)),
            pl.BlockSpec((tm, CONV_WIDTH), lambda i: (i, 0)),
            pl.BlockSpec((tm, CONV_WIDTH), lambda i: (i, 0)),
        ],
        out_shape=[
            jax.ShapeDtypeStruct((2, t, FOURIER_WIDTH), F32),
            jax.ShapeDtypeStruct((t, CONV_WIDTH), BF16),
            jax.ShapeDtypeStruct((t, CONV_WIDTH), BF16),
        ],
        compiler_params=_cparams(("arbitrary",)),
        name="even_in_proj",
    )(x, g, mod, w_in, cs)


def _dft_major_kernel(m_ref, z_ref, o_ref):
    m = m_ref[...]
    n1, c = z_ref.shape[1], z_ref.shape[3]
    for j in range(z_ref.shape[2]):
        zc = z_ref[:, :, j, :].reshape(2 * n1, c).astype(BF16)
        g = jnp.dot(m, zc, preferred_element_type=F32)
        o_ref[:, :, j, :] = g.reshape(2, n1, c)


def _dft_major(m1, z4, n1):
    c = z4.shape[-1]
    cb = 8
    return pl.pallas_call(
        _dft_major_kernel,
        grid=(DFT_MINOR // cb,),
        in_specs=[pl.BlockSpec(m1.shape, lambda j: (0, 0)), pl.BlockSpec((2, n1, cb, c), lambda j: (0, 0, j, 0))],
        out_specs=pl.BlockSpec((2, n1, cb, c), lambda j: (0, 0, j, 0)),
        out_shape=jax.ShapeDtypeStruct((2, n1, DFT_MINOR, c), F32),
        compiler_params=_cparams(("arbitrary",)),
        name="dft_major",
    )(m1, z4)


def _dft_minor_kernel(a_ref, g_ref, o_ref):
    for j in range(a_ref.shape[0]):
        gcat = jnp.concatenate([g_ref[0, j], g_ref[1, j]], axis=0).astype(BF16)
        o_ref[:, j, :] = jnp.dot(a_ref[j], gcat, preferred_element_type=F32)


def _dft_minor(a_tab, g4):
    n1, m, k2 = a_tab.shape
    c = g4.shape[-1]
    kb = 8
    return pl.pallas_call(
        _dft_minor_kernel,
        grid=(n1 // kb,),
        in_specs=[
            pl.BlockSpec((kb, m, k2), lambda i: (i, 0, 0)),
            pl.BlockSpec((2, kb, DFT_MINOR, c), lambda i: (0, i, 0, 0)),
        ],
        out_specs=pl.BlockSpec((DFT_MINOR, kb, c), lambda i: (0, i, 0)),
        out_shape=jax.ShapeDtypeStruct((DFT_MINOR, n1, c), F32),
        compiler_params=_cparams(("arbitrary",)),
        name="dft_minor",
    )(a_tab, g4)


def _dft_ctx_kernel(a_ref, z_ref, o_ref):
    o_ref[...] = jnp.dot(a_ref[...], z_ref[...].astype(BF16), preferred_element_type=F32)


def _dft_ctx(mc, zc):
    lc, c = mc.shape[0], zc.shape[1]
    return pl.pallas_call(
        _dft_ctx_kernel,
        grid=(1,),
        in_specs=[pl.BlockSpec(mc.shape, lambda i: (0, 0)), pl.BlockSpec(zc.shape, lambda i: (0, 0))],
        out_specs=pl.BlockSpec((lc, c), lambda i: (0, 0)),
        out_shape=jax.ShapeDtypeStruct((lc, c), F32),
        compiler_params=_cparams(("arbitrary",)),
        name="dft_context",
    )(mc, zc)


def _dft_tables(l, lc):
    gd = FOURIER_GROUP_DIM
    kk = np.arange(gd)
    ang = 2.0 * np.pi * ((kk[:, None] * kk[None, :]) % gd) / gd
    cs = np.concatenate([np.cos(ang), -np.sin(ang)], axis=1)
    n1 = l // DFT_MINOR
    k1 = np.arange(n1)
    ang1 = 2.0 * np.pi * ((k1[:, None] * k1[None, :]) % n1) / n1
    c1, s1 = np.cos(ang1), np.sin(ang1)
    m1 = np.block([[c1, s1], [-s1, c1]])
    l2 = np.arange(DFT_MINOR)
    kfull = k1[:, None, None] + n1 * l2[None, :, None]
    ang2 = 2.0 * np.pi * ((kfull * l2[None, None, :]) % l) / l
    sc = 1.0 / math.sqrt(l * gd)
    a_tab = np.concatenate([np.cos(ang2), np.sin(ang2)], axis=2) * sc
    kc = np.arange(lc)
    angc = 2.0 * np.pi * ((kc[:, None] * kc[None, :]) % lc) / lc
    mc = np.concatenate([np.cos(angc), np.sin(angc)], axis=1) / math.sqrt(lc * gd)
    as_bf16 = lambda v: jnp.asarray(v, F32).astype(BF16)
    return as_bf16(cs), as_bf16(m1), as_bf16(a_tab), as_bf16(mc)


def _fourier_seq(z, l, m1, a_tab, mc):
    t, c = z.shape[1], z.shape[2]
    lc = t - l
    n1 = l // DFT_MINOR
    g4 = _dft_major(m1, z.reshape(2, t // DFT_MINOR, DFT_MINOR, c), n1)
    f_lat = _dft_minor(a_tab, g4).reshape(l, c)
    return f_lat, _dft_ctx(mc, z[:, l:].reshape(2 * lc, c))


ROUTER_EXPERT_ROW0 = 8


def _route_tile(x, g_ref, mod_ref, wr_ref, br_ref, tri_ref, h_ref, ri_ref, rg_ref, cnt_ref):
    h = _norm_mod(x, g_ref[...], mod_ref[3:4, :], mod_ref[4:5, :])
    h_ref[...] = h.astype(BF16)
    tm = h.shape[0]
    logits = lax.dot_general(wr_ref[...], h, (((1,), (1,)), ((), ())), precision=_HI,
                             preferred_element_type=F32) + br_ref[:, 0:1]
    gl = logits[0:N_GROUPS]
    gmax = jnp.max(gl, axis=0, keepdims=True)
    gi = lax.broadcasted_iota(I32, gl.shape, 0)
    g_idx = jnp.min(jnp.where(gl == gmax, gi, N_GROUPS), axis=0, keepdims=True)
    g_val = 1.0 / jnp.sum(jnp.exp(gl - gmax), axis=0, keepdims=True)
    e_in = logits[ROUTER_EXPERT_ROW0:ROUTER_EXPERT_ROW0 + EXPERTS_PER_GROUP]
    for grp in range(1, N_GROUPS):
        lo = ROUTER_EXPERT_ROW0 + grp * EXPERTS_PER_GROUP
        e_in = jnp.where(g_idx == grp, logits[lo:lo + EXPERTS_PER_GROUP], e_in)
    ei = lax.broadcasted_iota(I32, e_in.shape, 0)
    v1 = jnp.max(e_in, axis=0, keepdims=True)
    i1 = jnp.min(jnp.where(e_in == v1, ei, EXPERTS_PER_GROUP), axis=0, keepdims=True)
    rest = jnp.where(ei == i1, -jnp.inf, e_in)
    v2 = jnp.max(rest, axis=0, keepdims=True)
    i2 = jnp.min(jnp.where(rest == v2, ei, EXPERTS_PER_GROUP), axis=0, keepdims=True)
    w2 = jnp.exp(v2 - v1)
    gate1 = g_val / (1.0 + w2)
    gate2 = g_val * w2 / (1.0 + w2)
    e1 = g_idx * EXPERTS_PER_GROUP + i1
    e2 = g_idx * EXPERTS_PER_GROUP + i2
    xi = lax.broadcasted_iota(I32, (N_EXPERTS, tm), 0)
    oh1 = xi == e1
    oh2 = xi == e2
    oh = oh1.astype(F32) + oh2.astype(F32)
    before = jnp.dot(oh.astype(BF16), tri_ref[...], preferred_element_type=F32)
    rank1 = jnp.sum(jnp.where(oh1, before, 0.0), axis=0, keepdims=True)
    rank2 = jnp.sum(jnp.where(oh2, before, 0.0), axis=0, keepdims=True)
    cnt_ref[...] = jnp.broadcast_to(jnp.sum(oh, axis=1, keepdims=True), cnt_ref.shape)
    orow = lax.broadcasted_iota(I32, (8, tm), 0)
    ri_ref[...] = jnp.where(orow == 0, e1, jnp.where(orow == 1, e2, jnp.where(
        orow == 2, rank1.astype(I32), jnp.where(orow == 3, rank2.astype(I32), 0))))
    rg_ref[...] = jnp.where(orow == 0, gate1, jnp.where(orow == 1, gate2, 0.0))


def _route_specs(route, d, n_tiles):
    tm = ROW_TILE
    const = lambda a: pl.BlockSpec(a.shape, lambda i: (0,) * a.ndim)
    in_specs = [const(a) for a in route]
    out_specs = [
        pl.BlockSpec((tm, d), lambda i: (i, 0)),
        pl.BlockSpec((8, tm), lambda i: (0, i)),
        pl.BlockSpec((8, tm), lambda i: (0, i)),
        pl.BlockSpec((None, N_EXPERTS, LANES), lambda i: (i, 0, 0)),
    ]
    rows = n_tiles * tm
    out_shape = [
        jax.ShapeDtypeStruct((rows, d), BF16),
        jax.ShapeDtypeStruct((8, rows), I32),
        jax.ShapeDtypeStruct((8, rows), F32),
        jax.ShapeDtypeStruct((n_tiles, N_EXPERTS, LANES), F32),
    ]
    return in_specs, out_specs, out_shape


def _even_out_kernel(fl_ref, fc_ref, bg_ref, u_ref, up_ref, un_ref, cw_ref, w_ref, x_ref, mod_ref,
                     gf_ref, wr_ref, br_ref, tri_ref, o_ref, h_ref, ri_ref, rg_ref, cnt_ref, *, n_lat_tiles, n_tiles):
    i = pl.program_id(0)
    tm = u_ref.shape[0]
    u = u_ref[...].astype(F32)
    row = lax.broadcasted_iota(I32, u.shape, 0)
    first = jnp.logical_or(i == 0, i == n_lat_tiles)
    last = jnp.logical_or(i == n_lat_tiles - 1, i == n_tiles - 1)
    hb = up_ref.shape[0]
    halo_p = up_ref[...].astype(F32)[hb - 1:hb, :] * jnp.where(first, 0.0, 1.0)
    halo_n = un_ref[...].astype(F32)[0:1, :] * jnp.where(last, 0.0, 1.0)
    u_prev = jnp.where(row == 0, halo_p, pltpu.roll(u, 1, axis=0))
    u_next = jnp.where(row == tm - 1, halo_n, pltpu.roll(u, tm - 1, axis=0))
    cw = cw_ref[...]
    y = bg_ref[...].astype(F32) * (cw[0:1, :] * u_prev + cw[1:2, :] * u + cw[2:3, :] * u_next)
    is_ctx = jnp.full(fl_ref.shape, i, I32) >= n_lat_tiles
    f = jnp.where(is_ctx, fc_ref[...], fl_ref[...])
    acc = jnp.dot(f.astype(BF16), w_ref[0:FOURIER_WIDTH, :], preferred_element_type=F32)
    acc += jnp.dot(y.astype(BF16), w_ref[FOURIER_WIDTH:, :], preferred_element_type=F32)
    x_new = x_ref[...] + mod_ref[2:3, :] * acc
    o_ref[...] = x_new
    _route_tile(x_new, gf_ref, mod_ref, wr_ref, br_ref, tri_ref, h_ref, ri_ref, rg_ref, cnt_ref)


def _even_out(f_lat, f_ctx, bg, u, conv_w, w_out, x, mod, route, n_lat_tiles):
    t, d = x.shape
    tm = ROW_TILE
    hb = 16
    n_tiles = t // tm
    r = tm // hb
    r_in, r_out, r_shape = _route_specs(route, d, n_tiles)
    return pl.pallas_call(
        functools.partial(_even_out_kernel, n_lat_tiles=n_lat_tiles, n_tiles=n_tiles),
        grid=(n_tiles,),
        in_specs=[
            pl.BlockSpec((tm, FOURIER_WIDTH), lambda i: (jnp.minimum(i, n_lat_tiles - 1), 0)),
            pl.BlockSpec((tm, FOURIER_WIDTH), lambda i: (jnp.clip(i - n_lat_tiles, 0, n_tiles - n_lat_tiles - 1), 0)),
            pl.BlockSpec((tm, CONV_WIDTH), lambda i: (i, 0)),
            pl.BlockSpec((tm, CONV_WIDTH), lambda i: (i, 0)),
            pl.BlockSpec((hb, CONV_WIDTH), lambda i: (jnp.maximum(i * r - 1, 0), 0)),
            pl.BlockSpec((hb, CONV_WIDTH), lambda i: (jnp.minimum((i + 1) * r, t // hb - 1), 0)),
            pl.BlockSpec(conv_w.shape, lambda i: (0, 0)),
            pl.BlockSpec(w_out.shape, lambda i: (0, 0)),
            pl.BlockSpec((tm, d), lambda i: (i, 0)),
            pl.BlockSpec((None, 6, d), _stream_of(n_lat_tiles)),
        ] + r_in,
        out_specs=[pl.BlockSpec((tm, d), lambda i: (i, 0))] + r_out,
        out_shape=[jax.ShapeDtypeStruct((t, d), F32)] + r_shape,
        compiler_params=_cparams(("arbitrary",)),
        name="even_out_proj",
    )(f_lat, f_ctx, bg, u, u, u, conv_w, w_out, x, mod, *route)


def _seg_rms_scale(v, seg, seg_t):
    hi, lo = _split_bf16(v * v)
    ss = jnp.dot(hi, seg, preferred_element_type=F32) + jnp.dot(lo, seg, preferred_element_type=F32)
    inv = lax.rsqrt(ss * (1.0 / HEAD_DIM) + EPS)
    ihi, ilo = _split_bf16(inv)
    return (jnp.dot(ihi, seg_t, preferred_element_type=F32)
            + jnp.dot(ilo, seg_t, preferred_element_type=F32))


def _rope_cols(v, cos, sa, sb, scale):
    cols = []
    for j in range(v.shape[1] // LANES):
        c = v[:, j * LANES:(j + 1) * LANES]
        r = c * cos + pltpu.roll(c, LANES - ROPE_PAIRS, axis=1) * sa + pltpu.roll(c, ROPE_PAIRS, axis=1) * sb
        cols.append(r * scale if scale != 1.0 else r)
    return jnp.concatenate(cols, axis=1)


def _odd_in_kernel(x_ref, g_ref, mod_ref, w_ref, qg_ref, kg_ref, segq_ref, segqt_ref, segk_ref, segkt_ref,
                   rope_ref, q_ref, k_ref, v_ref):
    h = _norm_mod(x_ref[...], g_ref[...], mod_ref[0:1, :], mod_ref[1:2, :])
    p = jnp.dot(h.astype(BF16), w_ref[...], preferred_element_type=F32)
    qd = N_HEADS * HEAD_DIM
    kd = 2 * N_KV_HEADS * HEAD_DIM
    cos, sa, sb = rope_ref[0], rope_ref[1], rope_ref[2]
    q = p[:, :qd]
    q = q * _seg_rms_scale(q, segq_ref[...], segqt_ref[...]) * qg_ref[...]
    q_ref[...] = _rope_cols(q, cos, sa, sb, HEAD_DIM ** -0.5 * LOG2E).astype(BF16)
    k = p[:, qd:qd + kd]
    k = k * _seg_rms_scale(k, segk_ref[...], segkt_ref[...]) * kg_ref[...]
    k_ref[...] = _rope_cols(k, cos, sa, sb, 1.0).astype(BF16)
    v_ref[...] = p[:, qd + kd:].astype(BF16)


def _odd_in(x, g, mod, w_qkv, qg, kg, segs, rope, n_lat_tiles, n_tiles):
    t, d = x.shape
    n = w_qkv.shape[1]
    tm = ROW_TILE
    rows = n_tiles * tm
    qd = N_HEADS * HEAD_DIM
    kd = 2 * N_KV_HEADS * HEAD_DIM
    segq, segqt, segk, segkt = segs
    const = lambda a: pl.BlockSpec(a.shape, lambda i: (0,) * a.ndim)
    return pl.pallas_call(
        _odd_in_kernel,
        grid=(n_tiles,),
        in_specs=[
            pl.BlockSpec((tm, d), lambda i: (i, 0)),
            const(g),
            pl.BlockSpec((None, 6, d), _stream_of(n_lat_tiles)),
            const(w_qkv), const(qg), const(kg), const(segq), const(segqt), const(segk), const(segkt),
            pl.BlockSpec((3, tm, LANES), lambda i: (0, i, 0)),
        ],
        out_specs=[
            pl.BlockSpec((tm, qd), lambda i: (i, 0)),
            pl.BlockSpec((tm, kd), lambda i: (i, 0)),
            pl.BlockSpec((tm, kd), lambda i: (i, 0)),
        ],
        out_shape=[
            jax.ShapeDtypeStruct((rows, qd), BF16),
            jax.ShapeDtypeStruct((rows, kd), BF16),
            jax.ShapeDtypeStruct((rows, kd), BF16),
        ],
        compiler_params=_cparams(("arbitrary",)),
        name="odd_in_proj",
    )(x, g, mod, w_qkv, qg, kg, segq, segqt, segk, segkt, rope)


def _rope_tables(l, t):
    pos = np.arange(l)
    freqs = ROPE_BASE ** (-np.arange(ROPE_PAIRS, dtype=np.float32) / ROPE_PAIRS)
    lane = np.arange(LANES) % HEAD_DIM
    axis = lane // (2 * ROPE_PAIRS)
    half = (lane % (2 * ROPE_PAIRS)) // ROPE_PAIRS
    pair = lane % ROPE_PAIRS
    p = np.where(axis[None, :] == 0, (pos // GRID_W)[:, None], (pos % GRID_W)[:, None]).astype(np.float32)
    ang = p * freqs[pair][None, :].astype(np.float32)
    cos, sin = np.cos(ang), np.sin(ang)
    sa = np.where(half[None, :] == 0, -sin, 0.0)
    sb = np.where(half[None, :] == 1, sin, 0.0)
    tab = np.zeros((3, t, LANES), np.float32)
    tab[0, :l], tab[1, :l], tab[2, :l] = cos, sa, sb
    tab[0, l:] = 1.0
    return jnp.asarray(tab)


def _segment_matrices():
    def seg(width):
        m = np.zeros((width, LANES), np.float32)
        m[np.arange(width), np.arange(width) // HEAD_DIM] = 1.0
        return m
    sq, sk = seg(N_HEADS * HEAD_DIM), seg(2 * N_KV_HEADS * HEAD_DIM)
    b = lambda v: jnp.asarray(v).astype(BF16)
    return b(sq), b(sq.T), b(sk), b(sk.T)


def _attn_kernel(sink_ref, q_ref, kp_ref, kc_ref, kn_ref, vp_ref, vc_ref, vn_ref, kx_ref, vx_ref, bias_ref, o_ref):
    bq = q_ref.shape[0]
    low = lax.broadcasted_iota(I32, (bq, LANES), 1) < HEAD_DIM
    top = lax.broadcasted_iota(I32, (LANES, bq), 0) < HEAD_DIM
    bias = bias_ref[...]
    nt = (((1,), (1,)), ((), ()))
    tn = (((0,), (0,)), ((), ()))
    for g in range(N_KV_HEADS):
        ks = slice(g * LANES, (g + 1) * LANES)
        parts, sinks = [], []
        for j in range(GQA_GROUP):
            col = 2 * g + j // 2
            c = q_ref[:, col * LANES:(col + 1) * LANES]
            keep = low if j % 2 == 0 else jnp.logical_not(low)
            parts.append(jnp.where(keep, c, jnp.zeros_like(c)))
            sinks.append(jnp.full((1, bq), sink_ref[g * GQA_GROUP + j], F32))
        qs = jnp.concatenate(parts, axis=0)
        sink = jnp.concatenate(sinks, axis=1)
        kwin = jnp.concatenate([kp_ref[:, ks], kc_ref[:, ks], kn_ref[:, ks]], axis=0)
        vwin = jnp.concatenate([vp_ref[:, ks], vc_ref[:, ks], vn_ref[:, ks]], axis=0)
        s_loc = lax.dot_general(kwin, qs, nt, preferred_element_type=F32) + bias
        s_ctx = lax.dot_general(kx_ref[:, ks], qs, nt, preferred_element_type=F32)
        m = jnp.maximum(jnp.maximum(jnp.max(s_loc, axis=0, keepdims=True),
                                    jnp.max(s_ctx, axis=0, keepdims=True)), sink)
        p_loc = jnp.exp2(s_loc - m)
        p_ctx = jnp.exp2(s_ctx - m)
        den = (jnp.sum(p_loc, axis=0, keepdims=True) + jnp.sum(p_ctx, axis=0, keepdims=True)
               + jnp.exp2(sink - m))
        ot = lax.dot_general(vwin, p_loc.astype(BF16), tn, preferred_element_type=F32)
        ot += lax.dot_general(vx_ref[:, ks], p_ctx.astype(BF16), tn, preferred_element_type=F32)
        ot = ot * (1.0 / den)
        t0 = jnp.where(top, ot[:, 0:bq], ot[:, bq:2 * bq])
        t1 = jnp.where(top, ot[:, 2 * bq:3 * bq], ot[:, 3 * bq:4 * bq])
        o_ref[:, 2 * g * LANES:(2 * g + 1) * LANES] = t0.T.astype(o_ref.dtype)
        o_ref[:, (2 * g + 1) * LANES:(2 * g + 2) * LANES] = t1.T.astype(o_ref.dtype)


def _attn_bias(l):
    bq = ATT_BLOCK
    r = np.arange(GQA_GROUP * bq)[None, :] % bq
    col = np.arange(3 * bq)[:, None]
    band = np.abs(col - bq - r) <= WINDOW
    no_prev, no_next = col >= bq, col < 2 * bq
    masks = [band, band & no_prev, band & no_next, band & no_prev & no_next, np.zeros_like(band)]
    return jnp.asarray(np.where(np.stack(masks), 0.0, NEG_INF).astype(np.float32))


def _attention(q, k, v, sink, bias, l, n_q_blocks):
    t = q.shape[0]
    bq = ATT_BLOCK
    nlb = l // bq
    lc = t - l
    ctx_blk = l // lc
    kw = k.shape[1]
    kspec = lambda f: pl.BlockSpec((bq, kw), f)
    prev = lambda b, s: (jnp.clip(b - 1, 0, nlb - 1), 0)
    cur = lambda b, s: (jnp.minimum(b, nlb - 1), 0)
    nxt = lambda b, s: (jnp.clip(b + 1, 0, nlb - 1), 0)
    ctx = pl.BlockSpec((lc, kw), lambda b, s: (ctx_blk, 0))
    kind = lambda b, s: (jnp.where(b >= nlb, 4, (b == 0).astype(I32) + 2 * (b == nlb - 1).astype(I32)), 0, 0)
    return pl.pallas_call(
        _attn_kernel,
        grid_spec=pltpu.PrefetchScalarGridSpec(
            num_scalar_prefetch=1,
            grid=(n_q_blocks,),
            in_specs=[
                pl.BlockSpec((bq, q.shape[1]), lambda b, s: (b, 0)),
                kspec(prev), kspec(cur), kspec(nxt), kspec(prev), kspec(cur), kspec(nxt), ctx, ctx,
                pl.BlockSpec((None,) + bias.shape[1:], kind),
            ],
            out_specs=pl.BlockSpec((bq, q.shape[1]), lambda b, s: (b, 0)),
        ),
        out_shape=jax.ShapeDtypeStruct((n_q_blocks * bq, q.shape[1]), BF16),
        compiler_params=_cparams(("arbitrary",)),
        name="window_attention",
    )(sink, q, k, k, k, v, v, v, k, v, bias)


def _odd_out_kernel(a_ref, w_ref, x_ref, mod_ref, gf_ref, wr_ref, br_ref, tri_ref, o_ref, h_ref, ri_ref, rg_ref,
                    cnt_ref):
    acc = jnp.dot(a_ref[...], w_ref[...], preferred_element_type=F32)
    x_new = x_ref[...] + mod_ref[2:3, :] * acc
    o_ref[...] = x_new
    _route_tile(x_new, gf_ref, mod_ref, wr_ref, br_ref, tri_ref, h_ref, ri_ref, rg_ref, cnt_ref)


def _odd_out(a, w_o, x, mod, route, n_lat_tiles, n_tiles):
    d = x.shape[1]
    tm = ROW_TILE
    r_in, r_out, r_shape = _route_specs(route, d, n_tiles)
    return pl.pallas_call(
        _odd_out_kernel,
        grid=(n_tiles,),
        in_specs=[
            pl.BlockSpec((tm, a.shape[1]), lambda i: (i, 0)),
            pl.BlockSpec(w_o.shape, lambda i: (0, 0)),
            pl.BlockSpec((tm, d), lambda i: (i, 0)),
            pl.BlockSpec((None, 6, d), _stream_of(n_lat_tiles)),
        ] + r_in,
        out_specs=[pl.BlockSpec((tm, d), lambda i: (i, 0))] + r_out,
        out_shape=[jax.ShapeDtypeStruct((n_tiles * tm, d), F32)] + r_shape,
        compiler_params=_cparams(("arbitrary",)),
        name="odd_out_proj",
    )(a, w_o, x, mod, *route)


CHUNK = 16
TILE_BUF = 2 * ROW_TILE + N_EXPERTS * CHUNK
MAX_TILE_CHUNKS = TILE_BUF // CHUNK


def _chunk_rows(c):
    return pl.ds(pl.multiple_of(c * CHUNK, CHUNK), CHUNK)


def _wait_rows(copy_of_rows, n_chunks):
    bit = 1
    while bit <= MAX_TILE_CHUNKS:
        @pl.when((n_chunks & bit) != 0)
        def _(bit=bit):
            copy_of_rows(bit * CHUNK).wait()
        bit *= 2


def _dispatch_kernel(tab_ref, lused_ref, fill_ref, h_ref, pos_ref, xb_ref, hs, zbuf, sem, fsem, *, n_tiles, n_blocks):
    i = pl.program_id(0)
    slot = i % 2
    tm = h_ref.shape[0]

    def tail_copy(c):
        return pltpu.make_async_copy(zbuf.at[pl.ds(0, CHUNK), :], xb_ref.at[_chunk_rows(c), :], fsem)

    def block_copy(b):
        return pltpu.make_async_copy(zbuf, xb_ref.at[pl.ds(pl.multiple_of(b * MOE_ROWS, MOE_ROWS), MOE_ROWS), :], fsem)

    def fill(start):
        def tail(e, c):
            st, n = fill_ref[e], fill_ref[N_EXPERTS + e]

            def one(c2, cc):
                cp = tail_copy(st + c2)
                cp.start() if start else cp.wait()
                return cc
            return lax.fori_loop(0, n, one, c)
        lax.fori_loop(0, N_EXPERTS, tail, 0)

        def blk(b, c):
            cp = block_copy(b)
            cp.start() if start else cp.wait()
            return c
        lax.fori_loop(fill_ref[2 * N_EXPERTS], n_blocks, blk, 0)

    @pl.when(i == 0)
    def _():
        zbuf[...] = jnp.zeros_like(zbuf)
        fill(True)

    pos = pos_ref[...]
    r = lax.broadcasted_iota(I32, (TILE_BUF, tm), 0)
    onehot = jnp.where(jnp.logical_or(r == pos[0:1, :], r == pos[1:2, :]), 1.0, 0.0).astype(BF16)
    hs[slot] = jnp.dot(onehot, h_ref[...], preferred_element_type=F32).astype(BF16)

    def chunk_copy(sl, src, dst):
        return pltpu.make_async_copy(hs.at[sl, _chunk_rows(src), :], xb_ref.at[_chunk_rows(dst), :], sem.at[sl])

    base = i * MAX_TILE_CHUNKS

    def per_chunk(c, cc):
        chunk_copy(slot, c, tab_ref[base + c]).start()
        return cc
    lax.fori_loop(0, lused_ref[i], per_chunk, 0)

    def wait_chunks(sl, n):
        _wait_rows(lambda rows: pltpu.make_async_copy(hs.at[sl, pl.ds(0, rows), :], xb_ref.at[pl.ds(0, rows), :],
                                                      sem.at[sl]), n)

    @pl.when(i > 0)
    def _():
        wait_chunks(1 - slot, lused_ref[jnp.maximum(i - 1, 0)])

    @pl.when(i == n_tiles - 1)
    def _():
        wait_chunks(slot, lused_ref[i])
        fill(False)


def _dispatch(h, pos_rows, tab, lused, fill, n_blocks):
    t, d = h.shape
    tm = ROW_TILE
    n_tiles = t // tm
    return pl.pallas_call(
        functools.partial(_dispatch_kernel, n_tiles=n_tiles, n_blocks=n_blocks),
        grid_spec=pltpu.PrefetchScalarGridSpec(
            num_scalar_prefetch=3,
            grid=(n_tiles,),
            in_specs=[
                pl.BlockSpec((tm, d), lambda i, *_: (i, 0)),
                pl.BlockSpec((8, tm), lambda i, *_: (0, i)),
            ],
            out_specs=pl.BlockSpec(memory_space=pl.ANY),
            scratch_shapes=[
                pltpu.VMEM((2, TILE_BUF, d), BF16),
                pltpu.VMEM((MOE_ROWS, d), BF16),
                pltpu.SemaphoreType.DMA((2,)),
                pltpu.SemaphoreType.DMA(()),
            ],
        ),
        out_shape=jax.ShapeDtypeStruct((n_blocks * MOE_ROWS, d), BF16),
        compiler_params=_cparams(("arbitrary",)),
        name="moe_dispatch",
    )(tab, lused, fill, h, pos_rows)


def _ffn_kernel(b0_ref, nb_ref, nv_ref, fill_ref, xb_ref, w1_ref, w3_ref, w2_ref, yb_ref,
                w1s, w3s, w2s, xbuf, ybuf, zbuf, sem_in, sem_out, fsem, *, n_blocks):
    e = pl.program_id(0)
    w1s[...] = w1_ref[...].astype(BF16)
    w3s[...] = w3_ref[...].astype(BF16)
    w2s[...] = w2_ref[...].astype(BF16)
    b0, nb = b0_ref[e], nb_ref[e]

    def rows(b):
        return pl.ds(pl.multiple_of(b * MOE_ROWS, MOE_ROWS), MOE_ROWS)

    def x_copy(b, sl):
        return pltpu.make_async_copy(xb_ref.at[rows(b), :], xbuf.at[sl], sem_in.at[sl])

    def y_copy(b, sl):
        return pltpu.make_async_copy(ybuf.at[sl], yb_ref.at[rows(b), :], sem_out.at[sl])

    @pl.when(nb > 0)
    def _():
        x_copy(b0, 0).start()

    def block(j, c):
        sl = j % 2
        x_copy(b0 + j, sl).wait()

        @pl.when(j + 1 < nb)
        def _():
            x_copy(b0 + j + 1, 1 - sl).start()

        @pl.when(j >= 2)
        def _():
            y_copy(b0 + j - 2, sl).wait()

        x = xbuf[sl]
        row = lax.broadcasted_iota(I32, x.shape, 0)
        xb = jnp.where(row < nv_ref[b0 + j], x, jnp.zeros_like(x))
        a = jnp.dot(xb, w1s[...], preferred_element_type=F32)
        b = jnp.dot(xb, w3s[...], preferred_element_type=F32)
        hid = (a * jax.nn.sigmoid(a) * b).astype(BF16)
        ybuf[sl] = jnp.dot(hid, w2s[...], preferred_element_type=F32).astype(BF16)
        y_copy(b0 + j, sl).start()
        return c
    lax.fori_loop(0, nb, block, 0)

    @pl.when(nb >= 2)
    def _():
        y_copy(b0 + nb - 2, nb % 2).wait()

    @pl.when(nb >= 1)
    def _():
        y_copy(b0 + nb - 1, (nb - 1) % 2).wait()

    @pl.when(e == N_EXPERTS - 1)
    def _():
        zbuf[...] = jnp.zeros_like(zbuf)

        def z_copy(b):
            return pltpu.make_async_copy(zbuf, yb_ref.at[rows(b), :], fsem)

        def start(b, c):
            z_copy(b).start()
            return c

        def wait(b, c):
            z_copy(b).wait()
            return c
        lax.fori_loop(fill_ref[0], n_blocks, start, 0)
        lax.fori_loop(fill_ref[0], n_blocks, wait, 0)


def _expert_ffn(xb, blk_start, blk_count, n_valid, used_blocks, w1, w3, w2, layer):
    r = xb.shape[0]
    d, f = w1.shape[2], w1.shape[3]
    bm = MOE_ROWS
    return pl.pallas_call(
        functools.partial(_ffn_kernel, n_blocks=r // bm),
        grid_spec=pltpu.PrefetchScalarGridSpec(
            num_scalar_prefetch=4,
            grid=(N_EXPERTS,),
            in_specs=[
                pl.BlockSpec(memory_space=pl.ANY),
                pl.BlockSpec((None, None, d, f), lambda e, *_: (layer, e, 0, 0)),
                pl.BlockSpec((None, None, d, f), lambda e, *_: (layer, e, 0, 0)),
                pl.BlockSpec((None, None, f, d), lambda e, *_: (layer, e, 0, 0)),
            ],
            out_specs=pl.BlockSpec(memory_space=pl.ANY),
            scratch_shapes=[
                pltpu.VMEM((d, f), BF16), pltpu.VMEM((d, f), BF16), pltpu.VMEM((f, d), BF16),
                pltpu.VMEM((2, bm, d), BF16), pltpu.VMEM((2, bm, d), BF16), pltpu.VMEM((bm, d), BF16),
                pltpu.SemaphoreType.DMA((2,)), pltpu.SemaphoreType.DMA((2,)), pltpu.SemaphoreType.DMA(()),
            ],
        ),
        out_shape=jax.ShapeDtypeStruct((r, d), BF16),
        compiler_params=_cparams(("arbitrary",)),
        name="moe_expert_mlp",
    )(blk_start, blk_count, n_valid, used_blocks, xb, w1, w3, w2)


def _combine_kernel(tab_ref, lused_ref, yb_ref, pos_ref, gate_ref, x_ref, mod_ref, o_ref, ys, sem, *, n_tiles):
    i = pl.program_id(0)
    slot = i % 2
    tm = x_ref.shape[0]

    def chunk_copy(sl, src, dst):
        return pltpu.make_async_copy(yb_ref.at[_chunk_rows(src), :], ys.at[sl, _chunk_rows(dst), :], sem.at[sl])

    def fetch(tile, sl):
        base = tile * MAX_TILE_CHUNKS

        def per_chunk(c, cc):
            chunk_copy(sl, tab_ref[base + c], c).start()
            return cc
        lax.fori_loop(0, lused_ref[tile], per_chunk, 0)

    @pl.when(i == 0)
    def _():
        fetch(0, 0)

    @pl.when(i + 1 < n_tiles)
    def _():
        fetch(jnp.minimum(i + 1, n_tiles - 1), 1 - slot)

    _wait_rows(lambda rows: pltpu.make_async_copy(yb_ref.at[pl.ds(0, rows), :], ys.at[slot, pl.ds(0, rows), :],
                                                  sem.at[slot]), lused_ref[i])

    y = ys[slot]
    used = lused_ref[i] * CHUNK
    rowi = lax.broadcasted_iota(I32, y.shape, 0)
    y16 = jnp.where(rowi < used, y, jnp.zeros_like(y))
    pos = pos_ref[...]
    gate = gate_ref[...]
    lane = lax.broadcasted_iota(I32, (tm, TILE_BUF), 1)
    gm = (jnp.where(lane == pos[:, 0:1], gate[:, 0:1], 0.0) + jnp.where(lane == pos[:, 1:2], gate[:, 1:2], 0.0))
    ghi, glo = _split_bf16(gm)
    mix = jnp.dot(ghi, y16, preferred_element_type=F32) + jnp.dot(glo, y16, preferred_element_type=F32)
    o_ref[...] = x_ref[...] + mod_ref[5:6, :] * mix


def _combine(yb, pos_cols, gates, tab, lused, x, mod, n_lat_tiles, n_tiles):
    d = x.shape[1]
    tm = ROW_TILE
    return pl.pallas_call(
        functools.partial(_combine_kernel, n_tiles=n_tiles),
        grid_spec=pltpu.PrefetchScalarGridSpec(
            num_scalar_prefetch=2,
            grid=(n_tiles,),
            in_specs=[
                pl.BlockSpec(memory_space=pl.ANY),
                pl.BlockSpec((tm, 8), lambda i, *_: (i, 0)),
                pl.BlockSpec((tm, 8), lambda i, *_: (i, 0)),
                pl.BlockSpec((tm, d), lambda i, *_: (i, 0)),
                pl.BlockSpec((None, 6, d), lambda i, *_: (jnp.where(i >= n_lat_tiles, 1, 0), 0, 0)),
            ],
            out_specs=pl.BlockSpec((tm, d), lambda i, *_: (i, 0)),
            scratch_shapes=[pltpu.VMEM((2, TILE_BUF, d), BF16), pltpu.SemaphoreType.DMA((2,))],
        ),
        out_shape=jax.ShapeDtypeStruct((n_tiles * tm, d), F32),
        compiler_params=_cparams(("arbitrary",)),
        name="moe_combine",
    )(tab, lused, yb, pos_cols, gates, x, mod)


def _moe_layer(x, routing, mod, w1, w3, w2, layer, n_lat_tiles, n_tiles):
    tm = ROW_TILE
    rows = n_tiles * tm
    cpb = MOE_ROWS // CHUNK
    h, ri, rg, cnt3 = routing
    cnt = cnt3[:, :, 0].astype(I32)
    nch = (cnt + CHUNK - 1) // CHUNK
    lbase = jnp.cumsum(nch, axis=1) - nch
    lused = jnp.sum(nch, axis=1).astype(I32)
    tot = jnp.sum(nch, axis=0)
    reg = (tot + cpb - 1) // cpb * cpb
    gend = jnp.cumsum(reg)
    gstart = gend - reg
    gpos = gstart[None, :] + jnp.cumsum(nch, axis=0) - nch
    rows_max = 2 * rows + n_tiles * N_EXPERTS * (CHUNK - 1) + N_EXPERTS * (MOE_ROWS - CHUNK)
    n_blocks = -(-rows_max // MOE_ROWS)
    ex = jnp.arange(N_EXPERTS, dtype=I32)
    blk0 = jnp.arange(n_blocks, dtype=I32) * cpb
    block_exp = jnp.minimum(jnp.sum((gend[None, :] <= blk0[:, None]).astype(I32), axis=1), N_EXPERTS - 1)
    sel = block_exp[:, None] == ex[None, :]
    tot_b = jnp.sum(jnp.where(sel, tot[None, :], 0), axis=1)
    st_b = jnp.sum(jnp.where(sel, gstart[None, :], 0), axis=1)
    n_valid = jnp.clip((tot_b - (blk0 - st_b)) * CHUNK, 0, MOE_ROWS).astype(I32)
    lb_tok = jnp.repeat(lbase, tm, axis=0)
    at = lambda e: jnp.sum(jnp.where(e[:, None] == ex[None, :], lb_tok, 0), axis=1)
    pos1 = CHUNK * at(ri[0]) + ri[2]
    pos2 = CHUNK * at(ri[1]) + ri[3]
    zero = jnp.zeros_like(pos1)
    pos_rows = jnp.stack([pos1, pos2] + [zero] * 6, axis=0).astype(I32)
    slot_id = jnp.arange(MAX_TILE_CHUNKS, dtype=I32)
    owner = jnp.sum((lbase + nch)[:, None, :] <= slot_id[None, :, None], axis=2)
    own = jnp.minimum(owner, N_EXPERTS - 1)[:, :, None] == ex[None, None, :]
    tab = jnp.sum(jnp.where(own, (gpos - lbase)[:, None, :], 0), axis=2) + slot_id[None, :]
    tab = tab.astype(I32).reshape(-1)
    fill = jnp.concatenate([gstart + tot, reg - tot, gend[-1:] // cpb]).astype(I32)
    xb = _dispatch(h, pos_rows, tab, lused, fill, n_blocks)
    yb = _expert_ffn(xb, (gstart // cpb).astype(I32), (reg // cpb).astype(I32), n_valid,
                     (gend[-1:] // cpb).astype(I32), w1, w3, w2, layer)
    return _combine(yb, pos_rows.T, rg.T, tab, lused, x, mod, n_lat_tiles, n_tiles)


def _router_matrix(w_rg, b_rg, w_re, b_re):
    d = w_rg.shape[0]
    wr = jnp.zeros((LANES, d), F32)
    wr = wr.at[0:N_GROUPS].set(w_rg.T.astype(F32))
    wr = wr.at[ROUTER_EXPERT_ROW0:ROUTER_EXPERT_ROW0 + N_EXPERTS].set(w_re.T.astype(F32))
    br = jnp.zeros((LANES,), F32)
    br = br.at[0:N_GROUPS].set(b_rg.astype(F32))
    br = br.at[ROUTER_EXPERT_ROW0:ROUTER_EXPERT_ROW0 + N_EXPERTS].set(b_re.astype(F32))
    return wr, jnp.broadcast_to(br[:, None], (LANES, LANES))


def _dup_heads(w):
    d = w.shape[0]
    w4 = w.reshape(d, N_KV_HEADS, 1, HEAD_DIM)
    return jnp.broadcast_to(w4, (d, N_KV_HEADS, 2, HEAD_DIM)).reshape(d, 2 * N_KV_HEADS * HEAD_DIM)


def kernel(x, c, ctx, c_ctx, w_mod, b_mod, norm_mix_g, norm_ffn_g, w_in_even, conv_w, w_out_even, w_qkv, q_norm_g,
           k_norm_g, sink_logit, w_o, w_router_g, b_router_g, w_router_e, b_router_e, w1, w3, w2):
    bsz, l, d = x.shape
    lc = ctx.shape[1]
    assert bsz == 1, "one sample per call"
    tm = ROW_TILE
    assert l % tm == 0 and lc % tm == 0 and l % lc == 0 and l % (DFT_MINOR * 8) == 0
    depth = w_mod.shape[0]
    t = l + lc
    assert t % DFT_MINOR == 0
    nl, nt = l // tm, t // tm

    xs = jnp.concatenate([x.reshape(l, d), ctx.reshape(lc, d)], axis=0)
    mod_all = _modulation(c, c_ctx, w_mod, b_mod).reshape(depth, 2, 6, d)
    cs, m1, a_tab, mc = _dft_tables(l, lc)
    rope = _rope_tables(l, t)
    segs = _segment_matrices()
    attn_bias = _attn_bias(l)
    tri = jnp.asarray(np.triu(np.ones((tm, tm), np.float32), 1)).astype(BF16)
    qd = N_HEADS * HEAD_DIM

    for layer in range(depth):
        last = layer == depth - 1
        j = layer // 2
        mod = mod_all[layer]
        g_mix = norm_mix_g[layer].reshape(1, d)
        g_ffn = norm_ffn_g[layer].reshape(1, d)
        wr, br = _router_matrix(w_router_g[layer], b_router_g[layer], w_router_e[layer], b_router_e[layer])
        route = (g_ffn, wr, br, tri)
        if layer % 2 == 0:
            z, bg, u = _even_in(xs, g_mix, mod, w_in_even[j].astype(BF16), cs, nl)
            f_lat, f_ctx = _fourier_seq(z, l, m1, a_tab, mc)
            xs, *routing = _even_out(f_lat, f_ctx, bg, u, conv_w[j], w_out_even[j].astype(BF16), xs, mod, route, nl)
        else:
            wq = w_qkv[j]
            w_all = jnp.concatenate([wq[:, :qd], _dup_heads(wq[:, qd:qd + N_KV_HEADS * HEAD_DIM]),
                                     _dup_heads(wq[:, qd + N_KV_HEADS * HEAD_DIM:])], axis=1).astype(BF16)
            qg = jnp.tile(q_norm_g[j], N_HEADS).reshape(1, qd)
            kg = jnp.tile(k_norm_g[j], 2 * N_KV_HEADS).reshape(1, 2 * N_KV_HEADS * HEAD_DIM)
            q, k, v = _odd_in(xs, g_mix, mod, w_all, qg, kg, segs, rope, nl, nt)
            n_out = nl if last else nt
            att = _attention(q, k, v, sink_logit[j].astype(F32) * LOG2E, attn_bias, l,
                             n_out * (tm // ATT_BLOCK))
            xs, *routing = _odd_out(att, w_o[j].astype(BF16), xs, mod, route, nl, n_out)
        n_moe = nl if last else nt
        xs = _moe_layer(xs, routing, mod, w1, w3, w2, layer, nl, n_moe)
    return xs[:l].reshape(bsz, l, d)
```

```python
import functools
import math

import numpy as np
import jax
import jax.numpy as jnp
from jax import lax
from jax.experimental import pallas as pl
from jax.experimental.pallas import tpu as pltpu

F32 = jnp.float32
BF16 = jnp.bfloat16
I32 = jnp.int32

EPS = 1e-6
NEG_INF = -1e30

GRID_W = 64
FOURIER_GROUPS = 4
FOURIER_GROUP_DIM = 128
FOURIER_WIDTH = FOURIER_GROUPS * FOURIER_GROUP_DIM
CONV_WIDTH = 512
N_HEADS = 16
N_KV_HEADS = 4
GQA_GROUP = N_HEADS // N_KV_HEADS
HEAD_DIM = 64
WINDOW = 128
ROPE_BASE = 10000.0
ROPE_PAIRS = HEAD_DIM // 4
N_GROUPS = 4
EXPERTS_PER_GROUP = 8
N_EXPERTS = N_GROUPS * EXPERTS_PER_GROUP

LANES = 128
ROW_TILE = 256
ATT_BLOCK = 128
MOE_ROWS = 256
DFT_MINOR = 128
VMEM_LIMIT = 48 * 1024 * 1024

LOG2E = math.log2(math.e)


def _cparams(sem):
    return pltpu.CompilerParams(dimension_semantics=sem, vmem_limit_bytes=VMEM_LIMIT)


def _split_bf16(x):
    hi = x.astype(BF16)
    lo = (x - hi.astype(F32)).astype(BF16)
    return hi, lo


def _mod_kernel(ct_ref, w_ref, b_ref, o_ref):
    ct = ct_ref[...]
    s = ct * jax.nn.sigmoid(ct)
    w = w_ref[...]
    r0 = jnp.sum(w * s[:, 0:1], axis=0, keepdims=True)
    r1 = jnp.sum(w * s[:, 1:2], axis=0, keepdims=True)
    o_ref[...] = jnp.concatenate([r0, r1], axis=0) + b_ref[...]


def _modulation(c, c_ctx, w_mod, b_mod):
    depth, d, n = w_mod.shape
    tn = 512
    ct = jnp.stack([c.reshape(d), c_ctx.reshape(d)], axis=1)
    return pl.pallas_call(
        _mod_kernel,
        grid=(depth, n // tn),
        in_specs=[
            pl.BlockSpec((d, 2), lambda l, j: (0, 0)),
            pl.BlockSpec((None, d, tn), lambda l, j: (l, 0, j)),
            pl.BlockSpec((None, 1, tn), lambda l, j: (l, 0, j)),
        ],
        out_specs=pl.BlockSpec((None, 2, tn), lambda l, j: (l, 0, j)),
        out_shape=jax.ShapeDtypeStruct((depth, 2, n), F32),
        compiler_params=_cparams(("arbitrary", "arbitrary")),
        name="modulation",
    )(ct, w_mod, b_mod.reshape(depth, 1, n))


def _norm_mod(x, g, shift, scale):
    ms = jnp.mean(x * x, axis=-1, keepdims=True)
    y = x * lax.rsqrt(ms + EPS) * g
    return y * (1.0 + scale) + shift


def _stream_of(n_lat_tiles):
    return lambda i: (jnp.where(i >= n_lat_tiles, 1, 0), 0, 0)


def _even_in_kernel(x_ref, g_ref, mod_ref, w_ref, cs_ref, z_ref, bg_ref, u_ref):
    h = _norm_mod(x_ref[...], g_ref[...], mod_ref[0:1, :], mod_ref[1:2, :])
    p = jnp.dot(h.astype(BF16), w_ref[...], preferred_element_type=F32)
    cs = cs_ref[...]
    for grp in range(FOURIER_GROUPS):
        lo = grp * FOURIER_GROUP_DIM
        a = p[:, lo:lo + FOURIER_GROUP_DIM].astype(BF16)
        z = jnp.dot(a, cs, preferred_element_type=F32)
        z_ref[0, :, lo:lo + FOURIER_GROUP_DIM] = z[:, :FOURIER_GROUP_DIM]
        z_ref[1, :, lo:lo + FOURIER_GROUP_DIM] = z[:, FOURIER_GROUP_DIM:]
    o = FOURIER_WIDTH
    bg_ref[...] = p[:, o:o + CONV_WIDTH].astype(BF16)
    u_ref[...] = (p[:, o + CONV_WIDTH:o + 2 * CONV_WIDTH] * p[:, o + 2 * CONV_WIDTH:]).astype(BF16)


def _even_in(x, g, mod, w_in, cs, n_lat_tiles):
    t, d = x.shape
    n = w_in.shape[1]
    tm = ROW_TILE
    return pl.pallas_call(
        _even_in_kernel,
        grid=(t // tm,),
        in_specs=[
            pl.BlockSpec((tm, d), lambda i: (i, 0)),
            pl.BlockSpec((1, d), lambda i: (0, 0)),
            pl.BlockSpec((None, 6, d), _stream_of(n_lat_tiles)),
            pl.BlockSpec((d, n), lambda i: (0, 0)),
            pl.BlockSpec(cs.shape, lambda i: (0, 0)),
        ],
        out_specs=[
            pl.BlockSpec((2, tm, FOURIER_WIDTH), lambda i: (0, i, 0)),
            pl.BlockSpec((tm, CONV_WIDTH), lambda i: (i, 0)),
            pl.BlockSpec((tm, CONV_WIDTH), lambda i: (i, 0)),
        ],
        out_shape=[
            jax.ShapeDtypeStruct((2, t, FOURIER_WIDTH), F32),
            jax.ShapeDtypeStruct((t, CONV_WIDTH), BF16),
            jax.ShapeDtypeStruct((t, CONV_WIDTH), BF16),
        ],
        compiler_params=_cparams(("arbitrary",)),
        name="even_in_proj",
    )(x, g, mod, w_in, cs)


def _dft_major_kernel(m_ref, z_ref, o_ref):
    m = m_ref[...]
    n1, c = z_ref.shape[1], z_ref.shape[3]
    for j in range(z_ref.shape[2]):
        zc = z_ref[:, :, j, :].reshape(2 * n1, c).astype(BF16)
        g = jnp.dot(m, zc, preferred_element_type=F32)
        o_ref[:, :, j, :] = g.reshape(2, n1, c)


def _dft_major(m1, z4, n1):
    c = z4.shape[-1]
    cb = 8
    return pl.pallas_call(
        _dft_major_kernel,
        grid=(DFT_MINOR // cb,),
        in_specs=[pl.BlockSpec(m1.shape, lambda j: (0, 0)), pl.BlockSpec((2, n1, cb, c), lambda j: (0, 0, j, 0))],
        out_specs=pl.BlockSpec((2, n1, cb, c), lambda j: (0, 0, j, 0)),
        out_shape=jax.ShapeDtypeStruct((2, n1, DFT_MINOR, c), F32),
        compiler_params=_cparams(("arbitrary",)),
        name="dft_major",
    )(m1, z4)


def _dft_minor_kernel(a_ref, g_ref, o_ref):
    for j in range(a_ref.shape[0]):
        gcat = jnp.concatenate([g_ref[0, j], g_ref[1, j]], axis=0).astype(BF16)
        o_ref[:, j, :] = jnp.dot(a_ref[j], gcat, preferred_element_type=F32)


def _dft_minor(a_tab, g4):
    n1, m, k2 = a_tab.shape
    c = g4.shape[-1]
    kb = 8
    return pl.pallas_call(
        _dft_minor_kernel,
        grid=(n1 // kb,),
        in_specs=[
            pl.BlockSpec((kb, m, k2), lambda i: (i, 0, 0)),
            pl.BlockSpec((2, kb, DFT_MINOR, c), lambda i: (0, i, 0, 0)),
        ],
        out_specs=pl.BlockSpec((DFT_MINOR, kb, c), lambda i: (0, i, 0)),
        out_shape=jax.ShapeDtypeStruct((DFT_MINOR, n1, c), F32),
        compiler_params=_cparams(("arbitrary",)),
        name="dft_minor",
    )(a_tab, g4)


def _dft_ctx_kernel(a_ref, z_ref, o_ref):
    o_ref[...] = jnp.dot(a_ref[...], z_ref[...].astype(BF16), preferred_element_type=F32)


def _dft_ctx(mc, zc):
    lc, c = mc.shape[0], zc.shape[1]
    return pl.pallas_call(
        _dft_ctx_kernel,
        grid=(1,),
        in_specs=[pl.BlockSpec(mc.shape, lambda i: (0, 0)), pl.BlockSpec(zc.shape, lambda i: (0, 0))],
        out_specs=pl.BlockSpec((lc, c), lambda i: (0, 0)),
        out_shape=jax.ShapeDtypeStruct((lc, c), F32),
        compiler_params=_cparams(("arbitrary",)),
        name="dft_context",
    )(mc, zc)


def _dft_tables(l, lc):
    gd = FOURIER_GROUP_DIM
    kk = np.arange(gd)
    ang = 2.0 * np.pi * ((kk[:, None] * kk[None, :]) % gd) / gd
    cs = np.concatenate([np.cos(ang), -np.sin(ang)], axis=1)
    n1 = l // DFT_MINOR
    k1 = np.arange(n1)
    ang1 = 2.0 * np.pi * ((k1[:, None] * k1[None, :]) % n1) / n1
    c1, s1 = np.cos(ang1), np.sin(ang1)
    m1 = np.block([[c1, s1], [-s1, c1]])
    l2 = np.arange(DFT_MINOR)
    kfull = k1[:, None, None] + n1 * l2[None, :, None]
    ang2 = 2.0 * np.pi * ((kfull * l2[None, None, :]) % l) / l
    sc = 1.0 / math.sqrt(l * gd)
    a_tab = np.concatenate([np.cos(ang2), np.sin(ang2)], axis=2) * sc
    kc = np.arange(lc)
    angc = 2.0 * np.pi * ((kc[:, None] * kc[None, :]) % lc) / lc
    mc = np.concatenate([np.cos(angc), np.sin(angc)], axis=1) / math.sqrt(lc * gd)
    as_bf16 = lambda v: jnp.asarray(v, F32).astype(BF16)
    return as_bf16(cs), as_bf16(m1), as_bf16(a_tab), as_bf16(mc)


def _fourier_seq(z, l, m1, a_tab, mc):
    t, c = z.shape[1], z.shape[2]
    lc = t - l
    n1 = l // DFT_MINOR
    g4 = _dft_major(m1, z.reshape(2, t // DFT_MINOR, DFT_MINOR, c), n1)
    f_lat = _dft_minor(a_tab, g4).reshape(l, c)
    return f_lat, _dft_ctx(mc, z[:, l:].reshape(2 * lc, c))


ROUTER_EXPERT_ROW0 = 8


def _route_tile(x, g_ref, mod_ref, wr_ref, br_ref, tri_ref, h_ref, ri_ref, rg_ref, cnt_ref):
    h = _norm_mod(x, g_ref[...], mod_ref[3:4, :], mod_ref[4:5, :])
    h_ref[...] = h.astype(BF16)
    tm = h.shape[0]
    h_hi, h_lo = _split_bf16(h)
    w_hi, w_lo = wr_ref[0], wr_ref[1]
    logits = (jnp.dot(h_hi, w_hi, preferred_element_type=F32) + jnp.dot(h_hi, w_lo, preferred_element_type=F32)
              + jnp.dot(h_lo, w_hi, preferred_element_type=F32))
    logits = logits.T + br_ref[:, 0:1]
    gl = logits[0:N_GROUPS]
    gmax = jnp.max(gl, axis=0, keepdims=True)
    gi = lax.broadcasted_iota(I32, gl.shape, 0)
    g_idx = jnp.min(jnp.where(gl == gmax, gi, N_GROUPS), axis=0, keepdims=True)
    g_val = 1.0 / jnp.sum(jnp.exp(gl - gmax), axis=0, keepdims=True)
    e_in = logits[ROUTER_EXPERT_ROW0:ROUTER_EXPERT_ROW0 + EXPERTS_PER_GROUP]
    for grp in range(1, N_GROUPS):
        lo = ROUTER_EXPERT_ROW0 + grp * EXPERTS_PER_GROUP
        e_in = jnp.where(g_idx == grp, logits[lo:lo + EXPERTS_PER_GROUP], e_in)
    ei = lax.broadcasted_iota(I32, e_in.shape, 0)
    v1 = jnp.max(e_in, axis=0, keepdims=True)
    i1 = jnp.min(jnp.where(e_in == v1, ei, EXPERTS_PER_GROUP), axis=0, keepdims=True)
    rest = jnp.where(ei == i1, -jnp.inf, e_in)
    v2 = jnp.max(rest, axis=0, keepdims=True)
    i2 = jnp.min(jnp.where(rest == v2, ei, EXPERTS_PER_GROUP), axis=0, keepdims=True)
    w2 = jnp.exp(v2 - v1)
    gate1 = g_val / (1.0 + w2)
    gate2 = g_val * w2 / (1.0 + w2)
    e1 = g_idx * EXPERTS_PER_GROUP + i1
    e2 = g_idx * EXPERTS_PER_GROUP + i2
    xi = lax.broadcasted_iota(I32, (N_EXPERTS, tm), 0)
    oh1 = xi == e1
    oh2 = xi == e2
    oh = oh1.astype(F32) + oh2.astype(F32)
    before = jnp.dot(oh.astype(BF16), tri_ref[...], preferred_element_type=F32)
    rank1 = jnp.sum(jnp.where(oh1, before, 0.0), axis=0, keepdims=True)
    rank2 = jnp.sum(jnp.where(oh2, before, 0.0), axis=0, keepdims=True)
    cnt_ref[...] = jnp.broadcast_to(jnp.sum(oh, axis=1, keepdims=True), cnt_ref.shape)
    orow = lax.broadcasted_iota(I32, (8, tm), 0)
    ri_ref[...] = jnp.where(orow == 0, e1, jnp.where(orow == 1, e2, jnp.where(
        orow == 2, rank1.astype(I32), jnp.where(orow == 3, rank2.astype(I32), 0))))
    rg_ref[...] = jnp.where(orow == 0, gate1, jnp.where(orow == 1, gate2, 0.0))


def _route_specs(route, d, n_tiles):
    tm = ROW_TILE
    const = lambda a: pl.BlockSpec(a.shape, lambda i: (0,) * a.ndim)
    in_specs = [const(a) for a in route]
    out_specs = [
        pl.BlockSpec((tm, d), lambda i: (i, 0)),
        pl.BlockSpec((8, tm), lambda i: (0, i)),
        pl.BlockSpec((8, tm), lambda i: (0, i)),
        pl.BlockSpec((None, N_EXPERTS, LANES), lambda i: (i, 0, 0)),
    ]
    rows = n_tiles * tm
    out_shape = [
        jax.ShapeDtypeStruct((rows, d), BF16),
        jax.ShapeDtypeStruct((8, rows), I32),
        jax.ShapeDtypeStruct((8, rows), F32),
        jax.ShapeDtypeStruct((n_tiles, N_EXPERTS, LANES), F32),
    ]
    return in_specs, out_specs, out_shape


def _even_out_kernel(fl_ref, fc_ref, bg_ref, u_ref, up_ref, un_ref, cw_ref, w_ref, x_ref, mod_ref,
                     gf_ref, wr_ref, br_ref, tri_ref, o_ref, h_ref, ri_ref, rg_ref, cnt_ref, *, n_lat_tiles, n_tiles):
    i = pl.program_id(0)
    tm = u_ref.shape[0]
    u = u_ref[...].astype(F32)
    row = lax.broadcasted_iota(I32, u.shape, 0)
    first = jnp.logical_or(i == 0, i == n_lat_tiles)
    last = jnp.logical_or(i == n_lat_tiles - 1, i == n_tiles - 1)
    hb = up_ref.shape[0]
    halo_p = up_ref[...].astype(F32)[hb - 1:hb, :] * jnp.where(first, 0.0, 1.0)
    halo_n = un_ref[...].astype(F32)[0:1, :] * jnp.where(last, 0.0, 1.0)
    u_prev = jnp.where(row == 0, halo_p, pltpu.roll(u, 1, axis=0))
    u_next = jnp.where(row == tm - 1, halo_n, pltpu.roll(u, tm - 1, axis=0))
    cw = cw_ref[...]
    y = bg_ref[...].astype(F32) * (cw[0:1, :] * u_prev + cw[1:2, :] * u + cw[2:3, :] * u_next)
    is_ctx = jnp.full(fl_ref.shape, i, I32) >= n_lat_tiles
    f = jnp.where(is_ctx, fc_ref[...], fl_ref[...])
    acc = jnp.dot(f.astype(BF16), w_ref[0:FOURIER_WIDTH, :], preferred_element_type=F32)
    acc += jnp.dot(y.astype(BF16), w_ref[FOURIER_WIDTH:, :], preferred_element_type=F32)
    x_new = x_ref[...] + mod_ref[2:3, :] * acc
    o_ref[...] = x_new
    _route_tile(x_new, gf_ref, mod_ref, wr_ref, br_ref, tri_ref, h_ref, ri_ref, rg_ref, cnt_ref)


def _even_out(f_lat, f_ctx, bg, u, conv_w, w_out, x, mod, route, n_lat_tiles):
    t, d = x.shape
    tm = ROW_TILE
    hb = 16
    n_tiles = t // tm
    r = tm // hb
    r_in, r_out, r_shape = _route_specs(route, d, n_tiles)
    return pl.pallas_call(
        functools.partial(_even_out_kernel, n_lat_tiles=n_lat_tiles, n_tiles=n_tiles),
        grid=(n_tiles,),
        in_specs=[
            pl.BlockSpec((tm, FOURIER_WIDTH), lambda i: (jnp.minimum(i, n_lat_tiles - 1), 0)),
            pl.BlockSpec((tm, FOURIER_WIDTH), lambda i: (jnp.clip(i - n_lat_tiles, 0, n_tiles - n_lat_tiles - 1), 0)),
            pl.BlockSpec((tm, CONV_WIDTH), lambda i: (i, 0)),
            pl.BlockSpec((tm, CONV_WIDTH), lambda i: (i, 0)),
            pl.BlockSpec((hb, CONV_WIDTH), lambda i: (jnp.maximum(i * r - 1, 0), 0)),
            pl.BlockSpec((hb, CONV_WIDTH), lambda i: (jnp.minimum((i + 1) * r, t // hb - 1), 0)),
            pl.BlockSpec(conv_w.shape, lambda i: (0, 0)),
            pl.BlockSpec(w_out.shape, lambda i: (0, 0)),
            pl.BlockSpec((tm, d), lambda i: (i, 0)),
            pl.BlockSpec((None, 6, d), _stream_of(n_lat_tiles)),
        ] + r_in,
        out_specs=[pl.BlockSpec((tm, d), lambda i: (i, 0))] + r_out,
        out_shape=[jax.ShapeDtypeStruct((t, d), F32)] + r_shape,
        compiler_params=_cparams(("arbitrary",)),
        name="even_out_proj",
    )(f_lat, f_ctx, bg, u, u, u, conv_w, w_out, x, mod, *route)


def _seg_rms_scale(v, seg, seg_t):
    hi, lo = _split_bf16(v * v)
    ss = jnp.dot(hi, seg, preferred_element_type=F32) + jnp.dot(lo, seg, preferred_element_type=F32)
    inv = lax.rsqrt(ss * (1.0 / HEAD_DIM) + EPS)
    ihi, ilo = _split_bf16(inv)
    return (jnp.dot(ihi, seg_t, preferred_element_type=F32)
            + jnp.dot(ilo, seg_t, preferred_element_type=F32))


def _rope_cols(v, cos, sa, sb, scale):
    cols = []
    for j in range(v.shape[1] // LANES):
        c = v[:, j * LANES:(j + 1) * LANES]
        r = c * cos + pltpu.roll(c, LANES - ROPE_PAIRS, axis=1) * sa + pltpu.roll(c, ROPE_PAIRS, axis=1) * sb
        cols.append(r * scale if scale != 1.0 else r)
    return jnp.concatenate(cols, axis=1)


def _odd_in_kernel(x_ref, g_ref, mod_ref, w_ref, qg_ref, kg_ref, segq_ref, segqt_ref, segk_ref, segkt_ref,
                   rope_ref, q_ref, k_ref, v_ref):
    h = _norm_mod(x_ref[...], g_ref[...], mod_ref[0:1, :], mod_ref[1:2, :])
    p = jnp.dot(h.astype(BF16), w_ref[...], preferred_element_type=F32)
    qd = N_HEADS * HEAD_DIM
    kd = 2 * N_KV_HEADS * HEAD_DIM
    cos, sa, sb = rope_ref[0], rope_ref[1], rope_ref[2]
    q = p[:, :qd]
    q = q * _seg_rms_scale(q, segq_ref[...], segqt_ref[...]) * qg_ref[...]
    q_ref[...] = _rope_cols(q, cos, sa, sb, HEAD_DIM ** -0.5 * LOG2E).astype(BF16)
    k = p[:, qd:qd + kd]
    k = k * _seg_rms_scale(k, segk_ref[...], segkt_ref[...]) * kg_ref[...]
    k_ref[...] = _rope_cols(k, cos, sa, sb, 1.0).astype(BF16)
    v_ref[...] = p[:, qd + kd:].astype(BF16)


def _odd_in(x, g, mod, w_qkv, qg, kg, segs, rope, n_lat_tiles, n_tiles):
    t, d = x.shape
    n = w_qkv.shape[1]
    tm = ROW_TILE
    rows = n_tiles * tm
    qd = N_HEADS * HEAD_DIM
    kd = 2 * N_KV_HEADS * HEAD_DIM
    segq, segqt, segk, segkt = segs
    const = lambda a: pl.BlockSpec(a.shape, lambda i: (0,) * a.ndim)
    return pl.pallas_call(
        _odd_in_kernel,
        grid=(n_tiles,),
        in_specs=[
            pl.BlockSpec((tm, d), lambda i: (i, 0)),
            const(g),
            pl.BlockSpec((None, 6, d), _stream_of(n_lat_tiles)),
            const(w_qkv), const(qg), const(kg), const(segq), const(segqt), const(segk), const(segkt),
            pl.BlockSpec((3, tm, LANES), lambda i: (0, i, 0)),
        ],
        out_specs=[
            pl.BlockSpec((tm, qd), lambda i: (i, 0)),
            pl.BlockSpec((tm, kd), lambda i: (i, 0)),
            pl.BlockSpec((tm, kd), lambda i: (i, 0)),
        ],
        out_shape=[
            jax.ShapeDtypeStruct((rows, qd), BF16),
            jax.ShapeDtypeStruct((rows, kd), BF16),
            jax.ShapeDtypeStruct((rows, kd), BF16),
        ],
        compiler_params=_cparams(("arbitrary",)),
        name="odd_in_proj",
    )(x, g, mod, w_qkv, qg, kg, segq, segqt, segk, segkt, rope)


def _rope_tables(l, t):
    pos = np.arange(l)
    freqs = ROPE_BASE ** (-np.arange(ROPE_PAIRS, dtype=np.float32) / ROPE_PAIRS)
    lane = np.arange(LANES) % HEAD_DIM
    axis = lane // (2 * ROPE_PAIRS)
    half = (lane % (2 * ROPE_PAIRS)) // ROPE_PAIRS
    pair = lane % ROPE_PAIRS
    p = np.where(axis[None, :] == 0, (pos // GRID_W)[:, None], (pos % GRID_W)[:, None]).astype(np.float32)
    ang = p * freqs[pair][None, :].astype(np.float32)
    cos, sin = np.cos(ang), np.sin(ang)
    sa = np.where(half[None, :] == 0, -sin, 0.0)
    sb = np.where(half[None, :] == 1, sin, 0.0)
    tab = np.zeros((3, t, LANES), np.float32)
    tab[0, :l], tab[1, :l], tab[2, :l] = cos, sa, sb
    tab[0, l:] = 1.0
    return jnp.asarray(tab)


def _segment_matrices():
    def seg(width):
        m = np.zeros((width, LANES), np.float32)
        m[np.arange(width), np.arange(width) // HEAD_DIM] = 1.0
        return m
    sq, sk = seg(N_HEADS * HEAD_DIM), seg(2 * N_KV_HEADS * HEAD_DIM)
    b = lambda v: jnp.asarray(v).astype(BF16)
    return b(sq), b(sq.T), b(sk), b(sk.T)


def _attn_kernel(sink_ref, q_ref, kp_ref, kc_ref, kn_ref, vp_ref, vc_ref, vn_ref, kx_ref, vx_ref, bias_ref, o_ref):
    bq = q_ref.shape[0]
    low = lax.broadcasted_iota(I32, (bq, LANES), 1) < HEAD_DIM
    top = lax.broadcasted_iota(I32, (LANES, bq), 0) < HEAD_DIM
    bias = bias_ref[...]
    nt = (((1,), (1,)), ((), ()))
    tn = (((0,), (0,)), ((), ()))
    for g in range(N_KV_HEADS):
        ks = slice(g * LANES, (g + 1) * LANES)
        parts, sinks = [], []
        for j in range(GQA_GROUP):
            col = 2 * g + j // 2
            c = q_ref[:, col * LANES:(col + 1) * LANES]
            keep = low if j % 2 == 0 else jnp.logical_not(low)
            parts.append(jnp.where(keep, c, jnp.zeros_like(c)))
            sinks.append(jnp.full((1, bq), sink_ref[g * GQA_GROUP + j], F32))
        qs = jnp.concatenate(parts, axis=0)
        sink = jnp.concatenate(sinks, axis=1)
        kwin = jnp.concatenate([kp_ref[:, ks], kc_ref[:, ks], kn_ref[:, ks]], axis=0)
        vwin = jnp.concatenate([vp_ref[:, ks], vc_ref[:, ks], vn_ref[:, ks]], axis=0)
        s_loc = lax.dot_general(kwin, qs, nt, preferred_element_type=F32) + bias
        s_ctx = lax.dot_general(kx_ref[:, ks], qs, nt, preferred_element_type=F32)
        m = jnp.maximum(jnp.maximum(jnp.max(s_loc, axis=0, keepdims=True),
                                    jnp.max(s_ctx, axis=0, keepdims=True)), sink)
        p_loc = jnp.exp2(s_loc - m)
        p_ctx = jnp.exp2(s_ctx - m)
        den = (jnp.sum(p_loc, axis=0, keepdims=True) + jnp.sum(p_ctx, axis=0, keepdims=True)
               + jnp.exp2(sink - m))
        ot = lax.dot_general(vwin, p_loc.astype(BF16), tn, preferred_element_type=F32)
        ot += lax.dot_general(vx_ref[:, ks], p_ctx.astype(BF16), tn, preferred_element_type=F32)
        ot = ot * (1.0 / den)
        t0 = jnp.where(top, ot[:, 0:bq], ot[:, bq:2 * bq])
        t1 = jnp.where(top, ot[:, 2 * bq:3 * bq], ot[:, 3 * bq:4 * bq])
        o_ref[:, 2 * g * LANES:(2 * g + 1) * LANES] = t0.T.astype(o_ref.dtype)
        o_ref[:, (2 * g + 1) * LANES:(2 * g + 2) * LANES] = t1.T.astype(o_ref.dtype)


def _attn_bias(l):
    bq = ATT_BLOCK
    r = np.arange(GQA_GROUP * bq)[None, :] % bq
    col = np.arange(3 * bq)[:, None]
    band = np.abs(col - bq - r) <= WINDOW
    no_prev, no_next = col >= bq, col < 2 * bq
    masks = [band, band & no_prev, band & no_next, band & no_prev & no_next, np.zeros_like(band)]
    return jnp.asarray(np.where(np.stack(masks), 0.0, NEG_INF).astype(np.float32))


def _attention(q, k, v, sink, bias, l, n_q_blocks):
    t = q.shape[0]
    bq = ATT_BLOCK
    nlb = l // bq
    lc = t - l
    ctx_blk = l // lc
    kw = k.shape[1]
    kspec = lambda f: pl.BlockSpec((bq, kw), f)
    prev = lambda b, s: (jnp.clip(b - 1, 0, nlb - 1), 0)
    cur = lambda b, s: (jnp.minimum(b, nlb - 1), 0)
    nxt = lambda b, s: (jnp.clip(b + 1, 0, nlb - 1), 0)
    ctx = pl.BlockSpec((lc, kw), lambda b, s: (ctx_blk, 0))
    kind = lambda b, s: (jnp.where(b >= nlb, 4, (b == 0).astype(I32) + 2 * (b == nlb - 1).astype(I32)), 0, 0)
    return pl.pallas_call(
        _attn_kernel,
        grid_spec=pltpu.PrefetchScalarGridSpec(
            num_scalar_prefetch=1,
            grid=(n_q_blocks,),
            in_specs=[
                pl.BlockSpec((bq, q.shape[1]), lambda b, s: (b, 0)),
                kspec(prev), kspec(cur), kspec(nxt), kspec(prev), kspec(cur), kspec(nxt), ctx, ctx,
                pl.BlockSpec((None,) + bias.shape[1:], kind),
            ],
            out_specs=pl.BlockSpec((bq, q.shape[1]), lambda b, s: (b, 0)),
        ),
        out_shape=jax.ShapeDtypeStruct((n_q_blocks * bq, q.shape[1]), BF16),
        compiler_params=_cparams(("arbitrary",)),
        name="window_attention",
    )(sink, q, k, k, k, v, v, v, k, v, bias)


def _odd_out_kernel(a_ref, w_ref, x_ref, mod_ref, gf_ref, wr_ref, br_ref, tri_ref, o_ref, h_ref, ri_ref, rg_ref,
                    cnt_ref):
    acc = jnp.dot(a_ref[...], w_ref[...], preferred_element_type=F32)
    x_new = x_ref[...] + mod_ref[2:3, :] * acc
    o_ref[...] = x_new
    _route_tile(x_new, gf_ref, mod_ref, wr_ref, br_ref, tri_ref, h_ref, ri_ref, rg_ref, cnt_ref)


def _odd_out(a, w_o, x, mod, route, n_lat_tiles, n_tiles):
    d = x.shape[1]
    tm = ROW_TILE
    r_in, r_out, r_shape = _route_specs(route, d, n_tiles)
    return pl.pallas_call(
        _odd_out_kernel,
        grid=(n_tiles,),
        in_specs=[
            pl.BlockSpec((tm, a.shape[1]), lambda i: (i, 0)),
            pl.BlockSpec(w_o.shape, lambda i: (0, 0)),
            pl.BlockSpec((tm, d), lambda i: (i, 0)),
            pl.BlockSpec((None, 6, d), _stream_of(n_lat_tiles)),
        ] + r_in,
        out_specs=[pl.BlockSpec((tm, d), lambda i: (i, 0))] + r_out,
        out_shape=[jax.ShapeDtypeStruct((n_tiles * tm, d), F32)] + r_shape,
        compiler_params=_cparams(("arbitrary",)),
        name="odd_out_proj",
    )(a, w_o, x, mod, *route)


CHUNK = 16
TILE_BUF = 2 * ROW_TILE + N_EXPERTS * CHUNK
MAX_TILE_CHUNKS = TILE_BUF // CHUNK


def _chunk_rows(c):
    return pl.ds(pl.multiple_of(c * CHUNK, CHUNK), CHUNK)


def _wait_rows(copy_of_rows, n_chunks):
    bit = 1
    while bit <= MAX_TILE_CHUNKS:
        @pl.when((n_chunks & bit) != 0)
        def _(bit=bit):
            copy_of_rows(bit * CHUNK).wait()
        bit *= 2


def _dispatch_kernel(tab_ref, lused_ref, fill_ref, h_ref, pos_ref, xb_ref, hs, zbuf, sem, fsem, *, n_tiles, n_blocks):
    i = pl.program_id(0)
    slot = i % 2
    tm = h_ref.shape[0]

    def tail_copy(c):
        return pltpu.make_async_copy(zbuf.at[pl.ds(0, CHUNK), :], xb_ref.at[_chunk_rows(c), :], fsem)

    def block_copy(b):
        return pltpu.make_async_copy(zbuf, xb_ref.at[pl.ds(pl.multiple_of(b * MOE_ROWS, MOE_ROWS), MOE_ROWS), :], fsem)

    def fill(start):
        def tail(e, c):
            st, n = fill_ref[e], fill_ref[N_EXPERTS + e]

            def one(c2, cc):
                cp = tail_copy(st + c2)
                cp.start() if start else cp.wait()
                return cc
            return lax.fori_loop(0, n, one, c)
        lax.fori_loop(0, N_EXPERTS, tail, 0)

        def blk(b, c):
            cp = block_copy(b)
            cp.start() if start else cp.wait()
            return c
        lax.fori_loop(fill_ref[2 * N_EXPERTS], n_blocks, blk, 0)

    @pl.when(i == 0)
    def _():
        zbuf[...] = jnp.zeros_like(zbuf)
        fill(True)

    pos = pos_ref[...]
    r = lax.broadcasted_iota(I32, (TILE_BUF, tm), 0)
    onehot = jnp.where(jnp.logical_or(r == pos[0:1, :], r == pos[1:2, :]), 1.0, 0.0).astype(BF16)
    hs[slot] = jnp.dot(onehot, h_ref[...], preferred_element_type=F32).astype(BF16)

    def chunk_copy(sl, src, dst):
        return pltpu.make_async_copy(hs.at[sl, _chunk_rows(src), :], xb_ref.at[_chunk_rows(dst), :], sem.at[sl])

    base = i * MAX_TILE_CHUNKS

    def per_chunk(c, cc):
        chunk_copy(slot, c, tab_ref[base + c]).start()
        return cc
    lax.fori_loop(0, lused_ref[i], per_chunk, 0)

    def wait_chunks(sl, n):
        _wait_rows(lambda rows: pltpu.make_async_copy(hs.at[sl, pl.ds(0, rows), :], xb_ref.at[pl.ds(0, rows), :],
                                                      sem.at[sl]), n)

    @pl.when(i > 0)
    def _():
        wait_chunks(1 - slot, lused_ref[jnp.maximum(i - 1, 0)])

    @pl.when(i == n_tiles - 1)
    def _():
        wait_chunks(slot, lused_ref[i])
        fill(False)


def _dispatch(h, pos_rows, tab, lused, fill, n_blocks):
    t, d = h.shape
    tm = ROW_TILE
    n_tiles = t // tm
    return pl.pallas_call(
        functools.partial(_dispatch_kernel, n_tiles=n_tiles, n_blocks=n_blocks),
        grid_spec=pltpu.PrefetchScalarGridSpec(
            num_scalar_prefetch=3,
            grid=(n_tiles,),
            in_specs=[
                pl.BlockSpec((tm, d), lambda i, *_: (i, 0)),
                pl.BlockSpec((8, tm), lambda i, *_: (0, i)),
            ],
            out_specs=pl.BlockSpec(memory_space=pl.ANY),
            scratch_shapes=[
                pltpu.VMEM((2, TILE_BUF, d), BF16),
                pltpu.VMEM((MOE_ROWS, d), BF16),
                pltpu.SemaphoreType.DMA((2,)),
                pltpu.SemaphoreType.DMA(()),
            ],
        ),
        out_shape=jax.ShapeDtypeStruct((n_blocks * MOE_ROWS, d), BF16),
        compiler_params=_cparams(("arbitrary",)),
        name="moe_dispatch",
    )(tab, lused, fill, h, pos_rows)


def _ffn_kernel(b0_ref, nb_ref, nv_ref, fill_ref, xb_ref, w1_ref, w3_ref, w2_ref, yb_ref,
                w1s, w3s, w2s, xbuf, ybuf, zbuf, sem_in, sem_out, fsem, *, n_blocks):
    e = pl.program_id(0)
    w1s[...] = w1_ref[...].astype(BF16)
    w3s[...] = w3_ref[...].astype(BF16)
    w2s[...] = w2_ref[...].astype(BF16)
    b0, nb = b0_ref[e], nb_ref[e]

    def rows(b):
        return pl.ds(pl.multiple_of(b * MOE_ROWS, MOE_ROWS), MOE_ROWS)

    def x_copy(b, sl):
        return pltpu.make_async_copy(xb_ref.at[rows(b), :], xbuf.at[sl], sem_in.at[sl])

    def y_copy(b, sl):
        return pltpu.make_async_copy(ybuf.at[sl], yb_ref.at[rows(b), :], sem_out.at[sl])

    @pl.when(nb > 0)
    def _():
        x_copy(b0, 0).start(priority=1)

    def block(j, c):
        sl = j % 2
        x_copy(b0 + j, sl).wait()

        @pl.when(j + 1 < nb)
        def _():
            x_copy(b0 + j + 1, 1 - sl).start(priority=1)

        @pl.when(j >= 2)
        def _():
            y_copy(b0 + j - 2, sl).wait()

        x = xbuf[sl]
        row = lax.broadcasted_iota(I32, x.shape, 0)
        xb = jnp.where(row < nv_ref[b0 + j], x, jnp.zeros_like(x))
        a = jnp.dot(xb, w1s[...], preferred_element_type=F32)
        b = jnp.dot(xb, w3s[...], preferred_element_type=F32)
        hid = (a * jax.nn.sigmoid(a) * b).astype(BF16)
        ybuf[sl] = jnp.dot(hid, w2s[...], preferred_element_type=F32).astype(BF16)
        y_copy(b0 + j, sl).start(priority=1)
        return c
    lax.fori_loop(0, nb, block, 0)

    @pl.when(nb >= 2)
    def _():
        y_copy(b0 + nb - 2, nb % 2).wait()

    @pl.when(nb >= 1)
    def _():
        y_copy(b0 + nb - 1, (nb - 1) % 2).wait()

    @pl.when(e == N_EXPERTS - 1)
    def _():
        zbuf[...] = jnp.zeros_like(zbuf)

        def z_copy(b):
            return pltpu.make_async_copy(zbuf, yb_ref.at[rows(b), :], fsem)

        def start(b, c):
            z_copy(b).start()
            return c

        def wait(b, c):
            z_copy(b).wait()
            return c
        lax.fori_loop(fill_ref[0], n_blocks, start, 0)
        lax.fori_loop(fill_ref[0], n_blocks, wait, 0)


def _expert_ffn(xb, blk_start, blk_count, n_valid, used_blocks, w1, w3, w2, layer):
    r = xb.shape[0]
    d, f = w1.shape[2], w1.shape[3]
    bm = MOE_ROWS
    return pl.pallas_call(
        functools.partial(_ffn_kernel, n_blocks=r // bm),
        grid_spec=pltpu.PrefetchScalarGridSpec(
            num_scalar_prefetch=4,
            grid=(N_EXPERTS,),
            in_specs=[
                pl.BlockSpec(memory_space=pl.ANY),
                pl.BlockSpec((None, None, d, f), lambda e, *_: (layer, e, 0, 0)),
                pl.BlockSpec((None, None, d, f), lambda e, *_: (layer, e, 0, 0)),
                pl.BlockSpec((None, None, f, d), lambda e, *_: (layer, e, 0, 0)),
            ],
            out_specs=pl.BlockSpec(memory_space=pl.ANY),
            scratch_shapes=[
                pltpu.VMEM((d, f), BF16), pltpu.VMEM((d, f), BF16), pltpu.VMEM((f, d), BF16),
                pltpu.VMEM((2, bm, d), BF16), pltpu.VMEM((2, bm, d), BF16), pltpu.VMEM((bm, d), BF16),
                pltpu.SemaphoreType.DMA((2,)), pltpu.SemaphoreType.DMA((2,)), pltpu.SemaphoreType.DMA(()),
            ],
        ),
        out_shape=jax.ShapeDtypeStruct((r, d), BF16),
        compiler_params=_cparams(("arbitrary",)),
        name="moe_expert_mlp",
    )(blk_start, blk_count, n_valid, used_blocks, xb, w1, w3, w2)


def _combine_kernel(tab_ref, lused_ref, yb_ref, pos_ref, gate_ref, x_ref, mod_ref, o_ref, ys, sem, *, n_tiles):
    i = pl.program_id(0)
    slot = i % 2
    tm = x_ref.shape[0]

    def chunk_copy(sl, src, dst):
        return pltpu.make_async_copy(yb_ref.at[_chunk_rows(src), :], ys.at[sl, _chunk_rows(dst), :], sem.at[sl])

    def fetch(tile, sl):
        base = tile * MAX_TILE_CHUNKS

        def per_chunk(c, cc):
            chunk_copy(sl, tab_ref[base + c], c).start()
            return cc
        lax.fori_loop(0, lused_ref[tile], per_chunk, 0)

    @pl.when(i == 0)
    def _():
        fetch(0, 0)

    @pl.when(i + 1 < n_tiles)
    def _():
        fetch(jnp.minimum(i + 1, n_tiles - 1), 1 - slot)

    _wait_rows(lambda rows: pltpu.make_async_copy(yb_ref.at[pl.ds(0, rows), :], ys.at[slot, pl.ds(0, rows), :],
                                                  sem.at[slot]), lused_ref[i])

    y = ys[slot]
    used = lused_ref[i] * CHUNK
    rowi = lax.broadcasted_iota(I32, y.shape, 0)
    y16 = jnp.where(rowi < used, y, jnp.zeros_like(y))
    pos = pos_ref[...]
    gate = gate_ref[...]
    lane = lax.broadcasted_iota(I32, (tm, TILE_BUF), 1)
    gm = (jnp.where(lane == pos[:, 0:1], gate[:, 0:1], 0.0) + jnp.where(lane == pos[:, 1:2], gate[:, 1:2], 0.0))
    ghi, glo = _split_bf16(gm)
    mix = jnp.dot(ghi, y16, preferred_element_type=F32) + jnp.dot(glo, y16, preferred_element_type=F32)
    o_ref[...] = x_ref[...] + mod_ref[5:6, :] * mix


def _combine(yb, pos_cols, gates, tab, lused, x, mod, n_lat_tiles, n_tiles):
    d = x.shape[1]
    tm = ROW_TILE
    return pl.pallas_call(
        functools.partial(_combine_kernel, n_tiles=n_tiles),
        grid_spec=pltpu.PrefetchScalarGridSpec(
            num_scalar_prefetch=2,
            grid=(n_tiles,),
            in_specs=[
                pl.BlockSpec(memory_space=pl.ANY),
                pl.BlockSpec((tm, 8), lambda i, *_: (i, 0)),
                pl.BlockSpec((tm, 8), lambda i, *_: (i, 0)),
                pl.BlockSpec((tm, d), lambda i, *_: (i, 0)),
                pl.BlockSpec((None, 6, d), lambda i, *_: (jnp.where(i >= n_lat_tiles, 1, 0), 0, 0)),
            ],
            out_specs=pl.BlockSpec((tm, d), lambda i, *_: (i, 0)),
            scratch_shapes=[pltpu.VMEM((2, TILE_BUF, d), BF16), pltpu.SemaphoreType.DMA((2,))],
        ),
        out_shape=jax.ShapeDtypeStruct((n_tiles * tm, d), F32),
        compiler_params=_cparams(("arbitrary",)),
        name="moe_combine",
    )(tab, lused, yb, pos_cols, gates, x, mod)


def _moe_layer(x, routing, mod, w1, w3, w2, layer, n_lat_tiles, n_tiles):
    tm = ROW_TILE
    rows = n_tiles * tm
    cpb = MOE_ROWS // CHUNK
    h, ri, rg, cnt3 = routing
    cnt = cnt3[:, :, 0].astype(I32)
    nch = (cnt + CHUNK - 1) // CHUNK
    lbase = jnp.cumsum(nch, axis=1) - nch
    lused = jnp.sum(nch, axis=1).astype(I32)
    tot = jnp.sum(nch, axis=0)
    reg = (tot + cpb - 1) // cpb * cpb
    gend = jnp.cumsum(reg)
    gstart = gend - reg
    gpos = gstart[None, :] + jnp.cumsum(nch, axis=0) - nch
    rows_max = 2 * rows + n_tiles * N_EXPERTS * (CHUNK - 1) + N_EXPERTS * (MOE_ROWS - CHUNK)
    n_blocks = -(-rows_max // MOE_ROWS)
    ex = jnp.arange(N_EXPERTS, dtype=I32)
    blk0 = jnp.arange(n_blocks, dtype=I32) * cpb
    block_exp = jnp.minimum(jnp.sum((gend[None, :] <= blk0[:, None]).astype(I32), axis=1), N_EXPERTS - 1)
    sel = block_exp[:, None] == ex[None, :]
    tot_b = jnp.sum(jnp.where(sel, tot[None, :], 0), axis=1)
    st_b = jnp.sum(jnp.where(sel, gstart[None, :], 0), axis=1)
    n_valid = jnp.clip((tot_b - (blk0 - st_b)) * CHUNK, 0, MOE_ROWS).astype(I32)
    lb_tok = jnp.repeat(lbase, tm, axis=0)
    at = lambda e: jnp.sum(jnp.where(e[:, None] == ex[None, :], lb_tok, 0), axis=1)
    pos1 = CHUNK * at(ri[0]) + ri[2]
    pos2 = CHUNK * at(ri[1]) + ri[3]
    zero = jnp.zeros_like(pos1)
    pos_rows = jnp.stack([pos1, pos2] + [zero] * 6, axis=0).astype(I32)
    slot_id = jnp.arange(MAX_TILE_CHUNKS, dtype=I32)
    owner = jnp.sum((lbase + nch)[:, None, :] <= slot_id[None, :, None], axis=2)
    own = jnp.minimum(owner, N_EXPERTS - 1)[:, :, None] == ex[None, None, :]
    tab = jnp.sum(jnp.where(own, (gpos - lbase)[:, None, :], 0), axis=2) + slot_id[None, :]
    tab = tab.astype(I32).reshape(-1)
    fill = jnp.concatenate([gstart + tot, reg - tot, gend[-1:] // cpb]).astype(I32)
    xb = _dispatch(h, pos_rows, tab, lused, fill, n_blocks)
    yb = _expert_ffn(xb, (gstart // cpb).astype(I32), (reg // cpb).astype(I32), n_valid,
                     (gend[-1:] // cpb).astype(I32), w1, w3, w2, layer)
    return _combine(yb, pos_rows.T, rg.T, tab, lused, x, mod, n_lat_tiles, n_tiles)


def _router_matrix(w_rg, b_rg, w_re, b_re):
    d = w_rg.shape[0]
    wr = jnp.zeros((d, LANES), F32)
    wr = wr.at[:, 0:N_GROUPS].set(w_rg.astype(F32))
    wr = wr.at[:, ROUTER_EXPERT_ROW0:ROUTER_EXPERT_ROW0 + N_EXPERTS].set(w_re.astype(F32))
    wr = jnp.stack(_split_bf16(wr), axis=0)
    br = jnp.zeros((LANES,), F32)
    br = br.at[0:N_GROUPS].set(b_rg.astype(F32))
    br = br.at[ROUTER_EXPERT_ROW0:ROUTER_EXPERT_ROW0 + N_EXPERTS].set(b_re.astype(F32))
    return wr, jnp.broadcast_to(br[:, None], (LANES, LANES))


def _dup_heads(w):
    d = w.shape[0]
    w4 = w.reshape(d, N_KV_HEADS, 1, HEAD_DIM)
    return jnp.broadcast_to(w4, (d, N_KV_HEADS, 2, HEAD_DIM)).reshape(d, 2 * N_KV_HEADS * HEAD_DIM)


def kernel(x, c, ctx, c_ctx, w_mod, b_mod, norm_mix_g, norm_ffn_g, w_in_even, conv_w, w_out_even, w_qkv, q_norm_g,
           k_norm_g, sink_logit, w_o, w_router_g, b_router_g, w_router_e, b_router_e, w1, w3, w2):
    bsz, l, d = x.shape
    lc = ctx.shape[1]
    assert bsz == 1, "one sample per call"
    tm = ROW_TILE
    assert l % tm == 0 and lc % tm == 0 and l % lc == 0 and l % (DFT_MINOR * 8) == 0
    depth = w_mod.shape[0]
    t = l + lc
    assert t % DFT_MINOR == 0
    nl, nt = l // tm, t // tm

    xs = jnp.concatenate([x.reshape(l, d), ctx.reshape(lc, d)], axis=0)
    mod_all = _modulation(c, c_ctx, w_mod, b_mod).reshape(depth, 2, 6, d)
    cs, m1, a_tab, mc = _dft_tables(l, lc)
    rope = _rope_tables(l, t)
    segs = _segment_matrices()
    attn_bias = _attn_bias(l)
    tri = jnp.asarray(np.triu(np.ones((tm, tm), np.float32), 1)).astype(BF16)
    qd = N_HEADS * HEAD_DIM

    for layer in range(depth):
        last = layer == depth - 1
        j = layer // 2
        mod = mod_all[layer]
        g_mix = norm_mix_g[layer].reshape(1, d)
        g_ffn = norm_ffn_g[layer].reshape(1, d)
        wr, br = _router_matrix(w_router_g[layer], b_router_g[layer], w_router_e[layer], b_router_e[layer])
        route = (g_ffn, wr, br, tri)
        if layer % 2 == 0:
            z, bg, u = _even_in(xs, g_mix, mod, w_in_even[j].astype(BF16), cs, nl)
            f_lat, f_ctx = _fourier_seq(z, l, m1, a_tab, mc)
            xs, *routing = _even_out(f_lat, f_ctx, bg, u, conv_w[j], w_out_even[j].astype(BF16), xs, mod, route, nl)
        else:
            wq = w_qkv[j]
            w_all = jnp.concatenate([wq[:, :qd], _dup_heads(wq[:, qd:qd + N_KV_HEADS * HEAD_DIM]),
                                     _dup_heads(wq[:, qd + N_KV_HEADS * HEAD_DIM:])], axis=1).astype(BF16)
            qg = jnp.tile(q_norm_g[j], N_HEADS).reshape(1, qd)
            kg = jnp.tile(k_norm_g[j], 2 * N_KV_HEADS).reshape(1, 2 * N_KV_HEADS * HEAD_DIM)
            q, k, v = _odd_in(xs, g_mix, mod, w_all, qg, kg, segs, rope, nl, nt)
            n_out = nl if last else nt
            att = _attention(q, k, v, sink_logit[j].astype(F32) * LOG2E, attn_bias, l,
                             n_out * (tm // ATT_BLOCK))
            xs, *routing = _odd_out(att, w_o[j].astype(BF16), xs, mod, route, nl, n_out)
        n_moe = nl if last else nt
        xs = _moe_layer(xs, routing, mod, w1, w3, w2, layer, nl, n_moe)
    return xs[:l].reshape(bsz, l, d)
```

```python
import functools
import math

import numpy as np
import jax
import jax.numpy as jnp
from jax import lax
from jax.experimental import pallas as pl
from jax.experimental.pallas import tpu as pltpu

F32 = jnp.float32
BF16 = jnp.bfloat16
I32 = jnp.int32

EPS = 1e-6
NEG_INF = -1e30

GRID_W = 64
FOURIER_GROUPS = 4
FOURIER_GROUP_DIM = 128
FOURIER_WIDTH = FOURIER_GROUPS * FOURIER_GROUP_DIM
CONV_WIDTH = 512
N_HEADS = 16
N_KV_HEADS = 4
GQA_GROUP = N_HEADS // N_KV_HEADS
HEAD_DIM = 64
WINDOW = 128
ROPE_BASE = 10000.0
ROPE_PAIRS = HEAD_DIM // 4
N_GROUPS = 4
EXPERTS_PER_GROUP = 8
N_EXPERTS = N_GROUPS * EXPERTS_PER_GROUP

LANES = 128
ROW_TILE = 256
ATT_BLOCK = 128
MOE_ROWS = 256
DFT_MINOR = 128
VMEM_LIMIT = 48 * 1024 * 1024

LOG2E = math.log2(math.e)


def _cparams(sem):
    return pltpu.CompilerParams(dimension_semantics=sem, vmem_limit_bytes=VMEM_LIMIT)


def _split_bf16(x):
    hi = x.astype(BF16)
    lo = (x - hi.astype(F32)).astype(BF16)
    return hi, lo


def _mod_kernel(ct_ref, w_ref, b_ref, o_ref):
    ct = ct_ref[...]
    s = ct * jax.nn.sigmoid(ct)
    w = w_ref[...]
    r0 = jnp.sum(w * s[:, 0:1], axis=0, keepdims=True)
    r1 = jnp.sum(w * s[:, 1:2], axis=0, keepdims=True)
    o_ref[...] = jnp.concatenate([r0, r1], axis=0) + b_ref[...]


def _modulation(c, c_ctx, w_mod, b_mod):
    depth, d, n = w_mod.shape
    tn = 512
    ct = jnp.stack([c.reshape(d), c_ctx.reshape(d)], axis=1)
    return pl.pallas_call(
        _mod_kernel,
        grid=(depth, n // tn),
        in_specs=[
            pl.BlockSpec((d, 2), lambda l, j: (0, 0)),
            pl.BlockSpec((None, d, tn), lambda l, j: (l, 0, j)),
            pl.BlockSpec((None, 1, tn), lambda l, j: (l, 0, j)),
        ],
        out_specs=pl.BlockSpec((None, 2, tn), lambda l, j: (l, 0, j)),
        out_shape=jax.ShapeDtypeStruct((depth, 2, n), F32),
        compiler_params=_cparams(("arbitrary", "arbitrary")),
        name="modulation",
    )(ct, w_mod, b_mod.reshape(depth, 1, n))


def _norm_mod(x, g, shift, scale):
    ms = jnp.mean(x * x, axis=-1, keepdims=True)
    y = x * lax.rsqrt(ms + EPS) * g
    return y * (1.0 + scale) + shift


def _stream_of(n_lat_tiles):
    return lambda i: (jnp.where(i >= n_lat_tiles, 1, 0), 0, 0)


def _even_in_kernel(x_ref, g_ref, mod_ref, w_ref, cs_ref, z_ref, bg_ref, u_ref):
    h = _norm_mod(x_ref[...], g_ref[...], mod_ref[0:1, :], mod_ref[1:2, :])
    p = jnp.dot(h.astype(BF16), w_ref[...], preferred_element_type=F32)
    cs = cs_ref[...]
    for grp in range(FOURIER_GROUPS):
        lo = grp * FOURIER_GROUP_DIM
        a = p[:, lo:lo + FOURIER_GROUP_DIM].astype(BF16)
        z = jnp.dot(a, cs, preferred_element_type=F32)
        z_ref[0, :, lo:lo + FOURIER_GROUP_DIM] = z[:, :FOURIER_GROUP_DIM]
        z_ref[1, :, lo:lo + FOURIER_GROUP_DIM] = z[:, FOURIER_GROUP_DIM:]
    o = FOURIER_WIDTH
    bg_ref[...] = p[:, o:o + CONV_WIDTH].astype(BF16)
    u_ref[...] = (p[:, o + CONV_WIDTH:o + 2 * CONV_WIDTH] * p[:, o + 2 * CONV_WIDTH:]).astype(BF16)


def _even_in(x, g, mod, w_in, cs, n_lat_tiles):
    t, d = x.shape
    n = w_in.shape[1]
    tm = ROW_TILE
    return pl.pallas_call(
        _even_in_kernel,
        grid=(t // tm,),
        in_specs=[
            pl.BlockSpec((tm, d), lambda i: (i, 0)),
            pl.BlockSpec((1, d), lambda i: (0, 0)),
            pl.BlockSpec((None, 6, d), _stream_of(n_lat_tiles)),
            pl.BlockSpec((d, n), lambda i: (0, 0)),
            pl.BlockSpec(cs.shape, lambda i: (0, 0)),
        ],
        out_specs=[
            pl.BlockSpec((2, tm, FOURIER_WIDTH), lambda i: (0, i, 0)),
            pl.BlockSpec((tm, CONV_WIDTH), lambda i: (i, 0)),
            pl.BlockSpec((tm, CONV_WIDTH), lambda i: (i, 0)),
        ],
        out_shape=[
            jax.ShapeDtypeStruct((2, t, FOURIER_WIDTH), F32),
            jax.ShapeDtypeStruct((t, CONV_WIDTH), BF16),
            jax.ShapeDtypeStruct((t, CONV_WIDTH), BF16),
        ],
        compiler_params=_cparams(("arbitrary",)),
        name="even_in_proj",
    )(x, g, mod, w_in, cs)


def _dft_major_kernel(m_ref, z_ref, o_ref):
    m = m_ref[...]
    n1, c = z_ref.shape[1], z_ref.shape[3]
    for j in range(z_ref.shape[2]):
        zc = z_ref[:, :, j, :].reshape(2 * n1, c).astype(BF16)
        g = jnp.dot(m, zc, preferred_element_type=F32)
        o_ref[:, :, j, :] = g.reshape(2, n1, c)


def _dft_major(m1, z4, n1):
    c = z4.shape[-1]
    cb = 8
    return pl.pallas_call(
        _dft_major_kernel,
        grid=(DFT_MINOR // cb,),
        in_specs=[pl.BlockSpec(m1.shape, lambda j: (0, 0)), pl.BlockSpec((2, n1, cb, c), lambda j: (0, 0, j, 0))],
        out_specs=pl.BlockSpec((2, n1, cb, c), lambda j: (0, 0, j, 0)),
        out_shape=jax.ShapeDtypeStruct((2, n1, DFT_MINOR, c), F32),
        compiler_params=_cparams(("arbitrary",)),
        name="dft_major",
    )(m1, z4)


def _dft_minor_kernel(a_ref, g_ref, o_ref):
    for j in range(a_ref.shape[0]):
        gcat = jnp.concatenate([g_ref[0, j], g_ref[1, j]], axis=0).astype(BF16)
        o_ref[:, j, :] = jnp.dot(a_ref[j], gcat, preferred_element_type=F32)


def _dft_minor(a_tab, g4):
    n1, m, k2 = a_tab.shape
    c = g4.shape[-1]
    kb = 8
    return pl.pallas_call(
        _dft_minor_kernel,
        grid=(n1 // kb,),
        in_specs=[
            pl.BlockSpec((kb, m, k2), lambda i: (i, 0, 0)),
            pl.BlockSpec((2, kb, DFT_MINOR, c), lambda i: (0, i, 0, 0)),
        ],
        out_specs=pl.BlockSpec((DFT_MINOR, kb, c), lambda i: (0, i, 0)),
        out_shape=jax.ShapeDtypeStruct((DFT_MINOR, n1, c), F32),
        compiler_params=_cparams(("arbitrary",)),
        name="dft_minor",
    )(a_tab, g4)


def _dft_ctx_kernel(a_ref, z_ref, o_ref):
    o_ref[...] = jnp.dot(a_ref[...], z_ref[...].astype(BF16), preferred_element_type=F32)


def _dft_ctx(mc, zc):
    lc, c = mc.shape[0], zc.shape[1]
    return pl.pallas_call(
        _dft_ctx_kernel,
        grid=(1,),
        in_specs=[pl.BlockSpec(mc.shape, lambda i: (0, 0)), pl.BlockSpec(zc.shape, lambda i: (0, 0))],
        out_specs=pl.BlockSpec((lc, c), lambda i: (0, 0)),
        out_shape=jax.ShapeDtypeStruct((lc, c), F32),
        compiler_params=_cparams(("arbitrary",)),
        name="dft_context",
    )(mc, zc)


def _dft_tables(l, lc):
    gd = FOURIER_GROUP_DIM
    kk = np.arange(gd)
    ang = 2.0 * np.pi * ((kk[:, None] * kk[None, :]) % gd) / gd
    cs = np.concatenate([np.cos(ang), -np.sin(ang)], axis=1)
    n1 = l // DFT_MINOR
    k1 = np.arange(n1)
    ang1 = 2.0 * np.pi * ((k1[:, None] * k1[None, :]) % n1) / n1
    c1, s1 = np.cos(ang1), np.sin(ang1)
    m1 = np.block([[c1, s1], [-s1, c1]])
    l2 = np.arange(DFT_MINOR)
    kfull = k1[:, None, None] + n1 * l2[None, :, None]
    ang2 = 2.0 * np.pi * ((kfull * l2[None, None, :]) % l) / l
    sc = 1.0 / math.sqrt(l * gd)
    a_tab = np.concatenate([np.cos(ang2), np.sin(ang2)], axis=2) * sc
    kc = np.arange(lc)
    angc = 2.0 * np.pi * ((kc[:, None] * kc[None, :]) % lc) / lc
    mc = np.concatenate([np.cos(angc), np.sin(angc)], axis=1) / math.sqrt(lc * gd)
    as_bf16 = lambda v: jnp.asarray(v, F32).astype(BF16)
    return as_bf16(cs), as_bf16(m1), as_bf16(a_tab), as_bf16(mc)


def _fourier_seq(z, l, m1, a_tab, mc):
    t, c = z.shape[1], z.shape[2]
    lc = t - l
    n1 = l // DFT_MINOR
    g4 = _dft_major(m1, z.reshape(2, t // DFT_MINOR, DFT_MINOR, c), n1)
    f_lat = _dft_minor(a_tab, g4).reshape(l, c)
    return f_lat, _dft_ctx(mc, z[:, l:].reshape(2 * lc, c))


ROUTER_EXPERT_ROW0 = 8


def _route_tile(x, g_ref, mod_ref, wr_ref, br_ref, tri_ref, h_ref, ri_ref, rg_ref, cnt_ref):
    h = _norm_mod(x, g_ref[...], mod_ref[3:4, :], mod_ref[4:5, :])
    h_ref[...] = h.astype(BF16)
    tm = h.shape[0]
    h_hi, h_lo = _split_bf16(h)
    w_hi, w_lo = wr_ref[0], wr_ref[1]
    logits = (jnp.dot(h_hi, w_hi, preferred_element_type=F32) + jnp.dot(h_hi, w_lo, preferred_element_type=F32)
              + jnp.dot(h_lo, w_hi, preferred_element_type=F32))
    logits = logits.T + br_ref[:, 0:1]
    gl = logits[0:N_GROUPS]
    gmax = jnp.max(gl, axis=0, keepdims=True)
    gi = lax.broadcasted_iota(I32, gl.shape, 0)
    g_idx = jnp.min(jnp.where(gl == gmax, gi, N_GROUPS), axis=0, keepdims=True)
    g_val = 1.0 / jnp.sum(jnp.exp(gl - gmax), axis=0, keepdims=True)
    e_in = logits[ROUTER_EXPERT_ROW0:ROUTER_EXPERT_ROW0 + EXPERTS_PER_GROUP]
    for grp in range(1, N_GROUPS):
        lo = ROUTER_EXPERT_ROW0 + grp * EXPERTS_PER_GROUP
        e_in = jnp.where(g_idx == grp, logits[lo:lo + EXPERTS_PER_GROUP], e_in)
    ei = lax.broadcasted_iota(I32, e_in.shape, 0)
    v1 = jnp.max(e_in, axis=0, keepdims=True)
    i1 = jnp.min(jnp.where(e_in == v1, ei, EXPERTS_PER_GROUP), axis=0, keepdims=True)
    rest = jnp.where(ei == i1, -jnp.inf, e_in)
    v2 = jnp.max(rest, axis=0, keepdims=True)
    i2 = jnp.min(jnp.where(rest == v2, ei, EXPERTS_PER_GROUP), axis=0, keepdims=True)
    w2 = jnp.exp(v2 - v1)
    gate1 = g_val / (1.0 + w2)
    gate2 = g_val * w2 / (1.0 + w2)
    e1 = g_idx * EXPERTS_PER_GROUP + i1
    e2 = g_idx * EXPERTS_PER_GROUP + i2
    xi = lax.broadcasted_iota(I32, (N_EXPERTS, tm), 0)
    oh1 = xi == e1
    oh2 = xi == e2
    oh = oh1.astype(F32) + oh2.astype(F32)
    before = jnp.dot(oh.astype(BF16), tri_ref[...], preferred_element_type=F32)
    rank1 = jnp.sum(jnp.where(oh1, before, 0.0), axis=0, keepdims=True)
    rank2 = jnp.sum(jnp.where(oh2, before, 0.0), axis=0, keepdims=True)
    cnt_ref[...] = jnp.broadcast_to(jnp.sum(oh, axis=1, keepdims=True), cnt_ref.shape)
    orow = lax.broadcasted_iota(I32, (8, tm), 0)
    ri_ref[...] = jnp.where(orow == 0, e1, jnp.where(orow == 1, e2, jnp.where(
        orow == 2, rank1.astype(I32), jnp.where(orow == 3, rank2.astype(I32), 0))))
    rg_ref[...] = jnp.where(orow == 0, gate1, jnp.where(orow == 1, gate2, 0.0))


def _route_specs(route, d, n_tiles):
    tm = ROW_TILE
    const = lambda a: pl.BlockSpec(a.shape, lambda i: (0,) * a.ndim)
    in_specs = [const(a) for a in route]
    out_specs = [
        pl.BlockSpec((tm, d), lambda i: (i, 0)),
        pl.BlockSpec((8, tm), lambda i: (0, i)),
        pl.BlockSpec((8, tm), lambda i: (0, i)),
        pl.BlockSpec((None, N_EXPERTS, LANES), lambda i: (i, 0, 0)),
    ]
    rows = n_tiles * tm
    out_shape = [
        jax.ShapeDtypeStruct((rows, d), BF16),
        jax.ShapeDtypeStruct((8, rows), I32),
        jax.ShapeDtypeStruct((8, rows), F32),
        jax.ShapeDtypeStruct((n_tiles, N_EXPERTS, LANES), F32),
    ]
    return in_specs, out_specs, out_shape


def _even_out_kernel(fl_ref, fc_ref, bg_ref, u_ref, up_ref, un_ref, cw_ref, w_ref, x_ref, mod_ref,
                     gf_ref, wr_ref, br_ref, tri_ref, o_ref, h_ref, ri_ref, rg_ref, cnt_ref, *, n_lat_tiles, n_tiles):
    i = pl.program_id(0)
    tm = u_ref.shape[0]
    u = u_ref[...].astype(F32)
    row = lax.broadcasted_iota(I32, u.shape, 0)
    first = jnp.logical_or(i == 0, i == n_lat_tiles)
    last = jnp.logical_or(i == n_lat_tiles - 1, i == n_tiles - 1)
    hb = up_ref.shape[0]
    halo_p = up_ref[...].astype(F32)[hb - 1:hb, :] * jnp.where(first, 0.0, 1.0)
    halo_n = un_ref[...].astype(F32)[0:1, :] * jnp.where(last, 0.0, 1.0)
    u_prev = jnp.where(row == 0, halo_p, pltpu.roll(u, 1, axis=0))
    u_next = jnp.where(row == tm - 1, halo_n, pltpu.roll(u, tm - 1, axis=0))
    cw = cw_ref[...]
    y = bg_ref[...].astype(F32) * (cw[0:1, :] * u_prev + cw[1:2, :] * u + cw[2:3, :] * u_next)
    is_ctx = jnp.full(fl_ref.shape, i, I32) >= n_lat_tiles
    f = jnp.where(is_ctx, fc_ref[...], fl_ref[...])
    acc = jnp.dot(f.astype(BF16), w_ref[0:FOURIER_WIDTH, :], preferred_element_type=F32)
    acc += jnp.dot(y.astype(BF16), w_ref[FOURIER_WIDTH:, :], preferred_element_type=F32)
    x_new = x_ref[...] + mod_ref[2:3, :] * acc
    o_ref[...] = x_new
    _route_tile(x_new, gf_ref, mod_ref, wr_ref, br_ref, tri_ref, h_ref, ri_ref, rg_ref, cnt_ref)


def _even_out(f_lat, f_ctx, bg, u, conv_w, w_out, x, mod, route, n_lat_tiles):
    t, d = x.shape
    tm = ROW_TILE
    hb = 16
    n_tiles = t // tm
    r = tm // hb
    r_in, r_out, r_shape = _route_specs(route, d, n_tiles)
    return pl.pallas_call(
        functools.partial(_even_out_kernel, n_lat_tiles=n_lat_tiles, n_tiles=n_tiles),
        grid=(n_tiles,),
        in_specs=[
            pl.BlockSpec((tm, FOURIER_WIDTH), lambda i: (jnp.minimum(i, n_lat_tiles - 1), 0)),
            pl.BlockSpec((tm, FOURIER_WIDTH), lambda i: (jnp.clip(i - n_lat_tiles, 0, n_tiles - n_lat_tiles - 1), 0)),
            pl.BlockSpec((tm, CONV_WIDTH), lambda i: (i, 0)),
            pl.BlockSpec((tm, CONV_WIDTH), lambda i: (i, 0)),
            pl.BlockSpec((hb, CONV_WIDTH), lambda i: (jnp.maximum(i * r - 1, 0), 0)),
            pl.BlockSpec((hb, CONV_WIDTH), lambda i: (jnp.minimum((i + 1) * r, t // hb - 1), 0)),
            pl.BlockSpec(conv_w.shape, lambda i: (0, 0)),
            pl.BlockSpec(w_out.shape, lambda i: (0, 0)),
            pl.BlockSpec((tm, d), lambda i: (i, 0)),
            pl.BlockSpec((None, 6, d), _stream_of(n_lat_tiles)),
        ] + r_in,
        out_specs=[pl.BlockSpec((tm, d), lambda i: (i, 0))] + r_out,
        out_shape=[jax.ShapeDtypeStruct((t, d), F32)] + r_shape,
        compiler_params=_cparams(("arbitrary",)),
        name="even_out_proj",
    )(f_lat, f_ctx, bg, u, u, u, conv_w, w_out, x, mod, *route)


def _seg_rms_scale(v, seg, seg_t):
    hi, lo = _split_bf16(v * v)
    ss = jnp.dot(hi, seg, preferred_element_type=F32) + jnp.dot(lo, seg, preferred_element_type=F32)
    inv = lax.rsqrt(ss * (1.0 / HEAD_DIM) + EPS)
    ihi, ilo = _split_bf16(inv)
    return (jnp.dot(ihi, seg_t, preferred_element_type=F32)
            + jnp.dot(ilo, seg_t, preferred_element_type=F32))


def _rope_cols(v, cos, sa, sb, scale):
    cols = []
    for j in range(v.shape[1] // LANES):
        c = v[:, j * LANES:(j + 1) * LANES]
        r = c * cos + pltpu.roll(c, LANES - ROPE_PAIRS, axis=1) * sa + pltpu.roll(c, ROPE_PAIRS, axis=1) * sb
        cols.append(r * scale if scale != 1.0 else r)
    return jnp.concatenate(cols, axis=1)


def _odd_in_kernel(x_ref, g_ref, mod_ref, w_ref, qg_ref, kg_ref, segq_ref, segqt_ref, segk_ref, segkt_ref,
                   rope_ref, q_ref, k_ref, v_ref):
    h = _norm_mod(x_ref[...], g_ref[...], mod_ref[0:1, :], mod_ref[1:2, :])
    p = jnp.dot(h.astype(BF16), w_ref[...], preferred_element_type=F32)
    qd = N_HEADS * HEAD_DIM
    kd = 2 * N_KV_HEADS * HEAD_DIM
    cos, sa, sb = rope_ref[0], rope_ref[1], rope_ref[2]
    q = p[:, :qd]
    q = q * _seg_rms_scale(q, segq_ref[...], segqt_ref[...]) * qg_ref[...]
    q_ref[...] = _rope_cols(q, cos, sa, sb, HEAD_DIM ** -0.5 * LOG2E).astype(BF16)
    k = p[:, qd:qd + kd]
    k = k * _seg_rms_scale(k, segk_ref[...], segkt_ref[...]) * kg_ref[...]
    k_ref[...] = _rope_cols(k, cos, sa, sb, 1.0).astype(BF16)
    v_ref[...] = p[:, qd + kd:].astype(BF16)


def _odd_in(x, g, mod, w_qkv, qg, kg, segs, rope, n_lat_tiles, n_tiles):
    t, d = x.shape
    n = w_qkv.shape[1]
    tm = ROW_TILE
    rows = n_tiles * tm
    qd = N_HEADS * HEAD_DIM
    kd = 2 * N_KV_HEADS * HEAD_DIM
    segq, segqt, segk, segkt = segs
    const = lambda a: pl.BlockSpec(a.shape, lambda i: (0,) * a.ndim)
    return pl.pallas_call(
        _odd_in_kernel,
        grid=(n_tiles,),
        in_specs=[
            pl.BlockSpec((tm, d), lambda i: (i, 0)),
            const(g),
            pl.BlockSpec((None, 6, d), _stream_of(n_lat_tiles)),
            const(w_qkv), const(qg), const(kg), const(segq), const(segqt), const(segk), const(segkt),
            pl.BlockSpec((3, tm, LANES), lambda i: (0, i, 0)),
        ],
        out_specs=[
            pl.BlockSpec((tm, qd), lambda i: (i, 0)),
            pl.BlockSpec((tm, kd), lambda i: (i, 0)),
            pl.BlockSpec((tm, kd), lambda i: (i, 0)),
        ],
        out_shape=[
            jax.ShapeDtypeStruct((rows, qd), BF16),
            jax.ShapeDtypeStruct((rows, kd), BF16),
            jax.ShapeDtypeStruct((rows, kd), BF16),
        ],
        compiler_params=_cparams(("arbitrary",)),
        name="odd_in_proj",
    )(x, g, mod, w_qkv, qg, kg, segq, segqt, segk, segkt, rope)


def _rope_tables(l, t):
    pos = np.arange(l)
    freqs = ROPE_BASE ** (-np.arange(ROPE_PAIRS, dtype=np.float32) / ROPE_PAIRS)
    lane = np.arange(LANES) % HEAD_DIM
    axis = lane // (2 * ROPE_PAIRS)
    half = (lane % (2 * ROPE_PAIRS)) // ROPE_PAIRS
    pair = lane % ROPE_PAIRS
    p = np.where(axis[None, :] == 0, (pos // GRID_W)[:, None], (pos % GRID_W)[:, None]).astype(np.float32)
    ang = p * freqs[pair][None, :].astype(np.float32)
    cos, sin = np.cos(ang), np.sin(ang)
    sa = np.where(half[None, :] == 0, -sin, 0.0)
    sb = np.where(half[None, :] == 1, sin, 0.0)
    tab = np.zeros((3, t, LANES), np.float32)
    tab[0, :l], tab[1, :l], tab[2, :l] = cos, sa, sb
    tab[0, l:] = 1.0
    return jnp.asarray(tab)


def _segment_matrices():
    def seg(width):
        m = np.zeros((width, LANES), np.float32)
        m[np.arange(width), np.arange(width) // HEAD_DIM] = 1.0
        return m
    sq, sk = seg(N_HEADS * HEAD_DIM), seg(2 * N_KV_HEADS * HEAD_DIM)
    b = lambda v: jnp.asarray(v).astype(BF16)
    return b(sq), b(sq.T), b(sk), b(sk.T)


def _attn_kernel(sink_ref, q_ref, kp_ref, kc_ref, kn_ref, vp_ref, vc_ref, vn_ref, kx_ref, vx_ref, bias_ref, o_ref):
    bq = q_ref.shape[0]
    low = lax.broadcasted_iota(I32, (bq, LANES), 1) < HEAD_DIM
    top = lax.broadcasted_iota(I32, (LANES, bq), 0) < HEAD_DIM
    bias = bias_ref[...]
    nt = (((1,), (1,)), ((), ()))
    tn = (((0,), (0,)), ((), ()))
    for g in range(N_KV_HEADS):
        ks = slice(g * LANES, (g + 1) * LANES)
        parts, sinks = [], []
        for j in range(GQA_GROUP):
            col = 2 * g + j // 2
            c = q_ref[:, col * LANES:(col + 1) * LANES]
            keep = low if j % 2 == 0 else jnp.logical_not(low)
            parts.append(jnp.where(keep, c, jnp.zeros_like(c)))
            sinks.append(jnp.full((1, bq), sink_ref[g * GQA_GROUP + j], F32))
        qs = jnp.concatenate(parts, axis=0)
        sink = jnp.concatenate(sinks, axis=1)
        kwin = jnp.concatenate([kp_ref[:, ks], kc_ref[:, ks], kn_ref[:, ks]], axis=0)
        vwin = jnp.concatenate([vp_ref[:, ks], vc_ref[:, ks], vn_ref[:, ks]], axis=0)
        s_loc = lax.dot_general(kwin, qs, nt, preferred_element_type=F32) + bias
        s_ctx = lax.dot_general(kx_ref[:, ks], qs, nt, preferred_element_type=F32)
        m = jnp.maximum(jnp.maximum(jnp.max(s_loc, axis=0, keepdims=True),
                                    jnp.max(s_ctx, axis=0, keepdims=True)), sink)
        p_loc = jnp.exp2(s_loc - m)
        p_ctx = jnp.exp2(s_ctx - m)
        den = (jnp.sum(p_loc, axis=0, keepdims=True) + jnp.sum(p_ctx, axis=0, keepdims=True)
               + jnp.exp2(sink - m))
        ot = lax.dot_general(vwin, p_loc.astype(BF16), tn, preferred_element_type=F32)
        ot += lax.dot_general(vx_ref[:, ks], p_ctx.astype(BF16), tn, preferred_element_type=F32)
        ot = ot * (1.0 / den)
        t0 = jnp.where(top, ot[:, 0:bq], ot[:, bq:2 * bq])
        t1 = jnp.where(top, ot[:, 2 * bq:3 * bq], ot[:, 3 * bq:4 * bq])
        o_ref[:, 2 * g * LANES:(2 * g + 1) * LANES] = t0.T.astype(o_ref.dtype)
        o_ref[:, (2 * g + 1) * LANES:(2 * g + 2) * LANES] = t1.T.astype(o_ref.dtype)


def _attn_bias(l):
    bq = ATT_BLOCK
    r = np.arange(GQA_GROUP * bq)[None, :] % bq
    col = np.arange(3 * bq)[:, None]
    band = np.abs(col - bq - r) <= WINDOW
    no_prev, no_next = col >= bq, col < 2 * bq
    masks = [band, band & no_prev, band & no_next, band & no_prev & no_next, np.zeros_like(band)]
    return jnp.asarray(np.where(np.stack(masks), 0.0, NEG_INF).astype(np.float32))


def _attention(q, k, v, sink, bias, l, n_q_blocks):
    t = q.shape[0]
    bq = ATT_BLOCK
    nlb = l // bq
    lc = t - l
    ctx_blk = l // lc
    kw = k.shape[1]
    kspec = lambda f: pl.BlockSpec((bq, kw), f)
    prev = lambda b, s: (jnp.clip(b - 1, 0, nlb - 1), 0)
    cur = lambda b, s: (jnp.minimum(b, nlb - 1), 0)
    nxt = lambda b, s: (jnp.clip(b + 1, 0, nlb - 1), 0)
    ctx = pl.BlockSpec((lc, kw), lambda b, s: (ctx_blk, 0))
    kind = lambda b, s: (jnp.where(b >= nlb, 4, (b == 0).astype(I32) + 2 * (b == nlb - 1).astype(I32)), 0, 0)
    return pl.pallas_call(
        _attn_kernel,
        grid_spec=pltpu.PrefetchScalarGridSpec(
            num_scalar_prefetch=1,
            grid=(n_q_blocks,),
            in_specs=[
                pl.BlockSpec((bq, q.shape[1]), lambda b, s: (b, 0)),
                kspec(prev), kspec(cur), kspec(nxt), kspec(prev), kspec(cur), kspec(nxt), ctx, ctx,
                pl.BlockSpec((None,) + bias.shape[1:], kind),
            ],
            out_specs=pl.BlockSpec((bq, q.shape[1]), lambda b, s: (b, 0)),
        ),
        out_shape=jax.ShapeDtypeStruct((n_q_blocks * bq, q.shape[1]), BF16),
        compiler_params=_cparams(("arbitrary",)),
        name="window_attention",
    )(sink, q, k, k, k, v, v, v, k, v, bias)


def _odd_out_kernel(a_ref, w_ref, x_ref, mod_ref, gf_ref, wr_ref, br_ref, tri_ref, o_ref, h_ref, ri_ref, rg_ref,
                    cnt_ref):
    acc = jnp.dot(a_ref[...], w_ref[...], preferred_element_type=F32)
    x_new = x_ref[...] + mod_ref[2:3, :] * acc
    o_ref[...] = x_new
    _route_tile(x_new, gf_ref, mod_ref, wr_ref, br_ref, tri_ref, h_ref, ri_ref, rg_ref, cnt_ref)


def _odd_out(a, w_o, x, mod, route, n_lat_tiles, n_tiles):
    d = x.shape[1]
    tm = ROW_TILE
    r_in, r_out, r_shape = _route_specs(route, d, n_tiles)
    return pl.pallas_call(
        _odd_out_kernel,
        grid=(n_tiles,),
        in_specs=[
            pl.BlockSpec((tm, a.shape[1]), lambda i: (i, 0)),
            pl.BlockSpec(w_o.shape, lambda i: (0, 0)),
            pl.BlockSpec((tm, d), lambda i: (i, 0)),
            pl.BlockSpec((None, 6, d), _stream_of(n_lat_tiles)),
        ] + r_in,
        out_specs=[pl.BlockSpec((tm, d), lambda i: (i, 0))] + r_out,
        out_shape=[jax.ShapeDtypeStruct((n_tiles * tm, d), F32)] + r_shape,
        compiler_params=_cparams(("arbitrary",)),
        name="odd_out_proj",
    )(a, w_o, x, mod, *route)


CHUNK = 8
TILE_BUF = 2 * ROW_TILE + N_EXPERTS * CHUNK
MAX_TILE_CHUNKS = TILE_BUF // CHUNK
FFN_BUFS = 4


def _chunk_rows(c):
    return pl.ds(pl.multiple_of(c * CHUNK, CHUNK), CHUNK)


def _wait_rows(copy_of_rows, n_chunks):
    bit = 1
    while bit <= MAX_TILE_CHUNKS:
        @pl.when((n_chunks & bit) != 0)
        def _(bit=bit):
            copy_of_rows(bit * CHUNK).wait()
        bit *= 2


def _dispatch_kernel(tab_ref, lused_ref, fill_ref, h_ref, pos_ref, xb_ref, hs, zbuf, sem, fsem, *, n_tiles, n_blocks):
    i = pl.program_id(0)
    slot = i % 2
    tm = h_ref.shape[0]

    def tail_copy(c):
        return pltpu.make_async_copy(zbuf.at[pl.ds(0, CHUNK), :], xb_ref.at[_chunk_rows(c), :], fsem)

    def block_copy(b):
        return pltpu.make_async_copy(zbuf, xb_ref.at[pl.ds(pl.multiple_of(b * MOE_ROWS, MOE_ROWS), MOE_ROWS), :], fsem)

    def fill(start):
        def tail(e, c):
            st, n = fill_ref[e], fill_ref[N_EXPERTS + e]

            def one(c2, cc):
                cp = tail_copy(st + c2)
                cp.start() if start else cp.wait()
                return cc
            return lax.fori_loop(0, n, one, c)
        lax.fori_loop(0, N_EXPERTS, tail, 0)

        def blk(b, c):
            cp = block_copy(b)
            cp.start() if start else cp.wait()
            return c
        lax.fori_loop(fill_ref[2 * N_EXPERTS], n_blocks, blk, 0)

    @pl.when(i == 0)
    def _():
        zbuf[...] = jnp.zeros_like(zbuf)
        fill(True)

    pos = pos_ref[...]
    r = lax.broadcasted_iota(I32, (TILE_BUF, tm), 0)
    onehot = jnp.where(jnp.logical_or(r == pos[0:1, :], r == pos[1:2, :]), 1.0, 0.0).astype(BF16)
    hs[slot] = jnp.dot(onehot, h_ref[...], preferred_element_type=F32).astype(BF16)

    def chunk_copy(sl, src, dst):
        return pltpu.make_async_copy(hs.at[sl, _chunk_rows(src), :], xb_ref.at[_chunk_rows(dst), :], sem.at[sl])

    base = i * MAX_TILE_CHUNKS

    def per_chunk(c, cc):
        chunk_copy(slot, c, tab_ref[base + c]).start()
        return cc
    lax.fori_loop(0, lused_ref[i], per_chunk, 0)

    def wait_chunks(sl, n):
        _wait_rows(lambda rows: pltpu.make_async_copy(hs.at[sl, pl.ds(0, rows), :], xb_ref.at[pl.ds(0, rows), :],
                                                      sem.at[sl]), n)

    @pl.when(i > 0)
    def _():
        wait_chunks(1 - slot, lused_ref[jnp.maximum(i - 1, 0)])

    @pl.when(i == n_tiles - 1)
    def _():
        wait_chunks(slot, lused_ref[i])
        fill(False)


def _dispatch(h, pos_rows, tab, lused, fill, n_blocks):
    t, d = h.shape
    tm = ROW_TILE
    n_tiles = t // tm
    return pl.pallas_call(
        functools.partial(_dispatch_kernel, n_tiles=n_tiles, n_blocks=n_blocks),
        grid_spec=pltpu.PrefetchScalarGridSpec(
            num_scalar_prefetch=3,
            grid=(n_tiles,),
            in_specs=[
                pl.BlockSpec((tm, d), lambda i, *_: (i, 0)),
                pl.BlockSpec((8, tm), lambda i, *_: (0, i)),
            ],
            out_specs=pl.BlockSpec(memory_space=pl.ANY),
            scratch_shapes=[
                pltpu.VMEM((2, TILE_BUF, d), BF16),
                pltpu.VMEM((MOE_ROWS, d), BF16),
                pltpu.SemaphoreType.DMA((2,)),
                pltpu.SemaphoreType.DMA(()),
            ],
        ),
        out_shape=jax.ShapeDtypeStruct((n_blocks * MOE_ROWS, d), BF16),
        compiler_params=_cparams(("arbitrary",)),
        name="moe_dispatch",
    )(tab, lused, fill, h, pos_rows)


def _ffn_kernel(b0_ref, nb_ref, nv_ref, fill_ref, xb_ref, w1_ref, w3_ref, w2_ref, yb_ref,
                w1s, w3s, w2s, xbuf, ybuf, zbuf, sem_in, sem_out, fsem, *, n_blocks):
    e = pl.program_id(0)
    w1s[...] = w1_ref[...].astype(BF16)
    w3s[...] = w3_ref[...].astype(BF16)
    w2s[...] = w2_ref[...].astype(BF16)
    b0, nb = b0_ref[e], nb_ref[e]

    def rows(b):
        return pl.ds(pl.multiple_of(b * MOE_ROWS, MOE_ROWS), MOE_ROWS)

    def x_copy(b, sl):
        return pltpu.make_async_copy(xb_ref.at[rows(b), :], xbuf.at[sl], sem_in.at[sl])

    def y_copy(b, sl):
        return pltpu.make_async_copy(ybuf.at[sl], yb_ref.at[rows(b), :], sem_out.at[sl])

    for k in range(FFN_BUFS - 1):
        @pl.when(k < nb)
        def _(k=k):
            x_copy(b0 + k, k).start(priority=1)

    def block(j, c):
        sl = j % FFN_BUFS
        x_copy(b0 + j, sl).wait()
        ahead = j + (FFN_BUFS - 1)

        @pl.when(ahead < nb)
        def _():
            x_copy(b0 + ahead, ahead % FFN_BUFS).start(priority=1)

        @pl.when(j >= FFN_BUFS)
        def _():
            y_copy(b0 + j - FFN_BUFS, sl).wait()

        x = xbuf[sl]
        row = lax.broadcasted_iota(I32, x.shape, 0)
        xb = jnp.where(row < nv_ref[b0 + j], x, jnp.zeros_like(x))
        a = jnp.dot(xb, w1s[...], preferred_element_type=F32)
        b = jnp.dot(xb, w3s[...], preferred_element_type=F32)
        hid = (a * jax.nn.sigmoid(a) * b).astype(BF16)
        ybuf[sl] = jnp.dot(hid, w2s[...], preferred_element_type=F32).astype(BF16)
        y_copy(b0 + j, sl).start(priority=1)
        return c
    lax.fori_loop(0, nb, block, 0)

    for k in range(FFN_BUFS):
        @pl.when(nb > k)
        def _(k=k):
            y_copy(b0 + nb - 1 - k, (nb - 1 - k) % FFN_BUFS).wait()

    @pl.when(e == N_EXPERTS - 1)
    def _():
        zbuf[...] = jnp.zeros_like(zbuf)

        def z_copy(b):
            return pltpu.make_async_copy(zbuf, yb_ref.at[rows(b), :], fsem)

        def start(b, c):
            z_copy(b).start()
            return c

        def wait(b, c):
            z_copy(b).wait()
            return c
        lax.fori_loop(fill_ref[0], n_blocks, start, 0)
        lax.fori_loop(fill_ref[0], n_blocks, wait, 0)


def _expert_ffn(xb, blk_start, blk_count, n_valid, used_blocks, w1, w3, w2, layer):
    r = xb.shape[0]
    d, f = w1.shape[2], w1.shape[3]
    bm = MOE_ROWS
    return pl.pallas_call(
        functools.partial(_ffn_kernel, n_blocks=r // bm),
        grid_spec=pltpu.PrefetchScalarGridSpec(
            num_scalar_prefetch=4,
            grid=(N_EXPERTS,),
            in_specs=[
                pl.BlockSpec(memory_space=pl.ANY),
                pl.BlockSpec((None, None, d, f), lambda e, *_: (layer, e, 0, 0)),
                pl.BlockSpec((None, None, d, f), lambda e, *_: (layer, e, 0, 0)),
                pl.BlockSpec((None, None, f, d), lambda e, *_: (layer, e, 0, 0)),
            ],
            out_specs=pl.BlockSpec(memory_space=pl.ANY),
            scratch_shapes=[
                pltpu.VMEM((d, f), BF16), pltpu.VMEM((d, f), BF16), pltpu.VMEM((f, d), BF16),
                pltpu.VMEM((FFN_BUFS, bm, d), BF16), pltpu.VMEM((FFN_BUFS, bm, d), BF16), pltpu.VMEM((bm, d), BF16),
                pltpu.SemaphoreType.DMA((FFN_BUFS,)), pltpu.SemaphoreType.DMA((FFN_BUFS,)), pltpu.SemaphoreType.DMA(()),
            ],
        ),
        out_shape=jax.ShapeDtypeStruct((r, d), BF16),
        compiler_params=_cparams(("arbitrary",)),
        name="moe_expert_mlp",
    )(blk_start, blk_count, n_valid, used_blocks, xb, w1, w3, w2)


def _combine_kernel(tab_ref, lused_ref, yb_ref, pos_ref, gate_ref, x_ref, mod_ref, o_ref, ys, sem, *, n_tiles):
    i = pl.program_id(0)
    slot = i % 2
    tm = x_ref.shape[0]

    def chunk_copy(sl, src, dst):
        return pltpu.make_async_copy(yb_ref.at[_chunk_rows(src), :], ys.at[sl, _chunk_rows(dst), :], sem.at[sl])

    def fetch(tile, sl):
        base = tile * MAX_TILE_CHUNKS

        def per_chunk(c, cc):
            chunk_copy(sl, tab_ref[base + c], c).start()
            return cc
        lax.fori_loop(0, lused_ref[tile], per_chunk, 0)

    @pl.when(i == 0)
    def _():
        fetch(0, 0)

    @pl.when(i + 1 < n_tiles)
    def _():
        fetch(jnp.minimum(i + 1, n_tiles - 1), 1 - slot)

    _wait_rows(lambda rows: pltpu.make_async_copy(yb_ref.at[pl.ds(0, rows), :], ys.at[slot, pl.ds(0, rows), :],
                                                  sem.at[slot]), lused_ref[i])

    y = ys[slot]
    used = lused_ref[i] * CHUNK
    rowi = lax.broadcasted_iota(I32, y.shape, 0)
    y16 = jnp.where(rowi < used, y, jnp.zeros_like(y))
    pos = pos_ref[...]
    gate = gate_ref[...]
    lane = lax.broadcasted_iota(I32, (tm, TILE_BUF), 1)
    gm = (jnp.where(lane == pos[:, 0:1], gate[:, 0:1], 0.0) + jnp.where(lane == pos[:, 1:2], gate[:, 1:2], 0.0))
    ghi, glo = _split_bf16(gm)
    mix = jnp.dot(ghi, y16, preferred_element_type=F32) + jnp.dot(glo, y16, preferred_element_type=F32)
    o_ref[...] = x_ref[...] + mod_ref[5:6, :] * mix


def _combine(yb, pos_cols, gates, tab, lused, x, mod, n_lat_tiles, n_tiles):
    d = x.shape[1]
    tm = ROW_TILE
    return pl.pallas_call(
        functools.partial(_combine_kernel, n_tiles=n_tiles),
        grid_spec=pltpu.PrefetchScalarGridSpec(
            num_scalar_prefetch=2,
            grid=(n_tiles,),
            in_specs=[
                pl.BlockSpec(memory_space=pl.ANY),
                pl.BlockSpec((tm, 8), lambda i, *_: (i, 0)),
                pl.BlockSpec((tm, 8), lambda i, *_: (i, 0)),
                pl.BlockSpec((tm, d), lambda i, *_: (i, 0)),
                pl.BlockSpec((None, 6, d), lambda i, *_: (jnp.where(i >= n_lat_tiles, 1, 0), 0, 0)),
            ],
            out_specs=pl.BlockSpec((tm, d), lambda i, *_: (i, 0)),
            scratch_shapes=[pltpu.VMEM((2, TILE_BUF, d), BF16), pltpu.SemaphoreType.DMA((2,))],
        ),
        out_shape=jax.ShapeDtypeStruct((n_tiles * tm, d), F32),
        compiler_params=_cparams(("arbitrary",)),
        name="moe_combine",
    )(tab, lused, yb, pos_cols, gates, x, mod)


def _moe_layer(x, routing, mod, w1, w3, w2, layer, n_lat_tiles, n_tiles):
    tm = ROW_TILE
    rows = n_tiles * tm
    cpb = MOE_ROWS // CHUNK
    h, ri, rg, cnt3 = routing
    cnt = cnt3[:, :, 0].astype(I32)
    nch = (cnt + CHUNK - 1) // CHUNK
    lbase = jnp.cumsum(nch, axis=1) - nch
    lused = jnp.sum(nch, axis=1).astype(I32)
    tot = jnp.sum(nch, axis=0)
    reg = (tot + cpb - 1) // cpb * cpb
    gend = jnp.cumsum(reg)
    gstart = gend - reg
    gpos = gstart[None, :] + jnp.cumsum(nch, axis=0) - nch
    rows_max = 2 * rows + n_tiles * N_EXPERTS * (CHUNK - 1) + N_EXPERTS * (MOE_ROWS - CHUNK)
    n_blocks = -(-rows_max // MOE_ROWS)
    ex = jnp.arange(N_EXPERTS, dtype=I32)
    blk0 = jnp.arange(n_blocks, dtype=I32) * cpb
    block_exp = jnp.minimum(jnp.sum((gend[None, :] <= blk0[:, None]).astype(I32), axis=1), N_EXPERTS - 1)
    sel = block_exp[:, None] == ex[None, :]
    tot_b = jnp.sum(jnp.where(sel, tot[None, :], 0), axis=1)
    st_b = jnp.sum(jnp.where(sel, gstart[None, :], 0), axis=1)
    n_valid = jnp.clip((tot_b - (blk0 - st_b)) * CHUNK, 0, MOE_ROWS).astype(I32)
    lb_tok = jnp.repeat(lbase, tm, axis=0)
    at = lambda e: jnp.sum(jnp.where(e[:, None] == ex[None, :], lb_tok, 0), axis=1)
    pos1 = CHUNK * at(ri[0]) + ri[2]
    pos2 = CHUNK * at(ri[1]) + ri[3]
    zero = jnp.zeros_like(pos1)
    pos_rows = jnp.stack([pos1, pos2] + [zero] * 6, axis=0).astype(I32)
    slot_id = jnp.arange(MAX_TILE_CHUNKS, dtype=I32)
    owner = jnp.sum((lbase + nch)[:, None, :] <= slot_id[None, :, None], axis=2)
    own = jnp.minimum(owner, N_EXPERTS - 1)[:, :, None] == ex[None, None, :]
    tab = jnp.sum(jnp.where(own, (gpos - lbase)[:, None, :], 0), axis=2) + slot_id[None, :]
    tab = tab.astype(I32).reshape(-1)
    fill = jnp.concatenate([gstart + tot, reg - tot, gend[-1:] // cpb]).astype(I32)
    xb = _dispatch(h, pos_rows, tab, lused, fill, n_blocks)
    yb = _expert_ffn(xb, (gstart // cpb).astype(I32), (reg // cpb).astype(I32), n_valid,
                     (gend[-1:] // cpb).astype(I32), w1, w3, w2, layer)
    return _combine(yb, pos_rows.T, rg.T, tab, lused, x, mod, n_lat_tiles, n_tiles)


def _router_matrix(w_rg, b_rg, w_re, b_re):
    d = w_rg.shape[0]
    wr = jnp.zeros((d, LANES), F32)
    wr = wr.at[:, 0:N_GROUPS].set(w_rg.astype(F32))
    wr = wr.at[:, ROUTER_EXPERT_ROW0:ROUTER_EXPERT_ROW0 + N_EXPERTS].set(w_re.astype(F32))
    wr = jnp.stack(_split_bf16(wr), axis=0)
    br = jnp.zeros((LANES,), F32)
    br = br.at[0:N_GROUPS].set(b_rg.astype(F32))
    br = br.at[ROUTER_EXPERT_ROW0:ROUTER_EXPERT_ROW0 + N_EXPERTS].set(b_re.astype(F32))
    return wr, jnp.broadcast_to(br[:, None], (LANES, LANES))


def _dup_heads(w):
    d = w.shape[0]
    w4 = w.reshape(d, N_KV_HEADS, 1, HEAD_DIM)
    return jnp.broadcast_to(w4, (d, N_KV_HEADS, 2, HEAD_DIM)).reshape(d, 2 * N_KV_HEADS * HEAD_DIM)


def kernel(x, c, ctx, c_ctx, w_mod, b_mod, norm_mix_g, norm_ffn_g, w_in_even, conv_w, w_out_even, w_qkv, q_norm_g,
           k_norm_g, sink_logit, w_o, w_router_g, b_router_g, w_router_e, b_router_e, w1, w3, w2):
    bsz, l, d = x.shape
    lc = ctx.shape[1]
    assert bsz == 1, "one sample per call"
    tm = ROW_TILE
    assert l % tm == 0 and lc % tm == 0 and l % lc == 0 and l % (DFT_MINOR * 8) == 0
    depth = w_mod.shape[0]
    t = l + lc
    assert t % DFT_MINOR == 0
    nl, nt = l // tm, t // tm

    xs = jnp.concatenate([x.reshape(l, d), ctx.reshape(lc, d)], axis=0)
    mod_all = _modulation(c, c_ctx, w_mod, b_mod).reshape(depth, 2, 6, d)
    cs, m1, a_tab, mc = _dft_tables(l, lc)
    rope = _rope_tables(l, t)
    segs = _segment_matrices()
    attn_bias = _attn_bias(l)
    tri = jnp.asarray(np.triu(np.ones((tm, tm), np.float32), 1)).astype(BF16)
    qd = N_HEADS * HEAD_DIM

    for layer in range(depth):
        last = layer == depth - 1
        j = layer // 2
        mod = mod_all[layer]
        g_mix = norm_mix_g[layer].reshape(1, d)
        g_ffn = norm_ffn_g[layer].reshape(1, d)
        wr, br = _router_matrix(w_router_g[layer], b_router_g[layer], w_router_e[layer], b_router_e[layer])
        route = (g_ffn, wr, br, tri)
        if layer % 2 == 0:
            z, bg, u = _even_in(xs, g_mix, mod, w_in_even[j].astype(BF16), cs, nl)
            f_lat, f_ctx = _fourier_seq(z, l, m1, a_tab, mc)
            xs, *routing = _even_out(f_lat, f_ctx, bg, u, conv_w[j], w_out_even[j].astype(BF16), xs, mod, route, nl)
        else:
            wq = w_qkv[j]
            w_all = jnp.concatenate([wq[:, :qd], _dup_heads(wq[:, qd:qd + N_KV_HEADS * HEAD_DIM]),
                                     _dup_heads(wq[:, qd + N_KV_HEADS * HEAD_DIM:])], axis=1).astype(BF16)
            qg = jnp.tile(q_norm_g[j], N_HEADS).reshape(1, qd)
            kg = jnp.tile(k_norm_g[j], 2 * N_KV_HEADS).reshape(1, 2 * N_KV_HEADS * HEAD_DIM)
            q, k, v = _odd_in(xs, g_mix, mod, w_all, qg, kg, segs, rope, nl, nt)
            n_out = nl if last else nt
            att = _attention(q, k, v, sink_logit[j].astype(F32) * LOG2E, attn_bias, l,
                             n_out * (tm // ATT_BLOCK))
            xs, *routing = _odd_out(att, w_o[j].astype(BF16), xs, mod, route, nl, n_out)
        n_moe = nl if last else nt
        xs = _moe_layer(xs, routing, mod, w1, w3, w2, layer, nl, n_moe)
    return xs[:l].reshape(bsz, l, d)
```

```python
import functools
import math

import numpy as np
import jax
import jax.numpy as jnp
from jax import lax
from jax.experimental import pallas as pl
from jax.experimental.pallas import tpu as pltpu

F32 = jnp.float32
BF16 = jnp.bfloat16
I32 = jnp.int32

EPS = 1e-6
NEG_INF = -1e30

GRID_W = 64
FOURIER_GROUPS = 4
FOURIER_GROUP_DIM = 128
FOURIER_WIDTH = FOURIER_GROUPS * FOURIER_GROUP_DIM
CONV_WIDTH = 512
N_HEADS = 16
N_KV_HEADS = 4
GQA_GROUP = N_HEADS // N_KV_HEADS
HEAD_DIM = 64
WINDOW = 128
ROPE_BASE = 10000.0
ROPE_PAIRS = HEAD_DIM // 4
N_GROUPS = 4
EXPERTS_PER_GROUP = 8
N_EXPERTS = N_GROUPS * EXPERTS_PER_GROUP

LANES = 128
ROW_TILE = 256
ATT_BLOCK = 128
MOE_ROWS = 256
DFT_MINOR = 128
VMEM_LIMIT = 48 * 1024 * 1024

LOG2E = math.log2(math.e)


def _cparams(sem):
    return pltpu.CompilerParams(dimension_semantics=sem, vmem_limit_bytes=VMEM_LIMIT)


def _split_bf16(x):
    hi = x.astype(BF16)
    lo = (x - hi.astype(F32)).astype(BF16)
    return hi, lo


def _mod_kernel(ct_ref, w_ref, b_ref, o_ref):
    ct = ct_ref[...]
    s = ct * jax.nn.sigmoid(ct)
    w = w_ref[...]
    r0 = jnp.sum(w * s[:, 0:1], axis=0, keepdims=True)
    r1 = jnp.sum(w * s[:, 1:2], axis=0, keepdims=True)
    o_ref[...] = jnp.concatenate([r0, r1], axis=0) + b_ref[...]


def _modulation(c, c_ctx, w_mod, b_mod):
    depth, d, n = w_mod.shape
    tn = 512
    ct = jnp.stack([c.reshape(d), c_ctx.reshape(d)], axis=1)
    return pl.pallas_call(
        _mod_kernel,
        grid=(depth, n // tn),
        in_specs=[
            pl.BlockSpec((d, 2), lambda l, j: (0, 0)),
            pl.BlockSpec((None, d, tn), lambda l, j: (l, 0, j)),
            pl.BlockSpec((None, 1, tn), lambda l, j: (l, 0, j)),
        ],
        out_specs=pl.BlockSpec((None, 2, tn), lambda l, j: (l, 0, j)),
        out_shape=jax.ShapeDtypeStruct((depth, 2, n), F32),
        compiler_params=_cparams(("arbitrary", "arbitrary")),
        name="modulation",
    )(ct, w_mod, b_mod.reshape(depth, 1, n))


def _norm_mod(x, g, shift, scale):
    ms = jnp.mean(x * x, axis=-1, keepdims=True)
    y = x * lax.rsqrt(ms + EPS) * g
    return y * (1.0 + scale) + shift


def _stream_of(n_lat_tiles):
    return lambda i: (jnp.where(i >= n_lat_tiles, 1, 0), 0, 0)


def _row_sources(x, n_lat_tiles, n_tiles, d):
    tm = ROW_TILE
    lat = pl.BlockSpec((tm, d), lambda i: (jnp.minimum(i, n_lat_tiles - 1), 0))
    if isinstance(x, tuple):
        ctx = pl.BlockSpec((tm, d), lambda i: (jnp.clip(i - n_lat_tiles, 0, n_tiles - n_lat_tiles - 1), 0))
        return x, [lat, ctx]
    return (x, x), [lat, pl.BlockSpec((tm, d), lambda i: (jnp.clip(i, n_lat_tiles, n_tiles - 1), 0))]


def _pick_rows(xl_ref, xc_ref, n_lat_tiles):
    is_ctx = jnp.full(xl_ref.shape, pl.program_id(0), I32) >= n_lat_tiles
    return jnp.where(is_ctx, xc_ref[...], xl_ref[...])


def _even_in_kernel(xl_ref, xc_ref, g_ref, mod_ref, w_ref, cs_ref, z_ref, bg_ref, u_ref, *, n_lat_tiles):
    h = _norm_mod(_pick_rows(xl_ref, xc_ref, n_lat_tiles), g_ref[...], mod_ref[0:1, :], mod_ref[1:2, :])
    p = jnp.dot(h.astype(BF16), w_ref[...], preferred_element_type=F32)
    cs = cs_ref[...]
    for grp in range(FOURIER_GROUPS):
        lo = grp * FOURIER_GROUP_DIM
        a = p[:, lo:lo + FOURIER_GROUP_DIM].astype(BF16)
        z = jnp.dot(a, cs, preferred_element_type=F32)
        z_ref[0, :, lo:lo + FOURIER_GROUP_DIM] = z[:, :FOURIER_GROUP_DIM]
        z_ref[1, :, lo:lo + FOURIER_GROUP_DIM] = z[:, FOURIER_GROUP_DIM:]
    o = FOURIER_WIDTH
    bg_ref[...] = p[:, o:o + CONV_WIDTH].astype(BF16)
    u_ref[...] = (p[:, o + CONV_WIDTH:o + 2 * CONV_WIDTH] * p[:, o + 2 * CONV_WIDTH:]).astype(BF16)


def _even_in(x, g, mod, w_in, cs, n_lat_tiles, n_tiles):
    d, n = w_in.shape
    tm = ROW_TILE
    t = n_tiles * tm
    xs, x_specs = _row_sources(x, n_lat_tiles, n_tiles, d)
    return pl.pallas_call(
        functools.partial(_even_in_kernel, n_lat_tiles=n_lat_tiles),
        grid=(n_tiles,),
        in_specs=x_specs + [
            pl.BlockSpec((1, d), lambda i: (0, 0)),
            pl.BlockSpec((None, 6, d), _stream_of(n_lat_tiles)),
            pl.BlockSpec((d, n), lambda i: (0, 0)),
            pl.BlockSpec(cs.shape, lambda i: (0, 0)),
        ],
        out_specs=[
            pl.BlockSpec((2, tm, FOURIER_WIDTH), lambda i: (0, i, 0)),
            pl.BlockSpec((tm, CONV_WIDTH), lambda i: (i, 0)),
            pl.BlockSpec((tm, CONV_WIDTH), lambda i: (i, 0)),
        ],
        out_shape=[
            jax.ShapeDtypeStruct((2, t, FOURIER_WIDTH), F32),
            jax.ShapeDtypeStruct((t, CONV_WIDTH), BF16),
            jax.ShapeDtypeStruct((t, CONV_WIDTH), BF16),
        ],
        compiler_params=_cparams(("arbitrary",)),
        name="even_in_proj",
    )(*xs, g, mod, w_in, cs)


def _dft_major_kernel(m_ref, z_ref, o_ref):
    m = m_ref[...]
    n1, c = z_ref.shape[1], z_ref.shape[3]
    for j in range(z_ref.shape[2]):
        zc = z_ref[:, :, j, :].reshape(2 * n1, c).astype(BF16)
        g = jnp.dot(m, zc, preferred_element_type=F32)
        o_ref[:, :, j, :] = g.reshape(2, n1, c)


def _dft_major(m1, z4, n1):
    c = z4.shape[-1]
    cb = 8
    return pl.pallas_call(
        _dft_major_kernel,
        grid=(DFT_MINOR // cb,),
        in_specs=[pl.BlockSpec(m1.shape, lambda j: (0, 0)), pl.BlockSpec((2, n1, cb, c), lambda j: (0, 0, j, 0))],
        out_specs=pl.BlockSpec((2, n1, cb, c), lambda j: (0, 0, j, 0)),
        out_shape=jax.ShapeDtypeStruct((2, n1, DFT_MINOR, c), F32),
        compiler_params=_cparams(("arbitrary",)),
        name="dft_major",
    )(m1, z4)


def _dft_minor_kernel(a_ref, g_ref, o_ref):
    for j in range(a_ref.shape[0]):
        gcat = jnp.concatenate([g_ref[0, j], g_ref[1, j]], axis=0).astype(BF16)
        o_ref[:, j, :] = jnp.dot(a_ref[j], gcat, preferred_element_type=F32)


def _dft_minor(a_tab, g4):
    n1, m, k2 = a_tab.shape
    c = g4.shape[-1]
    kb = 8
    return pl.pallas_call(
        _dft_minor_kernel,
        grid=(n1 // kb,),
        in_specs=[
            pl.BlockSpec((kb, m, k2), lambda i: (i, 0, 0)),
            pl.BlockSpec((2, kb, DFT_MINOR, c), lambda i: (0, i, 0, 0)),
        ],
        out_specs=pl.BlockSpec((DFT_MINOR, kb, c), lambda i: (0, i, 0)),
        out_shape=jax.ShapeDtypeStruct((DFT_MINOR, n1, c), F32),
        compiler_params=_cparams(("arbitrary",)),
        name="dft_minor",
    )(a_tab, g4)


def _dft_ctx_kernel(a_ref, z_ref, o_ref):
    o_ref[...] = jnp.dot(a_ref[...], z_ref[...].astype(BF16), preferred_element_type=F32)


def _dft_ctx(mc, zc):
    lc, c = mc.shape[0], zc.shape[1]
    return pl.pallas_call(
        _dft_ctx_kernel,
        grid=(1,),
        in_specs=[pl.BlockSpec(mc.shape, lambda i: (0, 0)), pl.BlockSpec(zc.shape, lambda i: (0, 0))],
        out_specs=pl.BlockSpec((lc, c), lambda i: (0, 0)),
        out_shape=jax.ShapeDtypeStruct((lc, c), F32),
        compiler_params=_cparams(("arbitrary",)),
        name="dft_context",
    )(mc, zc)


def _dft_tables(l, lc):
    gd = FOURIER_GROUP_DIM
    kk = np.arange(gd)
    ang = 2.0 * np.pi * ((kk[:, None] * kk[None, :]) % gd) / gd
    cs = np.concatenate([np.cos(ang), -np.sin(ang)], axis=1)
    n1 = l // DFT_MINOR
    k1 = np.arange(n1)
    ang1 = 2.0 * np.pi * ((k1[:, None] * k1[None, :]) % n1) / n1
    c1, s1 = np.cos(ang1), np.sin(ang1)
    m1 = np.block([[c1, s1], [-s1, c1]])
    l2 = np.arange(DFT_MINOR)
    kfull = k1[:, None, None] + n1 * l2[None, :, None]
    ang2 = 2.0 * np.pi * ((kfull * l2[None, None, :]) % l) / l
    sc = 1.0 / math.sqrt(l * gd)
    a_tab = np.concatenate([np.cos(ang2), np.sin(ang2)], axis=2) * sc
    kc = np.arange(lc)
    angc = 2.0 * np.pi * ((kc[:, None] * kc[None, :]) % lc) / lc
    mc = np.concatenate([np.cos(angc), np.sin(angc)], axis=1) / math.sqrt(lc * gd)
    as_bf16 = lambda v: jnp.asarray(v, F32).astype(BF16)
    return as_bf16(cs), as_bf16(m1), as_bf16(a_tab), as_bf16(mc)


def _fourier_seq(z, l, m1, a_tab, mc):
    t, c = z.shape[1], z.shape[2]
    lc = t - l
    n1 = l // DFT_MINOR
    g4 = _dft_major(m1, z.reshape(2, t // DFT_MINOR, DFT_MINOR, c), n1)
    f_lat = _dft_minor(a_tab, g4).reshape(l, c)
    return f_lat, _dft_ctx(mc, z[:, l:].reshape(2 * lc, c))


ROUTER_EXPERT_ROW0 = 8


def _route_tile(x, g_ref, mod_ref, wr_ref, br_ref, tri_ref, h_ref, ri_ref, rg_ref, cnt_ref):
    h = _norm_mod(x, g_ref[...], mod_ref[3:4, :], mod_ref[4:5, :])
    h_ref[...] = h.astype(BF16)
    tm = h.shape[0]
    h_hi, h_lo = _split_bf16(h)
    w_hi, w_lo = wr_ref[0], wr_ref[1]
    logits = (jnp.dot(h_hi, w_hi, preferred_element_type=F32) + jnp.dot(h_hi, w_lo, preferred_element_type=F32)
              + jnp.dot(h_lo, w_hi, preferred_element_type=F32))
    logits = logits.T + br_ref[:, 0:1]
    gl = logits[0:N_GROUPS]
    gmax = jnp.max(gl, axis=0, keepdims=True)
    gi = lax.broadcasted_iota(I32, gl.shape, 0)
    g_idx = jnp.min(jnp.where(gl == gmax, gi, N_GROUPS), axis=0, keepdims=True)
    g_val = 1.0 / jnp.sum(jnp.exp(gl - gmax), axis=0, keepdims=True)
    e_in = logits[ROUTER_EXPERT_ROW0:ROUTER_EXPERT_ROW0 + EXPERTS_PER_GROUP]
    for grp in range(1, N_GROUPS):
        lo = ROUTER_EXPERT_ROW0 + grp * EXPERTS_PER_GROUP
        e_in = jnp.where(g_idx == grp, logits[lo:lo + EXPERTS_PER_GROUP], e_in)
    ei = lax.broadcasted_iota(I32, e_in.shape, 0)
    v1 = jnp.max(e_in, axis=0, keepdims=True)
    i1 = jnp.min(jnp.where(e_in == v1, ei, EXPERTS_PER_GROUP), axis=0, keepdims=True)
    rest = jnp.where(ei == i1, -jnp.inf, e_in)
    v2 = jnp.max(rest, axis=0, keepdims=True)
    i2 = jnp.min(jnp.where(rest == v2, ei, EXPERTS_PER_GROUP), axis=0, keepdims=True)
    w2 = jnp.exp(v2 - v1)
    gate1 = g_val / (1.0 + w2)
    gate2 = g_val * w2 / (1.0 + w2)
    e1 = g_idx * EXPERTS_PER_GROUP + i1
    e2 = g_idx * EXPERTS_PER_GROUP + i2
    xi = lax.broadcasted_iota(I32, (N_EXPERTS, tm), 0)
    oh1 = xi == e1
    oh2 = xi == e2
    oh = oh1.astype(F32) + oh2.astype(F32)
    before = jnp.dot(oh.astype(BF16), tri_ref[...], preferred_element_type=F32)
    rank1 = jnp.sum(jnp.where(oh1, before, 0.0), axis=0, keepdims=True)
    rank2 = jnp.sum(jnp.where(oh2, before, 0.0), axis=0, keepdims=True)
    cnt_ref[...] = jnp.broadcast_to(jnp.sum(oh, axis=1, keepdims=True), cnt_ref.shape)
    orow = lax.broadcasted_iota(I32, (8, tm), 0)
    ri_ref[...] = jnp.where(orow == 0, e1, jnp.where(orow == 1, e2, jnp.where(
        orow == 2, rank1.astype(I32), jnp.where(orow == 3, rank2.astype(I32), 0))))
    rg_ref[...] = jnp.where(orow == 0, gate1, jnp.where(orow == 1, gate2, 0.0))


def _route_specs(route, d, n_tiles):
    tm = ROW_TILE
    const = lambda a: pl.BlockSpec(a.shape, lambda i: (0,) * a.ndim)
    in_specs = [const(a) for a in route]
    out_specs = [
        pl.BlockSpec((tm, d), lambda i: (i, 0)),
        pl.BlockSpec((8, tm), lambda i: (0, i)),
        pl.BlockSpec((8, tm), lambda i: (0, i)),
        pl.BlockSpec((None, N_EXPERTS, LANES), lambda i: (i, 0, 0)),
    ]
    rows = n_tiles * tm
    out_shape = [
        jax.ShapeDtypeStruct((rows, d), BF16),
        jax.ShapeDtypeStruct((8, rows), I32),
        jax.ShapeDtypeStruct((8, rows), F32),
        jax.ShapeDtypeStruct((n_tiles, N_EXPERTS, LANES), F32),
    ]
    return in_specs, out_specs, out_shape


def _even_out_kernel(fl_ref, fc_ref, bg_ref, u_ref, up_ref, un_ref, cw_ref, w_ref, xl_ref, xc_ref, mod_ref,
                     gf_ref, wr_ref, br_ref, tri_ref, o_ref, h_ref, ri_ref, rg_ref, cnt_ref, *, n_lat_tiles, n_tiles):
    i = pl.program_id(0)
    tm = u_ref.shape[0]
    u = u_ref[...].astype(F32)
    row = lax.broadcasted_iota(I32, u.shape, 0)
    first = jnp.logical_or(i == 0, i == n_lat_tiles)
    last = jnp.logical_or(i == n_lat_tiles - 1, i == n_tiles - 1)
    hb = up_ref.shape[0]
    halo_p = up_ref[...].astype(F32)[hb - 1:hb, :] * jnp.where(first, 0.0, 1.0)
    halo_n = un_ref[...].astype(F32)[0:1, :] * jnp.where(last, 0.0, 1.0)
    u_prev = jnp.where(row == 0, halo_p, pltpu.roll(u, 1, axis=0))
    u_next = jnp.where(row == tm - 1, halo_n, pltpu.roll(u, tm - 1, axis=0))
    cw = cw_ref[...]
    y = bg_ref[...].astype(F32) * (cw[0:1, :] * u_prev + cw[1:2, :] * u + cw[2:3, :] * u_next)
    is_ctx = jnp.full(fl_ref.shape, i, I32) >= n_lat_tiles
    f = jnp.where(is_ctx, fc_ref[...], fl_ref[...])
    acc = jnp.dot(f.astype(BF16), w_ref[0:FOURIER_WIDTH, :], preferred_element_type=F32)
    acc += jnp.dot(y.astype(BF16), w_ref[FOURIER_WIDTH:, :], preferred_element_type=F32)
    x_new = _pick_rows(xl_ref, xc_ref, n_lat_tiles) + mod_ref[2:3, :] * acc
    o_ref[...] = x_new
    _route_tile(x_new, gf_ref, mod_ref, wr_ref, br_ref, tri_ref, h_ref, ri_ref, rg_ref, cnt_ref)


def _even_out(f_lat, f_ctx, bg, u, conv_w, w_out, x, mod, route, n_lat_tiles):
    t, d = u.shape[0], w_out.shape[1]
    tm = ROW_TILE
    hb = 16
    n_tiles = t // tm
    r = tm // hb
    r_in, r_out, r_shape = _route_specs(route, d, n_tiles)
    xs, x_specs = _row_sources(x, n_lat_tiles, n_tiles, d)
    return pl.pallas_call(
        functools.partial(_even_out_kernel, n_lat_tiles=n_lat_tiles, n_tiles=n_tiles),
        grid=(n_tiles,),
        in_specs=[
            pl.BlockSpec((tm, FOURIER_WIDTH), lambda i: (jnp.minimum(i, n_lat_tiles - 1), 0)),
            pl.BlockSpec((tm, FOURIER_WIDTH), lambda i: (jnp.clip(i - n_lat_tiles, 0, n_tiles - n_lat_tiles - 1), 0)),
            pl.BlockSpec((tm, CONV_WIDTH), lambda i: (i, 0)),
            pl.BlockSpec((tm, CONV_WIDTH), lambda i: (i, 0)),
            pl.BlockSpec((hb, CONV_WIDTH), lambda i: (jnp.maximum(i * r - 1, 0), 0)),
            pl.BlockSpec((hb, CONV_WIDTH), lambda i: (jnp.minimum((i + 1) * r, t // hb - 1), 0)),
            pl.BlockSpec(conv_w.shape, lambda i: (0, 0)),
            pl.BlockSpec(w_out.shape, lambda i: (0, 0)),
        ] + x_specs + [
            pl.BlockSpec((None, 6, d), _stream_of(n_lat_tiles)),
        ] + r_in,
        out_specs=[pl.BlockSpec((tm, d), lambda i: (i, 0))] + r_out,
        out_shape=[jax.ShapeDtypeStruct((t, d), F32)] + r_shape,
        compiler_params=_cparams(("arbitrary",)),
        name="even_out_proj",
    )(f_lat, f_ctx, bg, u, u, u, conv_w, w_out, *xs, mod, *route)


def _seg_rms_scale(v, seg, seg_t):
    ss = jnp.dot((v * v).astype(BF16), seg, preferred_element_type=F32)
    inv = lax.rsqrt(ss * (1.0 / HEAD_DIM) + EPS)
    ihi, ilo = _split_bf16(inv)
    return (jnp.dot(ihi, seg_t, preferred_element_type=F32)
            + jnp.dot(ilo, seg_t, preferred_element_type=F32))


def _rope_cols(v, cos, sa, sb, scale):
    cols = []
    for j in range(v.shape[1] // LANES):
        c = v[:, j * LANES:(j + 1) * LANES]
        r = c * cos + pltpu.roll(c, LANES - ROPE_PAIRS, axis=1) * sa + pltpu.roll(c, ROPE_PAIRS, axis=1) * sb
        cols.append(r * scale if scale != 1.0 else r)
    return jnp.concatenate(cols, axis=1)


def _odd_in_kernel(x_ref, g_ref, mod_ref, w_ref, qg_ref, kg_ref, segq_ref, segqt_ref, segk_ref, segkt_ref,
                   rope_ref, q_ref, k_ref, v_ref):
    h = _norm_mod(x_ref[...], g_ref[...], mod_ref[0:1, :], mod_ref[1:2, :])
    p = jnp.dot(h.astype(BF16), w_ref[...], preferred_element_type=F32)
    qd = N_HEADS * HEAD_DIM
    kd = 2 * N_KV_HEADS * HEAD_DIM
    cos, sa, sb = rope_ref[0], rope_ref[1], rope_ref[2]
    q = p[:, :qd]
    q = q * _seg_rms_scale(q, segq_ref[...], segqt_ref[...]) * qg_ref[...]
    q_ref[...] = _rope_cols(q, cos, sa, sb, HEAD_DIM ** -0.5 * LOG2E).astype(BF16)
    k = p[:, qd:qd + kd]
    k = k * _seg_rms_scale(k, segk_ref[...], segkt_ref[...]) * kg_ref[...]
    k_ref[...] = _rope_cols(k, cos, sa, sb, 1.0).astype(BF16)
    v_ref[...] = p[:, qd + kd:].astype(BF16)


def _odd_in(x, g, mod, w_qkv, qg, kg, segs, rope, n_lat_tiles, n_tiles):
    t, d = x.shape
    n = w_qkv.shape[1]
    tm = ROW_TILE
    rows = n_tiles * tm
    qd = N_HEADS * HEAD_DIM
    kd = 2 * N_KV_HEADS * HEAD_DIM
    segq, segqt, segk, segkt = segs
    const = lambda a: pl.BlockSpec(a.shape, lambda i: (0,) * a.ndim)
    return pl.pallas_call(
        _odd_in_kernel,
        grid=(n_tiles,),
        in_specs=[
            pl.BlockSpec((tm, d), lambda i: (i, 0)),
            const(g),
            pl.BlockSpec((None, 6, d), _stream_of(n_lat_tiles)),
            const(w_qkv), const(qg), const(kg), const(segq), const(segqt), const(segk), const(segkt),
            pl.BlockSpec((3, tm, LANES), lambda i: (0, i, 0)),
        ],
        out_specs=[
            pl.BlockSpec((tm, qd), lambda i: (i, 0)),
            pl.BlockSpec((tm, kd), lambda i: (i, 0)),
            pl.BlockSpec((tm, kd), lambda i: (i, 0)),
        ],
        out_shape=[
            jax.ShapeDtypeStruct((rows, qd), BF16),
            jax.ShapeDtypeStruct((rows, kd), BF16),
            jax.ShapeDtypeStruct((rows, kd), BF16),
        ],
        compiler_params=_cparams(("arbitrary",)),
        name="odd_in_proj",
    )(x, g, mod, w_qkv, qg, kg, segq, segqt, segk, segkt, rope)


def _rope_tables(l, t):
    pos = np.arange(l)
    freqs = ROPE_BASE ** (-np.arange(ROPE_PAIRS, dtype=np.float32) / ROPE_PAIRS)
    lane = np.arange(LANES) % HEAD_DIM
    axis = lane // (2 * ROPE_PAIRS)
    half = (lane % (2 * ROPE_PAIRS)) // ROPE_PAIRS
    pair = lane % ROPE_PAIRS
    p = np.where(axis[None, :] == 0, (pos // GRID_W)[:, None], (pos % GRID_W)[:, None]).astype(np.float32)
    ang = p * freqs[pair][None, :].astype(np.float32)
    cos, sin = np.cos(ang), np.sin(ang)
    sa = np.where(half[None, :] == 0, -sin, 0.0)
    sb = np.where(half[None, :] == 1, sin, 0.0)
    tab = np.zeros((3, t, LANES), np.float32)
    tab[0, :l], tab[1, :l], tab[2, :l] = cos, sa, sb
    tab[0, l:] = 1.0
    return jnp.asarray(tab)


def _segment_matrices():
    def seg(width):
        m = np.zeros((width, LANES), np.float32)
        m[np.arange(width), np.arange(width) // HEAD_DIM] = 1.0
        return m
    sq, sk = seg(N_HEADS * HEAD_DIM), seg(2 * N_KV_HEADS * HEAD_DIM)
    b = lambda v: jnp.asarray(v).astype(BF16)
    return b(sq), b(sq.T), b(sk), b(sk.T)


def _attn_kernel(sink_ref, q_ref, kp_ref, kc_ref, kn_ref, vp_ref, vc_ref, vn_ref, kx_ref, vx_ref, bias_ref, o_ref):
    bq = q_ref.shape[0]
    low = lax.broadcasted_iota(I32, (bq, LANES), 1) < HEAD_DIM
    top = lax.broadcasted_iota(I32, (LANES, bq), 0) < HEAD_DIM
    bias = bias_ref[...]
    nt = (((1,), (1,)), ((), ()))
    tn = (((0,), (0,)), ((), ()))
    st = []
    for g in range(N_KV_HEADS):
        ks = slice(g * LANES, (g + 1) * LANES)
        parts, sinks = [], []
        for j in range(GQA_GROUP):
            col = 2 * g + j // 2
            c = q_ref[:, col * LANES:(col + 1) * LANES]
            keep = low if j % 2 == 0 else jnp.logical_not(low)
            parts.append(jnp.where(keep, c, jnp.zeros_like(c)))
            sinks.append(jnp.full((1, bq), sink_ref[g * GQA_GROUP + j], F32))
        qs = jnp.concatenate(parts, axis=0)
        sink = jnp.concatenate(sinks, axis=1)
        kwin = jnp.concatenate([kp_ref[:, ks], kc_ref[:, ks], kn_ref[:, ks]], axis=0)
        s_loc = lax.dot_general(kwin, qs, nt, preferred_element_type=F32) + bias
        s_ctx = lax.dot_general(kx_ref[:, ks], qs, nt, preferred_element_type=F32)
        st.append((s_loc, s_ctx, sink))
    pr = []
    for g in range(N_KV_HEADS):
        s_loc, s_ctx, sink = st[g]
        m = jnp.maximum(jnp.maximum(jnp.max(s_loc, axis=0, keepdims=True),
                                    jnp.max(s_ctx, axis=0, keepdims=True)), sink)
        p_loc = jnp.exp2(s_loc - m)
        p_ctx = jnp.exp2(s_ctx - m)
        den = (jnp.sum(p_loc, axis=0, keepdims=True) + jnp.sum(p_ctx, axis=0, keepdims=True)
               + jnp.exp2(sink - m))
        pr.append((p_loc.astype(BF16), p_ctx.astype(BF16), 1.0 / den))
    for g in range(N_KV_HEADS):
        ks = slice(g * LANES, (g + 1) * LANES)
        p_loc, p_ctx, inv = pr[g]
        vwin = jnp.concatenate([vp_ref[:, ks], vc_ref[:, ks], vn_ref[:, ks]], axis=0)
        ot = lax.dot_general(vwin, p_loc, tn, preferred_element_type=F32)
        ot += lax.dot_general(vx_ref[:, ks], p_ctx, tn, preferred_element_type=F32)
        ot = ot * inv
        t0 = jnp.where(top, ot[:, 0:bq], ot[:, bq:2 * bq])
        t1 = jnp.where(top, ot[:, 2 * bq:3 * bq], ot[:, 3 * bq:4 * bq])
        o_ref[:, 2 * g * LANES:(2 * g + 1) * LANES] = t0.T.astype(o_ref.dtype)
        o_ref[:, (2 * g + 1) * LANES:(2 * g + 2) * LANES] = t1.T.astype(o_ref.dtype)


def _attn_bias(l):
    bq = ATT_BLOCK
    r = np.arange(GQA_GROUP * bq)[None, :] % bq
    col = np.arange(3 * bq)[:, None]
    band = np.abs(col - bq - r) <= WINDOW
    no_prev, no_next = col >= bq, col < 2 * bq
    masks = [band, band & no_prev, band & no_next, band & no_prev & no_next, np.zeros_like(band)]
    return jnp.asarray(np.where(np.stack(masks), 0.0, NEG_INF).astype(np.float32))


def _attention(q, k, v, sink, bias, l, n_q_blocks):
    t = q.shape[0]
    bq = ATT_BLOCK
    nlb = l // bq
    lc = t - l
    ctx_blk = l // lc
    kw = k.shape[1]
    kspec = lambda f: pl.BlockSpec((bq, kw), f)
    prev = lambda b, s: (jnp.clip(b - 1, 0, nlb - 1), 0)
    cur = lambda b, s: (jnp.minimum(b, nlb - 1), 0)
    nxt = lambda b, s: (jnp.clip(b + 1, 0, nlb - 1), 0)
    ctx = pl.BlockSpec((lc, kw), lambda b, s: (ctx_blk, 0))
    kind = lambda b, s: (jnp.where(b >= nlb, 4, (b == 0).astype(I32) + 2 * (b == nlb - 1).astype(I32)), 0, 0)
    return pl.pallas_call(
        _attn_kernel,
        grid_spec=pltpu.PrefetchScalarGridSpec(
            num_scalar_prefetch=1,
            grid=(n_q_blocks,),
            in_specs=[
                pl.BlockSpec((bq, q.shape[1]), lambda b, s: (b, 0)),
                kspec(prev), kspec(cur), kspec(nxt), kspec(prev), kspec(cur), kspec(nxt), ctx, ctx,
                pl.BlockSpec((None,) + bias.shape[1:], kind),
            ],
            out_specs=pl.BlockSpec((bq, q.shape[1]), lambda b, s: (b, 0)),
        ),
        out_shape=jax.ShapeDtypeStruct((n_q_blocks * bq, q.shape[1]), BF16),
        compiler_params=_cparams(("arbitrary",)),
        name="window_attention",
    )(sink, q, k, k, k, v, v, v, k, v, bias)


def _odd_out_kernel(a_ref, w_ref, x_ref, mod_ref, gf_ref, wr_ref, br_ref, tri_ref, o_ref, h_ref, ri_ref, rg_ref,
                    cnt_ref):
    acc = jnp.dot(a_ref[...], w_ref[...], preferred_element_type=F32)
    x_new = x_ref[...] + mod_ref[2:3, :] * acc
    o_ref[...] = x_new
    _route_tile(x_new, gf_ref, mod_ref, wr_ref, br_ref, tri_ref, h_ref, ri_ref, rg_ref, cnt_ref)


def _odd_out(a, w_o, x, mod, route, n_lat_tiles, n_tiles):
    d = x.shape[1]
    tm = ROW_TILE
    r_in, r_out, r_shape = _route_specs(route, d, n_tiles)
    return pl.pallas_call(
        _odd_out_kernel,
        grid=(n_tiles,),
        in_specs=[
            pl.BlockSpec((tm, a.shape[1]), lambda i: (i, 0)),
            pl.BlockSpec(w_o.shape, lambda i: (0, 0)),
            pl.BlockSpec((tm, d), lambda i: (i, 0)),
            pl.BlockSpec((None, 6, d), _stream_of(n_lat_tiles)),
        ] + r_in,
        out_specs=[pl.BlockSpec((tm, d), lambda i: (i, 0))] + r_out,
        out_shape=[jax.ShapeDtypeStruct((n_tiles * tm, d), F32)] + r_shape,
        compiler_params=_cparams(("arbitrary",)),
        name="odd_out_proj",
    )(a, w_o, x, mod, *route)


CHUNK = 8
TILE_BUF = 2 * ROW_TILE + N_EXPERTS * CHUNK
MAX_TILE_CHUNKS = TILE_BUF // CHUNK
FFN_BUFS = 4


def _chunk_rows(c):
    return pl.ds(pl.multiple_of(c * CHUNK, CHUNK), CHUNK)


def _wait_rows(copy_of_rows, n_chunks):
    bit = 1
    while bit <= MAX_TILE_CHUNKS:
        @pl.when((n_chunks & bit) != 0)
        def _(bit=bit):
            copy_of_rows(bit * CHUNK).wait()
        bit *= 2


def _dispatch_kernel(tab_ref, lused_ref, fill_ref, h_ref, pos_ref, xb_ref, hs, zbuf, sem, fsem, *, n_tiles, n_blocks):
    i = pl.program_id(0)
    slot = i % 2
    tm = h_ref.shape[0]

    def tail_copy(c):
        return pltpu.make_async_copy(zbuf.at[pl.ds(0, CHUNK), :], xb_ref.at[_chunk_rows(c), :], fsem)

    def block_copy(b):
        return pltpu.make_async_copy(zbuf, xb_ref.at[pl.ds(pl.multiple_of(b * MOE_ROWS, MOE_ROWS), MOE_ROWS), :], fsem)

    def fill(start):
        def tail(e, c):
            st, n = fill_ref[e], fill_ref[N_EXPERTS + e]

            def one(c2, cc):
                cp = tail_copy(st + c2)
                cp.start() if start else cp.wait()
                return cc
            return lax.fori_loop(0, n, one, c)
        lax.fori_loop(0, N_EXPERTS, tail, 0)

        def blk(b, c):
            cp = block_copy(b)
            cp.start() if start else cp.wait()
            return c
        lax.fori_loop(fill_ref[2 * N_EXPERTS], n_blocks, blk, 0)

    @pl.when(i == 0)
    def _():
        zbuf[...] = jnp.zeros_like(zbuf)
        fill(True)

    pos = pos_ref[...]
    r = lax.broadcasted_iota(I32, (TILE_BUF, tm), 0)
    onehot = jnp.where(jnp.logical_or(r == pos[0:1, :], r == pos[1:2, :]), 1.0, 0.0).astype(BF16)
    hs[slot] = jnp.dot(onehot, h_ref[...], preferred_element_type=F32).astype(BF16)

    def chunk_copy(sl, src, dst):
        return pltpu.make_async_copy(hs.at[sl, _chunk_rows(src), :], xb_ref.at[_chunk_rows(dst), :], sem.at[sl])

    base = i * MAX_TILE_CHUNKS

    def per_chunk(c, cc):
        chunk_copy(slot, c, tab_ref[base + c]).start()
        return cc
    lax.fori_loop(0, lused_ref[i], per_chunk, 0)

    def wait_chunks(sl, n):
        _wait_rows(lambda rows: pltpu.make_async_copy(hs.at[sl, pl.ds(0, rows), :], xb_ref.at[pl.ds(0, rows), :],
                                                      sem.at[sl]), n)

    @pl.when(i > 0)
    def _():
        wait_chunks(1 - slot, lused_ref[jnp.maximum(i - 1, 0)])

    @pl.when(i == n_tiles - 1)
    def _():
        wait_chunks(slot, lused_ref[i])
        fill(False)


def _dispatch(h, pos_rows, tab, lused, fill, n_blocks):
    t, d = h.shape
    tm = ROW_TILE
    n_tiles = t // tm
    return pl.pallas_call(
        functools.partial(_dispatch_kernel, n_tiles=n_tiles, n_blocks=n_blocks),
        grid_spec=pltpu.PrefetchScalarGridSpec(
            num_scalar_prefetch=3,
            grid=(n_tiles,),
            in_specs=[
                pl.BlockSpec((tm, d), lambda i, *_: (i, 0)),
                pl.BlockSpec((8, tm), lambda i, *_: (0, i)),
            ],
            out_specs=pl.BlockSpec(memory_space=pl.ANY),
            scratch_shapes=[
                pltpu.VMEM((2, TILE_BUF, d), BF16),
                pltpu.VMEM((MOE_ROWS, d), BF16),
                pltpu.SemaphoreType.DMA((2,)),
                pltpu.SemaphoreType.DMA(()),
            ],
        ),
        out_shape=jax.ShapeDtypeStruct((n_blocks * MOE_ROWS, d), BF16),
        compiler_params=_cparams(("arbitrary",)),
        name="moe_dispatch",
    )(tab, lused, fill, h, pos_rows)


def _ffn_kernel(b0_ref, nb_ref, nv_ref, fill_ref, xb_ref, w1_ref, w3_ref, w2_ref, yb_ref,
                w1s, w3s, w2s, xbuf, ybuf, zbuf, sem_in, sem_out, fsem, *, n_blocks):
    e = pl.program_id(0)
    w1s[...] = w1_ref[...].astype(BF16)
    w3s[...] = w3_ref[...].astype(BF16)
    w2s[...] = w2_ref[...].astype(BF16)
    b0, nb = b0_ref[e], nb_ref[e]

    def rows(b):
        return pl.ds(pl.multiple_of(b * MOE_ROWS, MOE_ROWS), MOE_ROWS)

    def x_copy(b, sl):
        return pltpu.make_async_copy(xb_ref.at[rows(b), :], xbuf.at[sl], sem_in.at[sl])

    def y_copy(b, sl):
        return pltpu.make_async_copy(ybuf.at[sl], yb_ref.at[rows(b), :], sem_out.at[sl])

    for k in range(FFN_BUFS - 1):
        @pl.when(k < nb)
        def _(k=k):
            x_copy(b0 + k, k).start(priority=1)

    def block(j, c):
        sl = j % FFN_BUFS
        x_copy(b0 + j, sl).wait()
        ahead = j + (FFN_BUFS - 1)

        @pl.when(ahead < nb)
        def _():
            x_copy(b0 + ahead, ahead % FFN_BUFS).start(priority=1)

        @pl.when(j >= FFN_BUFS)
        def _():
            y_copy(b0 + j - FFN_BUFS, sl).wait()

        x = xbuf[sl]
        row = lax.broadcasted_iota(I32, x.shape, 0)
        xb = jnp.where(row < nv_ref[b0 + j], x, jnp.zeros_like(x))
        a = jnp.dot(xb, w1s[...], preferred_element_type=F32)
        b = jnp.dot(xb, w3s[...], preferred_element_type=F32)
        hid = (a * jax.nn.sigmoid(a) * b).astype(BF16)
        ybuf[sl] = jnp.dot(hid, w2s[...], preferred_element_type=F32).astype(BF16)
        y_copy(b0 + j, sl).start(priority=1)
        return c
    lax.fori_loop(0, nb, block, 0)

    for k in range(FFN_BUFS):
        @pl.when(nb > k)
        def _(k=k):
            y_copy(b0 + nb - 1 - k, (nb - 1 - k) % FFN_BUFS).wait()

    @pl.when(e == N_EXPERTS - 1)
    def _():
        zbuf[...] = jnp.zeros_like(zbuf)

        def z_copy(b):
            return pltpu.make_async_copy(zbuf, yb_ref.at[rows(b), :], fsem)

        def start(b, c):
            z_copy(b).start()
            return c

        def wait(b, c):
            z_copy(b).wait()
            return c
        lax.fori_loop(fill_ref[0], n_blocks, start, 0)
        lax.fori_loop(fill_ref[0], n_blocks, wait, 0)


def _expert_ffn(xb, blk_start, blk_count, n_valid, used_blocks, w1, w3, w2, layer):
    r = xb.shape[0]
    d, f = w1.shape[2], w1.shape[3]
    bm = MOE_ROWS
    return pl.pallas_call(
        functools.partial(_ffn_kernel, n_blocks=r // bm),
        grid_spec=pltpu.PrefetchScalarGridSpec(
            num_scalar_prefetch=4,
            grid=(N_EXPERTS,),
            in_specs=[
                pl.BlockSpec(memory_space=pl.ANY),
                pl.BlockSpec((None, None, d, f), lambda e, *_: (layer, e, 0, 0)),
                pl.BlockSpec((None, None, d, f), lambda e, *_: (layer, e, 0, 0)),
                pl.BlockSpec((None, None, f, d), lambda e, *_: (layer, e, 0, 0)),
            ],
            out_specs=pl.BlockSpec(memory_space=pl.ANY),
            scratch_shapes=[
                pltpu.VMEM((d, f), BF16), pltpu.VMEM((d, f), BF16), pltpu.VMEM((f, d), BF16),
                pltpu.VMEM((FFN_BUFS, bm, d), BF16), pltpu.VMEM((FFN_BUFS, bm, d), BF16), pltpu.VMEM((bm, d), BF16),
                pltpu.SemaphoreType.DMA((FFN_BUFS,)), pltpu.SemaphoreType.DMA((FFN_BUFS,)), pltpu.SemaphoreType.DMA(()),
            ],
        ),
        out_shape=jax.ShapeDtypeStruct((r, d), BF16),
        compiler_params=_cparams(("arbitrary",)),
        name="moe_expert_mlp",
    )(blk_start, blk_count, n_valid, used_blocks, xb, w1, w3, w2)


def _combine_kernel(tab_ref, lused_ref, yb_ref, pos_ref, gate_ref, x_ref, mod_ref, o_ref, ys, sem, *, n_tiles):
    i = pl.program_id(0)
    slot = i % 2
    tm = x_ref.shape[0]

    def chunk_copy(sl, src, dst):
        return pltpu.make_async_copy(yb_ref.at[_chunk_rows(src), :], ys.at[sl, _chunk_rows(dst), :], sem.at[sl])

    def fetch(tile, sl):
        base = tile * MAX_TILE_CHUNKS

        def per_chunk(c, cc):
            chunk_copy(sl, tab_ref[base + c], c).start()
            return cc
        lax.fori_loop(0, lused_ref[tile], per_chunk, 0)

    @pl.when(i == 0)
    def _():
        fetch(0, 0)

    @pl.when(i + 1 < n_tiles)
    def _():
        fetch(jnp.minimum(i + 1, n_tiles - 1), 1 - slot)

    _wait_rows(lambda rows: pltpu.make_async_copy(yb_ref.at[pl.ds(0, rows), :], ys.at[slot, pl.ds(0, rows), :],
                                                  sem.at[slot]), lused_ref[i])

    y = ys[slot]
    used = lused_ref[i] * CHUNK
    rowi = lax.broadcasted_iota(I32, y.shape, 0)
    y16 = jnp.where(rowi < used, y, jnp.zeros_like(y))
    pos = pos_ref[...]
    gate = gate_ref[...]
    lane = lax.broadcasted_iota(I32, (tm, TILE_BUF), 1)
    gm = (jnp.where(lane == pos[:, 0:1], gate[:, 0:1], 0.0) + jnp.where(lane == pos[:, 1:2], gate[:, 1:2], 0.0))
    ghi, glo = _split_bf16(gm)
    mix = jnp.dot(ghi, y16, preferred_element_type=F32) + jnp.dot(glo, y16, preferred_element_type=F32)
    o_ref[...] = x_ref[...] + mod_ref[5:6, :] * mix


def _combine(yb, pos_cols, gates, tab, lused, x, mod, n_lat_tiles, n_tiles):
    d = x.shape[1]
    tm = ROW_TILE
    return pl.pallas_call(
        functools.partial(_combine_kernel, n_tiles=n_tiles),
        grid_spec=pltpu.PrefetchScalarGridSpec(
            num_scalar_prefetch=2,
            grid=(n_tiles,),
            in_specs=[
                pl.BlockSpec(memory_space=pl.ANY),
                pl.BlockSpec((tm, 8), lambda i, *_: (i, 0)),
                pl.BlockSpec((tm, 8), lambda i, *_: (i, 0)),
                pl.BlockSpec((tm, d), lambda i, *_: (i, 0)),
                pl.BlockSpec((None, 6, d), lambda i, *_: (jnp.where(i >= n_lat_tiles, 1, 0), 0, 0)),
            ],
            out_specs=pl.BlockSpec((tm, d), lambda i, *_: (i, 0)),
            scratch_shapes=[pltpu.VMEM((2, TILE_BUF, d), BF16), pltpu.SemaphoreType.DMA((2,))],
        ),
        out_shape=jax.ShapeDtypeStruct((n_tiles * tm, d), F32),
        compiler_params=_cparams(("arbitrary",)),
        name="moe_combine",
    )(tab, lused, yb, pos_cols, gates, x, mod)


def _moe_layer(x, routing, mod, w1, w3, w2, layer, n_lat_tiles, n_tiles):
    tm = ROW_TILE
    rows = n_tiles * tm
    cpb = MOE_ROWS // CHUNK
    h, ri, rg, cnt3 = routing
    cnt = cnt3[:, :, 0].astype(I32)
    nch = (cnt + CHUNK - 1) // CHUNK
    lbase = jnp.cumsum(nch, axis=1) - nch
    lused = jnp.sum(nch, axis=1).astype(I32)
    tot = jnp.sum(nch, axis=0)
    reg = (tot + cpb - 1) // cpb * cpb
    gend = jnp.cumsum(reg)
    gstart = gend - reg
    gpos = gstart[None, :] + jnp.cumsum(nch, axis=0) - nch
    rows_max = 2 * rows + n_tiles * N_EXPERTS * (CHUNK - 1) + N_EXPERTS * (MOE_ROWS - CHUNK)
    n_blocks = -(-rows_max // MOE_ROWS)
    ex = jnp.arange(N_EXPERTS, dtype=I32)
    blk0 = jnp.arange(n_blocks, dtype=I32) * cpb
    block_exp = jnp.minimum(jnp.sum((gend[None, :] <= blk0[:, None]).astype(I32), axis=1), N_EXPERTS - 1)
    sel = block_exp[:, None] == ex[None, :]
    tot_b = jnp.sum(jnp.where(sel, tot[None, :], 0), axis=1)
    st_b = jnp.sum(jnp.where(sel, gstart[None, :], 0), axis=1)
    n_valid = jnp.clip((tot_b - (blk0 - st_b)) * CHUNK, 0, MOE_ROWS).astype(I32)
    lb_tok = jnp.repeat(lbase, tm, axis=0)
    at = lambda e: jnp.sum(jnp.where(e[:, None] == ex[None, :], lb_tok, 0), axis=1)
    pos1 = CHUNK * at(ri[0]) + ri[2]
    pos2 = CHUNK * at(ri[1]) + ri[3]
    zero = jnp.zeros_like(pos1)
    pos_rows = jnp.stack([pos1, pos2] + [zero] * 6, axis=0).astype(I32)
    slot_id = jnp.arange(MAX_TILE_CHUNKS, dtype=I32)
    owner = jnp.sum((lbase + nch)[:, None, :] <= slot_id[None, :, None], axis=2)
    own = jnp.minimum(owner, N_EXPERTS - 1)[:, :, None] == ex[None, None, :]
    tab = jnp.sum(jnp.where(own, (gpos - lbase)[:, None, :], 0), axis=2) + slot_id[None, :]
    tab = tab.astype(I32).reshape(-1)
    fill = jnp.concatenate([gstart + tot, reg - tot, gend[-1:] // cpb]).astype(I32)
    xb = _dispatch(h, pos_rows, tab, lused, fill, n_blocks)
    yb = _expert_ffn(xb, (gstart // cpb).astype(I32), (reg // cpb).astype(I32), n_valid,
                     (gend[-1:] // cpb).astype(I32), w1, w3, w2, layer)
    return _combine(yb, pos_rows.T, rg.T, tab, lused, x, mod, n_lat_tiles, n_tiles)


def _router_matrix(w_rg, b_rg, w_re, b_re):
    d = w_rg.shape[0]
    wr = jnp.zeros((d, LANES), F32)
    wr = wr.at[:, 0:N_GROUPS].set(w_rg.astype(F32))
    wr = wr.at[:, ROUTER_EXPERT_ROW0:ROUTER_EXPERT_ROW0 + N_EXPERTS].set(w_re.astype(F32))
    wr = jnp.stack(_split_bf16(wr), axis=0)
    br = jnp.zeros((LANES,), F32)
    br = br.at[0:N_GROUPS].set(b_rg.astype(F32))
    br = br.at[ROUTER_EXPERT_ROW0:ROUTER_EXPERT_ROW0 + N_EXPERTS].set(b_re.astype(F32))
    return wr, jnp.broadcast_to(br[:, None], (LANES, LANES))


def _dup_heads(w):
    d = w.shape[0]
    w4 = w.reshape(d, N_KV_HEADS, 1, HEAD_DIM)
    return jnp.broadcast_to(w4, (d, N_KV_HEADS, 2, HEAD_DIM)).reshape(d, 2 * N_KV_HEADS * HEAD_DIM)


def kernel(x, c, ctx, c_ctx, w_mod, b_mod, norm_mix_g, norm_ffn_g, w_in_even, conv_w, w_out_even, w_qkv, q_norm_g,
           k_norm_g, sink_logit, w_o, w_router_g, b_router_g, w_router_e, b_router_e, w1, w3, w2):
    bsz, l, d = x.shape
    lc = ctx.shape[1]
    assert bsz == 1, "one sample per call"
    tm = ROW_TILE
    assert l % tm == 0 and lc % tm == 0 and l % lc == 0 and l % (DFT_MINOR * 8) == 0
    depth = w_mod.shape[0]
    t = l + lc
    assert t % DFT_MINOR == 0
    nl, nt = l // tm, t // tm

    xs = (x.reshape(l, d), ctx.reshape(lc, d))
    mod_all = _modulation(c, c_ctx, w_mod, b_mod).reshape(depth, 2, 6, d)
    cs, m1, a_tab, mc = _dft_tables(l, lc)
    rope = _rope_tables(l, t)
    segs = _segment_matrices()
    attn_bias = _attn_bias(l)
    tri = jnp.asarray(np.triu(np.ones((tm, tm), np.float32), 1)).astype(BF16)
    qd = N_HEADS * HEAD_DIM

    for layer in range(depth):
        last = layer == depth - 1
        j = layer // 2
        mod = mod_all[layer]
        g_mix = norm_mix_g[layer].reshape(1, d)
        g_ffn = norm_ffn_g[layer].reshape(1, d)
        wr, br = _router_matrix(w_router_g[layer], b_router_g[layer], w_router_e[layer], b_router_e[layer])
        route = (g_ffn, wr, br, tri)
        if layer % 2 == 0:
            z, bg, u = _even_in(xs, g_mix, mod, w_in_even[j].astype(BF16), cs, nl, nt)
            f_lat, f_ctx = _fourier_seq(z, l, m1, a_tab, mc)
            xs, *routing = _even_out(f_lat, f_ctx, bg, u, conv_w[j], w_out_even[j].astype(BF16), xs, mod, route, nl)
        else:
            wq = w_qkv[j]
            w_all = jnp.concatenate([wq[:, :qd], _dup_heads(wq[:, qd:qd + N_KV_HEADS * HEAD_DIM]),
                                     _dup_heads(wq[:, qd + N_KV_HEADS * HEAD_DIM:])], axis=1).astype(BF16)
            qg = jnp.tile(q_norm_g[j], N_HEADS).reshape(1, qd)
            kg = jnp.tile(k_norm_g[j], 2 * N_KV_HEADS).reshape(1, 2 * N_KV_HEADS * HEAD_DIM)
            q, k, v = _odd_in(xs, g_mix, mod, w_all, qg, kg, segs, rope, nl, nt)
            n_out = nl if last else nt
            att = _attention(q, k, v, sink_logit[j].astype(F32) * LOG2E, attn_bias, l,
                             n_out * (tm // ATT_BLOCK))
            xs, *routing = _odd_out(att, w_o[j].astype(BF16), xs, mod, route, nl, n_out)
        n_moe = nl if last else nt
        xs = _moe_layer(xs, routing, mod, w1, w3, w2, layer, nl, n_moe)
    return xs[:l].reshape(bsz, l, d)
```

```python
import functools
import math

import numpy as np
import jax
import jax.numpy as jnp
from jax import lax
from jax.experimental import pallas as pl
from jax.experimental.pallas import tpu as pltpu

F32 = jnp.float32
BF16 = jnp.bfloat16
I32 = jnp.int32

EPS = 1e-6
NEG_INF = -1e30

GRID_W = 64
FOURIER_GROUPS = 4
FOURIER_GROUP_DIM = 128
FOURIER_WIDTH = FOURIER_GROUPS * FOURIER_GROUP_DIM
CONV_WIDTH = 512
N_HEADS = 16
N_KV_HEADS = 4
GQA_GROUP = N_HEADS // N_KV_HEADS
HEAD_DIM = 64
WINDOW = 128
ROPE_BASE = 10000.0
ROPE_PAIRS = HEAD_DIM // 4
N_GROUPS = 4
EXPERTS_PER_GROUP = 8
N_EXPERTS = N_GROUPS * EXPERTS_PER_GROUP

LANES = 128
ROW_TILE = 256
ATT_BLOCK = 128
MOE_ROWS = 256
DFT_MINOR = 128
VMEM_LIMIT = 48 * 1024 * 1024

LOG2E = math.log2(math.e)


def _cparams(sem):
    return pltpu.CompilerParams(dimension_semantics=sem, vmem_limit_bytes=VMEM_LIMIT)


def _split_bf16(x):
    hi = x.astype(BF16)
    lo = (x - hi.astype(F32)).astype(BF16)
    return hi, lo


def _mod_kernel(ct_ref, w_ref, b_ref, o_ref):
    ct = ct_ref[...]
    s = ct * jax.nn.sigmoid(ct)
    w = w_ref[...]
    r0 = jnp.sum(w * s[:, 0:1], axis=0, keepdims=True)
    r1 = jnp.sum(w * s[:, 1:2], axis=0, keepdims=True)
    o_ref[...] = jnp.concatenate([r0, r1], axis=0) + b_ref[...]


def _modulation(c, c_ctx, w_mod, b_mod):
    depth, d, n = w_mod.shape
    tn = 512
    ct = jnp.stack([c.reshape(d), c_ctx.reshape(d)], axis=1)
    return pl.pallas_call(
        _mod_kernel,
        grid=(depth, n // tn),
        in_specs=[
            pl.BlockSpec((d, 2), lambda l, j: (0, 0)),
            pl.BlockSpec((None, d, tn), lambda l, j: (l, 0, j)),
            pl.BlockSpec((None, 1, tn), lambda l, j: (l, 0, j)),
        ],
        out_specs=pl.BlockSpec((None, 2, tn), lambda l, j: (l, 0, j)),
        out_shape=jax.ShapeDtypeStruct((depth, 2, n), F32),
        compiler_params=_cparams(("arbitrary", "arbitrary")),
        name="modulation",
    )(ct, w_mod, b_mod.reshape(depth, 1, n))


def _norm_mod(x, g, shift, scale):
    ms = jnp.mean(x * x, axis=-1, keepdims=True)
    y = x * lax.rsqrt(ms + EPS) * g
    return y * (1.0 + scale) + shift


def _stream_of(n_lat_tiles):
    return lambda i: (jnp.where(i >= n_lat_tiles, 1, 0), 0, 0)


def _row_sources(x, n_lat_tiles, n_tiles, d):
    tm = ROW_TILE
    lat = pl.BlockSpec((tm, d), lambda i: (jnp.minimum(i, n_lat_tiles - 1), 0))
    if isinstance(x, tuple):
        ctx = pl.BlockSpec((tm, d), lambda i: (jnp.clip(i - n_lat_tiles, 0, n_tiles - n_lat_tiles - 1), 0))
        return x, [lat, ctx]
    return (x, x), [lat, pl.BlockSpec((tm, d), lambda i: (jnp.clip(i, n_lat_tiles, n_tiles - 1), 0))]


def _pick_rows(xl_ref, xc_ref, n_lat_tiles):
    is_ctx = jnp.full(xl_ref.shape, pl.program_id(0), I32) >= n_lat_tiles
    return jnp.where(is_ctx, xc_ref[...], xl_ref[...])


def _even_in_kernel(xl_ref, xc_ref, g_ref, mod_ref, w_ref, cs_ref, z_ref, bg_ref, u_ref, *, n_lat_tiles):
    h = _norm_mod(_pick_rows(xl_ref, xc_ref, n_lat_tiles), g_ref[...], mod_ref[0:1, :], mod_ref[1:2, :])
    p = jnp.dot(h.astype(BF16), w_ref[...], preferred_element_type=F32)
    cs = cs_ref[...]
    for grp in range(FOURIER_GROUPS):
        lo = grp * FOURIER_GROUP_DIM
        a = p[:, lo:lo + FOURIER_GROUP_DIM].astype(BF16)
        z = jnp.dot(a, cs, preferred_element_type=F32)
        z_ref[0, :, lo:lo + FOURIER_GROUP_DIM] = z[:, :FOURIER_GROUP_DIM]
        z_ref[1, :, lo:lo + FOURIER_GROUP_DIM] = z[:, FOURIER_GROUP_DIM:]
    o = FOURIER_WIDTH
    bg_ref[...] = p[:, o:o + CONV_WIDTH].astype(BF16)
    u_ref[...] = (p[:, o + CONV_WIDTH:o + 2 * CONV_WIDTH] * p[:, o + 2 * CONV_WIDTH:]).astype(BF16)


def _even_in(x, g, mod, w_in, cs, n_lat_tiles, n_tiles):
    d, n = w_in.shape
    tm = ROW_TILE
    t = n_tiles * tm
    xs, x_specs = _row_sources(x, n_lat_tiles, n_tiles, d)
    return pl.pallas_call(
        functools.partial(_even_in_kernel, n_lat_tiles=n_lat_tiles),
        grid=(n_tiles,),
        in_specs=x_specs + [
            pl.BlockSpec((1, d), lambda i: (0, 0)),
            pl.BlockSpec((None, 6, d), _stream_of(n_lat_tiles)),
            pl.BlockSpec((d, n), lambda i: (0, 0)),
            pl.BlockSpec(cs.shape, lambda i: (0, 0)),
        ],
        out_specs=[
            pl.BlockSpec((2, tm, FOURIER_WIDTH), lambda i: (0, i, 0)),
            pl.BlockSpec((tm, CONV_WIDTH), lambda i: (i, 0)),
            pl.BlockSpec((tm, CONV_WIDTH), lambda i: (i, 0)),
        ],
        out_shape=[
            jax.ShapeDtypeStruct((2, t, FOURIER_WIDTH), F32),
            jax.ShapeDtypeStruct((t, CONV_WIDTH), BF16),
            jax.ShapeDtypeStruct((t, CONV_WIDTH), BF16),
        ],
        compiler_params=_cparams(("arbitrary",)),
        name="even_in_proj",
    )(*xs, g, mod, w_in, cs)


def _dft_major_kernel(m_ref, z_ref, o_ref):
    m = m_ref[...]
    n1, c = z_ref.shape[1], z_ref.shape[3]
    for j in range(z_ref.shape[2]):
        zc = z_ref[:, :, j, :].reshape(2 * n1, c).astype(BF16)
        g = jnp.dot(m, zc, preferred_element_type=F32)
        o_ref[:, :, j, :] = g.reshape(2, n1, c)


def _dft_major(m1, z4, n1):
    c = z4.shape[-1]
    cb = 8
    return pl.pallas_call(
        _dft_major_kernel,
        grid=(DFT_MINOR // cb,),
        in_specs=[pl.BlockSpec(m1.shape, lambda j: (0, 0)), pl.BlockSpec((2, n1, cb, c), lambda j: (0, 0, j, 0))],
        out_specs=pl.BlockSpec((2, n1, cb, c), lambda j: (0, 0, j, 0)),
        out_shape=jax.ShapeDtypeStruct((2, n1, DFT_MINOR, c), F32),
        compiler_params=_cparams(("arbitrary",)),
        name="dft_major",
    )(m1, z4)


def _dft_minor_kernel(a_ref, g_ref, o_ref):
    for j in range(a_ref.shape[0]):
        gcat = jnp.concatenate([g_ref[0, j], g_ref[1, j]], axis=0).astype(BF16)
        o_ref[:, j, :] = jnp.dot(a_ref[j], gcat, preferred_element_type=F32)


def _dft_minor(a_tab, g4):
    n1, m, k2 = a_tab.shape
    c = g4.shape[-1]
    kb = 8
    return pl.pallas_call(
        _dft_minor_kernel,
        grid=(n1 // kb,),
        in_specs=[
            pl.BlockSpec((kb, m, k2), lambda i: (i, 0, 0)),
            pl.BlockSpec((2, kb, DFT_MINOR, c), lambda i: (0, i, 0, 0)),
        ],
        out_specs=pl.BlockSpec((DFT_MINOR, kb, c), lambda i: (0, i, 0)),
        out_shape=jax.ShapeDtypeStruct((DFT_MINOR, n1, c), F32),
        compiler_params=_cparams(("arbitrary",)),
        name="dft_minor",
    )(a_tab, g4)


def _dft_ctx_kernel(a_ref, z_ref, o_ref):
    o_ref[...] = jnp.dot(a_ref[...], z_ref[...].astype(BF16), preferred_element_type=F32)


def _dft_ctx(mc, zc):
    lc, c = mc.shape[0], zc.shape[1]
    return pl.pallas_call(
        _dft_ctx_kernel,
        grid=(1,),
        in_specs=[pl.BlockSpec(mc.shape, lambda i: (0, 0)), pl.BlockSpec(zc.shape, lambda i: (0, 0))],
        out_specs=pl.BlockSpec((lc, c), lambda i: (0, 0)),
        out_shape=jax.ShapeDtypeStruct((lc, c), F32),
        compiler_params=_cparams(("arbitrary",)),
        name="dft_context",
    )(mc, zc)


def _dft_tables(l, lc):
    gd = FOURIER_GROUP_DIM
    kk = np.arange(gd)
    ang = 2.0 * np.pi * ((kk[:, None] * kk[None, :]) % gd) / gd
    cs = np.concatenate([np.cos(ang), -np.sin(ang)], axis=1)
    n1 = l // DFT_MINOR
    k1 = np.arange(n1)
    ang1 = 2.0 * np.pi * ((k1[:, None] * k1[None, :]) % n1) / n1
    c1, s1 = np.cos(ang1), np.sin(ang1)
    m1 = np.block([[c1, s1], [-s1, c1]])
    l2 = np.arange(DFT_MINOR)
    kfull = k1[:, None, None] + n1 * l2[None, :, None]
    ang2 = 2.0 * np.pi * ((kfull * l2[None, None, :]) % l) / l
    sc = 1.0 / math.sqrt(l * gd)
    a_tab = np.concatenate([np.cos(ang2), np.sin(ang2)], axis=2) * sc
    kc = np.arange(lc)
    angc = 2.0 * np.pi * ((kc[:, None] * kc[None, :]) % lc) / lc
    mc = np.concatenate([np.cos(angc), np.sin(angc)], axis=1) / math.sqrt(lc * gd)
    as_bf16 = lambda v: jnp.asarray(v, F32).astype(BF16)
    return as_bf16(cs), as_bf16(m1), as_bf16(a_tab), as_bf16(mc)


def _fourier_seq(z, l, m1, a_tab, mc):
    t, c = z.shape[1], z.shape[2]
    lc = t - l
    n1 = l // DFT_MINOR
    g4 = _dft_major(m1, z.reshape(2, t // DFT_MINOR, DFT_MINOR, c), n1)
    f_lat = _dft_minor(a_tab, g4).reshape(l, c)
    return f_lat, _dft_ctx(mc, z[:, l:].reshape(2 * lc, c))


ROUTER_EXPERT_ROW0 = 8


def _route_tile(x, g_ref, mod_ref, wr_ref, br_ref, tri_ref, h_ref, ri_ref, rg_ref, cnt_ref):
    h = _norm_mod(x, g_ref[...], mod_ref[3:4, :], mod_ref[4:5, :])
    h_ref[...] = h.astype(BF16)
    tm = h.shape[0]
    h_hi, h_lo = _split_bf16(h)
    wr = wr_ref[...]
    both = jnp.dot(h_hi, wr, preferred_element_type=F32)
    logits = (both[:, :LANES] + both[:, LANES:]
              + jnp.dot(h_lo, wr[:, :LANES], preferred_element_type=F32))
    logits = logits.T + br_ref[:, 0:1]
    gl = logits[0:N_GROUPS]
    gmax = jnp.max(gl, axis=0, keepdims=True)
    gi = lax.broadcasted_iota(I32, gl.shape, 0)
    g_idx = jnp.min(jnp.where(gl == gmax, gi, N_GROUPS), axis=0, keepdims=True)
    g_val = 1.0 / jnp.sum(jnp.exp(gl - gmax), axis=0, keepdims=True)
    e_in = logits[ROUTER_EXPERT_ROW0:ROUTER_EXPERT_ROW0 + EXPERTS_PER_GROUP]
    for grp in range(1, N_GROUPS):
        lo = ROUTER_EXPERT_ROW0 + grp * EXPERTS_PER_GROUP
        e_in = jnp.where(g_idx == grp, logits[lo:lo + EXPERTS_PER_GROUP], e_in)
    ei = lax.broadcasted_iota(I32, e_in.shape, 0)
    v1 = jnp.max(e_in, axis=0, keepdims=True)
    i1 = jnp.min(jnp.where(e_in == v1, ei, EXPERTS_PER_GROUP), axis=0, keepdims=True)
    rest = jnp.where(ei == i1, -jnp.inf, e_in)
    v2 = jnp.max(rest, axis=0, keepdims=True)
    i2 = jnp.min(jnp.where(rest == v2, ei, EXPERTS_PER_GROUP), axis=0, keepdims=True)
    w2 = jnp.exp(v2 - v1)
    gate1 = g_val / (1.0 + w2)
    gate2 = g_val * w2 / (1.0 + w2)
    e1 = g_idx * EXPERTS_PER_GROUP + i1
    e2 = g_idx * EXPERTS_PER_GROUP + i2
    xi = lax.broadcasted_iota(I32, (N_EXPERTS, tm), 0)
    oh1 = xi == e1
    oh2 = xi == e2
    oh = oh1.astype(F32) + oh2.astype(F32)
    before = jnp.dot(oh.astype(BF16), tri_ref[...], preferred_element_type=F32)
    rank1 = jnp.sum(jnp.where(oh1, before, 0.0), axis=0, keepdims=True)
    rank2 = jnp.sum(jnp.where(oh2, before, 0.0), axis=0, keepdims=True)
    cnt_ref[...] = jnp.broadcast_to(jnp.sum(oh, axis=1, keepdims=True), cnt_ref.shape)
    orow = lax.broadcasted_iota(I32, (8, tm), 0)
    ri_ref[...] = jnp.where(orow == 0, e1, jnp.where(orow == 1, e2, jnp.where(
        orow == 2, rank1.astype(I32), jnp.where(orow == 3, rank2.astype(I32), 0))))
    rg_ref[...] = jnp.where(orow == 0, gate1, jnp.where(orow == 1, gate2, 0.0))


def _route_specs(route, d, n_tiles):
    tm = ROW_TILE
    const = lambda a: pl.BlockSpec(a.shape, lambda i: (0,) * a.ndim)
    in_specs = [const(a) for a in route]
    out_specs = [
        pl.BlockSpec((tm, d), lambda i: (i, 0)),
        pl.BlockSpec((8, tm), lambda i: (0, i)),
        pl.BlockSpec((8, tm), lambda i: (0, i)),
        pl.BlockSpec((None, N_EXPERTS, LANES), lambda i: (i, 0, 0)),
    ]
    rows = n_tiles * tm
    out_shape = [
        jax.ShapeDtypeStruct((rows, d), BF16),
        jax.ShapeDtypeStruct((8, rows), I32),
        jax.ShapeDtypeStruct((8, rows), F32),
        jax.ShapeDtypeStruct((n_tiles, N_EXPERTS, LANES), F32),
    ]
    return in_specs, out_specs, out_shape


def _even_out_kernel(fl_ref, fc_ref, bg_ref, u_ref, up_ref, un_ref, cw_ref, w_ref, xl_ref, xc_ref, mod_ref,
                     gf_ref, wr_ref, br_ref, tri_ref, o_ref, h_ref, ri_ref, rg_ref, cnt_ref, *, n_lat_tiles, n_tiles):
    i = pl.program_id(0)
    tm = u_ref.shape[0]
    u = u_ref[...].astype(F32)
    row = lax.broadcasted_iota(I32, u.shape, 0)
    first = jnp.logical_or(i == 0, i == n_lat_tiles)
    last = jnp.logical_or(i == n_lat_tiles - 1, i == n_tiles - 1)
    hb = up_ref.shape[0]
    halo_p = up_ref[...].astype(F32)[hb - 1:hb, :] * jnp.where(first, 0.0, 1.0)
    halo_n = un_ref[...].astype(F32)[0:1, :] * jnp.where(last, 0.0, 1.0)
    u_prev = jnp.where(row == 0, halo_p, pltpu.roll(u, 1, axis=0))
    u_next = jnp.where(row == tm - 1, halo_n, pltpu.roll(u, tm - 1, axis=0))
    cw = cw_ref[...]
    y = bg_ref[...].astype(F32) * (cw[0:1, :] * u_prev + cw[1:2, :] * u + cw[2:3, :] * u_next)
    is_ctx = jnp.full(fl_ref.shape, i, I32) >= n_lat_tiles
    f = jnp.where(is_ctx, fc_ref[...], fl_ref[...])
    acc = jnp.dot(f.astype(BF16), w_ref[0:FOURIER_WIDTH, :], preferred_element_type=F32)
    acc += jnp.dot(y.astype(BF16), w_ref[FOURIER_WIDTH:, :], preferred_element_type=F32)
    x_new = _pick_rows(xl_ref, xc_ref, n_lat_tiles) + mod_ref[2:3, :] * acc
    o_ref[...] = x_new
    _route_tile(x_new, gf_ref, mod_ref, wr_ref, br_ref, tri_ref, h_ref, ri_ref, rg_ref, cnt_ref)


def _even_out(f_lat, f_ctx, bg, u, conv_w, w_out, x, mod, route, n_lat_tiles):
    t, d = u.shape[0], w_out.shape[1]
    tm = ROW_TILE
    hb = 16
    n_tiles = t // tm
    r = tm // hb
    r_in, r_out, r_shape = _route_specs(route, d, n_tiles)
    xs, x_specs = _row_sources(x, n_lat_tiles, n_tiles, d)
    return pl.pallas_call(
        functools.partial(_even_out_kernel, n_lat_tiles=n_lat_tiles, n_tiles=n_tiles),
        grid=(n_tiles,),
        in_specs=[
            pl.BlockSpec((tm, FOURIER_WIDTH), lambda i: (jnp.minimum(i, n_lat_tiles - 1), 0)),
            pl.BlockSpec((tm, FOURIER_WIDTH), lambda i: (jnp.clip(i - n_lat_tiles, 0, n_tiles - n_lat_tiles - 1), 0)),
            pl.BlockSpec((tm, CONV_WIDTH), lambda i: (i, 0)),
            pl.BlockSpec((tm, CONV_WIDTH), lambda i: (i, 0)),
            pl.BlockSpec((hb, CONV_WIDTH), lambda i: (jnp.maximum(i * r - 1, 0), 0)),
            pl.BlockSpec((hb, CONV_WIDTH), lambda i: (jnp.minimum((i + 1) * r, t // hb - 1), 0)),
            pl.BlockSpec(conv_w.shape, lambda i: (0, 0)),
            pl.BlockSpec(w_out.shape, lambda i: (0, 0)),
        ] + x_specs + [
            pl.BlockSpec((None, 6, d), _stream_of(n_lat_tiles)),
        ] + r_in,
        out_specs=[pl.BlockSpec((tm, d), lambda i: (i, 0))] + r_out,
        out_shape=[jax.ShapeDtypeStruct((t, d), F32)] + r_shape,
        compiler_params=_cparams(("arbitrary",)),
        name="even_out_proj",
    )(f_lat, f_ctx, bg, u, u, u, conv_w, w_out, *xs, mod, *route)


def _seg_rms_scale(v, seg, seg_t):
    ss = jnp.dot((v * v).astype(BF16), seg, preferred_element_type=F32)
    inv = lax.rsqrt(ss * (1.0 / HEAD_DIM) + EPS)
    inv2 = jnp.concatenate(_split_bf16(inv), axis=1)
    return jnp.dot(inv2, seg_t, preferred_element_type=F32)


def _rope_cols(v, cos, sa, sb, scale):
    cols = []
    for j in range(v.shape[1] // LANES):
        c = v[:, j * LANES:(j + 1) * LANES]
        r = c * cos + pltpu.roll(c, LANES - ROPE_PAIRS, axis=1) * sa + pltpu.roll(c, ROPE_PAIRS, axis=1) * sb
        cols.append(r * scale if scale != 1.0 else r)
    return jnp.concatenate(cols, axis=1)


def _odd_in_kernel(x_ref, g_ref, mod_ref, w_ref, qg_ref, kg_ref, segq_ref, segqt_ref, segk_ref, segkt_ref,
                   rope_ref, q_ref, k_ref, v_ref):
    h = _norm_mod(x_ref[...], g_ref[...], mod_ref[0:1, :], mod_ref[1:2, :])
    p = jnp.dot(h.astype(BF16), w_ref[...], preferred_element_type=F32)
    qd = N_HEADS * HEAD_DIM
    kd = 2 * N_KV_HEADS * HEAD_DIM
    cos, sa, sb = rope_ref[0], rope_ref[1], rope_ref[2]
    q = p[:, :qd]
    q = q * _seg_rms_scale(q, segq_ref[...], segqt_ref[...]) * qg_ref[...]
    q_ref[...] = _rope_cols(q, cos, sa, sb, HEAD_DIM ** -0.5 * LOG2E).astype(BF16)
    k = p[:, qd:qd + kd]
    k = k * _seg_rms_scale(k, segk_ref[...], segkt_ref[...]) * kg_ref[...]
    k_ref[...] = _rope_cols(k, cos, sa, sb, 1.0).astype(BF16)
    v_ref[...] = p[:, qd + kd:].astype(BF16)


def _odd_in(x, g, mod, w_qkv, qg, kg, segs, rope, n_lat_tiles, n_tiles):
    t, d = x.shape
    n = w_qkv.shape[1]
    tm = ROW_TILE
    rows = n_tiles * tm
    qd = N_HEADS * HEAD_DIM
    kd = 2 * N_KV_HEADS * HEAD_DIM
    segq, segqt, segk, segkt = segs
    const = lambda a: pl.BlockSpec(a.shape, lambda i: (0,) * a.ndim)
    return pl.pallas_call(
        _odd_in_kernel,
        grid=(n_tiles,),
        in_specs=[
            pl.BlockSpec((tm, d), lambda i: (i, 0)),
            const(g),
            pl.BlockSpec((None, 6, d), _stream_of(n_lat_tiles)),
            const(w_qkv), const(qg), const(kg), const(segq), const(segqt), const(segk), const(segkt),
            pl.BlockSpec((3, tm, LANES), lambda i: (0, i, 0)),
        ],
        out_specs=[
            pl.BlockSpec((tm, qd), lambda i: (i, 0)),
            pl.BlockSpec((tm, kd), lambda i: (i, 0)),
            pl.BlockSpec((tm, kd), lambda i: (i, 0)),
        ],
        out_shape=[
            jax.ShapeDtypeStruct((rows, qd), BF16),
            jax.ShapeDtypeStruct((rows, kd), BF16),
            jax.ShapeDtypeStruct((rows, kd), BF16),
        ],
        compiler_params=_cparams(("arbitrary",)),
        name="odd_in_proj",
    )(x, g, mod, w_qkv, qg, kg, segq, segqt, segk, segkt, rope)


def _rope_tables(l, t):
    pos = np.arange(l)
    freqs = ROPE_BASE ** (-np.arange(ROPE_PAIRS, dtype=np.float32) / ROPE_PAIRS)
    lane = np.arange(LANES) % HEAD_DIM
    axis = lane // (2 * ROPE_PAIRS)
    half = (lane % (2 * ROPE_PAIRS)) // ROPE_PAIRS
    pair = lane % ROPE_PAIRS
    p = np.where(axis[None, :] == 0, (pos // GRID_W)[:, None], (pos % GRID_W)[:, None]).astype(np.float32)
    ang = p * freqs[pair][None, :].astype(np.float32)
    cos, sin = np.cos(ang), np.sin(ang)
    sa = np.where(half[None, :] == 0, -sin, 0.0)
    sb = np.where(half[None, :] == 1, sin, 0.0)
    tab = np.zeros((3, t, LANES), np.float32)
    tab[0, :l], tab[1, :l], tab[2, :l] = cos, sa, sb
    tab[0, l:] = 1.0
    return jnp.asarray(tab)


def _segment_matrices():
    def seg(width):
        m = np.zeros((width, LANES), np.float32)
        m[np.arange(width), np.arange(width) // HEAD_DIM] = 1.0
        return m
    sq, sk = seg(N_HEADS * HEAD_DIM), seg(2 * N_KV_HEADS * HEAD_DIM)
    b = lambda v: jnp.asarray(v).astype(BF16)
    twice = lambda m: np.concatenate([m, m], axis=0)
    return b(sq), b(twice(sq.T)), b(sk), b(twice(sk.T))


def _attn_kernel(sink_ref, q_ref, kp_ref, kc_ref, kn_ref, vp_ref, vc_ref, vn_ref, kx_ref, vx_ref, bias_ref, o_ref):
    bq = q_ref.shape[0]
    low = lax.broadcasted_iota(I32, (bq, LANES), 1) < HEAD_DIM
    top = lax.broadcasted_iota(I32, (LANES, bq), 0) < HEAD_DIM
    bias = bias_ref[...]
    nt = (((1,), (1,)), ((), ()))
    tn = (((0,), (0,)), ((), ()))
    st = []
    for g in range(N_KV_HEADS):
        ks = slice(g * LANES, (g + 1) * LANES)
        parts, sinks = [], []
        for j in range(GQA_GROUP):
            col = 2 * g + j // 2
            c = q_ref[:, col * LANES:(col + 1) * LANES]
            keep = low if j % 2 == 0 else jnp.logical_not(low)
            parts.append(jnp.where(keep, c, jnp.zeros_like(c)))
            sinks.append(jnp.full((1, bq), sink_ref[g * GQA_GROUP + j], F32))
        qs = jnp.concatenate(parts, axis=0)
        sink = jnp.concatenate(sinks, axis=1)
        kwin = jnp.concatenate([kp_ref[:, ks], kc_ref[:, ks], kn_ref[:, ks]], axis=0)
        s_loc = lax.dot_general(kwin, qs, nt, preferred_element_type=F32) + bias
        s_ctx = lax.dot_general(kx_ref[:, ks], qs, nt, preferred_element_type=F32)
        st.append((s_loc, s_ctx, sink))
    pr = []
    for g in range(N_KV_HEADS):
        s_loc, s_ctx, sink = st[g]
        m = jnp.maximum(jnp.maximum(jnp.max(s_loc, axis=0, keepdims=True),
                                    jnp.max(s_ctx, axis=0, keepdims=True)), sink)
        p_loc = jnp.exp2(s_loc - m)
        p_ctx = jnp.exp2(s_ctx - m)
        den = (jnp.sum(p_loc, axis=0, keepdims=True) + jnp.sum(p_ctx, axis=0, keepdims=True)
               + jnp.exp2(sink - m))
        pr.append((p_loc.astype(BF16), p_ctx.astype(BF16), 1.0 / den))
    for g in range(N_KV_HEADS):
        ks = slice(g * LANES, (g + 1) * LANES)
        p_loc, p_ctx, inv = pr[g]
        vwin = jnp.concatenate([vp_ref[:, ks], vc_ref[:, ks], vn_ref[:, ks]], axis=0)
        ot = lax.dot_general(vwin, p_loc, tn, preferred_element_type=F32)
        ot += lax.dot_general(vx_ref[:, ks], p_ctx, tn, preferred_element_type=F32)
        ot = ot * inv
        t0 = jnp.where(top, ot[:, 0:bq], ot[:, bq:2 * bq])
        t1 = jnp.where(top, ot[:, 2 * bq:3 * bq], ot[:, 3 * bq:4 * bq])
        o_ref[:, 2 * g * LANES:(2 * g + 1) * LANES] = t0.T.astype(o_ref.dtype)
        o_ref[:, (2 * g + 1) * LANES:(2 * g + 2) * LANES] = t1.T.astype(o_ref.dtype)


def _attn_bias(l):
    bq = ATT_BLOCK
    r = np.arange(GQA_GROUP * bq)[None, :] % bq
    col = np.arange(3 * bq)[:, None]
    band = np.abs(col - bq - r) <= WINDOW
    no_prev, no_next = col >= bq, col < 2 * bq
    masks = [band, band & no_prev, band & no_next, band & no_prev & no_next, np.zeros_like(band)]
    return jnp.asarray(np.where(np.stack(masks), 0.0, NEG_INF).astype(np.float32))


def _attention(q, k, v, sink, bias, l, n_q_blocks):
    t = q.shape[0]
    bq = ATT_BLOCK
    nlb = l // bq
    lc = t - l
    ctx_blk = l // lc
    kw = k.shape[1]
    kspec = lambda f: pl.BlockSpec((bq, kw), f)
    prev = lambda b, s: (jnp.clip(b - 1, 0, nlb - 1), 0)
    cur = lambda b, s: (jnp.minimum(b, nlb - 1), 0)
    nxt = lambda b, s: (jnp.clip(b + 1, 0, nlb - 1), 0)
    ctx = pl.BlockSpec((lc, kw), lambda b, s: (ctx_blk, 0))
    kind = lambda b, s: (jnp.where(b >= nlb, 4, (b == 0).astype(I32) + 2 * (b == nlb - 1).astype(I32)), 0, 0)
    return pl.pallas_call(
        _attn_kernel,
        grid_spec=pltpu.PrefetchScalarGridSpec(
            num_scalar_prefetch=1,
            grid=(n_q_blocks,),
            in_specs=[
                pl.BlockSpec((bq, q.shape[1]), lambda b, s: (b, 0)),
                kspec(prev), kspec(cur), kspec(nxt), kspec(prev), kspec(cur), kspec(nxt), ctx, ctx,
                pl.BlockSpec((None,) + bias.shape[1:], kind),
            ],
            out_specs=pl.BlockSpec((bq, q.shape[1]), lambda b, s: (b, 0)),
        ),
        out_shape=jax.ShapeDtypeStruct((n_q_blocks * bq, q.shape[1]), BF16),
        compiler_params=_cparams(("arbitrary",)),
        name="window_attention",
    )(sink, q, k, k, k, v, v, v, k, v, bias)


def _odd_out_kernel(a_ref, w_ref, x_ref, mod_ref, gf_ref, wr_ref, br_ref, tri_ref, o_ref, h_ref, ri_ref, rg_ref,
                    cnt_ref):
    acc = jnp.dot(a_ref[...], w_ref[...], preferred_element_type=F32)
    x_new = x_ref[...] + mod_ref[2:3, :] * acc
    o_ref[...] = x_new
    _route_tile(x_new, gf_ref, mod_ref, wr_ref, br_ref, tri_ref, h_ref, ri_ref, rg_ref, cnt_ref)


def _odd_out(a, w_o, x, mod, route, n_lat_tiles, n_tiles):
    d = x.shape[1]
    tm = ROW_TILE
    r_in, r_out, r_shape = _route_specs(route, d, n_tiles)
    return pl.pallas_call(
        _odd_out_kernel,
        grid=(n_tiles,),
        in_specs=[
            pl.BlockSpec((tm, a.shape[1]), lambda i: (i, 0)),
            pl.BlockSpec(w_o.shape, lambda i: (0, 0)),
            pl.BlockSpec((tm, d), lambda i: (i, 0)),
            pl.BlockSpec((None, 6, d), _stream_of(n_lat_tiles)),
        ] + r_in,
        out_specs=[pl.BlockSpec((tm, d), lambda i: (i, 0))] + r_out,
        out_shape=[jax.ShapeDtypeStruct((n_tiles * tm, d), F32)] + r_shape,
        compiler_params=_cparams(("arbitrary",)),
        name="odd_out_proj",
    )(a, w_o, x, mod, *route)


CHUNK = 8
TILE_BUF = 2 * ROW_TILE + N_EXPERTS * CHUNK
MAX_TILE_CHUNKS = TILE_BUF // CHUNK
FFN_BUFS = 4


def _chunk_rows(c):
    return pl.ds(pl.multiple_of(c * CHUNK, CHUNK), CHUNK)


def _wait_rows(copy_of_rows, n_chunks):
    bit = 1
    while bit <= MAX_TILE_CHUNKS:
        @pl.when((n_chunks & bit) != 0)
        def _(bit=bit):
            copy_of_rows(bit * CHUNK).wait()
        bit *= 2


def _dispatch_kernel(tab_ref, lused_ref, fill_ref, h_ref, pos_ref, xb_ref, hs, zbuf, sem, fsem, *, n_tiles, n_blocks):
    i = pl.program_id(0)
    slot = i % 2
    tm = h_ref.shape[0]

    def tail_copy(c):
        return pltpu.make_async_copy(zbuf.at[pl.ds(0, CHUNK), :], xb_ref.at[_chunk_rows(c), :], fsem)

    def block_copy(b):
        return pltpu.make_async_copy(zbuf, xb_ref.at[pl.ds(pl.multiple_of(b * MOE_ROWS, MOE_ROWS), MOE_ROWS), :], fsem)

    def fill(start):
        def tail(e, c):
            st, n = fill_ref[e], fill_ref[N_EXPERTS + e]

            def one(c2, cc):
                cp = tail_copy(st + c2)
                cp.start() if start else cp.wait()
                return cc
            return lax.fori_loop(0, n, one, c)
        lax.fori_loop(0, N_EXPERTS, tail, 0)

        def blk(b, c):
            cp = block_copy(b)
            cp.start() if start else cp.wait()
            return c
        lax.fori_loop(fill_ref[2 * N_EXPERTS], n_blocks, blk, 0)

    @pl.when(i == 0)
    def _():
        zbuf[...] = jnp.zeros_like(zbuf)
        fill(True)

    pos = pos_ref[...]
    r = lax.broadcasted_iota(I32, (TILE_BUF, tm), 0)
    onehot = jnp.where(jnp.logical_or(r == pos[0:1, :], r == pos[1:2, :]), 1.0, 0.0).astype(BF16)
    hs[slot] = jnp.dot(onehot, h_ref[...], preferred_element_type=F32).astype(BF16)

    def chunk_copy(sl, src, dst):
        return pltpu.make_async_copy(hs.at[sl, _chunk_rows(src), :], xb_ref.at[_chunk_rows(dst), :], sem.at[sl])

    base = i * MAX_TILE_CHUNKS

    def per_chunk(c, cc):
        chunk_copy(slot, c, tab_ref[base + c]).start()
        return cc
    lax.fori_loop(0, lused_ref[i], per_chunk, 0)

    def wait_chunks(sl, n):
        _wait_rows(lambda rows: pltpu.make_async_copy(hs.at[sl, pl.ds(0, rows), :], xb_ref.at[pl.ds(0, rows), :],
                                                      sem.at[sl]), n)

    @pl.when(i > 0)
    def _():
        wait_chunks(1 - slot, lused_ref[jnp.maximum(i - 1, 0)])

    @pl.when(i == n_tiles - 1)
    def _():
        wait_chunks(slot, lused_ref[i])
        fill(False)


def _dispatch(h, pos_rows, tab, lused, fill, n_blocks):
    t, d = h.shape
    tm = ROW_TILE
    n_tiles = t // tm
    return pl.pallas_call(
        functools.partial(_dispatch_kernel, n_tiles=n_tiles, n_blocks=n_blocks),
        grid_spec=pltpu.PrefetchScalarGridSpec(
            num_scalar_prefetch=3,
            grid=(n_tiles,),
            in_specs=[
                pl.BlockSpec((tm, d), lambda i, *_: (i, 0)),
                pl.BlockSpec((8, tm), lambda i, *_: (0, i)),
            ],
            out_specs=pl.BlockSpec(memory_space=pl.ANY),
            scratch_shapes=[
                pltpu.VMEM((2, TILE_BUF, d), BF16),
                pltpu.VMEM((MOE_ROWS, d), BF16),
                pltpu.SemaphoreType.DMA((2,)),
                pltpu.SemaphoreType.DMA(()),
            ],
        ),
        out_shape=jax.ShapeDtypeStruct((n_blocks * MOE_ROWS, d), BF16),
        compiler_params=_cparams(("arbitrary",)),
        name="moe_dispatch",
    )(tab, lused, fill, h, pos_rows)


def _ffn_kernel(b0_ref, nb_ref, nv_ref, fill_ref, xb_ref, w1_ref, w3_ref, w2_ref, yb_ref,
                w1s, w3s, w2s, xbuf, ybuf, zbuf, sem_in, sem_out, fsem, *, n_blocks):
    e = pl.program_id(0)
    w1s[...] = w1_ref[...].astype(BF16)
    w3s[...] = w3_ref[...].astype(BF16)
    w2s[...] = w2_ref[...].astype(BF16)
    b0, nb = b0_ref[e], nb_ref[e]

    def rows(b):
        return pl.ds(pl.multiple_of(b * MOE_ROWS, MOE_ROWS), MOE_ROWS)

    def x_copy(b, sl):
        return pltpu.make_async_copy(xb_ref.at[rows(b), :], xbuf.at[sl], sem_in.at[sl])

    def y_copy(b, sl):
        return pltpu.make_async_copy(ybuf.at[sl], yb_ref.at[rows(b), :], sem_out.at[sl])

    for k in range(FFN_BUFS - 1):
        @pl.when(k < nb)
        def _(k=k):
            x_copy(b0 + k, k).start(priority=1)

    def block(j, c):
        sl = j % FFN_BUFS
        x_copy(b0 + j, sl).wait()
        ahead = j + (FFN_BUFS - 1)

        @pl.when(ahead < nb)
        def _():
            x_copy(b0 + ahead, ahead % FFN_BUFS).start(priority=1)

        @pl.when(j >= FFN_BUFS)
        def _():
            y_copy(b0 + j - FFN_BUFS, sl).wait()

        x = xbuf[sl]
        row = lax.broadcasted_iota(I32, x.shape, 0)
        xb = jnp.where(row < nv_ref[b0 + j], x, jnp.zeros_like(x))
        a = jnp.dot(xb, w1s[...], preferred_element_type=F32)
        b = jnp.dot(xb, w3s[...], preferred_element_type=F32)
        hid = (a * jax.nn.sigmoid(a) * b).astype(BF16)
        ybuf[sl] = jnp.dot(hid, w2s[...], preferred_element_type=F32).astype(BF16)
        y_copy(b0 + j, sl).start(priority=1)
        return c
    lax.fori_loop(0, nb, block, 0)

    for k in range(FFN_BUFS):
        @pl.when(nb > k)
        def _(k=k):
            y_copy(b0 + nb - 1 - k, (nb - 1 - k) % FFN_BUFS).wait()

    @pl.when(e == N_EXPERTS - 1)
    def _():
        zbuf[...] = jnp.zeros_like(zbuf)

        def z_copy(b):
            return pltpu.make_async_copy(zbuf, yb_ref.at[rows(b), :], fsem)

        def start(b, c):
            z_copy(b).start()
            return c

        def wait(b, c):
            z_copy(b).wait()
            return c
        lax.fori_loop(fill_ref[0], n_blocks, start, 0)
        lax.fori_loop(fill_ref[0], n_blocks, wait, 0)


def _expert_ffn(xb, blk_start, blk_count, n_valid, used_blocks, w1, w3, w2, layer):
    r = xb.shape[0]
    d, f = w1.shape[2], w1.shape[3]
    bm = MOE_ROWS
    return pl.pallas_call(
        functools.partial(_ffn_kernel, n_blocks=r // bm),
        grid_spec=pltpu.PrefetchScalarGridSpec(
            num_scalar_prefetch=4,
            grid=(N_EXPERTS,),
            in_specs=[
                pl.BlockSpec(memory_space=pl.ANY),
                pl.BlockSpec((None, None, d, f), lambda e, *_: (layer, e, 0, 0)),
                pl.BlockSpec((None, None, d, f), lambda e, *_: (layer, e, 0, 0)),
                pl.BlockSpec((None, None, f, d), lambda e, *_: (layer, e, 0, 0)),
            ],
            out_specs=pl.BlockSpec(memory_space=pl.ANY),
            scratch_shapes=[
                pltpu.VMEM((d, f), BF16), pltpu.VMEM((d, f), BF16), pltpu.VMEM((f, d), BF16),
                pltpu.VMEM((FFN_BUFS, bm, d), BF16), pltpu.VMEM((FFN_BUFS, bm, d), BF16), pltpu.VMEM((bm, d), BF16),
                pltpu.SemaphoreType.DMA((FFN_BUFS,)), pltpu.SemaphoreType.DMA((FFN_BUFS,)), pltpu.SemaphoreType.DMA(()),
            ],
        ),
        out_shape=jax.ShapeDtypeStruct((r, d), BF16),
        compiler_params=_cparams(("arbitrary",)),
        name="moe_expert_mlp",
    )(blk_start, blk_count, n_valid, used_blocks, xb, w1, w3, w2)


def _combine_kernel(tab_ref, lused_ref, yb_ref, pos_ref, gate_ref, x_ref, mod_ref, o_ref, ys, sem, *, n_tiles):
    i = pl.program_id(0)
    slot = i % 2
    tm = x_ref.shape[0]

    def chunk_copy(sl, src, dst):
        return pltpu.make_async_copy(yb_ref.at[_chunk_rows(src), :], ys.at[sl, _chunk_rows(dst), :], sem.at[sl])

    def fetch(tile, sl):
        base = tile * MAX_TILE_CHUNKS

        def per_chunk(c, cc):
            chunk_copy(sl, tab_ref[base + c], c).start()
            return cc
        lax.fori_loop(0, lused_ref[tile], per_chunk, 0)

    @pl.when(i == 0)
    def _():
        fetch(0, 0)

    @pl.when(i + 1 < n_tiles)
    def _():
        fetch(jnp.minimum(i + 1, n_tiles - 1), 1 - slot)

    _wait_rows(lambda rows: pltpu.make_async_copy(yb_ref.at[pl.ds(0, rows), :], ys.at[slot, pl.ds(0, rows), :],
                                                  sem.at[slot]), lused_ref[i])

    y = ys[slot]
    used = lused_ref[i] * CHUNK
    rowi = lax.broadcasted_iota(I32, y.shape, 0)
    y16 = jnp.where(rowi < used, y, jnp.zeros_like(y))
    pos = pos_ref[...]
    gate = gate_ref[...]
    lane = lax.broadcasted_iota(I32, (tm, TILE_BUF), 1)
    gm = (jnp.where(lane == pos[:, 0:1], gate[:, 0:1], 0.0) + jnp.where(lane == pos[:, 1:2], gate[:, 1:2], 0.0))
    ghi, glo = _split_bf16(gm)
    mix = jnp.dot(ghi, y16, preferred_element_type=F32) + jnp.dot(glo, y16, preferred_element_type=F32)
    o_ref[...] = x_ref[...] + mod_ref[5:6, :] * mix


def _combine(yb, pos_cols, gates, tab, lused, x, mod, n_lat_tiles, n_tiles):
    d = x.shape[1]
    tm = ROW_TILE
    return pl.pallas_call(
        functools.partial(_combine_kernel, n_tiles=n_tiles),
        grid_spec=pltpu.PrefetchScalarGridSpec(
            num_scalar_prefetch=2,
            grid=(n_tiles,),
            in_specs=[
                pl.BlockSpec(memory_space=pl.ANY),
                pl.BlockSpec((tm, 8), lambda i, *_: (i, 0)),
                pl.BlockSpec((tm, 8), lambda i, *_: (i, 0)),
                pl.BlockSpec((tm, d), lambda i, *_: (i, 0)),
                pl.BlockSpec((None, 6, d), lambda i, *_: (jnp.where(i >= n_lat_tiles, 1, 0), 0, 0)),
            ],
            out_specs=pl.BlockSpec((tm, d), lambda i, *_: (i, 0)),
            scratch_shapes=[pltpu.VMEM((2, TILE_BUF, d), BF16), pltpu.SemaphoreType.DMA((2,))],
        ),
        out_shape=jax.ShapeDtypeStruct((n_tiles * tm, d), F32),
        compiler_params=_cparams(("arbitrary",)),
        name="moe_combine",
    )(tab, lused, yb, pos_cols, gates, x, mod)


def _moe_layer(x, routing, mod, w1, w3, w2, layer, n_lat_tiles, n_tiles):
    tm = ROW_TILE
    rows = n_tiles * tm
    cpb = MOE_ROWS // CHUNK
    h, ri, rg, cnt3 = routing
    cnt = cnt3[:, :, 0].astype(I32)
    nch = (cnt + CHUNK - 1) // CHUNK
    lbase = jnp.cumsum(nch, axis=1) - nch
    lused = jnp.sum(nch, axis=1).astype(I32)
    tot = jnp.sum(nch, axis=0)
    reg = (tot + cpb - 1) // cpb * cpb
    gend = jnp.cumsum(reg)
    gstart = gend - reg
    gpos = gstart[None, :] + jnp.cumsum(nch, axis=0) - nch
    rows_max = 2 * rows + n_tiles * N_EXPERTS * (CHUNK - 1) + N_EXPERTS * (MOE_ROWS - CHUNK)
    n_blocks = -(-rows_max // MOE_ROWS)
    ex = jnp.arange(N_EXPERTS, dtype=I32)
    blk0 = jnp.arange(n_blocks, dtype=I32) * cpb
    block_exp = jnp.minimum(jnp.sum((gend[None, :] <= blk0[:, None]).astype(I32), axis=1), N_EXPERTS - 1)
    sel = block_exp[:, None] == ex[None, :]
    tot_b = jnp.sum(jnp.where(sel, tot[None, :], 0), axis=1)
    st_b = jnp.sum(jnp.where(sel, gstart[None, :], 0), axis=1)
    n_valid = jnp.clip((tot_b - (blk0 - st_b)) * CHUNK, 0, MOE_ROWS).astype(I32)
    lb_tok = jnp.repeat(lbase, tm, axis=0)
    at = lambda e: jnp.sum(jnp.where(e[:, None] == ex[None, :], lb_tok, 0), axis=1)
    pos1 = CHUNK * at(ri[0]) + ri[2]
    pos2 = CHUNK * at(ri[1]) + ri[3]
    zero = jnp.zeros_like(pos1)
    pos_rows = jnp.stack([pos1, pos2] + [zero] * 6, axis=0).astype(I32)
    slot_id = jnp.arange(MAX_TILE_CHUNKS, dtype=I32)
    owner = jnp.sum((lbase + nch)[:, None, :] <= slot_id[None, :, None], axis=2)
    own = jnp.minimum(owner, N_EXPERTS - 1)[:, :, None] == ex[None, None, :]
    tab = jnp.sum(jnp.where(own, (gpos - lbase)[:, None, :], 0), axis=2) + slot_id[None, :]
    tab = tab.astype(I32).reshape(-1)
    fill = jnp.concatenate([gstart + tot, reg - tot, gend[-1:] // cpb]).astype(I32)
    xb = _dispatch(h, pos_rows, tab, lused, fill, n_blocks)
    yb = _expert_ffn(xb, (gstart // cpb).astype(I32), (reg // cpb).astype(I32), n_valid,
                     (gend[-1:] // cpb).astype(I32), w1, w3, w2, layer)
    return _combine(yb, pos_rows.T, rg.T, tab, lused, x, mod, n_lat_tiles, n_tiles)


def _router_matrix(w_rg, b_rg, w_re, b_re):
    d = w_rg.shape[0]
    wr = jnp.zeros((d, LANES), F32)
    wr = wr.at[:, 0:N_GROUPS].set(w_rg.astype(F32))
    wr = wr.at[:, ROUTER_EXPERT_ROW0:ROUTER_EXPERT_ROW0 + N_EXPERTS].set(w_re.astype(F32))
    wr = jnp.concatenate(_split_bf16(wr), axis=1)
    br = jnp.zeros((LANES,), F32)
    br = br.at[0:N_GROUPS].set(b_rg.astype(F32))
    br = br.at[ROUTER_EXPERT_ROW0:ROUTER_EXPERT_ROW0 + N_EXPERTS].set(b_re.astype(F32))
    return wr, jnp.broadcast_to(br[:, None], (LANES, LANES))


def _dup_heads(w):
    d = w.shape[0]
    w4 = w.reshape(d, N_KV_HEADS, 1, HEAD_DIM)
    return jnp.broadcast_to(w4, (d, N_KV_HEADS, 2, HEAD_DIM)).reshape(d, 2 * N_KV_HEADS * HEAD_DIM)


def kernel(x, c, ctx, c_ctx, w_mod, b_mod, norm_mix_g, norm_ffn_g, w_in_even, conv_w, w_out_even, w_qkv, q_norm_g,
           k_norm_g, sink_logit, w_o, w_router_g, b_router_g, w_router_e, b_router_e, w1, w3, w2):
    bsz, l, d = x.shape
    lc = ctx.shape[1]
    assert bsz == 1, "one sample per call"
    tm = ROW_TILE
    assert l % tm == 0 and lc % tm == 0 and l % lc == 0 and l % (DFT_MINOR * 8) == 0
    depth = w_mod.shape[0]
    t = l + lc
    assert t % DFT_MINOR == 0
    nl, nt = l // tm, t // tm

    xs = (x.reshape(l, d), ctx.reshape(lc, d))
    mod_all = _modulation(c, c_ctx, w_mod, b_mod).reshape(depth, 2, 6, d)
    cs, m1, a_tab, mc = _dft_tables(l, lc)
    rope = _rope_tables(l, t)
    segs = _segment_matrices()
    attn_bias = _attn_bias(l)
    tri = jnp.asarray(np.triu(np.ones((tm, tm), np.float32), 1)).astype(BF16)
    qd = N_HEADS * HEAD_DIM

    for layer in range(depth):
        last = layer == depth - 1
        j = layer // 2
        mod = mod_all[layer]
        g_mix = norm_mix_g[layer].reshape(1, d)
        g_ffn = norm_ffn_g[layer].reshape(1, d)
        wr, br = _router_matrix(w_router_g[layer], b_router_g[layer], w_router_e[layer], b_router_e[layer])
        route = (g_ffn, wr, br, tri)
        if layer % 2 == 0:
            z, bg, u = _even_in(xs, g_mix, mod, w_in_even[j].astype(BF16), cs, nl, nt)
            f_lat, f_ctx = _fourier_seq(z, l, m1, a_tab, mc)
            xs, *routing = _even_out(f_lat, f_ctx, bg, u, conv_w[j], w_out_even[j].astype(BF16), xs, mod, route, nl)
        else:
            wq = w_qkv[j]
            w_all = jnp.concatenate([wq[:, :qd], _dup_heads(wq[:, qd:qd + N_KV_HEADS * HEAD_DIM]),
                                     _dup_heads(wq[:, qd + N_KV_HEADS * HEAD_DIM:])], axis=1).astype(BF16)
            qg = jnp.tile(q_norm_g[j], N_HEADS).reshape(1, qd)
            kg = jnp.tile(k_norm_g[j], 2 * N_KV_HEADS).reshape(1, 2 * N_KV_HEADS * HEAD_DIM)
            q, k, v = _odd_in(xs, g_mix, mod, w_all, qg, kg, segs, rope, nl, nt)
            n_out = nl if last else nt
            att = _attention(q, k, v, sink_logit[j].astype(F32) * LOG2E, attn_bias, l,
                             n_out * (tm // ATT_BLOCK))
            xs, *routing = _odd_out(att, w_o[j].astype(BF16), xs, mod, route, nl, n_out)
        n_moe = nl if last else nt
        xs = _moe_layer(xs, routing, mod, w1, w3, w2, layer, nl, n_moe)
    return xs[:l].reshape(bsz, l, d)
```

```python
import functools
import math

import numpy as np
import jax
import jax.numpy as jnp
from jax import lax
from jax.experimental import pallas as pl
from jax.experimental.pallas import tpu as pltpu

F32 = jnp.float32
BF16 = jnp.bfloat16
I32 = jnp.int32

EPS = 1e-6
NEG_INF = -1e30

GRID_W = 64
FOURIER_GROUPS = 4
FOURIER_GROUP_DIM = 128
FOURIER_WIDTH = FOURIER_GROUPS * FOURIER_GROUP_DIM
CONV_WIDTH = 512
N_HEADS = 16
N_KV_HEADS = 4
GQA_GROUP = N_HEADS // N_KV_HEADS
HEAD_DIM = 64
WINDOW = 128
ROPE_BASE = 10000.0
ROPE_PAIRS = HEAD_DIM // 4
N_GROUPS = 4
EXPERTS_PER_GROUP = 8
N_EXPERTS = N_GROUPS * EXPERTS_PER_GROUP

LANES = 128
ROW_TILE = 256
ATT_BLOCK = 128
MOE_ROWS = 256
DFT_MINOR = 128
DFT_MAJOR_COLS = 8
VMEM_LIMIT = 48 * 1024 * 1024

LOG2E = math.log2(math.e)


def _cparams(sem):
    return pltpu.CompilerParams(dimension_semantics=sem, vmem_limit_bytes=VMEM_LIMIT)


def _split_bf16(x):
    hi = x.astype(BF16)
    lo = (x - hi.astype(F32)).astype(BF16)
    return hi, lo


def _mod_kernel(ct_ref, w_ref, b_ref, o_ref):
    ct = ct_ref[...]
    s = ct * jax.nn.sigmoid(ct)
    w = w_ref[...]
    r0 = jnp.sum(w * s[:, 0:1], axis=0, keepdims=True)
    r1 = jnp.sum(w * s[:, 1:2], axis=0, keepdims=True)
    o_ref[...] = jnp.concatenate([r0, r1], axis=0) + b_ref[...]


def _modulation(c, c_ctx, w_mod, b_mod):
    depth, d, n = w_mod.shape
    tn = 512
    ct = jnp.stack([c.reshape(d), c_ctx.reshape(d)], axis=1)
    return pl.pallas_call(
        _mod_kernel,
        grid=(depth, n // tn),
        in_specs=[
            pl.BlockSpec((d, 2), lambda l, j: (0, 0)),
            pl.BlockSpec((None, d, tn), lambda l, j: (l, 0, j)),
            pl.BlockSpec((None, 1, tn), lambda l, j: (l, 0, j)),
        ],
        out_specs=pl.BlockSpec((None, 2, tn), lambda l, j: (l, 0, j)),
        out_shape=jax.ShapeDtypeStruct((depth, 2, n), F32),
        compiler_params=_cparams(("arbitrary", "arbitrary")),
        name="modulation",
    )(ct, w_mod, b_mod.reshape(depth, 1, n))


def _norm_mod(x, g, shift, scale):
    ms = jnp.mean(x * x, axis=-1, keepdims=True)
    y = x * lax.rsqrt(ms + EPS) * g
    return y * (1.0 + scale) + shift


def _stream_of(n_lat_tiles):
    return lambda i: (jnp.where(i >= n_lat_tiles, 1, 0), 0, 0)


def _row_sources(x, n_lat_tiles, n_tiles, d):
    tm = ROW_TILE
    lat = pl.BlockSpec((tm, d), lambda i: (jnp.minimum(i, n_lat_tiles - 1), 0))
    if isinstance(x, tuple):
        ctx = pl.BlockSpec((tm, d), lambda i: (jnp.clip(i - n_lat_tiles, 0, n_tiles - n_lat_tiles - 1), 0))
        return x, [lat, ctx]
    return (x, x), [lat, pl.BlockSpec((tm, d), lambda i: (jnp.clip(i, n_lat_tiles, n_tiles - 1), 0))]


def _pick_rows(xl_ref, xc_ref, n_lat_tiles):
    is_ctx = jnp.full(xl_ref.shape, pl.program_id(0), I32) >= n_lat_tiles
    return jnp.where(is_ctx, xc_ref[...], xl_ref[...])


def _even_in_kernel(xl_ref, xc_ref, g_ref, mod_ref, w_ref, cs_ref, z_ref, bg_ref, u_ref, *, n_lat_tiles):
    h = _norm_mod(_pick_rows(xl_ref, xc_ref, n_lat_tiles), g_ref[...], mod_ref[0:1, :], mod_ref[1:2, :])
    p = jnp.dot(h.astype(BF16), w_ref[...], preferred_element_type=F32)
    cs = cs_ref[...]
    for grp in range(FOURIER_GROUPS):
        lo = grp * FOURIER_GROUP_DIM
        a = p[:, lo:lo + FOURIER_GROUP_DIM].astype(BF16)
        z = jnp.dot(a, cs, preferred_element_type=F32)
        z_ref[0, :, lo:lo + FOURIER_GROUP_DIM] = z[:, :FOURIER_GROUP_DIM]
        z_ref[1, :, lo:lo + FOURIER_GROUP_DIM] = z[:, FOURIER_GROUP_DIM:]
    o = FOURIER_WIDTH
    bg_ref[...] = p[:, o:o + CONV_WIDTH].astype(BF16)
    u_ref[...] = (p[:, o + CONV_WIDTH:o + 2 * CONV_WIDTH] * p[:, o + 2 * CONV_WIDTH:]).astype(BF16)


def _even_in(x, g, mod, w_in, cs, n_lat_tiles, n_tiles):
    d, n = w_in.shape
    tm = ROW_TILE
    t = n_tiles * tm
    xs, x_specs = _row_sources(x, n_lat_tiles, n_tiles, d)
    return pl.pallas_call(
        functools.partial(_even_in_kernel, n_lat_tiles=n_lat_tiles),
        grid=(n_tiles,),
        in_specs=x_specs + [
            pl.BlockSpec((1, d), lambda i: (0, 0)),
            pl.BlockSpec((None, 6, d), _stream_of(n_lat_tiles)),
            pl.BlockSpec((d, n), lambda i: (0, 0)),
            pl.BlockSpec(cs.shape, lambda i: (0, 0)),
        ],
        out_specs=[
            pl.BlockSpec((2, tm, FOURIER_WIDTH), lambda i: (0, i, 0)),
            pl.BlockSpec((tm, CONV_WIDTH), lambda i: (i, 0)),
            pl.BlockSpec((tm, CONV_WIDTH), lambda i: (i, 0)),
        ],
        out_shape=[
            jax.ShapeDtypeStruct((2, t, FOURIER_WIDTH), F32),
            jax.ShapeDtypeStruct((t, CONV_WIDTH), BF16),
            jax.ShapeDtypeStruct((t, CONV_WIDTH), BF16),
        ],
        compiler_params=_cparams(("arbitrary",)),
        name="even_in_proj",
    )(*xs, g, mod, w_in, cs)


def _dft_major_kernel(m_ref, z_ref, o_ref):
    shape = z_ref.shape
    z = z_ref[...].reshape(shape[0] * shape[1] * shape[2], shape[3]).astype(BF16)
    o_ref[...] = jnp.dot(m_ref[...], z, preferred_element_type=F32).reshape(shape)


def _dft_major(m1, z4, n1):
    c = z4.shape[-1]
    cb = DFT_MAJOR_COLS
    return pl.pallas_call(
        _dft_major_kernel,
        grid=(DFT_MINOR // cb,),
        in_specs=[pl.BlockSpec(m1.shape, lambda j: (0, 0)), pl.BlockSpec((2, n1, cb, c), lambda j: (0, 0, j, 0))],
        out_specs=pl.BlockSpec((2, n1, cb, c), lambda j: (0, 0, j, 0)),
        out_shape=jax.ShapeDtypeStruct((2, n1, DFT_MINOR, c), F32),
        compiler_params=_cparams(("arbitrary",)),
        name="dft_major",
    )(m1, z4)


def _dft_minor_kernel(a_ref, g_ref, o_ref):
    for j in range(a_ref.shape[0]):
        gcat = jnp.concatenate([g_ref[0, j], g_ref[1, j]], axis=0).astype(BF16)
        o_ref[:, j, :] = jnp.dot(a_ref[j], gcat, preferred_element_type=F32)


def _dft_minor(a_tab, g4):
    n1, m, k2 = a_tab.shape
    c = g4.shape[-1]
    kb = 8
    return pl.pallas_call(
        _dft_minor_kernel,
        grid=(n1 // kb,),
        in_specs=[
            pl.BlockSpec((kb, m, k2), lambda i: (i, 0, 0)),
            pl.BlockSpec((2, kb, DFT_MINOR, c), lambda i: (0, i, 0, 0)),
        ],
        out_specs=pl.BlockSpec((DFT_MINOR, kb, c), lambda i: (0, i, 0)),
        out_shape=jax.ShapeDtypeStruct((DFT_MINOR, n1, c), F32),
        compiler_params=_cparams(("arbitrary",)),
        name="dft_minor",
    )(a_tab, g4)


def _dft_ctx_kernel(a_ref, z_ref, o_ref):
    o_ref[...] = jnp.dot(a_ref[...], z_ref[...].astype(BF16), preferred_element_type=F32)


def _dft_ctx(mc, zc):
    lc, c = mc.shape[0], zc.shape[1]
    return pl.pallas_call(
        _dft_ctx_kernel,
        grid=(1,),
        in_specs=[pl.BlockSpec(mc.shape, lambda i: (0, 0)), pl.BlockSpec(zc.shape, lambda i: (0, 0))],
        out_specs=pl.BlockSpec((lc, c), lambda i: (0, 0)),
        out_shape=jax.ShapeDtypeStruct((lc, c), F32),
        compiler_params=_cparams(("arbitrary",)),
        name="dft_context",
    )(mc, zc)


def _dft_tables(l, lc):
    gd = FOURIER_GROUP_DIM
    kk = np.arange(gd)
    ang = 2.0 * np.pi * ((kk[:, None] * kk[None, :]) % gd) / gd
    cs = np.concatenate([np.cos(ang), -np.sin(ang)], axis=1)
    n1 = l // DFT_MINOR
    k1 = np.arange(n1)
    ang1 = 2.0 * np.pi * ((k1[:, None] * k1[None, :]) % n1) / n1
    c1, s1 = np.cos(ang1), np.sin(ang1)
    m1 = np.kron(np.block([[c1, s1], [-s1, c1]]), np.eye(DFT_MAJOR_COLS))
    l2 = np.arange(DFT_MINOR)
    kfull = k1[:, None, None] + n1 * l2[None, :, None]
    ang2 = 2.0 * np.pi * ((kfull * l2[None, None, :]) % l) / l
    sc = 1.0 / math.sqrt(l * gd)
    a_tab = np.concatenate([np.cos(ang2), np.sin(ang2)], axis=2) * sc
    kc = np.arange(lc)
    angc = 2.0 * np.pi * ((kc[:, None] * kc[None, :]) % lc) / lc
    mc = np.concatenate([np.cos(angc), np.sin(angc)], axis=1) / math.sqrt(lc * gd)
    as_bf16 = lambda v: jnp.asarray(v, F32).astype(BF16)
    return as_bf16(cs), as_bf16(m1), as_bf16(a_tab), as_bf16(mc)


def _fourier_seq(z, l, m1, a_tab, mc):
    t, c = z.shape[1], z.shape[2]
    lc = t - l
    n1 = l // DFT_MINOR
    g4 = _dft_major(m1, z.reshape(2, t // DFT_MINOR, DFT_MINOR, c), n1)
    f_lat = _dft_minor(a_tab, g4).reshape(l, c)
    return f_lat, _dft_ctx(mc, z[:, l:].reshape(2 * lc, c))


ROUTER_EXPERT_ROW0 = 8


def _route_tile(x, g_ref, mod_ref, wr_ref, br_ref, tri_ref, h_ref, ri_ref, rg_ref, cnt_ref):
    h = _norm_mod(x, g_ref[...], mod_ref[3:4, :], mod_ref[4:5, :])
    h_ref[...] = h.astype(BF16)
    tm = h.shape[0]
    h_hi, h_lo = _split_bf16(h)
    wr = wr_ref[...]
    both = jnp.dot(h_hi, wr, preferred_element_type=F32)
    logits = (both[:, :LANES] + both[:, LANES:]
              + jnp.dot(h_lo, wr[:, :LANES], preferred_element_type=F32))
    logits = logits.T + br_ref[:, 0:1]
    gl = logits[0:N_GROUPS]
    gmax = jnp.max(gl, axis=0, keepdims=True)
    gi = lax.broadcasted_iota(I32, gl.shape, 0)
    g_idx = jnp.min(jnp.where(gl == gmax, gi, N_GROUPS), axis=0, keepdims=True)
    g_val = 1.0 / jnp.sum(jnp.exp(gl - gmax), axis=0, keepdims=True)
    e_in = logits[ROUTER_EXPERT_ROW0:ROUTER_EXPERT_ROW0 + EXPERTS_PER_GROUP]
    for grp in range(1, N_GROUPS):
        lo = ROUTER_EXPERT_ROW0 + grp * EXPERTS_PER_GROUP
        e_in = jnp.where(g_idx == grp, logits[lo:lo + EXPERTS_PER_GROUP], e_in)
    ei = lax.broadcasted_iota(I32, e_in.shape, 0)
    v1 = jnp.max(e_in, axis=0, keepdims=True)
    i1 = jnp.min(jnp.where(e_in == v1, ei, EXPERTS_PER_GROUP), axis=0, keepdims=True)
    rest = jnp.where(ei == i1, -jnp.inf, e_in)
    v2 = jnp.max(rest, axis=0, keepdims=True)
    i2 = jnp.min(jnp.where(rest == v2, ei, EXPERTS_PER_GROUP), axis=0, keepdims=True)
    w2 = jnp.exp(v2 - v1)
    gate1 = g_val / (1.0 + w2)
    gate2 = g_val * w2 / (1.0 + w2)
    e1 = g_idx * EXPERTS_PER_GROUP + i1
    e2 = g_idx * EXPERTS_PER_GROUP + i2
    xi = lax.broadcasted_iota(I32, (N_EXPERTS, tm), 0)
    oh1 = xi == e1
    oh2 = xi == e2
    oh = oh1.astype(F32) + oh2.astype(F32)
    before = jnp.dot(oh.astype(BF16), tri_ref[...], preferred_element_type=F32)
    rank1 = jnp.sum(jnp.where(oh1, before, 0.0), axis=0, keepdims=True)
    rank2 = jnp.sum(jnp.where(oh2, before, 0.0), axis=0, keepdims=True)
    cnt_ref[...] = jnp.broadcast_to(jnp.sum(oh, axis=1, keepdims=True), cnt_ref.shape)
    orow = lax.broadcasted_iota(I32, (8, tm), 0)
    ri_ref[...] = jnp.where(orow == 0, e1, jnp.where(orow == 1, e2, jnp.where(
        orow == 2, rank1.astype(I32), jnp.where(orow == 3, rank2.astype(I32), 0))))
    rg_ref[...] = jnp.where(orow == 0, gate1, jnp.where(orow == 1, gate2, 0.0))


def _route_specs(route, d, n_tiles):
    tm = ROW_TILE
    const = lambda a: pl.BlockSpec(a.shape, lambda i: (0,) * a.ndim)
    in_specs = [const(a) for a in route]
    out_specs = [
        pl.BlockSpec((tm, d), lambda i: (i, 0)),
        pl.BlockSpec((8, tm), lambda i: (0, i)),
        pl.BlockSpec((8, tm), lambda i: (0, i)),
        pl.BlockSpec((None, N_EXPERTS, LANES), lambda i: (i, 0, 0)),
    ]
    rows = n_tiles * tm
    out_shape = [
        jax.ShapeDtypeStruct((rows, d), BF16),
        jax.ShapeDtypeStruct((8, rows), I32),
        jax.ShapeDtypeStruct((8, rows), F32),
        jax.ShapeDtypeStruct((n_tiles, N_EXPERTS, LANES), F32),
    ]
    return in_specs, out_specs, out_shape


def _even_out_kernel(fl_ref, fc_ref, bg_ref, u_ref, up_ref, un_ref, cw_ref, w_ref, xl_ref, xc_ref, mod_ref,
                     gf_ref, wr_ref, br_ref, tri_ref, o_ref, h_ref, ri_ref, rg_ref, cnt_ref, *, n_lat_tiles, n_tiles):
    i = pl.program_id(0)
    tm = u_ref.shape[0]
    u = u_ref[...].astype(F32)
    row = lax.broadcasted_iota(I32, u.shape, 0)
    first = jnp.logical_or(i == 0, i == n_lat_tiles)
    last = jnp.logical_or(i == n_lat_tiles - 1, i == n_tiles - 1)
    hb = up_ref.shape[0]
    halo_p = up_ref[...].astype(F32)[hb - 1:hb, :] * jnp.where(first, 0.0, 1.0)
    halo_n = un_ref[...].astype(F32)[0:1, :] * jnp.where(last, 0.0, 1.0)
    u_prev = jnp.where(row == 0, halo_p, pltpu.roll(u, 1, axis=0))
    u_next = jnp.where(row == tm - 1, halo_n, pltpu.roll(u, tm - 1, axis=0))
    cw = cw_ref[...]
    y = bg_ref[...].astype(F32) * (cw[0:1, :] * u_prev + cw[1:2, :] * u + cw[2:3, :] * u_next)
    is_ctx = jnp.full(fl_ref.shape, i, I32) >= n_lat_tiles
    f = jnp.where(is_ctx, fc_ref[...], fl_ref[...])
    acc = jnp.dot(f.astype(BF16), w_ref[0:FOURIER_WIDTH, :], preferred_element_type=F32)
    acc += jnp.dot(y.astype(BF16), w_ref[FOURIER_WIDTH:, :], preferred_element_type=F32)
    x_new = _pick_rows(xl_ref, xc_ref, n_lat_tiles) + mod_ref[2:3, :] * acc
    o_ref[...] = x_new
    _route_tile(x_new, gf_ref, mod_ref, wr_ref, br_ref, tri_ref, h_ref, ri_ref, rg_ref, cnt_ref)


def _even_out(f_lat, f_ctx, bg, u, conv_w, w_out, x, mod, route, n_lat_tiles):
    t, d = u.shape[0], w_out.shape[1]
    tm = ROW_TILE
    hb = 16
    n_tiles = t // tm
    r = tm // hb
    r_in, r_out, r_shape = _route_specs(route, d, n_tiles)
    xs, x_specs = _row_sources(x, n_lat_tiles, n_tiles, d)
    return pl.pallas_call(
        functools.partial(_even_out_kernel, n_lat_tiles=n_lat_tiles, n_tiles=n_tiles),
        grid=(n_tiles,),
        in_specs=[
            pl.BlockSpec((tm, FOURIER_WIDTH), lambda i: (jnp.minimum(i, n_lat_tiles - 1), 0)),
            pl.BlockSpec((tm, FOURIER_WIDTH), lambda i: (jnp.clip(i - n_lat_tiles, 0, n_tiles - n_lat_tiles - 1), 0)),
            pl.BlockSpec((tm, CONV_WIDTH), lambda i: (i, 0)),
            pl.BlockSpec((tm, CONV_WIDTH), lambda i: (i, 0)),
            pl.BlockSpec((hb, CONV_WIDTH), lambda i: (jnp.maximum(i * r - 1, 0), 0)),
            pl.BlockSpec((hb, CONV_WIDTH), lambda i: (jnp.minimum((i + 1) * r, t // hb - 1), 0)),
            pl.BlockSpec(conv_w.shape, lambda i: (0, 0)),
            pl.BlockSpec(w_out.shape, lambda i: (0, 0)),
        ] + x_specs + [
            pl.BlockSpec((None, 6, d), _stream_of(n_lat_tiles)),
        ] + r_in,
        out_specs=[pl.BlockSpec((tm, d), lambda i: (i, 0))] + r_out,
        out_shape=[jax.ShapeDtypeStruct((t, d), F32)] + r_shape,
        compiler_params=_cparams(("arbitrary",)),
        name="even_out_proj",
    )(f_lat, f_ctx, bg, u, u, u, conv_w, w_out, *xs, mod, *route)


def _seg_rms_scale(v, seg, seg_t):
    ss = jnp.dot((v * v).astype(BF16), seg, preferred_element_type=F32)
    inv = lax.rsqrt(ss * (1.0 / HEAD_DIM) + EPS)
    inv2 = jnp.concatenate(_split_bf16(inv), axis=1)
    return jnp.dot(inv2, seg_t, preferred_element_type=F32)


def _rope_cols(v, cos, sa, sb, scale):
    cols = []
    for j in range(v.shape[1] // LANES):
        c = v[:, j * LANES:(j + 1) * LANES]
        r = c * cos + pltpu.roll(c, LANES - ROPE_PAIRS, axis=1) * sa + pltpu.roll(c, ROPE_PAIRS, axis=1) * sb
        cols.append(r * scale if scale != 1.0 else r)
    return jnp.concatenate(cols, axis=1)


def _odd_in_kernel(x_ref, g_ref, mod_ref, w_ref, qg_ref, kg_ref, segq_ref, segqt_ref, segk_ref, segkt_ref,
                   rope_ref, q_ref, k_ref, v_ref):
    h = _norm_mod(x_ref[...], g_ref[...], mod_ref[0:1, :], mod_ref[1:2, :])
    p = jnp.dot(h.astype(BF16), w_ref[...], preferred_element_type=F32)
    qd = N_HEADS * HEAD_DIM
    kd = 2 * N_KV_HEADS * HEAD_DIM
    cos, sa, sb = rope_ref[0], rope_ref[1], rope_ref[2]
    q = p[:, :qd]
    q = q * _seg_rms_scale(q, segq_ref[...], segqt_ref[...]) * qg_ref[...]
    q_ref[...] = _rope_cols(q, cos, sa, sb, HEAD_DIM ** -0.5 * LOG2E).astype(BF16)
    k = p[:, qd:qd + kd]
    k = k * _seg_rms_scale(k, segk_ref[...], segkt_ref[...]) * kg_ref[...]
    k_ref[...] = _rope_cols(k, cos, sa, sb, 1.0).astype(BF16)
    v_ref[...] = p[:, qd + kd:].astype(BF16)


def _odd_in(x, g, mod, w_qkv, qg, kg, segs, rope, n_lat_tiles, n_tiles):
    t, d = x.shape
    n = w_qkv.shape[1]
    tm = ROW_TILE
    rows = n_tiles * tm
    qd = N_HEADS * HEAD_DIM
    kd = 2 * N_KV_HEADS * HEAD_DIM
    segq, segqt, segk, segkt = segs
    const = lambda a: pl.BlockSpec(a.shape, lambda i: (0,) * a.ndim)
    return pl.pallas_call(
        _odd_in_kernel,
        grid=(n_tiles,),
        in_specs=[
            pl.BlockSpec((tm, d), lambda i: (i, 0)),
            const(g),
            pl.BlockSpec((None, 6, d), _stream_of(n_lat_tiles)),
            const(w_qkv), const(qg), const(kg), const(segq), const(segqt), const(segk), const(segkt),
            pl.BlockSpec((3, tm, LANES), lambda i: (0, i, 0)),
        ],
        out_specs=[
            pl.BlockSpec((tm, qd), lambda i: (i, 0)),
            pl.BlockSpec((tm, kd), lambda i: (i, 0)),
            pl.BlockSpec((tm, kd), lambda i: (i, 0)),
        ],
        out_shape=[
            jax.ShapeDtypeStruct((rows, qd), BF16),
            jax.ShapeDtypeStruct((rows, kd), BF16),
            jax.ShapeDtypeStruct((rows, kd), BF16),
        ],
        compiler_params=_cparams(("arbitrary",)),
        name="odd_in_proj",
    )(x, g, mod, w_qkv, qg, kg, segq, segqt, segk, segkt, rope)


def _rope_tables(l, t):
    pos = np.arange(l)
    freqs = ROPE_BASE ** (-np.arange(ROPE_PAIRS, dtype=np.float32) / ROPE_PAIRS)
    lane = np.arange(LANES) % HEAD_DIM
    axis = lane // (2 * ROPE_PAIRS)
    half = (lane % (2 * ROPE_PAIRS)) // ROPE_PAIRS
    pair = lane % ROPE_PAIRS
    p = np.where(axis[None, :] == 0, (pos // GRID_W)[:, None], (pos % GRID_W)[:, None]).astype(np.float32)
    ang = p * freqs[pair][None, :].astype(np.float32)
    cos, sin = np.cos(ang), np.sin(ang)
    sa = np.where(half[None, :] == 0, -sin, 0.0)
    sb = np.where(half[None, :] == 1, sin, 0.0)
    tab = np.zeros((3, t, LANES), np.float32)
    tab[0, :l], tab[1, :l], tab[2, :l] = cos, sa, sb
    tab[0, l:] = 1.0
    return jnp.asarray(tab)


def _segment_matrices():
    def seg(width):
        m = np.zeros((width, LANES), np.float32)
        m[np.arange(width), np.arange(width) // HEAD_DIM] = 1.0
        return m
    sq, sk = seg(N_HEADS * HEAD_DIM), seg(2 * N_KV_HEADS * HEAD_DIM)
    b = lambda v: jnp.asarray(v).astype(BF16)
    twice = lambda m: np.concatenate([m, m], axis=0)
    return b(sq), b(twice(sq.T)), b(sk), b(twice(sk.T))


def _attn_block(sink_ref, q_ref, r0, k_refs, v_refs, kx_ref, vx_ref, bias, o_ref):
    bq = ATT_BLOCK
    kp_ref, kc_ref, kn_ref = k_refs
    vp_ref, vc_ref, vn_ref = v_refs
    rows = slice(r0, r0 + bq)
    low = lax.broadcasted_iota(I32, (bq, LANES), 1) < HEAD_DIM
    top = lax.broadcasted_iota(I32, (LANES, bq), 0) < HEAD_DIM
    nt = (((1,), (1,)), ((), ()))
    tn = (((0,), (0,)), ((), ()))
    st = []
    for g in range(N_KV_HEADS):
        ks = slice(g * LANES, (g + 1) * LANES)
        parts, sinks = [], []
        for j in range(GQA_GROUP):
            col = 2 * g + j // 2
            c = q_ref[rows, col * LANES:(col + 1) * LANES]
            keep = low if j % 2 == 0 else jnp.logical_not(low)
            parts.append(jnp.where(keep, c, jnp.zeros_like(c)))
            sinks.append(jnp.full((1, bq), sink_ref[g * GQA_GROUP + j], F32))
        qs = jnp.concatenate(parts, axis=0)
        sink = jnp.concatenate(sinks, axis=1)
        kwin = jnp.concatenate([kp_ref[:, ks], kc_ref[:, ks], kn_ref[:, ks]], axis=0)
        s_loc = lax.dot_general(kwin, qs, nt, preferred_element_type=F32) + bias
        s_ctx = lax.dot_general(kx_ref[:, ks], qs, nt, preferred_element_type=F32)
        st.append((s_loc, s_ctx, sink))
    pr = []
    for g in range(N_KV_HEADS):
        s_loc, s_ctx, sink = st[g]
        m = jnp.maximum(jnp.maximum(jnp.max(s_loc, axis=0, keepdims=True),
                                    jnp.max(s_ctx, axis=0, keepdims=True)), sink)
        p_loc = jnp.exp2(s_loc - m)
        p_ctx = jnp.exp2(s_ctx - m)
        den = (jnp.sum(p_loc, axis=0, keepdims=True) + jnp.sum(p_ctx, axis=0, keepdims=True)
               + jnp.exp2(sink - m))
        pr.append((p_loc.astype(BF16), p_ctx.astype(BF16), 1.0 / den))
    for g in range(N_KV_HEADS):
        ks = slice(g * LANES, (g + 1) * LANES)
        p_loc, p_ctx, inv = pr[g]
        vwin = jnp.concatenate([vp_ref[:, ks], vc_ref[:, ks], vn_ref[:, ks]], axis=0)
        ot = lax.dot_general(vwin, p_loc, tn, preferred_element_type=F32)
        ot += lax.dot_general(vx_ref[:, ks], p_ctx, tn, preferred_element_type=F32)
        ot = ot * inv
        t0 = jnp.where(top, ot[:, 0:bq], ot[:, bq:2 * bq])
        t1 = jnp.where(top, ot[:, 2 * bq:3 * bq], ot[:, 3 * bq:4 * bq])
        o_ref[rows, 2 * g * LANES:(2 * g + 1) * LANES] = t0.T.astype(o_ref.dtype)
        o_ref[rows, (2 * g + 1) * LANES:(2 * g + 2) * LANES] = t1.T.astype(o_ref.dtype)


def _attn_kernel(sink_ref, q_ref, k0, k1, k2, k3, v0, v1, v2, v3, kx_ref, vx_ref, bias_a_ref, bias_b_ref, o_ref):
    _attn_block(sink_ref, q_ref, 0, (k0, k1, k2), (v0, v1, v2), kx_ref, vx_ref, bias_a_ref[...], o_ref)
    _attn_block(sink_ref, q_ref, ATT_BLOCK, (k1, k2, k3), (v1, v2, v3), kx_ref, vx_ref, bias_b_ref[...], o_ref)


def _attn_bias(l):
    bq = ATT_BLOCK
    r = np.arange(GQA_GROUP * bq)[None, :] % bq
    col = np.arange(3 * bq)[:, None]
    band = np.abs(col - bq - r) <= WINDOW
    no_prev, no_next = col >= bq, col < 2 * bq
    masks = [band, band & no_prev, band & no_next, band & no_prev & no_next, np.zeros_like(band)]
    return jnp.asarray(np.where(np.stack(masks), 0.0, NEG_INF).astype(np.float32))


def _attention(q, k, v, sink, bias, l, n_q_blocks):
    t = q.shape[0]
    bq = ATT_BLOCK
    nlb = l // bq
    lc = t - l
    ctx_blk = l // lc
    kw = k.shape[1]
    assert n_q_blocks % 2 == 0 and nlb % 2 == 0
    key_block = lambda off: pl.BlockSpec((bq, kw), lambda i, s: (jnp.clip(2 * i + off, 0, nlb - 1), 0))
    keys = [key_block(off) for off in (-1, 0, 1, 2)]
    ctx = pl.BlockSpec((lc, kw), lambda i, s: (ctx_blk, 0))

    def kind(off):
        def index(i, s):
            b = 2 * i + off
            return (jnp.where(b >= nlb, 4, (b == 0).astype(I32) + 2 * (b == nlb - 1).astype(I32)), 0, 0)
        return pl.BlockSpec((None,) + bias.shape[1:], index)
    return pl.pallas_call(
        _attn_kernel,
        grid_spec=pltpu.PrefetchScalarGridSpec(
            num_scalar_prefetch=1,
            grid=(n_q_blocks // 2,),
            in_specs=[pl.BlockSpec((2 * bq, q.shape[1]), lambda i, s: (i, 0))] + keys + keys
                     + [ctx, ctx, kind(0), kind(1)],
            out_specs=pl.BlockSpec((2 * bq, q.shape[1]), lambda i, s: (i, 0)),
        ),
        out_shape=jax.ShapeDtypeStruct((n_q_blocks * bq, q.shape[1]), BF16),
        compiler_params=_cparams(("arbitrary",)),
        name="window_attention",
    )(sink, q, k, k, k, k, v, v, v, v, k, v, bias, bias)


def _odd_out_kernel(a_ref, w_ref, x_ref, mod_ref, gf_ref, wr_ref, br_ref, tri_ref, o_ref, h_ref, ri_ref, rg_ref,
                    cnt_ref):
    acc = jnp.dot(a_ref[...], w_ref[...], preferred_element_type=F32)
    x_new = x_ref[...] + mod_ref[2:3, :] * acc
    o_ref[...] = x_new
    _route_tile(x_new, gf_ref, mod_ref, wr_ref, br_ref, tri_ref, h_ref, ri_ref, rg_ref, cnt_ref)


def _odd_out(a, w_o, x, mod, route, n_lat_tiles, n_tiles):
    d = x.shape[1]
    tm = ROW_TILE
    r_in, r_out, r_shape = _route_specs(route, d, n_tiles)
    return pl.pallas_call(
        _odd_out_kernel,
        grid=(n_tiles,),
        in_specs=[
            pl.BlockSpec((tm, a.shape[1]), lambda i: (i, 0)),
            pl.BlockSpec(w_o.shape, lambda i: (0, 0)),
            pl.BlockSpec((tm, d), lambda i: (i, 0)),
            pl.BlockSpec((None, 6, d), _stream_of(n_lat_tiles)),
        ] + r_in,
        out_specs=[pl.BlockSpec((tm, d), lambda i: (i, 0))] + r_out,
        out_shape=[jax.ShapeDtypeStruct((n_tiles * tm, d), F32)] + r_shape,
        compiler_params=_cparams(("arbitrary",)),
        name="odd_out_proj",
    )(a, w_o, x, mod, *route)


CHUNK = 8
TILE_BUF = 2 * ROW_TILE + N_EXPERTS * CHUNK
MAX_TILE_CHUNKS = TILE_BUF // CHUNK
FFN_BUFS = 4


def _chunk_rows(c):
    return pl.ds(pl.multiple_of(c * CHUNK, CHUNK), CHUNK)


def _wait_rows(copy_of_rows, n_chunks):
    bit = 1
    while bit <= MAX_TILE_CHUNKS:
        @pl.when((n_chunks & bit) != 0)
        def _(bit=bit):
            copy_of_rows(bit * CHUNK).wait()
        bit *= 2


def _dispatch_kernel(tab_ref, lused_ref, fill_ref, h_ref, pos_ref, xb_ref, hs, zbuf, sem, fsem, *, n_tiles, n_blocks):
    i = pl.program_id(0)
    slot = i % 2
    tm = h_ref.shape[0]

    def tail_copy(c):
        return pltpu.make_async_copy(zbuf.at[pl.ds(0, CHUNK), :], xb_ref.at[_chunk_rows(c), :], fsem)

    def block_copy(b):
        return pltpu.make_async_copy(zbuf, xb_ref.at[pl.ds(pl.multiple_of(b * MOE_ROWS, MOE_ROWS), MOE_ROWS), :], fsem)

    def fill(start):
        def tail(e, c):
            st, n = fill_ref[e], fill_ref[N_EXPERTS + e]

            def one(c2, cc):
                cp = tail_copy(st + c2)
                cp.start() if start else cp.wait()
                return cc
            return lax.fori_loop(0, n, one, c)
        lax.fori_loop(0, N_EXPERTS, tail, 0)

        def blk(b, c):
            cp = block_copy(b)
            cp.start() if start else cp.wait()
            return c
        lax.fori_loop(fill_ref[2 * N_EXPERTS], n_blocks, blk, 0)

    @pl.when(i == 0)
    def _():
        zbuf[...] = jnp.zeros_like(zbuf)
        fill(True)

    pos = pos_ref[...]
    r = lax.broadcasted_iota(I32, (TILE_BUF, tm), 0)
    onehot = jnp.where(jnp.logical_or(r == pos[0:1, :], r == pos[1:2, :]), 1.0, 0.0).astype(BF16)
    hs[slot] = jnp.dot(onehot, h_ref[...], preferred_element_type=F32).astype(BF16)

    def chunk_copy(sl, src, dst):
        return pltpu.make_async_copy(hs.at[sl, _chunk_rows(src), :], xb_ref.at[_chunk_rows(dst), :], sem.at[sl])

    base = i * MAX_TILE_CHUNKS

    def per_chunk(c, cc):
        chunk_copy(slot, c, tab_ref[base + c]).start()
        return cc
    lax.fori_loop(0, lused_ref[i], per_chunk, 0)

    def wait_chunks(sl, n):
        _wait_rows(lambda rows: pltpu.make_async_copy(hs.at[sl, pl.ds(0, rows), :], xb_ref.at[pl.ds(0, rows), :],
                                                      sem.at[sl]), n)

    @pl.when(i > 0)
    def _():
        wait_chunks(1 - slot, lused_ref[jnp.maximum(i - 1, 0)])

    @pl.when(i == n_tiles - 1)
    def _():
        wait_chunks(slot, lused_ref[i])
        fill(False)


def _dispatch(h, pos_rows, tab, lused, fill, n_blocks):
    t, d = h.shape
    tm = ROW_TILE
    n_tiles = t // tm
    return pl.pallas_call(
        functools.partial(_dispatch_kernel, n_tiles=n_tiles, n_blocks=n_blocks),
        grid_spec=pltpu.PrefetchScalarGridSpec(
            num_scalar_prefetch=3,
            grid=(n_tiles,),
            in_specs=[
                pl.BlockSpec((tm, d), lambda i, *_: (i, 0)),
                pl.BlockSpec((8, tm), lambda i, *_: (0, i)),
            ],
            out_specs=pl.BlockSpec(memory_space=pl.ANY),
            scratch_shapes=[
                pltpu.VMEM((2, TILE_BUF, d), BF16),
                pltpu.VMEM((MOE_ROWS, d), BF16),
                pltpu.SemaphoreType.DMA((2,)),
                pltpu.SemaphoreType.DMA(()),
            ],
        ),
        out_shape=jax.ShapeDtypeStruct((n_blocks * MOE_ROWS, d), BF16),
        compiler_params=_cparams(("arbitrary",)),
        name="moe_dispatch",
    )(tab, lused, fill, h, pos_rows)


def _ffn_kernel(b0_ref, nb_ref, nv_ref, fill_ref, xb_ref, w1_ref, w3_ref, w2_ref, yb_ref,
                w1s, w3s, w2s, xbuf, ybuf, zbuf, sem_in, sem_out, fsem, *, n_blocks):
    e = pl.program_id(0)
    w1s[...] = w1_ref[...].astype(BF16)
    w3s[...] = w3_ref[...].astype(BF16)
    w2s[...] = w2_ref[...].astype(BF16)
    b0, nb = b0_ref[e], nb_ref[e]

    def rows(b):
        return pl.ds(pl.multiple_of(b * MOE_ROWS, MOE_ROWS), MOE_ROWS)

    def x_copy(b, sl):
        return pltpu.make_async_copy(xb_ref.at[rows(b), :], xbuf.at[sl], sem_in.at[sl])

    def y_copy(b, sl):
        return pltpu.make_async_copy(ybuf.at[sl], yb_ref.at[rows(b), :], sem_out.at[sl])

    for k in range(FFN_BUFS - 1):
        @pl.when(k < nb)
        def _(k=k):
            x_copy(b0 + k, k).start(priority=1)

    def block(j, c):
        sl = j % FFN_BUFS
        x_copy(b0 + j, sl).wait()
        ahead = j + (FFN_BUFS - 1)

        @pl.when(ahead < nb)
        def _():
            x_copy(b0 + ahead, ahead % FFN_BUFS).start(priority=1)

        @pl.when(j >= FFN_BUFS)
        def _():
            y_copy(b0 + j - FFN_BUFS, sl).wait()

        x = xbuf[sl]
        row = lax.broadcasted_iota(I32, x.shape, 0)
        xb = jnp.where(row < nv_ref[b0 + j], x, jnp.zeros_like(x))
        a = jnp.dot(xb, w1s[...], preferred_element_type=F32)
        b = jnp.dot(xb, w3s[...], preferred_element_type=F32)
        hid = (a * jax.nn.sigmoid(a) * b).astype(BF16)
        ybuf[sl] = jnp.dot(hid, w2s[...], preferred_element_type=F32).astype(BF16)
        y_copy(b0 + j, sl).start(priority=1)
        return c
    lax.fori_loop(0, nb, block, 0)

    for k in range(FFN_BUFS):
        @pl.when(nb > k)
        def _(k=k):
            y_copy(b0 + nb - 1 - k, (nb - 1 - k) % FFN_BUFS).wait()

    @pl.when(e == N_EXPERTS - 1)
    def _():
        zbuf[...] = jnp.zeros_like(zbuf)

        def z_copy(b):
            return pltpu.make_async_copy(zbuf, yb_ref.at[rows(b), :], fsem)

        def start(b, c):
            z_copy(b).start()
            return c

        def wait(b, c):
            z_copy(b).wait()
            return c
        lax.fori_loop(fill_ref[0], n_blocks, start, 0)
        lax.fori_loop(fill_ref[0], n_blocks, wait, 0)


def _expert_ffn(xb, blk_start, blk_count, n_valid, used_blocks, w1, w3, w2, layer):
    r = xb.shape[0]
    d, f = w1.shape[2], w1.shape[3]
    bm = MOE_ROWS
    return pl.pallas_call(
        functools.partial(_ffn_kernel, n_blocks=r // bm),
        grid_spec=pltpu.PrefetchScalarGridSpec(
            num_scalar_prefetch=4,
            grid=(N_EXPERTS,),
            in_specs=[
                pl.BlockSpec(memory_space=pl.ANY),
                pl.BlockSpec((None, None, d, f), lambda e, *_: (layer, e, 0, 0)),
                pl.BlockSpec((None, None, d, f), lambda e, *_: (layer, e, 0, 0)),
                pl.BlockSpec((None, None, f, d), lambda e, *_: (layer, e, 0, 0)),
            ],
            out_specs=pl.BlockSpec(memory_space=pl.ANY),
            scratch_shapes=[
                pltpu.VMEM((d, f), BF16), pltpu.VMEM((d, f), BF16), pltpu.VMEM((f, d), BF16),
                pltpu.VMEM((FFN_BUFS, bm, d), BF16), pltpu.VMEM((FFN_BUFS, bm, d), BF16), pltpu.VMEM((bm, d), BF16),
                pltpu.SemaphoreType.DMA((FFN_BUFS,)), pltpu.SemaphoreType.DMA((FFN_BUFS,)), pltpu.SemaphoreType.DMA(()),
            ],
        ),
        out_shape=jax.ShapeDtypeStruct((r, d), BF16),
        compiler_params=_cparams(("arbitrary",)),
        name="moe_expert_mlp",
    )(blk_start, blk_count, n_valid, used_blocks, xb, w1, w3, w2)


def _combine_kernel(tab_ref, lused_ref, yb_ref, pos_ref, gate_ref, x_ref, mod_ref, o_ref, ys, sem, *, n_tiles):
    i = pl.program_id(0)
    slot = i % 2
    tm = x_ref.shape[0]

    def chunk_copy(sl, src, dst):
        return pltpu.make_async_copy(yb_ref.at[_chunk_rows(src), :], ys.at[sl, _chunk_rows(dst), :], sem.at[sl])

    def fetch(tile, sl):
        base = tile * MAX_TILE_CHUNKS

        def per_chunk(c, cc):
            chunk_copy(sl, tab_ref[base + c], c).start()
            return cc
        lax.fori_loop(0, lused_ref[tile], per_chunk, 0)

    @pl.when(i == 0)
    def _():
        fetch(0, 0)

    @pl.when(i + 1 < n_tiles)
    def _():
        fetch(jnp.minimum(i + 1, n_tiles - 1), 1 - slot)

    _wait_rows(lambda rows: pltpu.make_async_copy(yb_ref.at[pl.ds(0, rows), :], ys.at[slot, pl.ds(0, rows), :],
                                                  sem.at[slot]), lused_ref[i])

    y = ys[slot]
    used = lused_ref[i] * CHUNK
    rowi = lax.broadcasted_iota(I32, y.shape, 0)
    y16 = jnp.where(rowi < used, y, jnp.zeros_like(y))
    pos = pos_ref[...]
    gate = gate_ref[...]
    lane = lax.broadcasted_iota(I32, (tm, TILE_BUF), 1)
    gm = (jnp.where(lane == pos[:, 0:1], gate[:, 0:1], 0.0) + jnp.where(lane == pos[:, 1:2], gate[:, 1:2], 0.0))
    ghi, glo = _split_bf16(gm)
    mix = jnp.dot(ghi, y16, preferred_element_type=F32) + jnp.dot(glo, y16, preferred_element_type=F32)
    o_ref[...] = x_ref[...] + mod_ref[5:6, :] * mix


def _combine(yb, pos_cols, gates, tab, lused, x, mod, n_lat_tiles, n_tiles):
    d = x.shape[1]
    tm = ROW_TILE
    return pl.pallas_call(
        functools.partial(_combine_kernel, n_tiles=n_tiles),
        grid_spec=pltpu.PrefetchScalarGridSpec(
            num_scalar_prefetch=2,
            grid=(n_tiles,),
            in_specs=[
                pl.BlockSpec(memory_space=pl.ANY),
                pl.BlockSpec((tm, 8), lambda i, *_: (i, 0)),
                pl.BlockSpec((tm, 8), lambda i, *_: (i, 0)),
                pl.BlockSpec((tm, d), lambda i, *_: (i, 0)),
                pl.BlockSpec((None, 6, d), lambda i, *_: (jnp.where(i >= n_lat_tiles, 1, 0), 0, 0)),
            ],
            out_specs=pl.BlockSpec((tm, d), lambda i, *_: (i, 0)),
            scratch_shapes=[pltpu.VMEM((2, TILE_BUF, d), BF16), pltpu.SemaphoreType.DMA((2,))],
        ),
        out_shape=jax.ShapeDtypeStruct((n_tiles * tm, d), F32),
        compiler_params=_cparams(("arbitrary",)),
        name="moe_combine",
    )(tab, lused, yb, pos_cols, gates, x, mod)


def _moe_layer(x, routing, mod, w1, w3, w2, layer, n_lat_tiles, n_tiles):
    tm = ROW_TILE
    rows = n_tiles * tm
    cpb = MOE_ROWS // CHUNK
    h, ri, rg, cnt3 = routing
    cnt = cnt3[:, :, 0].astype(I32)
    nch = (cnt + CHUNK - 1) // CHUNK
    lbase = jnp.cumsum(nch, axis=1) - nch
    lused = jnp.sum(nch, axis=1).astype(I32)
    tot = jnp.sum(nch, axis=0)
    reg = (tot + cpb - 1) // cpb * cpb
    gend = jnp.cumsum(reg)
    gstart = gend - reg
    gpos = gstart[None, :] + jnp.cumsum(nch, axis=0) - nch
    rows_max = 2 * rows + n_tiles * N_EXPERTS * (CHUNK - 1) + N_EXPERTS * (MOE_ROWS - CHUNK)
    n_blocks = -(-rows_max // MOE_ROWS)
    ex = jnp.arange(N_EXPERTS, dtype=I32)
    blk0 = jnp.arange(n_blocks, dtype=I32) * cpb
    block_exp = jnp.minimum(jnp.sum((gend[None, :] <= blk0[:, None]).astype(I32), axis=1), N_EXPERTS - 1)
    sel = block_exp[:, None] == ex[None, :]
    tot_b = jnp.sum(jnp.where(sel, tot[None, :], 0), axis=1)
    st_b = jnp.sum(jnp.where(sel, gstart[None, :], 0), axis=1)
    n_valid = jnp.clip((tot_b - (blk0 - st_b)) * CHUNK, 0, MOE_ROWS).astype(I32)
    lb_tok = jnp.repeat(lbase, tm, axis=0)
    at = lambda e: jnp.sum(jnp.where(e[:, None] == ex[None, :], lb_tok, 0), axis=1)
    pos1 = CHUNK * at(ri[0]) + ri[2]
    pos2 = CHUNK * at(ri[1]) + ri[3]
    zero = jnp.zeros_like(pos1)
    pos_rows = jnp.stack([pos1, pos2] + [zero] * 6, axis=0).astype(I32)
    slot_id = jnp.arange(MAX_TILE_CHUNKS, dtype=I32)
    owner = jnp.sum((lbase + nch)[:, None, :] <= slot_id[None, :, None], axis=2)
    own = jnp.minimum(owner, N_EXPERTS - 1)[:, :, None] == ex[None, None, :]
    tab = jnp.sum(jnp.where(own, (gpos - lbase)[:, None, :], 0), axis=2) + slot_id[None, :]
    tab = tab.astype(I32).reshape(-1)
    fill = jnp.concatenate([gstart + tot, reg - tot, gend[-1:] // cpb]).astype(I32)
    xb = _dispatch(h, pos_rows, tab, lused, fill, n_blocks)
    yb = _expert_ffn(xb, (gstart // cpb).astype(I32), (reg // cpb).astype(I32), n_valid,
                     (gend[-1:] // cpb).astype(I32), w1, w3, w2, layer)
    return _combine(yb, pos_rows.T, rg.T, tab, lused, x, mod, n_lat_tiles, n_tiles)


def _router_matrix(w_rg, b_rg, w_re, b_re):
    d = w_rg.shape[0]
    wr = jnp.zeros((d, LANES), F32)
    wr = wr.at[:, 0:N_GROUPS].set(w_rg.astype(F32))
    wr = wr.at[:, ROUTER_EXPERT_ROW0:ROUTER_EXPERT_ROW0 + N_EXPERTS].set(w_re.astype(F32))
    wr = jnp.concatenate(_split_bf16(wr), axis=1)
    br = jnp.zeros((LANES,), F32)
    br = br.at[0:N_GROUPS].set(b_rg.astype(F32))
    br = br.at[ROUTER_EXPERT_ROW0:ROUTER_EXPERT_ROW0 + N_EXPERTS].set(b_re.astype(F32))
    return wr, jnp.broadcast_to(br[:, None], (LANES, LANES))


def _dup_heads(w):
    d = w.shape[0]
    w4 = w.reshape(d, N_KV_HEADS, 1, HEAD_DIM)
    return jnp.broadcast_to(w4, (d, N_KV_HEADS, 2, HEAD_DIM)).reshape(d, 2 * N_KV_HEADS * HEAD_DIM)


def kernel(x, c, ctx, c_ctx, w_mod, b_mod, norm_mix_g, norm_ffn_g, w_in_even, conv_w, w_out_even, w_qkv, q_norm_g,
           k_norm_g, sink_logit, w_o, w_router_g, b_router_g, w_router_e, b_router_e, w1, w3, w2):
    bsz, l, d = x.shape
    lc = ctx.shape[1]
    assert bsz == 1, "one sample per call"
    tm = ROW_TILE
    assert l % tm == 0 and lc % tm == 0 and l % lc == 0 and l % (DFT_MINOR * 8) == 0
    depth = w_mod.shape[0]
    t = l + lc
    assert t % DFT_MINOR == 0
    nl, nt = l // tm, t // tm

    xs = (x.reshape(l, d), ctx.reshape(lc, d))
    mod_all = _modulation(c, c_ctx, w_mod, b_mod).reshape(depth, 2, 6, d)
    cs, m1, a_tab, mc = _dft_tables(l, lc)
    rope = _rope_tables(l, t)
    segs = _segment_matrices()
    attn_bias = _attn_bias(l)
    tri = jnp.asarray(np.triu(np.ones((tm, tm), np.float32), 1)).astype(BF16)
    qd = N_HEADS * HEAD_DIM

    for layer in range(depth):
        last = layer == depth - 1
        j = layer // 2
        mod = mod_all[layer]
        g_mix = norm_mix_g[layer].reshape(1, d)
        g_ffn = norm_ffn_g[layer].reshape(1, d)
        wr, br = _router_matrix(w_router_g[layer], b_router_g[layer], w_router_e[layer], b_router_e[layer])
        route = (g_ffn, wr, br, tri)
        if layer % 2 == 0:
            z, bg, u = _even_in(xs, g_mix, mod, w_in_even[j].astype(BF16), cs, nl, nt)
            f_lat, f_ctx = _fourier_seq(z, l, m1, a_tab, mc)
            xs, *routing = _even_out(f_lat, f_ctx, bg, u, conv_w[j], w_out_even[j].astype(BF16), xs, mod, route, nl)
        else:
            wq = w_qkv[j]
            w_all = jnp.concatenate([wq[:, :qd], _dup_heads(wq[:, qd:qd + N_KV_HEADS * HEAD_DIM]),
                                     _dup_heads(wq[:, qd + N_KV_HEADS * HEAD_DIM:])], axis=1).astype(BF16)
            qg = jnp.tile(q_norm_g[j], N_HEADS).reshape(1, qd)
            kg = jnp.tile(k_norm_g[j], 2 * N_KV_HEADS).reshape(1, 2 * N_KV_HEADS * HEAD_DIM)
            q, k, v = _odd_in(xs, g_mix, mod, w_all, qg, kg, segs, rope, nl, nt)
            n_out = nl if last else nt
            att = _attention(q, k, v, sink_logit[j].astype(F32) * LOG2E, attn_bias, l,
                             n_out * (tm // ATT_BLOCK))
            xs, *routing = _odd_out(att, w_o[j].astype(BF16), xs, mod, route, nl, n_out)
        n_moe = nl if last else nt
        xs = _moe_layer(xs, routing, mod, w1, w3, w2, layer, nl, n_moe)
    return xs[:l].reshape(bsz, l, d)
```

```python
import functools
import math

import numpy as np
import jax
import jax.numpy as jnp
from jax import lax
from jax.experimental import pallas as pl
from jax.experimental.pallas import tpu as pltpu

F32 = jnp.float32
BF16 = jnp.bfloat16
I32 = jnp.int32

EPS = 1e-6
NEG_INF = -1e30

GRID_W = 64
FOURIER_GROUPS = 4
FOURIER_GROUP_DIM = 128
FOURIER_WIDTH = FOURIER_GROUPS * FOURIER_GROUP_DIM
CONV_WIDTH = 512
N_HEADS = 16
N_KV_HEADS = 4
GQA_GROUP = N_HEADS // N_KV_HEADS
HEAD_DIM = 64
WINDOW = 128
ROPE_BASE = 10000.0
ROPE_PAIRS = HEAD_DIM // 4
N_GROUPS = 4
EXPERTS_PER_GROUP = 8
N_EXPERTS = N_GROUPS * EXPERTS_PER_GROUP

LANES = 128
ROW_TILE = 256
ATT_BLOCK = 128
MOE_ROWS = 256
DFT_MINOR = 128
DFT_MAJOR_COLS = 8
VMEM_LIMIT = 48 * 1024 * 1024

LOG2E = math.log2(math.e)


def _cparams(sem):
    return pltpu.CompilerParams(dimension_semantics=sem, vmem_limit_bytes=VMEM_LIMIT)


def _split_bf16(x):
    hi = x.astype(BF16)
    lo = (x - hi.astype(F32)).astype(BF16)
    return hi, lo


def _mod_kernel(ct_ref, w_ref, b_ref, o_ref):
    ct = ct_ref[...]
    s = ct * jax.nn.sigmoid(ct)
    w = w_ref[...]
    r0 = jnp.sum(w * s[:, 0:1], axis=0, keepdims=True)
    r1 = jnp.sum(w * s[:, 1:2], axis=0, keepdims=True)
    o_ref[...] = jnp.concatenate([r0, r1], axis=0) + b_ref[...]


def _modulation(c, c_ctx, w_mod, b_mod):
    depth, d, n = w_mod.shape
    tn = 512
    ct = jnp.stack([c.reshape(d), c_ctx.reshape(d)], axis=1)
    return pl.pallas_call(
        _mod_kernel,
        grid=(depth, n // tn),
        in_specs=[
            pl.BlockSpec((d, 2), lambda l, j: (0, 0)),
            pl.BlockSpec((None, d, tn), lambda l, j: (l, 0, j)),
            pl.BlockSpec((None, 1, tn), lambda l, j: (l, 0, j)),
        ],
        out_specs=pl.BlockSpec((None, 2, tn), lambda l, j: (l, 0, j)),
        out_shape=jax.ShapeDtypeStruct((depth, 2, n), F32),
        compiler_params=_cparams(("arbitrary", "arbitrary")),
        name="modulation",
    )(ct, w_mod, b_mod.reshape(depth, 1, n))


def _norm_mod(x, g, shift, scale):
    ms = jnp.mean(x * x, axis=-1, keepdims=True)
    y = x * lax.rsqrt(ms + EPS) * g
    return y * (1.0 + scale) + shift


def _stream_of(n_lat_tiles):
    return lambda i: (jnp.where(i >= n_lat_tiles, 1, 0), 0, 0)


def _row_sources(x, n_lat_tiles, n_tiles, d):
    tm = ROW_TILE
    lat = pl.BlockSpec((tm, d), lambda i: (jnp.minimum(i, n_lat_tiles - 1), 0))
    if isinstance(x, tuple):
        ctx = pl.BlockSpec((tm, d), lambda i: (jnp.clip(i - n_lat_tiles, 0, n_tiles - n_lat_tiles - 1), 0))
        return x, [lat, ctx]
    return (x, x), [lat, pl.BlockSpec((tm, d), lambda i: (jnp.clip(i, n_lat_tiles, n_tiles - 1), 0))]


def _pick_rows(xl_ref, xc_ref, n_lat_tiles):
    is_ctx = jnp.full(xl_ref.shape, pl.program_id(0), I32) >= n_lat_tiles
    return jnp.where(is_ctx, xc_ref[...], xl_ref[...])


def _even_in_kernel(xl_ref, xc_ref, g_ref, mod_ref, w_ref, cs_ref, z_ref, bg_ref, u_ref, *, n_lat_tiles):
    h = _norm_mod(_pick_rows(xl_ref, xc_ref, n_lat_tiles), g_ref[...], mod_ref[0:1, :], mod_ref[1:2, :])
    p = jnp.dot(h.astype(BF16), w_ref[...], preferred_element_type=F32)
    cs = cs_ref[...]
    for grp in range(FOURIER_GROUPS):
        lo = grp * FOURIER_GROUP_DIM
        a = p[:, lo:lo + FOURIER_GROUP_DIM].astype(BF16)
        z = jnp.dot(a, cs, preferred_element_type=F32)
        z_ref[0, :, lo:lo + FOURIER_GROUP_DIM] = z[:, :FOURIER_GROUP_DIM]
        z_ref[1, :, lo:lo + FOURIER_GROUP_DIM] = z[:, FOURIER_GROUP_DIM:]
    o = FOURIER_WIDTH
    bg_ref[...] = p[:, o:o + CONV_WIDTH].astype(BF16)
    u_ref[...] = (p[:, o + CONV_WIDTH:o + 2 * CONV_WIDTH] * p[:, o + 2 * CONV_WIDTH:]).astype(BF16)


def _even_in(x, g, mod, w_in, cs, n_lat_tiles, n_tiles):
    d, n = w_in.shape
    tm = ROW_TILE
    t = n_tiles * tm
    xs, x_specs = _row_sources(x, n_lat_tiles, n_tiles, d)
    return pl.pallas_call(
        functools.partial(_even_in_kernel, n_lat_tiles=n_lat_tiles),
        grid=(n_tiles,),
        in_specs=x_specs + [
            pl.BlockSpec((1, d), lambda i: (0, 0)),
            pl.BlockSpec((None, 6, d), _stream_of(n_lat_tiles)),
            pl.BlockSpec((d, n), lambda i: (0, 0)),
            pl.BlockSpec(cs.shape, lambda i: (0, 0)),
        ],
        out_specs=[
            pl.BlockSpec((2, tm, FOURIER_WIDTH), lambda i: (0, i, 0)),
            pl.BlockSpec((tm, CONV_WIDTH), lambda i: (i, 0)),
            pl.BlockSpec((tm, CONV_WIDTH), lambda i: (i, 0)),
        ],
        out_shape=[
            jax.ShapeDtypeStruct((2, t, FOURIER_WIDTH), F32),
            jax.ShapeDtypeStruct((t, CONV_WIDTH), BF16),
            jax.ShapeDtypeStruct((t, CONV_WIDTH), BF16),
        ],
        compiler_params=_cparams(("arbitrary",)),
        name="even_in_proj",
    )(*xs, g, mod, w_in, cs)


def _dft_major_kernel(m_ref, z_ref, o_ref):
    shape = z_ref.shape
    z = z_ref[...].reshape(shape[0] * shape[1] * shape[2], shape[3]).astype(BF16)
    o_ref[...] = jnp.dot(m_ref[...], z, preferred_element_type=F32).reshape(shape)


def _dft_major(m1, z4, n1):
    c = z4.shape[-1]
    cb = DFT_MAJOR_COLS
    return pl.pallas_call(
        _dft_major_kernel,
        grid=(DFT_MINOR // cb,),
        in_specs=[pl.BlockSpec(m1.shape, lambda j: (0, 0)), pl.BlockSpec((2, n1, cb, c), lambda j: (0, 0, j, 0))],
        out_specs=pl.BlockSpec((2, n1, cb, c), lambda j: (0, 0, j, 0)),
        out_shape=jax.ShapeDtypeStruct((2, n1, DFT_MINOR, c), F32),
        compiler_params=_cparams(("arbitrary",)),
        name="dft_major",
    )(m1, z4)


def _dft_minor_kernel(a_ref, g_ref, o_ref):
    for j in range(a_ref.shape[0]):
        gcat = jnp.concatenate([g_ref[0, j], g_ref[1, j]], axis=0).astype(BF16)
        o_ref[:, j, :] = jnp.dot(a_ref[j], gcat, preferred_element_type=F32)


def _dft_minor(a_tab, g4):
    n1, m, k2 = a_tab.shape
    c = g4.shape[-1]
    kb = 8
    return pl.pallas_call(
        _dft_minor_kernel,
        grid=(n1 // kb,),
        in_specs=[
            pl.BlockSpec((kb, m, k2), lambda i: (i, 0, 0)),
            pl.BlockSpec((2, kb, DFT_MINOR, c), lambda i: (0, i, 0, 0)),
        ],
        out_specs=pl.BlockSpec((DFT_MINOR, kb, c), lambda i: (0, i, 0)),
        out_shape=jax.ShapeDtypeStruct((DFT_MINOR, n1, c), F32),
        compiler_params=_cparams(("arbitrary",)),
        name="dft_minor",
    )(a_tab, g4)


def _dft_ctx_kernel(a_ref, z_ref, o_ref):
    o_ref[...] = jnp.dot(a_ref[...], z_ref[...].astype(BF16), preferred_element_type=F32)


def _dft_ctx(mc, zc):
    lc, c = mc.shape[0], zc.shape[1]
    return pl.pallas_call(
        _dft_ctx_kernel,
        grid=(1,),
        in_specs=[pl.BlockSpec(mc.shape, lambda i: (0, 0)), pl.BlockSpec(zc.shape, lambda i: (0, 0))],
        out_specs=pl.BlockSpec((lc, c), lambda i: (0, 0)),
        out_shape=jax.ShapeDtypeStruct((lc, c), F32),
        compiler_params=_cparams(("arbitrary",)),
        name="dft_context",
    )(mc, zc)


def _dft_tables(l, lc):
    gd = FOURIER_GROUP_DIM
    kk = np.arange(gd)
    ang = 2.0 * np.pi * ((kk[:, None] * kk[None, :]) % gd) / gd
    cs = np.concatenate([np.cos(ang), -np.sin(ang)], axis=1)
    n1 = l // DFT_MINOR
    k1 = np.arange(n1)
    ang1 = 2.0 * np.pi * ((k1[:, None] * k1[None, :]) % n1) / n1
    c1, s1 = np.cos(ang1), np.sin(ang1)
    m1 = np.kron(np.block([[c1, s1], [-s1, c1]]), np.eye(DFT_MAJOR_COLS))
    l2 = np.arange(DFT_MINOR)
    kfull = k1[:, None, None] + n1 * l2[None, :, None]
    ang2 = 2.0 * np.pi * ((kfull * l2[None, None, :]) % l) / l
    sc = 1.0 / math.sqrt(l * gd)
    a_tab = np.concatenate([np.cos(ang2), np.sin(ang2)], axis=2) * sc
    kc = np.arange(lc)
    angc = 2.0 * np.pi * ((kc[:, None] * kc[None, :]) % lc) / lc
    mc = np.concatenate([np.cos(angc), np.sin(angc)], axis=1) / math.sqrt(lc * gd)
    as_bf16 = lambda v: jnp.asarray(v, F32).astype(BF16)
    return as_bf16(cs), as_bf16(m1), as_bf16(a_tab), as_bf16(mc)


def _fourier_seq(z, l, m1, a_tab, mc):
    t, c = z.shape[1], z.shape[2]
    lc = t - l
    n1 = l // DFT_MINOR
    g4 = _dft_major(m1, z.reshape(2, t // DFT_MINOR, DFT_MINOR, c), n1)
    f_lat = _dft_minor(a_tab, g4).reshape(l, c)
    return f_lat, _dft_ctx(mc, z[:, l:].reshape(2 * lc, c))


ROUTER_EXPERT_ROW0 = 8


def _route_tile(x, g_ref, mod_ref, wr_ref, br_ref, tri_ref, h_ref, ri_ref, rg_ref, cnt_ref):
    h = _norm_mod(x, g_ref[...], mod_ref[3:4, :], mod_ref[4:5, :])
    h_ref[...] = h.astype(BF16)
    tm = h.shape[0]
    h_hi, h_lo = _split_bf16(h)
    wr = wr_ref[...]
    both = jnp.dot(h_hi, wr, preferred_element_type=F32)
    logits = (both[:, :LANES] + both[:, LANES:]
              + jnp.dot(h_lo, wr[:, :LANES], preferred_element_type=F32))
    logits = logits.T + br_ref[:, 0:1]
    gl = logits[0:N_GROUPS]
    gmax = jnp.max(gl, axis=0, keepdims=True)
    gi = lax.broadcasted_iota(I32, gl.shape, 0)
    g_idx = jnp.min(jnp.where(gl == gmax, gi, N_GROUPS), axis=0, keepdims=True)
    g_val = 1.0 / jnp.sum(jnp.exp(gl - gmax), axis=0, keepdims=True)
    e_in = logits[ROUTER_EXPERT_ROW0:ROUTER_EXPERT_ROW0 + EXPERTS_PER_GROUP]
    for grp in range(1, N_GROUPS):
        lo = ROUTER_EXPERT_ROW0 + grp * EXPERTS_PER_GROUP
        e_in = jnp.where(g_idx == grp, logits[lo:lo + EXPERTS_PER_GROUP], e_in)
    ei = lax.broadcasted_iota(I32, e_in.shape, 0)
    v1 = jnp.max(e_in, axis=0, keepdims=True)
    i1 = jnp.min(jnp.where(e_in == v1, ei, EXPERTS_PER_GROUP), axis=0, keepdims=True)
    rest = jnp.where(ei == i1, -jnp.inf, e_in)
    v2 = jnp.max(rest, axis=0, keepdims=True)
    i2 = jnp.min(jnp.where(rest == v2, ei, EXPERTS_PER_GROUP), axis=0, keepdims=True)
    w2 = jnp.exp(v2 - v1)
    gate1 = g_val / (1.0 + w2)
    gate2 = g_val * w2 / (1.0 + w2)
    e1 = g_idx * EXPERTS_PER_GROUP + i1
    e2 = g_idx * EXPERTS_PER_GROUP + i2
    xi = lax.broadcasted_iota(I32, (N_EXPERTS, tm), 0)
    oh1 = xi == e1
    oh2 = xi == e2
    oh = oh1.astype(F32) + oh2.astype(F32)
    before = jnp.dot(oh.astype(BF16), tri_ref[...], preferred_element_type=F32)
    rank1 = jnp.sum(jnp.where(oh1, before, 0.0), axis=0, keepdims=True)
    rank2 = jnp.sum(jnp.where(oh2, before, 0.0), axis=0, keepdims=True)
    cnt_ref[...] = jnp.broadcast_to(jnp.sum(oh, axis=1, keepdims=True), cnt_ref.shape)
    orow = lax.broadcasted_iota(I32, (8, tm), 0)
    ri_ref[...] = jnp.where(orow == 0, e1, jnp.where(orow == 1, e2, jnp.where(
        orow == 2, rank1.astype(I32), jnp.where(orow == 3, rank2.astype(I32), 0))))
    rg_ref[...] = jnp.where(orow == 0, gate1, jnp.where(orow == 1, gate2, 0.0))


def _route_specs(route, d, n_tiles):
    tm = ROW_TILE
    const = lambda a: pl.BlockSpec(a.shape, lambda i: (0,) * a.ndim)
    in_specs = [const(a) for a in route]
    out_specs = [
        pl.BlockSpec((tm, d), lambda i: (i, 0)),
        pl.BlockSpec((8, tm), lambda i: (0, i)),
        pl.BlockSpec((8, tm), lambda i: (0, i)),
        pl.BlockSpec((None, N_EXPERTS, LANES), lambda i: (i, 0, 0)),
    ]
    rows = n_tiles * tm
    out_shape = [
        jax.ShapeDtypeStruct((rows, d), BF16),
        jax.ShapeDtypeStruct((8, rows), I32),
        jax.ShapeDtypeStruct((8, rows), F32),
        jax.ShapeDtypeStruct((n_tiles, N_EXPERTS, LANES), F32),
    ]
    return in_specs, out_specs, out_shape


def _even_out_kernel(fl_ref, fc_ref, bg_ref, u_ref, up_ref, un_ref, cw_ref, w_ref, xl_ref, xc_ref, mod_ref,
                     gf_ref, wr_ref, br_ref, tri_ref, o_ref, h_ref, ri_ref, rg_ref, cnt_ref, *, n_lat_tiles, n_tiles):
    i = pl.program_id(0)
    tm = u_ref.shape[0]
    u = u_ref[...].astype(F32)
    row = lax.broadcasted_iota(I32, u.shape, 0)
    first = jnp.logical_or(i == 0, i == n_lat_tiles)
    last = jnp.logical_or(i == n_lat_tiles - 1, i == n_tiles - 1)
    hb = up_ref.shape[0]
    halo_p = up_ref[...].astype(F32)[hb - 1:hb, :] * jnp.where(first, 0.0, 1.0)
    halo_n = un_ref[...].astype(F32)[0:1, :] * jnp.where(last, 0.0, 1.0)
    u_prev = jnp.where(row == 0, halo_p, pltpu.roll(u, 1, axis=0))
    u_next = jnp.where(row == tm - 1, halo_n, pltpu.roll(u, tm - 1, axis=0))
    cw = cw_ref[...]
    y = bg_ref[...].astype(F32) * (cw[0:1, :] * u_prev + cw[1:2, :] * u + cw[2:3, :] * u_next)
    is_ctx = jnp.full(fl_ref.shape, i, I32) >= n_lat_tiles
    f = jnp.where(is_ctx, fc_ref[...], fl_ref[...])
    acc = jnp.dot(f.astype(BF16), w_ref[0:FOURIER_WIDTH, :], preferred_element_type=F32)
    acc += jnp.dot(y.astype(BF16), w_ref[FOURIER_WIDTH:, :], preferred_element_type=F32)
    x_new = _pick_rows(xl_ref, xc_ref, n_lat_tiles) + mod_ref[2:3, :] * acc
    o_ref[...] = x_new
    _route_tile(x_new, gf_ref, mod_ref, wr_ref, br_ref, tri_ref, h_ref, ri_ref, rg_ref, cnt_ref)


def _even_out(f_lat, f_ctx, bg, u, conv_w, w_out, x, mod, route, n_lat_tiles):
    t, d = u.shape[0], w_out.shape[1]
    tm = ROW_TILE
    hb = 16
    n_tiles = t // tm
    r = tm // hb
    r_in, r_out, r_shape = _route_specs(route, d, n_tiles)
    xs, x_specs = _row_sources(x, n_lat_tiles, n_tiles, d)
    return pl.pallas_call(
        functools.partial(_even_out_kernel, n_lat_tiles=n_lat_tiles, n_tiles=n_tiles),
        grid=(n_tiles,),
        in_specs=[
            pl.BlockSpec((tm, FOURIER_WIDTH), lambda i: (jnp.minimum(i, n_lat_tiles - 1), 0)),
            pl.BlockSpec((tm, FOURIER_WIDTH), lambda i: (jnp.clip(i - n_lat_tiles, 0, n_tiles - n_lat_tiles - 1), 0)),
            pl.BlockSpec((tm, CONV_WIDTH), lambda i: (i, 0)),
            pl.BlockSpec((tm, CONV_WIDTH), lambda i: (i, 0)),
            pl.BlockSpec((hb, CONV_WIDTH), lambda i: (jnp.maximum(i * r - 1, 0), 0)),
            pl.BlockSpec((hb, CONV_WIDTH), lambda i: (jnp.minimum((i + 1) * r, t // hb - 1), 0)),
            pl.BlockSpec(conv_w.shape, lambda i: (0, 0)),
            pl.BlockSpec(w_out.shape, lambda i: (0, 0)),
        ] + x_specs + [
            pl.BlockSpec((None, 6, d), _stream_of(n_lat_tiles)),
        ] + r_in,
        out_specs=[pl.BlockSpec((tm, d), lambda i: (i, 0))] + r_out,
        out_shape=[jax.ShapeDtypeStruct((t, d), F32)] + r_shape,
        compiler_params=_cparams(("arbitrary",)),
        name="even_out_proj",
    )(f_lat, f_ctx, bg, u, u, u, conv_w, w_out, *xs, mod, *route)


def _seg_rms_scale(v, seg, seg_t):
    ss = jnp.dot((v * v).astype(BF16), seg, preferred_element_type=F32)
    inv = lax.rsqrt(ss * (1.0 / HEAD_DIM) + EPS)
    inv2 = jnp.concatenate(_split_bf16(inv), axis=1)
    return jnp.dot(inv2, seg_t, preferred_element_type=F32)


def _rope_cols(v, cos, sa, sb, scale):
    cols = []
    for j in range(v.shape[1] // LANES):
        c = v[:, j * LANES:(j + 1) * LANES]
        r = c * cos + pltpu.roll(c, LANES - ROPE_PAIRS, axis=1) * sa + pltpu.roll(c, ROPE_PAIRS, axis=1) * sb
        cols.append(r * scale if scale != 1.0 else r)
    return jnp.concatenate(cols, axis=1)


def _odd_in_kernel(x_ref, g_ref, mod_ref, w_ref, qg_ref, kg_ref, segq_ref, segqt_ref, segk_ref, segkt_ref,
                   rope_ref, q_ref, k_ref, v_ref):
    h = _norm_mod(x_ref[...], g_ref[...], mod_ref[0:1, :], mod_ref[1:2, :])
    p = jnp.dot(h.astype(BF16), w_ref[...], preferred_element_type=F32)
    qd = N_HEADS * HEAD_DIM
    kd = 2 * N_KV_HEADS * HEAD_DIM
    cos, sa, sb = rope_ref[0], rope_ref[1], rope_ref[2]
    q = p[:, :qd]
    q = q * _seg_rms_scale(q, segq_ref[...], segqt_ref[...]) * qg_ref[...]
    q_ref[...] = _rope_cols(q, cos, sa, sb, HEAD_DIM ** -0.5 * LOG2E).astype(BF16)
    k = p[:, qd:qd + kd]
    k = k * _seg_rms_scale(k, segk_ref[...], segkt_ref[...]) * kg_ref[...]
    k_ref[...] = _rope_cols(k, cos, sa, sb, 1.0).astype(BF16)
    v_ref[...] = p[:, qd + kd:].astype(BF16)


def _odd_in(x, g, mod, w_qkv, qg, kg, segs, rope, n_lat_tiles, n_tiles):
    t, d = x.shape
    n = w_qkv.shape[1]
    tm = ROW_TILE
    rows = n_tiles * tm
    qd = N_HEADS * HEAD_DIM
    kd = 2 * N_KV_HEADS * HEAD_DIM
    segq, segqt, segk, segkt = segs
    const = lambda a: pl.BlockSpec(a.shape, lambda i: (0,) * a.ndim)
    return pl.pallas_call(
        _odd_in_kernel,
        grid=(n_tiles,),
        in_specs=[
            pl.BlockSpec((tm, d), lambda i: (i, 0)),
            const(g),
            pl.BlockSpec((None, 6, d), _stream_of(n_lat_tiles)),
            const(w_qkv), const(qg), const(kg), const(segq), const(segqt), const(segk), const(segkt),
            pl.BlockSpec((3, tm, LANES), lambda i: (0, i, 0)),
        ],
        out_specs=[
            pl.BlockSpec((tm, qd), lambda i: (i, 0)),
            pl.BlockSpec((tm, kd), lambda i: (i, 0)),
            pl.BlockSpec((tm, kd), lambda i: (i, 0)),
        ],
        out_shape=[
            jax.ShapeDtypeStruct((rows, qd), BF16),
            jax.ShapeDtypeStruct((rows, kd), BF16),
            jax.ShapeDtypeStruct((rows, kd), BF16),
        ],
        compiler_params=_cparams(("arbitrary",)),
        name="odd_in_proj",
    )(x, g, mod, w_qkv, qg, kg, segq, segqt, segk, segkt, rope)


def _rope_tables(l, t):
    pos = np.arange(l)
    freqs = ROPE_BASE ** (-np.arange(ROPE_PAIRS, dtype=np.float32) / ROPE_PAIRS)
    lane = np.arange(LANES) % HEAD_DIM
    axis = lane // (2 * ROPE_PAIRS)
    half = (lane % (2 * ROPE_PAIRS)) // ROPE_PAIRS
    pair = lane % ROPE_PAIRS
    p = np.where(axis[None, :] == 0, (pos // GRID_W)[:, None], (pos % GRID_W)[:, None]).astype(np.float32)
    ang = p * freqs[pair][None, :].astype(np.float32)
    cos, sin = np.cos(ang), np.sin(ang)
    sa = np.where(half[None, :] == 0, -sin, 0.0)
    sb = np.where(half[None, :] == 1, sin, 0.0)
    tab = np.zeros((3, t, LANES), np.float32)
    tab[0, :l], tab[1, :l], tab[2, :l] = cos, sa, sb
    tab[0, l:] = 1.0
    return jnp.asarray(tab)


def _segment_matrices():
    def seg(width):
        m = np.zeros((width, LANES), np.float32)
        m[np.arange(width), np.arange(width) // HEAD_DIM] = 1.0
        return m
    sq, sk = seg(N_HEADS * HEAD_DIM), seg(2 * N_KV_HEADS * HEAD_DIM)
    b = lambda v: jnp.asarray(v).astype(BF16)
    twice = lambda m: np.concatenate([m, m], axis=0)
    return b(sq), b(twice(sq.T)), b(sk), b(twice(sk.T))


def _attn_block(sink_ref, q_ref, r0, k_refs, v_refs, kx_ref, vx_ref, bias, o_ref):
    bq = ATT_BLOCK
    kp_ref, kc_ref, kn_ref = k_refs
    vp_ref, vc_ref, vn_ref = v_refs
    rows = slice(r0, r0 + bq)
    low = lax.broadcasted_iota(I32, (bq, LANES), 1) < HEAD_DIM
    top = lax.broadcasted_iota(I32, (LANES, bq), 0) < HEAD_DIM
    nt = (((1,), (1,)), ((), ()))
    tn = (((0,), (0,)), ((), ()))
    st = []
    for g in range(N_KV_HEADS):
        ks = slice(g * LANES, (g + 1) * LANES)
        parts, sinks = [], []
        for j in range(GQA_GROUP):
            col = 2 * g + j // 2
            c = q_ref[rows, col * LANES:(col + 1) * LANES]
            keep = low if j % 2 == 0 else jnp.logical_not(low)
            parts.append(jnp.where(keep, c, jnp.zeros_like(c)))
            sinks.append(jnp.full((1, bq), sink_ref[g * GQA_GROUP + j], F32))
        qs = jnp.concatenate(parts, axis=0)
        sink = jnp.concatenate(sinks, axis=1)
        kwin = jnp.concatenate([kp_ref[:, ks], kc_ref[:, ks], kn_ref[:, ks]], axis=0)
        s_loc = lax.dot_general(kwin, qs, nt, preferred_element_type=F32) + bias
        s_ctx = lax.dot_general(kx_ref[:, ks], qs, nt, preferred_element_type=F32)
        st.append((s_loc, s_ctx, sink))
    pr = []
    for g in range(N_KV_HEADS):
        s_loc, s_ctx, sink = st[g]
        m = jnp.maximum(jnp.maximum(jnp.max(s_loc, axis=0, keepdims=True),
                                    jnp.max(s_ctx, axis=0, keepdims=True)), sink)
        p_loc = jnp.exp2(s_loc - m)
        p_ctx = jnp.exp2(s_ctx - m)
        den = (jnp.sum(p_loc, axis=0, keepdims=True) + jnp.sum(p_ctx, axis=0, keepdims=True)
               + jnp.exp2(sink - m))
        pr.append((p_loc.astype(BF16), p_ctx.astype(BF16), 1.0 / den))
    for g in range(N_KV_HEADS):
        ks = slice(g * LANES, (g + 1) * LANES)
        p_loc, p_ctx, inv = pr[g]
        vwin = jnp.concatenate([vp_ref[:, ks], vc_ref[:, ks], vn_ref[:, ks]], axis=0)
        ot = lax.dot_general(vwin, p_loc, tn, preferred_element_type=F32)
        ot += lax.dot_general(vx_ref[:, ks], p_ctx, tn, preferred_element_type=F32)
        ot = ot * inv
        t0 = jnp.where(top, ot[:, 0:bq], ot[:, bq:2 * bq])
        t1 = jnp.where(top, ot[:, 2 * bq:3 * bq], ot[:, 3 * bq:4 * bq])
        o_ref[rows, 2 * g * LANES:(2 * g + 1) * LANES] = t0.T.astype(o_ref.dtype)
        o_ref[rows, (2 * g + 1) * LANES:(2 * g + 2) * LANES] = t1.T.astype(o_ref.dtype)


def _attn_kernel(sink_ref, q_ref, k0, k1, k2, k3, v0, v1, v2, v3, kx_ref, vx_ref, bias_a_ref, bias_b_ref, o_ref):
    _attn_block(sink_ref, q_ref, 0, (k0, k1, k2), (v0, v1, v2), kx_ref, vx_ref, bias_a_ref[...], o_ref)
    _attn_block(sink_ref, q_ref, ATT_BLOCK, (k1, k2, k3), (v1, v2, v3), kx_ref, vx_ref, bias_b_ref[...], o_ref)


def _attn_bias(l):
    bq = ATT_BLOCK
    r = np.arange(GQA_GROUP * bq)[None, :] % bq
    col = np.arange(3 * bq)[:, None]
    band = np.abs(col - bq - r) <= WINDOW
    no_prev, no_next = col >= bq, col < 2 * bq
    masks = [band, band & no_prev, band & no_next, band & no_prev & no_next, np.zeros_like(band)]
    return jnp.asarray(np.where(np.stack(masks), 0.0, NEG_INF).astype(np.float32))


def _attention(q, k, v, sink, bias, l, n_q_blocks):
    t = q.shape[0]
    bq = ATT_BLOCK
    nlb = l // bq
    lc = t - l
    ctx_blk = l // lc
    kw = k.shape[1]
    assert n_q_blocks % 2 == 0 and nlb % 2 == 0
    key_block = lambda off: pl.BlockSpec((bq, kw), lambda i, s: (jnp.clip(2 * i + off, 0, nlb - 1), 0))
    keys = [key_block(off) for off in (-1, 0, 1, 2)]
    ctx = pl.BlockSpec((lc, kw), lambda i, s: (ctx_blk, 0))

    def kind(off):
        def index(i, s):
            b = 2 * i + off
            return (jnp.where(b >= nlb, 4, (b == 0).astype(I32) + 2 * (b == nlb - 1).astype(I32)), 0, 0)
        return pl.BlockSpec((None,) + bias.shape[1:], index)
    return pl.pallas_call(
        _attn_kernel,
        grid_spec=pltpu.PrefetchScalarGridSpec(
            num_scalar_prefetch=1,
            grid=(n_q_blocks // 2,),
            in_specs=[pl.BlockSpec((2 * bq, q.shape[1]), lambda i, s: (i, 0))] + keys + keys
                     + [ctx, ctx, kind(0), kind(1)],
            out_specs=pl.BlockSpec((2 * bq, q.shape[1]), lambda i, s: (i, 0)),
        ),
        out_shape=jax.ShapeDtypeStruct((n_q_blocks * bq, q.shape[1]), BF16),
        compiler_params=_cparams(("arbitrary",)),
        name="window_attention",
    )(sink, q, k, k, k, k, v, v, v, v, k, v, bias, bias)


def _odd_out_kernel(a_ref, w_ref, x_ref, mod_ref, gf_ref, wr_ref, br_ref, tri_ref, o_ref, h_ref, ri_ref, rg_ref,
                    cnt_ref):
    acc = jnp.dot(a_ref[...], w_ref[...], preferred_element_type=F32)
    x_new = x_ref[...] + mod_ref[2:3, :] * acc
    o_ref[...] = x_new
    _route_tile(x_new, gf_ref, mod_ref, wr_ref, br_ref, tri_ref, h_ref, ri_ref, rg_ref, cnt_ref)


def _odd_out(a, w_o, x, mod, route, n_lat_tiles, n_tiles):
    d = x.shape[1]
    tm = ROW_TILE
    r_in, r_out, r_shape = _route_specs(route, d, n_tiles)
    return pl.pallas_call(
        _odd_out_kernel,
        grid=(n_tiles,),
        in_specs=[
            pl.BlockSpec((tm, a.shape[1]), lambda i: (i, 0)),
            pl.BlockSpec(w_o.shape, lambda i: (0, 0)),
            pl.BlockSpec((tm, d), lambda i: (i, 0)),
            pl.BlockSpec((None, 6, d), _stream_of(n_lat_tiles)),
        ] + r_in,
        out_specs=[pl.BlockSpec((tm, d), lambda i: (i, 0))] + r_out,
        out_shape=[jax.ShapeDtypeStruct((n_tiles * tm, d), F32)] + r_shape,
        compiler_params=_cparams(("arbitrary",)),
        name="odd_out_proj",
    )(a, w_o, x, mod, *route)


CHUNK = 8
TILE_BUF = 2 * ROW_TILE + N_EXPERTS * CHUNK
MAX_TILE_CHUNKS = TILE_BUF // CHUNK
FFN_BUFS = 4


def _chunk_rows(c):
    return pl.ds(pl.multiple_of(c * CHUNK, CHUNK), CHUNK)


def _wait_rows(copy_of_rows, n_chunks):
    bit = 1
    while bit <= MAX_TILE_CHUNKS:
        @pl.when((n_chunks & bit) != 0)
        def _(bit=bit):
            copy_of_rows(bit * CHUNK).wait()
        bit *= 2


def _dispatch_kernel(tab_ref, lused_ref, fill_ref, h_ref, pos_ref, xb_ref, hs, zbuf, sem, fsem, *, n_tiles, n_blocks):
    i = pl.program_id(0)
    slot = i % 2
    tm = h_ref.shape[0]

    def tail_copy(c):
        return pltpu.make_async_copy(zbuf.at[pl.ds(0, CHUNK), :], xb_ref.at[_chunk_rows(c), :], fsem)

    def block_copy(b):
        return pltpu.make_async_copy(zbuf, xb_ref.at[pl.ds(pl.multiple_of(b * MOE_ROWS, MOE_ROWS), MOE_ROWS), :], fsem)

    def fill(start):
        def tail(e, c):
            st, n = fill_ref[e], fill_ref[N_EXPERTS + e]

            def one(c2, cc):
                cp = tail_copy(st + c2)
                cp.start() if start else cp.wait()
                return cc
            return lax.fori_loop(0, n, one, c)
        lax.fori_loop(0, N_EXPERTS, tail, 0)

        def blk(b, c):
            cp = block_copy(b)
            cp.start() if start else cp.wait()
            return c
        lax.fori_loop(fill_ref[2 * N_EXPERTS], n_blocks, blk, 0)

    @pl.when(i == 0)
    def _():
        zbuf[...] = jnp.zeros_like(zbuf)
        fill(True)

    pos = pos_ref[...]
    r = lax.broadcasted_iota(I32, (TILE_BUF, tm), 0)
    onehot = jnp.where(jnp.logical_or(r == pos[0:1, :], r == pos[1:2, :]), 1.0, 0.0).astype(BF16)
    hs[slot] = jnp.dot(onehot, h_ref[...], preferred_element_type=F32).astype(BF16)

    def chunk_copy(sl, src, dst):
        return pltpu.make_async_copy(hs.at[sl, _chunk_rows(src), :], xb_ref.at[_chunk_rows(dst), :], sem.at[sl])

    base = i * MAX_TILE_CHUNKS

    def per_chunk(c, cc):
        chunk_copy(slot, c, tab_ref[base + c]).start()
        return cc
    lax.fori_loop(0, lused_ref[i], per_chunk, 0)

    def wait_chunks(sl, n):
        _wait_rows(lambda rows: pltpu.make_async_copy(hs.at[sl, pl.ds(0, rows), :], xb_ref.at[pl.ds(0, rows), :],
                                                      sem.at[sl]), n)

    @pl.when(i > 0)
    def _():
        wait_chunks(1 - slot, lused_ref[jnp.maximum(i - 1, 0)])

    @pl.when(i == n_tiles - 1)
    def _():
        wait_chunks(slot, lused_ref[i])
        fill(False)


def _dispatch(h, pos_rows, tab, lused, fill, n_blocks):
    t, d = h.shape
    tm = ROW_TILE
    n_tiles = t // tm
    return pl.pallas_call(
        functools.partial(_dispatch_kernel, n_tiles=n_tiles, n_blocks=n_blocks),
        grid_spec=pltpu.PrefetchScalarGridSpec(
            num_scalar_prefetch=3,
            grid=(n_tiles,),
            in_specs=[
                pl.BlockSpec((tm, d), lambda i, *_: (i, 0)),
                pl.BlockSpec((8, tm), lambda i, *_: (0, i)),
            ],
            out_specs=pl.BlockSpec(memory_space=pl.ANY),
            scratch_shapes=[
                pltpu.VMEM((2, TILE_BUF, d), BF16),
                pltpu.VMEM((MOE_ROWS, d), BF16),
                pltpu.SemaphoreType.DMA((2,)),
                pltpu.SemaphoreType.DMA(()),
            ],
        ),
        out_shape=jax.ShapeDtypeStruct((n_blocks * MOE_ROWS, d), BF16),
        compiler_params=_cparams(("arbitrary",)),
        name="moe_dispatch",
    )(tab, lused, fill, h, pos_rows)


def _ffn_kernel(b0_ref, nb_ref, nv_ref, fill_ref, xb_ref, w1_ref, w3_ref, w2_ref, yb_ref,
                w1s, w3s, w2s, xbuf, ybuf, zbuf, sem_in, sem_out, fsem, *, n_blocks):
    e = pl.program_id(0)
    w1s[...] = w1_ref[...].astype(BF16)
    w3s[...] = w3_ref[...].astype(BF16)
    w2s[...] = w2_ref[...].astype(BF16)
    b0, nb = b0_ref[e], nb_ref[e]
    used = fill_ref[0]

    def rows(b):
        return pl.ds(pl.multiple_of(b * MOE_ROWS, MOE_ROWS), MOE_ROWS)

    def x_copy(g, sl):
        return pltpu.make_async_copy(xb_ref.at[rows(g), :], xbuf.at[sl], sem_in.at[sl])

    def y_copy(g, sl):
        return pltpu.make_async_copy(ybuf.at[sl], yb_ref.at[rows(g), :], sem_out.at[sl])

    @pl.when(e == 0)
    def _():
        for k in range(FFN_BUFS - 1):
            @pl.when(k < used)
            def _(k=k):
                x_copy(k, k).start(priority=1)

    def block(j, c):
        g = b0 + j
        sl = g % FFN_BUFS
        x_copy(g, sl).wait()
        ahead = g + (FFN_BUFS - 1)

        @pl.when(ahead < used)
        def _():
            x_copy(ahead, ahead % FFN_BUFS).start(priority=1)

        @pl.when(g >= FFN_BUFS)
        def _():
            y_copy(g - FFN_BUFS, sl).wait()

        x = xbuf[sl]
        row = lax.broadcasted_iota(I32, x.shape, 0)
        xb = jnp.where(row < nv_ref[g], x, jnp.zeros_like(x))
        a = jnp.dot(xb, w1s[...], preferred_element_type=F32)
        b = jnp.dot(xb, w3s[...], preferred_element_type=F32)
        hid = (a * jax.nn.sigmoid(a) * b).astype(BF16)
        ybuf[sl] = jnp.dot(hid, w2s[...], preferred_element_type=F32).astype(BF16)
        y_copy(g, sl).start(priority=1)
        return c
    lax.fori_loop(0, nb, block, 0)

    @pl.when(e == N_EXPERTS - 1)
    def _():
        for k in range(FFN_BUFS):
            @pl.when(used > k)
            def _(k=k):
                y_copy(used - 1 - k, (used - 1 - k) % FFN_BUFS).wait()
        zbuf[...] = jnp.zeros_like(zbuf)

        def z_copy(b):
            return pltpu.make_async_copy(zbuf, yb_ref.at[rows(b), :], fsem)

        def start(b, c):
            z_copy(b).start()
            return c

        def wait(b, c):
            z_copy(b).wait()
            return c
        lax.fori_loop(fill_ref[0], n_blocks, start, 0)
        lax.fori_loop(fill_ref[0], n_blocks, wait, 0)


def _expert_ffn(xb, blk_start, blk_count, n_valid, used_blocks, w1, w3, w2, layer):
    r = xb.shape[0]
    d, f = w1.shape[2], w1.shape[3]
    bm = MOE_ROWS
    return pl.pallas_call(
        functools.partial(_ffn_kernel, n_blocks=r // bm),
        grid_spec=pltpu.PrefetchScalarGridSpec(
            num_scalar_prefetch=4,
            grid=(N_EXPERTS,),
            in_specs=[
                pl.BlockSpec(memory_space=pl.ANY),
                pl.BlockSpec((None, None, d, f), lambda e, *_: (layer, e, 0, 0)),
                pl.BlockSpec((None, None, d, f), lambda e, *_: (layer, e, 0, 0)),
                pl.BlockSpec((None, None, f, d), lambda e, *_: (layer, e, 0, 0)),
            ],
            out_specs=pl.BlockSpec(memory_space=pl.ANY),
            scratch_shapes=[
                pltpu.VMEM((d, f), BF16), pltpu.VMEM((d, f), BF16), pltpu.VMEM((f, d), BF16),
                pltpu.VMEM((FFN_BUFS, bm, d), BF16), pltpu.VMEM((FFN_BUFS, bm, d), BF16), pltpu.VMEM((bm, d), BF16),
                pltpu.SemaphoreType.DMA((FFN_BUFS,)), pltpu.SemaphoreType.DMA((FFN_BUFS,)), pltpu.SemaphoreType.DMA(()),
            ],
        ),
        out_shape=jax.ShapeDtypeStruct((r, d), BF16),
        compiler_params=_cparams(("arbitrary",)),
        name="moe_expert_mlp",
    )(blk_start, blk_count, n_valid, used_blocks, xb, w1, w3, w2)


def _combine_kernel(tab_ref, lused_ref, yb_ref, pos_ref, gate_ref, x_ref, mod_ref, o_ref, ys, sem, *, n_tiles):
    i = pl.program_id(0)
    slot = i % 2
    tm = x_ref.shape[0]

    def chunk_copy(sl, src, dst):
        return pltpu.make_async_copy(yb_ref.at[_chunk_rows(src), :], ys.at[sl, _chunk_rows(dst), :], sem.at[sl])

    def fetch(tile, sl):
        base = tile * MAX_TILE_CHUNKS

        def per_chunk(c, cc):
            chunk_copy(sl, tab_ref[base + c], c).start()
            return cc
        lax.fori_loop(0, lused_ref[tile], per_chunk, 0)

    @pl.when(i == 0)
    def _():
        fetch(0, 0)

    @pl.when(i + 1 < n_tiles)
    def _():
        fetch(jnp.minimum(i + 1, n_tiles - 1), 1 - slot)

    _wait_rows(lambda rows: pltpu.make_async_copy(yb_ref.at[pl.ds(0, rows), :], ys.at[slot, pl.ds(0, rows), :],
                                                  sem.at[slot]), lused_ref[i])

    y = ys[slot]
    used = lused_ref[i] * CHUNK
    rowi = lax.broadcasted_iota(I32, y.shape, 0)
    y16 = jnp.where(rowi < used, y, jnp.zeros_like(y))
    pos = pos_ref[...]
    gate = gate_ref[...]
    lane = lax.broadcasted_iota(I32, (tm, TILE_BUF), 1)
    gm = (jnp.where(lane == pos[:, 0:1], gate[:, 0:1], 0.0) + jnp.where(lane == pos[:, 1:2], gate[:, 1:2], 0.0))
    ghi, glo = _split_bf16(gm)
    mix = jnp.dot(ghi, y16, preferred_element_type=F32) + jnp.dot(glo, y16, preferred_element_type=F32)
    o_ref[...] = x_ref[...] + mod_ref[5:6, :] * mix


def _combine(yb, pos_cols, gates, tab, lused, x, mod, n_lat_tiles, n_tiles):
    d = x.shape[1]
    tm = ROW_TILE
    return pl.pallas_call(
        functools.partial(_combine_kernel, n_tiles=n_tiles),
        grid_spec=pltpu.PrefetchScalarGridSpec(
            num_scalar_prefetch=2,
            grid=(n_tiles,),
            in_specs=[
                pl.BlockSpec(memory_space=pl.ANY),
                pl.BlockSpec((tm, 8), lambda i, *_: (i, 0)),
                pl.BlockSpec((tm, 8), lambda i, *_: (i, 0)),
                pl.BlockSpec((tm, d), lambda i, *_: (i, 0)),
                pl.BlockSpec((None, 6, d), lambda i, *_: (jnp.where(i >= n_lat_tiles, 1, 0), 0, 0)),
            ],
            out_specs=pl.BlockSpec((tm, d), lambda i, *_: (i, 0)),
            scratch_shapes=[pltpu.VMEM((2, TILE_BUF, d), BF16), pltpu.SemaphoreType.DMA((2,))],
        ),
        out_shape=jax.ShapeDtypeStruct((n_tiles * tm, d), F32),
        compiler_params=_cparams(("arbitrary",)),
        name="moe_combine",
    )(tab, lused, yb, pos_cols, gates, x, mod)


def _moe_layer(x, routing, mod, w1, w3, w2, layer, n_lat_tiles, n_tiles):
    tm = ROW_TILE
    rows = n_tiles * tm
    cpb = MOE_ROWS // CHUNK
    h, ri, rg, cnt3 = routing
    cnt = cnt3[:, :, 0].astype(I32)
    nch = (cnt + CHUNK - 1) // CHUNK
    lbase = jnp.cumsum(nch, axis=1) - nch
    lused = jnp.sum(nch, axis=1).astype(I32)
    tot = jnp.sum(nch, axis=0)
    reg = (tot + cpb - 1) // cpb * cpb
    gend = jnp.cumsum(reg)
    gstart = gend - reg
    gpos = gstart[None, :] + jnp.cumsum(nch, axis=0) - nch
    rows_max = 2 * rows + n_tiles * N_EXPERTS * (CHUNK - 1) + N_EXPERTS * (MOE_ROWS - CHUNK)
    n_blocks = -(-rows_max // MOE_ROWS)
    ex = jnp.arange(N_EXPERTS, dtype=I32)
    blk0 = jnp.arange(n_blocks, dtype=I32) * cpb
    block_exp = jnp.minimum(jnp.sum((gend[None, :] <= blk0[:, None]).astype(I32), axis=1), N_EXPERTS - 1)
    sel = block_exp[:, None] == ex[None, :]
    tot_b = jnp.sum(jnp.where(sel, tot[None, :], 0), axis=1)
    st_b = jnp.sum(jnp.where(sel, gstart[None, :], 0), axis=1)
    n_valid = jnp.clip((tot_b - (blk0 - st_b)) * CHUNK, 0, MOE_ROWS).astype(I32)
    lb_tok = jnp.repeat(lbase, tm, axis=0)
    at = lambda e: jnp.sum(jnp.where(e[:, None] == ex[None, :], lb_tok, 0), axis=1)
    pos1 = CHUNK * at(ri[0]) + ri[2]
    pos2 = CHUNK * at(ri[1]) + ri[3]
    zero = jnp.zeros_like(pos1)
    pos_rows = jnp.stack([pos1, pos2] + [zero] * 6, axis=0).astype(I32)
    slot_id = jnp.arange(MAX_TILE_CHUNKS, dtype=I32)
    owner = jnp.sum((lbase + nch)[:, None, :] <= slot_id[None, :, None], axis=2)
    own = jnp.minimum(owner, N_EXPERTS - 1)[:, :, None] == ex[None, None, :]
    tab = jnp.sum(jnp.where(own, (gpos - lbase)[:, None, :], 0), axis=2) + slot_id[None, :]
    tab = tab.astype(I32).reshape(-1)
    fill = jnp.concatenate([gstart + tot, reg - tot, gend[-1:] // cpb]).astype(I32)
    xb = _dispatch(h, pos_rows, tab, lused, fill, n_blocks)
    yb = _expert_ffn(xb, (gstart // cpb).astype(I32), (reg // cpb).astype(I32), n_valid,
                     (gend[-1:] // cpb).astype(I32), w1, w3, w2, layer)
    return _combine(yb, pos_rows.T, rg.T, tab, lused, x, mod, n_lat_tiles, n_tiles)


def _router_matrix(w_rg, b_rg, w_re, b_re):
    d = w_rg.shape[0]
    wr = jnp.zeros((d, LANES), F32)
    wr = wr.at[:, 0:N_GROUPS].set(w_rg.astype(F32))
    wr = wr.at[:, ROUTER_EXPERT_ROW0:ROUTER_EXPERT_ROW0 + N_EXPERTS].set(w_re.astype(F32))
    wr = jnp.concatenate(_split_bf16(wr), axis=1)
    br = jnp.zeros((LANES,), F32)
    br = br.at[0:N_GROUPS].set(b_rg.astype(F32))
    br = br.at[ROUTER_EXPERT_ROW0:ROUTER_EXPERT_ROW0 + N_EXPERTS].set(b_re.astype(F32))
    return wr, jnp.broadcast_to(br[:, None], (LANES, LANES))


def _dup_heads(w):
    d = w.shape[0]
    w4 = w.reshape(d, N_KV_HEADS, 1, HEAD_DIM)
    return jnp.broadcast_to(w4, (d, N_KV_HEADS, 2, HEAD_DIM)).reshape(d, 2 * N_KV_HEADS * HEAD_DIM)


def kernel(x, c, ctx, c_ctx, w_mod, b_mod, norm_mix_g, norm_ffn_g, w_in_even, conv_w, w_out_even, w_qkv, q_norm_g,
           k_norm_g, sink_logit, w_o, w_router_g, b_router_g, w_router_e, b_router_e, w1, w3, w2):
    bsz, l, d = x.shape
    lc = ctx.shape[1]
    assert bsz == 1, "one sample per call"
    tm = ROW_TILE
    assert l % tm == 0 and lc % tm == 0 and l % lc == 0 and l % (DFT_MINOR * 8) == 0
    depth = w_mod.shape[0]
    t = l + lc
    assert t % DFT_MINOR == 0
    nl, nt = l // tm, t // tm

    xs = (x.reshape(l, d), ctx.reshape(lc, d))
    mod_all = _modulation(c, c_ctx, w_mod, b_mod).reshape(depth, 2, 6, d)
    cs, m1, a_tab, mc = _dft_tables(l, lc)
    rope = _rope_tables(l, t)
    segs = _segment_matrices()
    attn_bias = _attn_bias(l)
    tri = jnp.asarray(np.triu(np.ones((tm, tm), np.float32), 1)).astype(BF16)
    qd = N_HEADS * HEAD_DIM

    for layer in range(depth):
        last = layer == depth - 1
        j = layer // 2
        mod = mod_all[layer]
        g_mix = norm_mix_g[layer].reshape(1, d)
        g_ffn = norm_ffn_g[layer].reshape(1, d)
        wr, br = _router_matrix(w_router_g[layer], b_router_g[layer], w_router_e[layer], b_router_e[layer])
        route = (g_ffn, wr, br, tri)
        if layer % 2 == 0:
            z, bg, u = _even_in(xs, g_mix, mod, w_in_even[j].astype(BF16), cs, nl, nt)
            f_lat, f_ctx = _fourier_seq(z, l, m1, a_tab, mc)
            xs, *routing = _even_out(f_lat, f_ctx, bg, u, conv_w[j], w_out_even[j].astype(BF16), xs, mod, route, nl)
        else:
            wq = w_qkv[j]
            w_all = jnp.concatenate([wq[:, :qd], _dup_heads(wq[:, qd:qd + N_KV_HEADS * HEAD_DIM]),
                                     _dup_heads(wq[:, qd + N_KV_HEADS * HEAD_DIM:])], axis=1).astype(BF16)
            qg = jnp.tile(q_norm_g[j], N_HEADS).reshape(1, qd)
            kg = jnp.tile(k_norm_g[j], 2 * N_KV_HEADS).reshape(1, 2 * N_KV_HEADS * HEAD_DIM)
            q, k, v = _odd_in(xs, g_mix, mod, w_all, qg, kg, segs, rope, nl, nt)
            n_out = nl if last else nt
            att = _attention(q, k, v, sink_logit[j].astype(F32) * LOG2E, attn_bias, l,
                             n_out * (tm // ATT_BLOCK))
            xs, *routing = _odd_out(att, w_o[j].astype(BF16), xs, mod, route, nl, n_out)
        n_moe = nl if last else nt
        xs = _moe_layer(xs, routing, mod, w1, w3, w2, layer, nl, n_moe)
    return xs[:l].reshape(bsz, l, d)
```

```python
import functools
import math

import numpy as np
import jax
import jax.numpy as jnp
from jax import lax
from jax.experimental import pallas as pl
from jax.experimental.pallas import tpu as pltpu

F32 = jnp.float32
BF16 = jnp.bfloat16
I32 = jnp.int32

EPS = 1e-6
NEG_INF = -1e30

GRID_W = 64
FOURIER_GROUPS = 4
FOURIER_GROUP_DIM = 128
FOURIER_WIDTH = FOURIER_GROUPS * FOURIER_GROUP_DIM
CONV_WIDTH = 512
N_HEADS = 16
N_KV_HEADS = 4
GQA_GROUP = N_HEADS // N_KV_HEADS
HEAD_DIM = 64
WINDOW = 128
ROPE_BASE = 10000.0
ROPE_PAIRS = HEAD_DIM // 4
N_GROUPS = 4
EXPERTS_PER_GROUP = 8
N_EXPERTS = N_GROUPS * EXPERTS_PER_GROUP

LANES = 128
ROW_TILE = 256
ATT_BLOCK = 128
MOE_ROWS = 256
DFT_MINOR = 128
DFT_MAJOR_COLS = 8
VMEM_LIMIT = 48 * 1024 * 1024

LOG2E = math.log2(math.e)


def _cparams(sem):
    return pltpu.CompilerParams(dimension_semantics=sem, vmem_limit_bytes=VMEM_LIMIT)


def _split_bf16(x):
    hi = x.astype(BF16)
    lo = (x - hi.astype(F32)).astype(BF16)
    return hi, lo


def _mod_kernel(ct_ref, w_ref, b_ref, o_ref):
    ct = ct_ref[...]
    s = ct * jax.nn.sigmoid(ct)
    w = w_ref[...]
    r0 = jnp.sum(w * s[:, 0:1], axis=0, keepdims=True)
    r1 = jnp.sum(w * s[:, 1:2], axis=0, keepdims=True)
    o_ref[...] = jnp.concatenate([r0, r1], axis=0) + b_ref[...]


def _modulation(c, c_ctx, w_mod, b_mod):
    depth, d, n = w_mod.shape
    tn = 512
    ct = jnp.stack([c.reshape(d), c_ctx.reshape(d)], axis=1)
    return pl.pallas_call(
        _mod_kernel,
        grid=(depth, n // tn),
        in_specs=[
            pl.BlockSpec((d, 2), lambda l, j: (0, 0)),
            pl.BlockSpec((None, d, tn), lambda l, j: (l, 0, j)),
            pl.BlockSpec((None, 1, tn), lambda l, j: (l, 0, j)),
        ],
        out_specs=pl.BlockSpec((None, 2, tn), lambda l, j: (l, 0, j)),
        out_shape=jax.ShapeDtypeStruct((depth, 2, n), F32),
        compiler_params=_cparams(("arbitrary", "arbitrary")),
        name="modulation",
    )(ct, w_mod, b_mod.reshape(depth, 1, n))


def _norm_mod(x, g, shift, scale):
    ms = jnp.mean(x * x, axis=-1, keepdims=True)
    y = x * lax.rsqrt(ms + EPS) * g
    return y * (1.0 + scale) + shift


def _stream_of(n_lat_tiles):
    return lambda i: (jnp.where(i >= n_lat_tiles, 1, 0), 0, 0)


def _row_sources(x, n_lat_tiles, n_tiles, d):
    tm = ROW_TILE
    lat = pl.BlockSpec((tm, d), lambda i: (jnp.minimum(i, n_lat_tiles - 1), 0))
    if isinstance(x, tuple):
        ctx = pl.BlockSpec((tm, d), lambda i: (jnp.clip(i - n_lat_tiles, 0, n_tiles - n_lat_tiles - 1), 0))
        return x, [lat, ctx]
    return (x, x), [lat, pl.BlockSpec((tm, d), lambda i: (jnp.clip(i, n_lat_tiles, n_tiles - 1), 0))]


def _pick_rows(xl_ref, xc_ref, n_lat_tiles):
    is_ctx = jnp.full(xl_ref.shape, pl.program_id(0), I32) >= n_lat_tiles
    return jnp.where(is_ctx, xc_ref[...], xl_ref[...])


def _even_in_kernel(xl_ref, xc_ref, g_ref, mod_ref, w_ref, cs_ref, z_ref, bg_ref, u_ref, *, n_lat_tiles):
    h = _norm_mod(_pick_rows(xl_ref, xc_ref, n_lat_tiles), g_ref[...], mod_ref[0:1, :], mod_ref[1:2, :])
    p = jnp.dot(h.astype(BF16), w_ref[...], preferred_element_type=F32)
    cs = cs_ref[...]
    for grp in range(FOURIER_GROUPS):
        lo = grp * FOURIER_GROUP_DIM
        a = p[:, lo:lo + FOURIER_GROUP_DIM].astype(BF16)
        z = jnp.dot(a, cs, preferred_element_type=F32)
        z_ref[0, :, lo:lo + FOURIER_GROUP_DIM] = z[:, :FOURIER_GROUP_DIM]
        z_ref[1, :, lo:lo + FOURIER_GROUP_DIM] = z[:, FOURIER_GROUP_DIM:]
    o = FOURIER_WIDTH
    bg_ref[...] = p[:, o:o + CONV_WIDTH].astype(BF16)
    u_ref[...] = (p[:, o + CONV_WIDTH:o + 2 * CONV_WIDTH] * p[:, o + 2 * CONV_WIDTH:]).astype(BF16)


def _even_in(x, g, mod, w_in, cs, n_lat_tiles, n_tiles):
    d, n = w_in.shape
    tm = ROW_TILE
    t = n_tiles * tm
    xs, x_specs = _row_sources(x, n_lat_tiles, n_tiles, d)
    return pl.pallas_call(
        functools.partial(_even_in_kernel, n_lat_tiles=n_lat_tiles),
        grid=(n_tiles,),
        in_specs=x_specs + [
            pl.BlockSpec((1, d), lambda i: (0, 0)),
            pl.BlockSpec((None, 6, d), _stream_of(n_lat_tiles)),
            pl.BlockSpec((d, n), lambda i: (0, 0)),
            pl.BlockSpec(cs.shape, lambda i: (0, 0)),
        ],
        out_specs=[
            pl.BlockSpec((2, tm, FOURIER_WIDTH), lambda i: (0, i, 0)),
            pl.BlockSpec((tm, CONV_WIDTH), lambda i: (i, 0)),
            pl.BlockSpec((tm, CONV_WIDTH), lambda i: (i, 0)),
        ],
        out_shape=[
            jax.ShapeDtypeStruct((2, t, FOURIER_WIDTH), F32),
            jax.ShapeDtypeStruct((t, CONV_WIDTH), BF16),
            jax.ShapeDtypeStruct((t, CONV_WIDTH), BF16),
        ],
        compiler_params=_cparams(("arbitrary",)),
        name="even_in_proj",
    )(*xs, g, mod, w_in, cs)


def _dft_major_kernel(m_ref, z_ref, o_ref):
    shape = z_ref.shape
    z = z_ref[...].reshape(shape[0] * shape[1] * shape[2], shape[3]).astype(BF16)
    o_ref[...] = jnp.dot(m_ref[...], z, preferred_element_type=F32).reshape(shape)


def _dft_major(m1, z4, n1):
    c = z4.shape[-1]
    cb = DFT_MAJOR_COLS
    return pl.pallas_call(
        _dft_major_kernel,
        grid=(DFT_MINOR // cb,),
        in_specs=[pl.BlockSpec(m1.shape, lambda j: (0, 0)), pl.BlockSpec((2, n1, cb, c), lambda j: (0, 0, j, 0))],
        out_specs=pl.BlockSpec((2, n1, cb, c), lambda j: (0, 0, j, 0)),
        out_shape=jax.ShapeDtypeStruct((2, n1, DFT_MINOR, c), F32),
        compiler_params=_cparams(("arbitrary",)),
        name="dft_major",
    )(m1, z4)


def _dft_minor_kernel(a_ref, g_ref, o_ref):
    for j in range(a_ref.shape[0]):
        gcat = jnp.concatenate([g_ref[0, j], g_ref[1, j]], axis=0).astype(BF16)
        o_ref[:, j, :] = jnp.dot(a_ref[j], gcat, preferred_element_type=F32)


def _dft_minor(a_tab, g4):
    n1, m, k2 = a_tab.shape
    c = g4.shape[-1]
    kb = 8
    return pl.pallas_call(
        _dft_minor_kernel,
        grid=(n1 // kb,),
        in_specs=[
            pl.BlockSpec((kb, m, k2), lambda i: (i, 0, 0)),
            pl.BlockSpec((2, kb, DFT_MINOR, c), lambda i: (0, i, 0, 0)),
        ],
        out_specs=pl.BlockSpec((DFT_MINOR, kb, c), lambda i: (0, i, 0)),
        out_shape=jax.ShapeDtypeStruct((DFT_MINOR, n1, c), F32),
        compiler_params=_cparams(("arbitrary",)),
        name="dft_minor",
    )(a_tab, g4)


def _dft_ctx_kernel(a_ref, z_ref, o_ref):
    o_ref[...] = jnp.dot(a_ref[...], z_ref[...].astype(BF16), preferred_element_type=F32)


def _dft_ctx(mc, zc):
    lc, c = mc.shape[0], zc.shape[1]
    return pl.pallas_call(
        _dft_ctx_kernel,
        grid=(1,),
        in_specs=[pl.BlockSpec(mc.shape, lambda i: (0, 0)), pl.BlockSpec(zc.shape, lambda i: (0, 0))],
        out_specs=pl.BlockSpec((lc, c), lambda i: (0, 0)),
        out_shape=jax.ShapeDtypeStruct((lc, c), F32),
        compiler_params=_cparams(("arbitrary",)),
        name="dft_context",
    )(mc, zc)


def _dft_tables(l, lc):
    gd = FOURIER_GROUP_DIM
    kk = np.arange(gd)
    ang = 2.0 * np.pi * ((kk[:, None] * kk[None, :]) % gd) / gd
    cs = np.concatenate([np.cos(ang), -np.sin(ang)], axis=1)
    n1 = l // DFT_MINOR
    k1 = np.arange(n1)
    ang1 = 2.0 * np.pi * ((k1[:, None] * k1[None, :]) % n1) / n1
    c1, s1 = np.cos(ang1), np.sin(ang1)
    m1 = np.kron(np.block([[c1, s1], [-s1, c1]]), np.eye(DFT_MAJOR_COLS))
    l2 = np.arange(DFT_MINOR)
    kfull = k1[:, None, None] + n1 * l2[None, :, None]
    ang2 = 2.0 * np.pi * ((kfull * l2[None, None, :]) % l) / l
    sc = 1.0 / math.sqrt(l * gd)
    a_tab = np.concatenate([np.cos(ang2), np.sin(ang2)], axis=2) * sc
    kc = np.arange(lc)
    angc = 2.0 * np.pi * ((kc[:, None] * kc[None, :]) % lc) / lc
    mc = np.concatenate([np.cos(angc), np.sin(angc)], axis=1) / math.sqrt(lc * gd)
    as_bf16 = lambda v: jnp.asarray(v, F32).astype(BF16)
    return as_bf16(cs), as_bf16(m1), as_bf16(a_tab), as_bf16(mc)


def _fourier_seq(z, l, m1, a_tab, mc):
    t, c = z.shape[1], z.shape[2]
    lc = t - l
    n1 = l // DFT_MINOR
    g4 = _dft_major(m1, z.reshape(2, t // DFT_MINOR, DFT_MINOR, c), n1)
    f_lat = _dft_minor(a_tab, g4).reshape(l, c)
    return f_lat, _dft_ctx(mc, z[:, l:].reshape(2 * lc, c))


ROUTER_EXPERT_ROW0 = 8


def _route_tile(x, g_ref, mod_ref, wr_ref, br_ref, tri_ref, h_ref, ri_ref, rg_ref, cnt_ref):
    h = _norm_mod(x, g_ref[...], mod_ref[3:4, :], mod_ref[4:5, :])
    h_ref[...] = h.astype(BF16)
    tm = h.shape[0]
    h_hi, h_lo = _split_bf16(h)
    wr = wr_ref[...]
    both = jnp.dot(h_hi, wr, preferred_element_type=F32)
    logits = (both[:, :LANES] + both[:, LANES:]
              + jnp.dot(h_lo, wr[:, :LANES], preferred_element_type=F32))
    logits = logits.T + br_ref[:, 0:1]
    gl = logits[0:N_GROUPS]
    gmax = jnp.max(gl, axis=0, keepdims=True)
    gi = lax.broadcasted_iota(I32, gl.shape, 0)
    g_idx = jnp.min(jnp.where(gl == gmax, gi, N_GROUPS), axis=0, keepdims=True)
    g_val = 1.0 / jnp.sum(jnp.exp(gl - gmax), axis=0, keepdims=True)
    e_in = logits[ROUTER_EXPERT_ROW0:ROUTER_EXPERT_ROW0 + EXPERTS_PER_GROUP]
    for grp in range(1, N_GROUPS):
        lo = ROUTER_EXPERT_ROW0 + grp * EXPERTS_PER_GROUP
        e_in = jnp.where(g_idx == grp, logits[lo:lo + EXPERTS_PER_GROUP], e_in)
    ei = lax.broadcasted_iota(I32, e_in.shape, 0)
    v1 = jnp.max(e_in, axis=0, keepdims=True)
    i1 = jnp.min(jnp.where(e_in == v1, ei, EXPERTS_PER_GROUP), axis=0, keepdims=True)
    rest = jnp.where(ei == i1, -jnp.inf, e_in)
    v2 = jnp.max(rest, axis=0, keepdims=True)
    i2 = jnp.min(jnp.where(rest == v2, ei, EXPERTS_PER_GROUP), axis=0, keepdims=True)
    w2 = jnp.exp(v2 - v1)
    gate1 = g_val / (1.0 + w2)
    gate2 = g_val * w2 / (1.0 + w2)
    e1 = g_idx * EXPERTS_PER_GROUP + i1
    e2 = g_idx * EXPERTS_PER_GROUP + i2
    xi = lax.broadcasted_iota(I32, (N_EXPERTS, tm), 0)
    oh1 = xi == e1
    oh2 = xi == e2
    oh = oh1.astype(F32) + oh2.astype(F32)
    before = jnp.dot(oh.astype(BF16), tri_ref[...], preferred_element_type=F32)
    rank1 = jnp.sum(jnp.where(oh1, before, 0.0), axis=0, keepdims=True)
    rank2 = jnp.sum(jnp.where(oh2, before, 0.0), axis=0, keepdims=True)
    cnt_ref[...] = jnp.broadcast_to(jnp.sum(oh, axis=1, keepdims=True), cnt_ref.shape)
    orow = lax.broadcasted_iota(I32, (8, tm), 0)
    ri_ref[...] = jnp.where(orow == 0, e1, jnp.where(orow == 1, e2, jnp.where(
        orow == 2, rank1.astype(I32), jnp.where(orow == 3, rank2.astype(I32), 0))))
    rg_ref[...] = jnp.where(orow == 0, gate1, jnp.where(orow == 1, gate2, 0.0))


def _route_specs(route, d, n_tiles):
    tm = ROW_TILE
    const = lambda a: pl.BlockSpec(a.shape, lambda i: (0,) * a.ndim)
    in_specs = [const(a) for a in route]
    out_specs = [
        pl.BlockSpec((tm, d), lambda i: (i, 0)),
        pl.BlockSpec((8, tm), lambda i: (0, i)),
        pl.BlockSpec((8, tm), lambda i: (0, i)),
        pl.BlockSpec((None, N_EXPERTS, LANES), lambda i: (i, 0, 0)),
    ]
    rows = n_tiles * tm
    out_shape = [
        jax.ShapeDtypeStruct((rows, d), BF16),
        jax.ShapeDtypeStruct((8, rows), I32),
        jax.ShapeDtypeStruct((8, rows), F32),
        jax.ShapeDtypeStruct((n_tiles, N_EXPERTS, LANES), F32),
    ]
    return in_specs, out_specs, out_shape


def _even_out_kernel(fl_ref, fc_ref, bg_ref, u_ref, up_ref, un_ref, cw_ref, w_ref, xl_ref, xc_ref, mod_ref,
                     gf_ref, wr_ref, br_ref, tri_ref, o_ref, h_ref, ri_ref, rg_ref, cnt_ref, *, n_lat_tiles, n_tiles):
    i = pl.program_id(0)
    tm = u_ref.shape[0]
    u = u_ref[...].astype(F32)
    row = lax.broadcasted_iota(I32, u.shape, 0)
    first = jnp.logical_or(i == 0, i == n_lat_tiles)
    last = jnp.logical_or(i == n_lat_tiles - 1, i == n_tiles - 1)
    hb = up_ref.shape[0]
    halo_p = up_ref[...].astype(F32)[hb - 1:hb, :] * jnp.where(first, 0.0, 1.0)
    halo_n = un_ref[...].astype(F32)[0:1, :] * jnp.where(last, 0.0, 1.0)
    u_prev = jnp.where(row == 0, halo_p, pltpu.roll(u, 1, axis=0))
    u_next = jnp.where(row == tm - 1, halo_n, pltpu.roll(u, tm - 1, axis=0))
    cw = cw_ref[...]
    y = bg_ref[...].astype(F32) * (cw[0:1, :] * u_prev + cw[1:2, :] * u + cw[2:3, :] * u_next)
    is_ctx = jnp.full(fl_ref.shape, i, I32) >= n_lat_tiles
    f = jnp.where(is_ctx, fc_ref[...], fl_ref[...])
    acc = jnp.dot(f.astype(BF16), w_ref[0:FOURIER_WIDTH, :], preferred_element_type=F32)
    acc += jnp.dot(y.astype(BF16), w_ref[FOURIER_WIDTH:, :], preferred_element_type=F32)
    x_new = _pick_rows(xl_ref, xc_ref, n_lat_tiles) + mod_ref[2:3, :] * acc
    o_ref[...] = x_new
    _route_tile(x_new, gf_ref, mod_ref, wr_ref, br_ref, tri_ref, h_ref, ri_ref, rg_ref, cnt_ref)


def _even_out(f_lat, f_ctx, bg, u, conv_w, w_out, x, mod, route, n_lat_tiles):
    t, d = u.shape[0], w_out.shape[1]
    tm = ROW_TILE
    hb = 16
    n_tiles = t // tm
    r = tm // hb
    r_in, r_out, r_shape = _route_specs(route, d, n_tiles)
    xs, x_specs = _row_sources(x, n_lat_tiles, n_tiles, d)
    return pl.pallas_call(
        functools.partial(_even_out_kernel, n_lat_tiles=n_lat_tiles, n_tiles=n_tiles),
        grid=(n_tiles,),
        in_specs=[
            pl.BlockSpec((tm, FOURIER_WIDTH), lambda i: (jnp.minimum(i, n_lat_tiles - 1), 0)),
            pl.BlockSpec((tm, FOURIER_WIDTH), lambda i: (jnp.clip(i - n_lat_tiles, 0, n_tiles - n_lat_tiles - 1), 0)),
            pl.BlockSpec((tm, CONV_WIDTH), lambda i: (i, 0)),
            pl.BlockSpec((tm, CONV_WIDTH), lambda i: (i, 0)),
            pl.BlockSpec((hb, CONV_WIDTH), lambda i: (jnp.maximum(i * r - 1, 0), 0)),
            pl.BlockSpec((hb, CONV_WIDTH), lambda i: (jnp.minimum((i + 1) * r, t // hb - 1), 0)),
            pl.BlockSpec(conv_w.shape, lambda i: (0, 0)),
            pl.BlockSpec(w_out.shape, lambda i: (0, 0)),
        ] + x_specs + [
            pl.BlockSpec((None, 6, d), _stream_of(n_lat_tiles)),
        ] + r_in,
        out_specs=[pl.BlockSpec((tm, d), lambda i: (i, 0))] + r_out,
        out_shape=[jax.ShapeDtypeStruct((t, d), F32)] + r_shape,
        compiler_params=_cparams(("arbitrary",)),
        name="even_out_proj",
    )(f_lat, f_ctx, bg, u, u, u, conv_w, w_out, *xs, mod, *route)


def _seg_rms_scale(v, seg, seg_t):
    ss = jnp.dot((v * v).astype(BF16), seg, preferred_element_type=F32)
    inv = lax.rsqrt(ss * (1.0 / HEAD_DIM) + EPS)
    inv2 = jnp.concatenate(_split_bf16(inv), axis=1)
    return jnp.dot(inv2, seg_t, preferred_element_type=F32)


def _rope_cols(v, cos, sa, sb, scale):
    cols = []
    for j in range(v.shape[1] // LANES):
        c = v[:, j * LANES:(j + 1) * LANES]
        r = c * cos + pltpu.roll(c, LANES - ROPE_PAIRS, axis=1) * sa + pltpu.roll(c, ROPE_PAIRS, axis=1) * sb
        cols.append(r * scale if scale != 1.0 else r)
    return jnp.concatenate(cols, axis=1)


def _odd_in_kernel(x_ref, g_ref, mod_ref, w_ref, qg_ref, kg_ref, segq_ref, segqt_ref, segk_ref, segkt_ref,
                   rope_ref, q_ref, k_ref, v_ref):
    h = _norm_mod(x_ref[...], g_ref[...], mod_ref[0:1, :], mod_ref[1:2, :])
    p = jnp.dot(h.astype(BF16), w_ref[...], preferred_element_type=F32)
    qd = N_HEADS * HEAD_DIM
    kd = 2 * N_KV_HEADS * HEAD_DIM
    cos, sa, sb = rope_ref[0], rope_ref[1], rope_ref[2]
    q = p[:, :qd]
    q = q * _seg_rms_scale(q, segq_ref[...], segqt_ref[...]) * qg_ref[...]
    q_ref[...] = _rope_cols(q, cos, sa, sb, HEAD_DIM ** -0.5 * LOG2E).astype(BF16)
    k = p[:, qd:qd + kd]
    k = k * _seg_rms_scale(k, segk_ref[...], segkt_ref[...]) * kg_ref[...]
    k_ref[...] = _rope_cols(k, cos, sa, sb, 1.0).astype(BF16)
    v_ref[...] = p[:, qd + kd:].astype(BF16)


def _odd_in(x, g, mod, w_qkv, qg, kg, segs, rope, n_lat_tiles, n_tiles):
    t, d = x.shape
    n = w_qkv.shape[1]
    tm = ROW_TILE
    rows = n_tiles * tm
    qd = N_HEADS * HEAD_DIM
    kd = 2 * N_KV_HEADS * HEAD_DIM
    segq, segqt, segk, segkt = segs
    const = lambda a: pl.BlockSpec(a.shape, lambda i: (0,) * a.ndim)
    return pl.pallas_call(
        _odd_in_kernel,
        grid=(n_tiles,),
        in_specs=[
            pl.BlockSpec((tm, d), lambda i: (i, 0)),
            const(g),
            pl.BlockSpec((None, 6, d), _stream_of(n_lat_tiles)),
            const(w_qkv), const(qg), const(kg), const(segq), const(segqt), const(segk), const(segkt),
            pl.BlockSpec((3, tm, LANES), lambda i: (0, i, 0)),
        ],
        out_specs=[
            pl.BlockSpec((tm, qd), lambda i: (i, 0)),
            pl.BlockSpec((tm, kd), lambda i: (i, 0)),
            pl.BlockSpec((tm, kd), lambda i: (i, 0)),
        ],
        out_shape=[
            jax.ShapeDtypeStruct((rows, qd), BF16),
            jax.ShapeDtypeStruct((rows, kd), BF16),
            jax.ShapeDtypeStruct((rows, kd), BF16),
        ],
        compiler_params=_cparams(("arbitrary",)),
        name="odd_in_proj",
    )(x, g, mod, w_qkv, qg, kg, segq, segqt, segk, segkt, rope)


def _rope_tables(l, t):
    pos = np.arange(l)
    freqs = ROPE_BASE ** (-np.arange(ROPE_PAIRS, dtype=np.float32) / ROPE_PAIRS)
    lane = np.arange(LANES) % HEAD_DIM
    axis = lane // (2 * ROPE_PAIRS)
    half = (lane % (2 * ROPE_PAIRS)) // ROPE_PAIRS
    pair = lane % ROPE_PAIRS
    p = np.where(axis[None, :] == 0, (pos // GRID_W)[:, None], (pos % GRID_W)[:, None]).astype(np.float32)
    ang = p * freqs[pair][None, :].astype(np.float32)
    cos, sin = np.cos(ang), np.sin(ang)
    sa = np.where(half[None, :] == 0, -sin, 0.0)
    sb = np.where(half[None, :] == 1, sin, 0.0)
    tab = np.zeros((3, t, LANES), np.float32)
    tab[0, :l], tab[1, :l], tab[2, :l] = cos, sa, sb
    tab[0, l:] = 1.0
    return jnp.asarray(tab)


def _segment_matrices():
    def seg(width):
        m = np.zeros((width, LANES), np.float32)
        m[np.arange(width), np.arange(width) // HEAD_DIM] = 1.0
        return m
    sq, sk = seg(N_HEADS * HEAD_DIM), seg(2 * N_KV_HEADS * HEAD_DIM)
    b = lambda v: jnp.asarray(v).astype(BF16)
    twice = lambda m: np.concatenate([m, m], axis=0)
    return b(sq), b(twice(sq.T)), b(sk), b(twice(sk.T))


def _attn_block(sink_ref, q_ref, r0, k_refs, v_refs, kx_ref, vx_ref, bias, o_ref):
    bq = ATT_BLOCK
    kp_ref, kc_ref, kn_ref = k_refs
    vp_ref, vc_ref, vn_ref = v_refs
    rows = slice(r0, r0 + bq)
    low = lax.broadcasted_iota(I32, (bq, LANES), 1) < HEAD_DIM
    top = lax.broadcasted_iota(I32, (LANES, bq), 0) < HEAD_DIM
    nt = (((1,), (1,)), ((), ()))
    tn = (((0,), (0,)), ((), ()))
    st = []
    for g in range(N_KV_HEADS):
        ks = slice(g * LANES, (g + 1) * LANES)
        parts, sinks = [], []
        for j in range(GQA_GROUP):
            col = 2 * g + j // 2
            c = q_ref[rows, col * LANES:(col + 1) * LANES]
            keep = low if j % 2 == 0 else jnp.logical_not(low)
            parts.append(jnp.where(keep, c, jnp.zeros_like(c)))
            sinks.append(jnp.full((1, bq), sink_ref[g * GQA_GROUP + j], F32))
        qs = jnp.concatenate(parts, axis=0)
        sink = jnp.concatenate(sinks, axis=1)
        kwin = jnp.concatenate([kp_ref[:, ks], kc_ref[:, ks], kn_ref[:, ks]], axis=0)
        s_loc = lax.dot_general(kwin, qs, nt, preferred_element_type=F32) + bias
        s_ctx = lax.dot_general(kx_ref[:, ks], qs, nt, preferred_element_type=F32)
        st.append((s_loc, s_ctx, sink))
    pr = []
    for g in range(N_KV_HEADS):
        s_loc, s_ctx, sink = st[g]
        m = jnp.maximum(jnp.maximum(jnp.max(s_loc, axis=0, keepdims=True),
                                    jnp.max(s_ctx, axis=0, keepdims=True)), sink)
        p_loc = jnp.exp2(s_loc - m)
        p_ctx = jnp.exp2(s_ctx - m)
        den = (jnp.sum(p_loc, axis=0, keepdims=True) + jnp.sum(p_ctx, axis=0, keepdims=True)
               + jnp.exp2(sink - m))
        pr.append((p_loc.astype(BF16), p_ctx.astype(BF16), 1.0 / den))
    for g in range(N_KV_HEADS):
        ks = slice(g * LANES, (g + 1) * LANES)
        p_loc, p_ctx, inv = pr[g]
        vwin = jnp.concatenate([vp_ref[:, ks], vc_ref[:, ks], vn_ref[:, ks]], axis=0)
        ot = lax.dot_general(vwin, p_loc, tn, preferred_element_type=F32)
        ot += lax.dot_general(vx_ref[:, ks], p_ctx, tn, preferred_element_type=F32)
        ot = ot * inv
        t0 = jnp.where(top, ot[:, 0:bq], ot[:, bq:2 * bq])
        t1 = jnp.where(top, ot[:, 2 * bq:3 * bq], ot[:, 3 * bq:4 * bq])
        o_ref[rows, 2 * g * LANES:(2 * g + 1) * LANES] = t0.T.astype(o_ref.dtype)
        o_ref[rows, (2 * g + 1) * LANES:(2 * g + 2) * LANES] = t1.T.astype(o_ref.dtype)


def _attn_kernel(sink_ref, q_ref, k0, k1, k2, k3, v0, v1, v2, v3, kx_ref, vx_ref, bias_a_ref, bias_b_ref, o_ref):
    _attn_block(sink_ref, q_ref, 0, (k0, k1, k2), (v0, v1, v2), kx_ref, vx_ref, bias_a_ref[...], o_ref)
    _attn_block(sink_ref, q_ref, ATT_BLOCK, (k1, k2, k3), (v1, v2, v3), kx_ref, vx_ref, bias_b_ref[...], o_ref)


def _attn_bias(l):
    bq = ATT_BLOCK
    r = np.arange(GQA_GROUP * bq)[None, :] % bq
    col = np.arange(3 * bq)[:, None]
    band = np.abs(col - bq - r) <= WINDOW
    no_prev, no_next = col >= bq, col < 2 * bq
    masks = [band, band & no_prev, band & no_next, band & no_prev & no_next, np.zeros_like(band)]
    return jnp.asarray(np.where(np.stack(masks), 0.0, NEG_INF).astype(np.float32))


def _attention(q, k, v, sink, bias, l, n_q_blocks):
    t = q.shape[0]
    bq = ATT_BLOCK
    nlb = l // bq
    lc = t - l
    ctx_blk = l // lc
    kw = k.shape[1]
    assert n_q_blocks % 2 == 0 and nlb % 2 == 0
    key_block = lambda off: pl.BlockSpec((bq, kw), lambda i, s: (jnp.clip(2 * i + off, 0, nlb - 1), 0))
    keys = [key_block(off) for off in (-1, 0, 1, 2)]
    ctx = pl.BlockSpec((lc, kw), lambda i, s: (ctx_blk, 0))

    def kind(off):
        def index(i, s):
            b = 2 * i + off
            return (jnp.where(b >= nlb, 4, (b == 0).astype(I32) + 2 * (b == nlb - 1).astype(I32)), 0, 0)
        return pl.BlockSpec((None,) + bias.shape[1:], index)
    return pl.pallas_call(
        _attn_kernel,
        grid_spec=pltpu.PrefetchScalarGridSpec(
            num_scalar_prefetch=1,
            grid=(n_q_blocks // 2,),
            in_specs=[pl.BlockSpec((2 * bq, q.shape[1]), lambda i, s: (i, 0))] + keys + keys
                     + [ctx, ctx, kind(0), kind(1)],
            out_specs=pl.BlockSpec((2 * bq, q.shape[1]), lambda i, s: (i, 0)),
        ),
        out_shape=jax.ShapeDtypeStruct((n_q_blocks * bq, q.shape[1]), BF16),
        compiler_params=_cparams(("arbitrary",)),
        name="window_attention",
    )(sink, q, k, k, k, k, v, v, v, v, k, v, bias, bias)


def _odd_out_kernel(a_ref, w_ref, x_ref, mod_ref, gf_ref, wr_ref, br_ref, tri_ref, o_ref, h_ref, ri_ref, rg_ref,
                    cnt_ref):
    acc = jnp.dot(a_ref[...], w_ref[...], preferred_element_type=F32)
    x_new = x_ref[...] + mod_ref[2:3, :] * acc
    o_ref[...] = x_new
    _route_tile(x_new, gf_ref, mod_ref, wr_ref, br_ref, tri_ref, h_ref, ri_ref, rg_ref, cnt_ref)


def _odd_out(a, w_o, x, mod, route, n_lat_tiles, n_tiles):
    d = x.shape[1]
    tm = ROW_TILE
    r_in, r_out, r_shape = _route_specs(route, d, n_tiles)
    return pl.pallas_call(
        _odd_out_kernel,
        grid=(n_tiles,),
        in_specs=[
            pl.BlockSpec((tm, a.shape[1]), lambda i: (i, 0)),
            pl.BlockSpec(w_o.shape, lambda i: (0, 0)),
            pl.BlockSpec((tm, d), lambda i: (i, 0)),
            pl.BlockSpec((None, 6, d), _stream_of(n_lat_tiles)),
        ] + r_in,
        out_specs=[pl.BlockSpec((tm, d), lambda i: (i, 0))] + r_out,
        out_shape=[jax.ShapeDtypeStruct((n_tiles * tm, d), F32)] + r_shape,
        compiler_params=_cparams(("arbitrary",)),
        name="odd_out_proj",
    )(a, w_o, x, mod, *route)


CHUNK = 8
TILE_BUF = 2 * ROW_TILE + N_EXPERTS * CHUNK
MAX_TILE_CHUNKS = TILE_BUF // CHUNK
FFN_BUFS = 4


def _chunk_rows(c):
    return pl.ds(pl.multiple_of(c * CHUNK, CHUNK), CHUNK)


def _wait_rows(copy_of_rows, n_chunks):
    bit = 1
    while bit <= MAX_TILE_CHUNKS:
        @pl.when((n_chunks & bit) != 0)
        def _(bit=bit):
            copy_of_rows(bit * CHUNK).wait()
        bit *= 2


def _dispatch_kernel(tab_ref, lused_ref, fill_ref, h_ref, pos_ref, gate_ref, xb_ref, hs, zbuf, sem, fsem, *, n_tiles,
                     n_blocks):
    i = pl.program_id(0)
    slot = i % 2
    tm = h_ref.shape[0]

    def tail_copy(c):
        return pltpu.make_async_copy(zbuf.at[pl.ds(0, CHUNK), :], xb_ref.at[_chunk_rows(c), :], fsem)

    def block_copy(b):
        return pltpu.make_async_copy(zbuf, xb_ref.at[pl.ds(pl.multiple_of(b * MOE_ROWS, MOE_ROWS), MOE_ROWS), :], fsem)

    def fill(start):
        def tail(e, c):
            st, n = fill_ref[e], fill_ref[N_EXPERTS + e]

            def one(c2, cc):
                cp = tail_copy(st + c2)
                cp.start() if start else cp.wait()
                return cc
            return lax.fori_loop(0, n, one, c)
        lax.fori_loop(0, N_EXPERTS, tail, 0)

        def blk(b, c):
            cp = block_copy(b)
            cp.start() if start else cp.wait()
            return c
        lax.fori_loop(fill_ref[2 * N_EXPERTS], n_blocks, blk, 0)

    @pl.when(i == 0)
    def _():
        zbuf[...] = jnp.zeros_like(zbuf)
        fill(True)

    pos = pos_ref[...]
    r = lax.broadcasted_iota(I32, (TILE_BUF, tm), 0)
    first = r == pos[0:1, :]
    second = r == pos[1:2, :]
    onehot = jnp.where(jnp.logical_or(first, second), 1.0, 0.0).astype(BF16)
    d = h_ref.shape[1]
    hs[slot, :, 0:d] = jnp.dot(onehot, h_ref[...], preferred_element_type=F32).astype(BF16)
    gate = gate_ref[...]
    lane = lax.broadcasted_iota(I32, (tm, LANES), 1)

    def hi_lo_lanes(g):
        hi = g.astype(BF16).astype(F32)
        return jnp.where(lane == 0, hi, jnp.where(lane == 1, g - hi, 0.0)).astype(BF16)
    gs = (jnp.dot(jnp.where(first, 1.0, 0.0).astype(BF16), hi_lo_lanes(gate[:, 0:1]), preferred_element_type=F32)
          + jnp.dot(jnp.where(second, 1.0, 0.0).astype(BF16), hi_lo_lanes(gate[:, 1:2]), preferred_element_type=F32))
    hs[slot, :, d:d + LANES] = gs.astype(BF16)

    def chunk_copy(sl, src, dst):
        return pltpu.make_async_copy(hs.at[sl, _chunk_rows(src), :], xb_ref.at[_chunk_rows(dst), :], sem.at[sl])

    base = i * MAX_TILE_CHUNKS

    def per_chunk(c, cc):
        chunk_copy(slot, c, tab_ref[base + c]).start()
        return cc
    lax.fori_loop(0, lused_ref[i], per_chunk, 0)

    def wait_chunks(sl, n):
        _wait_rows(lambda rows: pltpu.make_async_copy(hs.at[sl, pl.ds(0, rows), :], xb_ref.at[pl.ds(0, rows), :],
                                                      sem.at[sl]), n)

    @pl.when(i > 0)
    def _():
        wait_chunks(1 - slot, lused_ref[jnp.maximum(i - 1, 0)])

    @pl.when(i == n_tiles - 1)
    def _():
        wait_chunks(slot, lused_ref[i])
        fill(False)


def _dispatch(h, pos_rows, gate_cols, tab, lused, fill, n_blocks):
    t, d = h.shape
    dw = d + LANES
    tm = ROW_TILE
    n_tiles = t // tm
    return pl.pallas_call(
        functools.partial(_dispatch_kernel, n_tiles=n_tiles, n_blocks=n_blocks),
        grid_spec=pltpu.PrefetchScalarGridSpec(
            num_scalar_prefetch=3,
            grid=(n_tiles,),
            in_specs=[
                pl.BlockSpec((tm, d), lambda i, *_: (i, 0)),
                pl.BlockSpec((8, tm), lambda i, *_: (0, i)),
                pl.BlockSpec((tm, 8), lambda i, *_: (i, 0)),
            ],
            out_specs=pl.BlockSpec(memory_space=pl.ANY),
            scratch_shapes=[
                pltpu.VMEM((2, TILE_BUF, dw), BF16),
                pltpu.VMEM((MOE_ROWS, dw), BF16),
                pltpu.SemaphoreType.DMA((2,)),
                pltpu.SemaphoreType.DMA(()),
            ],
        ),
        out_shape=jax.ShapeDtypeStruct((n_blocks * MOE_ROWS, dw), BF16),
        compiler_params=_cparams(("arbitrary",)),
        name="moe_dispatch",
    )(tab, lused, fill, h, pos_rows, gate_cols)


def _ffn_kernel(b0_ref, nb_ref, nv_ref, fill_ref, xb_ref, w1_ref, w3_ref, w2_ref, yb_ref,
                w1s, w3s, w2s, xbuf, ybuf, zbuf, sem_in, sem_out, fsem, *, n_blocks):
    e = pl.program_id(0)
    w1s[...] = w1_ref[...].astype(BF16)
    w3s[...] = w3_ref[...].astype(BF16)
    w2s[...] = w2_ref[...].astype(BF16)
    b0, nb = b0_ref[e], nb_ref[e]
    used = fill_ref[0]

    def rows(b):
        return pl.ds(pl.multiple_of(b * MOE_ROWS, MOE_ROWS), MOE_ROWS)

    def x_copy(g, sl):
        return pltpu.make_async_copy(xb_ref.at[rows(g), :], xbuf.at[sl], sem_in.at[sl])

    def y_copy(g, sl):
        return pltpu.make_async_copy(ybuf.at[sl], yb_ref.at[rows(g), :], sem_out.at[sl])

    @pl.when(e == 0)
    def _():
        for k in range(FFN_BUFS - 1):
            @pl.when(k < used)
            def _(k=k):
                x_copy(k, k).start(priority=1)

    def block(j, c):
        g = b0 + j
        sl = g % FFN_BUFS
        x_copy(g, sl).wait()
        ahead = g + (FFN_BUFS - 1)

        @pl.when(ahead < used)
        def _():
            x_copy(ahead, ahead % FFN_BUFS).start(priority=1)

        @pl.when(g >= FFN_BUFS)
        def _():
            y_copy(g - FFN_BUFS, sl).wait()

        d = w1s.shape[0]
        x = xbuf[sl, :, 0:d]
        row = lax.broadcasted_iota(I32, x.shape, 0)
        xb = jnp.where(row < nv_ref[g], x, jnp.zeros_like(x))
        a = jnp.dot(xb, w1s[...], preferred_element_type=F32)
        b = jnp.dot(xb, w3s[...], preferred_element_type=F32)
        hid = (a * jax.nn.sigmoid(a) * b).astype(BF16)
        gl = xbuf[sl, :, d:d + LANES].astype(F32)
        gate = gl[:, 0:1] + gl[:, 1:2]
        ybuf[sl] = (jnp.dot(hid, w2s[...], preferred_element_type=F32) * gate).astype(BF16)
        y_copy(g, sl).start(priority=1)
        return c
    lax.fori_loop(0, nb, block, 0)

    @pl.when(e == N_EXPERTS - 1)
    def _():
        for k in range(FFN_BUFS):
            @pl.when(used > k)
            def _(k=k):
                y_copy(used - 1 - k, (used - 1 - k) % FFN_BUFS).wait()
        zbuf[...] = jnp.zeros_like(zbuf)

        def z_copy(b):
            return pltpu.make_async_copy(zbuf, yb_ref.at[rows(b), :], fsem)

        def start(b, c):
            z_copy(b).start()
            return c

        def wait(b, c):
            z_copy(b).wait()
            return c
        lax.fori_loop(fill_ref[0], n_blocks, start, 0)
        lax.fori_loop(fill_ref[0], n_blocks, wait, 0)


def _expert_ffn(xb, blk_start, blk_count, n_valid, used_blocks, w1, w3, w2, layer):
    r = xb.shape[0]
    d, f = w1.shape[2], w1.shape[3]
    bm = MOE_ROWS
    return pl.pallas_call(
        functools.partial(_ffn_kernel, n_blocks=r // bm),
        grid_spec=pltpu.PrefetchScalarGridSpec(
            num_scalar_prefetch=4,
            grid=(N_EXPERTS,),
            in_specs=[
                pl.BlockSpec(memory_space=pl.ANY),
                pl.BlockSpec((None, None, d, f), lambda e, *_: (layer, e, 0, 0)),
                pl.BlockSpec((None, None, d, f), lambda e, *_: (layer, e, 0, 0)),
                pl.BlockSpec((None, None, f, d), lambda e, *_: (layer, e, 0, 0)),
            ],
            out_specs=pl.BlockSpec(memory_space=pl.ANY),
            scratch_shapes=[
                pltpu.VMEM((d, f), BF16), pltpu.VMEM((d, f), BF16), pltpu.VMEM((f, d), BF16),
                pltpu.VMEM((FFN_BUFS, bm, d + LANES), BF16), pltpu.VMEM((FFN_BUFS, bm, d), BF16),
                pltpu.VMEM((bm, d), BF16),
                pltpu.SemaphoreType.DMA((FFN_BUFS,)), pltpu.SemaphoreType.DMA((FFN_BUFS,)), pltpu.SemaphoreType.DMA(()),
            ],
        ),
        out_shape=jax.ShapeDtypeStruct((r, d), BF16),
        compiler_params=_cparams(("arbitrary",)),
        name="moe_expert_mlp",
    )(blk_start, blk_count, n_valid, used_blocks, xb, w1, w3, w2)


def _combine_kernel(tab_ref, lused_ref, yb_ref, pos_ref, x_ref, mod_ref, o_ref, ys, sem, *, n_tiles):
    i = pl.program_id(0)
    slot = i % 2
    tm = x_ref.shape[0]

    def chunk_copy(sl, src, dst):
        return pltpu.make_async_copy(yb_ref.at[_chunk_rows(src), :], ys.at[sl, _chunk_rows(dst), :], sem.at[sl])

    def fetch(tile, sl):
        base = tile * MAX_TILE_CHUNKS

        def per_chunk(c, cc):
            chunk_copy(sl, tab_ref[base + c], c).start()
            return cc
        lax.fori_loop(0, lused_ref[tile], per_chunk, 0)

    @pl.when(i == 0)
    def _():
        fetch(0, 0)

    @pl.when(i + 1 < n_tiles)
    def _():
        fetch(jnp.minimum(i + 1, n_tiles - 1), 1 - slot)

    _wait_rows(lambda rows: pltpu.make_async_copy(yb_ref.at[pl.ds(0, rows), :], ys.at[slot, pl.ds(0, rows), :],
                                                  sem.at[slot]), lused_ref[i])

    y = ys[slot]
    used = lused_ref[i] * CHUNK
    rowi = lax.broadcasted_iota(I32, y.shape, 0)
    y16 = jnp.where(rowi < used, y, jnp.zeros_like(y))
    pos = pos_ref[...]
    lane = lax.broadcasted_iota(I32, (tm, TILE_BUF), 1)
    pick = jnp.where(jnp.logical_or(lane == pos[:, 0:1], lane == pos[:, 1:2]), 1.0, 0.0).astype(BF16)
    mix = jnp.dot(pick, y16, preferred_element_type=F32)
    o_ref[...] = x_ref[...] + mod_ref[5:6, :] * mix


def _combine(yb, pos_cols, tab, lused, x, mod, n_lat_tiles, n_tiles):
    d = x.shape[1]
    tm = ROW_TILE
    return pl.pallas_call(
        functools.partial(_combine_kernel, n_tiles=n_tiles),
        grid_spec=pltpu.PrefetchScalarGridSpec(
            num_scalar_prefetch=2,
            grid=(n_tiles,),
            in_specs=[
                pl.BlockSpec(memory_space=pl.ANY),
                pl.BlockSpec((tm, 8), lambda i, *_: (i, 0)),
                pl.BlockSpec((tm, d), lambda i, *_: (i, 0)),
                pl.BlockSpec((None, 6, d), lambda i, *_: (jnp.where(i >= n_lat_tiles, 1, 0), 0, 0)),
            ],
            out_specs=pl.BlockSpec((tm, d), lambda i, *_: (i, 0)),
            scratch_shapes=[pltpu.VMEM((2, TILE_BUF, d), BF16), pltpu.SemaphoreType.DMA((2,))],
        ),
        out_shape=jax.ShapeDtypeStruct((n_tiles * tm, d), F32),
        compiler_params=_cparams(("arbitrary",)),
        name="moe_combine",
    )(tab, lused, yb, pos_cols, x, mod)


def _moe_layer(x, routing, mod, w1, w3, w2, layer, n_lat_tiles, n_tiles):
    tm = ROW_TILE
    rows = n_tiles * tm
    cpb = MOE_ROWS // CHUNK
    h, ri, rg, cnt3 = routing
    cnt = cnt3[:, :, 0].astype(I32)
    nch = (cnt + CHUNK - 1) // CHUNK
    lbase = jnp.cumsum(nch, axis=1) - nch
    lused = jnp.sum(nch, axis=1).astype(I32)
    tot = jnp.sum(nch, axis=0)
    reg = (tot + cpb - 1) // cpb * cpb
    gend = jnp.cumsum(reg)
    gstart = gend - reg
    gpos = gstart[None, :] + jnp.cumsum(nch, axis=0) - nch
    rows_max = 2 * rows + n_tiles * N_EXPERTS * (CHUNK - 1) + N_EXPERTS * (MOE_ROWS - CHUNK)
    n_blocks = -(-rows_max // MOE_ROWS)
    ex = jnp.arange(N_EXPERTS, dtype=I32)
    blk0 = jnp.arange(n_blocks, dtype=I32) * cpb
    block_exp = jnp.minimum(jnp.sum((gend[None, :] <= blk0[:, None]).astype(I32), axis=1), N_EXPERTS - 1)
    sel = block_exp[:, None] == ex[None, :]
    tot_b = jnp.sum(jnp.where(sel, tot[None, :], 0), axis=1)
    st_b = jnp.sum(jnp.where(sel, gstart[None, :], 0), axis=1)
    n_valid = jnp.clip((tot_b - (blk0 - st_b)) * CHUNK, 0, MOE_ROWS).astype(I32)
    lb_tok = jnp.repeat(lbase, tm, axis=0)
    at = lambda e: jnp.sum(jnp.where(e[:, None] == ex[None, :], lb_tok, 0), axis=1)
    pos1 = CHUNK * at(ri[0]) + ri[2]
    pos2 = CHUNK * at(ri[1]) + ri[3]
    zero = jnp.zeros_like(pos1)
    pos_rows = jnp.stack([pos1, pos2] + [zero] * 6, axis=0).astype(I32)
    slot_id = jnp.arange(MAX_TILE_CHUNKS, dtype=I32)
    owner = jnp.sum((lbase + nch)[:, None, :] <= slot_id[None, :, None], axis=2)
    own = jnp.minimum(owner, N_EXPERTS - 1)[:, :, None] == ex[None, None, :]
    tab = jnp.sum(jnp.where(own, (gpos - lbase)[:, None, :], 0), axis=2) + slot_id[None, :]
    tab = tab.astype(I32).reshape(-1)
    fill = jnp.concatenate([gstart + tot, reg - tot, gend[-1:] // cpb]).astype(I32)
    xb = _dispatch(h, pos_rows, rg.T, tab, lused, fill, n_blocks)
    yb = _expert_ffn(xb, (gstart // cpb).astype(I32), (reg // cpb).astype(I32), n_valid,
                     (gend[-1:] // cpb).astype(I32), w1, w3, w2, layer)
    return _combine(yb, pos_rows.T, tab, lused, x, mod, n_lat_tiles, n_tiles)


def _router_matrix(w_rg, b_rg, w_re, b_re):
    d = w_rg.shape[0]
    wr = jnp.zeros((d, LANES), F32)
    wr = wr.at[:, 0:N_GROUPS].set(w_rg.astype(F32))
    wr = wr.at[:, ROUTER_EXPERT_ROW0:ROUTER_EXPERT_ROW0 + N_EXPERTS].set(w_re.astype(F32))
    wr = jnp.concatenate(_split_bf16(wr), axis=1)
    br = jnp.zeros((LANES,), F32)
    br = br.at[0:N_GROUPS].set(b_rg.astype(F32))
    br = br.at[ROUTER_EXPERT_ROW0:ROUTER_EXPERT_ROW0 + N_EXPERTS].set(b_re.astype(F32))
    return wr, jnp.broadcast_to(br[:, None], (LANES, LANES))


def _dup_heads(w):
    d = w.shape[0]
    w4 = w.reshape(d, N_KV_HEADS, 1, HEAD_DIM)
    return jnp.broadcast_to(w4, (d, N_KV_HEADS, 2, HEAD_DIM)).reshape(d, 2 * N_KV_HEADS * HEAD_DIM)


def kernel(x, c, ctx, c_ctx, w_mod, b_mod, norm_mix_g, norm_ffn_g, w_in_even, conv_w, w_out_even, w_qkv, q_norm_g,
           k_norm_g, sink_logit, w_o, w_router_g, b_router_g, w_router_e, b_router_e, w1, w3, w2):
    bsz, l, d = x.shape
    lc = ctx.shape[1]
    assert bsz == 1, "one sample per call"
    tm = ROW_TILE
    assert l % tm == 0 and lc % tm == 0 and l % lc == 0 and l % (DFT_MINOR * 8) == 0
    depth = w_mod.shape[0]
    t = l + lc
    assert t % DFT_MINOR == 0
    nl, nt = l // tm, t // tm

    xs = (x.reshape(l, d), ctx.reshape(lc, d))
    mod_all = _modulation(c, c_ctx, w_mod, b_mod).reshape(depth, 2, 6, d)
    cs, m1, a_tab, mc = _dft_tables(l, lc)
    rope = _rope_tables(l, t)
    segs = _segment_matrices()
    attn_bias = _attn_bias(l)
    tri = jnp.asarray(np.triu(np.ones((tm, tm), np.float32), 1)).astype(BF16)
    qd = N_HEADS * HEAD_DIM

    for layer in range(depth):
        last = layer == depth - 1
        j = layer // 2
        mod = mod_all[layer]
        g_mix = norm_mix_g[layer].reshape(1, d)
        g_ffn = norm_ffn_g[layer].reshape(1, d)
        wr, br = _router_matrix(w_router_g[layer], b_router_g[layer], w_router_e[layer], b_router_e[layer])
        route = (g_ffn, wr, br, tri)
        if layer % 2 == 0:
            z, bg, u = _even_in(xs, g_mix, mod, w_in_even[j].astype(BF16), cs, nl, nt)
            f_lat, f_ctx = _fourier_seq(z, l, m1, a_tab, mc)
            xs, *routing = _even_out(f_lat, f_ctx, bg, u, conv_w[j], w_out_even[j].astype(BF16), xs, mod, route, nl)
        else:
            wq = w_qkv[j]
            w_all = jnp.concatenate([wq[:, :qd], _dup_heads(wq[:, qd:qd + N_KV_HEADS * HEAD_DIM]),
                                     _dup_heads(wq[:, qd + N_KV_HEADS * HEAD_DIM:])], axis=1).astype(BF16)
            qg = jnp.tile(q_norm_g[j], N_HEADS).reshape(1, qd)
            kg = jnp.tile(k_norm_g[j], 2 * N_KV_HEADS).reshape(1, 2 * N_KV_HEADS * HEAD_DIM)
            q, k, v = _odd_in(xs, g_mix, mod, w_all, qg, kg, segs, rope, nl, nt)
            n_out = nl if last else nt
            att = _attention(q, k, v, sink_logit[j].astype(F32) * LOG2E, attn_bias, l,
                             n_out * (tm // ATT_BLOCK))
            xs, *routing = _odd_out(att, w_o[j].astype(BF16), xs, mod, route, nl, n_out)
        n_moe = nl if last else nt
        xs = _moe_layer(xs, routing, mod, w1, w3, w2, layer, nl, n_moe)
    return xs[:l].reshape(bsz, l, d)
```

```python
import functools
import math

import numpy as np
import jax
import jax.numpy as jnp
from jax import lax
from jax.experimental import pallas as pl
from jax.experimental.pallas import tpu as pltpu

F32 = jnp.float32
BF16 = jnp.bfloat16
I32 = jnp.int32

EPS = 1e-6
NEG_INF = -1e30

GRID_W = 64
FOURIER_GROUPS = 4
FOURIER_GROUP_DIM = 128
FOURIER_WIDTH = FOURIER_GROUPS * FOURIER_GROUP_DIM
CONV_WIDTH = 512
N_HEADS = 16
N_KV_HEADS = 4
GQA_GROUP = N_HEADS // N_KV_HEADS
HEAD_DIM = 64
WINDOW = 128
ROPE_BASE = 10000.0
ROPE_PAIRS = HEAD_DIM // 4
N_GROUPS = 4
EXPERTS_PER_GROUP = 8
N_EXPERTS = N_GROUPS * EXPERTS_PER_GROUP

LANES = 128
ROW_TILE = 256
ATT_BLOCK = 128
MOE_ROWS = 256
DFT_MINOR = 128
DFT_MAJOR_COLS = 8
VMEM_LIMIT = 48 * 1024 * 1024

LOG2E = math.log2(math.e)


def _cparams(sem):
    return pltpu.CompilerParams(dimension_semantics=sem, vmem_limit_bytes=VMEM_LIMIT)


def _split_bf16(x):
    hi = x.astype(BF16)
    lo = (x - hi.astype(F32)).astype(BF16)
    return hi, lo


def _mod_kernel(ct_ref, w_ref, b_ref, o_ref):
    ct = ct_ref[...]
    s = ct * jax.nn.sigmoid(ct)
    w = w_ref[...]
    r0 = jnp.sum(w * s[:, 0:1], axis=0, keepdims=True)
    r1 = jnp.sum(w * s[:, 1:2], axis=0, keepdims=True)
    o_ref[...] = jnp.concatenate([r0, r1], axis=0) + b_ref[...]


def _modulation(c, c_ctx, w_mod, b_mod):
    depth, d, n = w_mod.shape
    tn = 512
    ct = jnp.stack([c.reshape(d), c_ctx.reshape(d)], axis=1)
    return pl.pallas_call(
        _mod_kernel,
        grid=(depth, n // tn),
        in_specs=[
            pl.BlockSpec((d, 2), lambda l, j: (0, 0)),
            pl.BlockSpec((None, d, tn), lambda l, j: (l, 0, j)),
            pl.BlockSpec((None, 1, tn), lambda l, j: (l, 0, j)),
        ],
        out_specs=pl.BlockSpec((None, 2, tn), lambda l, j: (l, 0, j)),
        out_shape=jax.ShapeDtypeStruct((depth, 2, n), F32),
        compiler_params=_cparams(("arbitrary", "arbitrary")),
        name="modulation",
    )(ct, w_mod, b_mod.reshape(depth, 1, n))


def _norm_mod(x, g, shift, scale):
    ms = jnp.mean(x * x, axis=-1, keepdims=True)
    y = x * lax.rsqrt(ms + EPS) * g
    return y * (1.0 + scale) + shift


def _stream_of(n_lat_tiles):
    return lambda i: (jnp.where(i >= n_lat_tiles, 1, 0), 0, 0)


def _row_sources(x, n_lat_tiles, n_tiles, d):
    tm = ROW_TILE
    lat = pl.BlockSpec((tm, d), lambda i: (jnp.minimum(i, n_lat_tiles - 1), 0))
    if isinstance(x, tuple):
        ctx = pl.BlockSpec((tm, d), lambda i: (jnp.clip(i - n_lat_tiles, 0, n_tiles - n_lat_tiles - 1), 0))
        return x, [lat, ctx]
    return (x, x), [lat, pl.BlockSpec((tm, d), lambda i: (jnp.clip(i, n_lat_tiles, n_tiles - 1), 0))]


def _pick_rows(xl_ref, xc_ref, n_lat_tiles):
    is_ctx = jnp.full(xl_ref.shape, pl.program_id(0), I32) >= n_lat_tiles
    return jnp.where(is_ctx, xc_ref[...], xl_ref[...])


def _even_in_kernel(xl_ref, xc_ref, g_ref, mod_ref, w_ref, cs_ref, z_ref, bg_ref, u_ref, *, n_lat_tiles):
    h = _norm_mod(_pick_rows(xl_ref, xc_ref, n_lat_tiles), g_ref[...], mod_ref[0:1, :], mod_ref[1:2, :])
    p = jnp.dot(h.astype(BF16), w_ref[...], preferred_element_type=F32)
    cs = cs_ref[...]
    for grp in range(FOURIER_GROUPS):
        lo = grp * FOURIER_GROUP_DIM
        a = p[:, lo:lo + FOURIER_GROUP_DIM].astype(BF16)
        z = jnp.dot(a, cs, preferred_element_type=F32)
        z_ref[0, :, lo:lo + FOURIER_GROUP_DIM] = z[:, :FOURIER_GROUP_DIM]
        z_ref[1, :, lo:lo + FOURIER_GROUP_DIM] = z[:, FOURIER_GROUP_DIM:]
    o = FOURIER_WIDTH
    bg_ref[...] = p[:, o:o + CONV_WIDTH].astype(BF16)
    u_ref[...] = (p[:, o + CONV_WIDTH:o + 2 * CONV_WIDTH] * p[:, o + 2 * CONV_WIDTH:]).astype(BF16)


def _even_in(x, g, mod, w_in, cs, n_lat_tiles, n_tiles):
    d, n = w_in.shape
    tm = ROW_TILE
    t = n_tiles * tm
    xs, x_specs = _row_sources(x, n_lat_tiles, n_tiles, d)
    return pl.pallas_call(
        functools.partial(_even_in_kernel, n_lat_tiles=n_lat_tiles),
        grid=(n_tiles,),
        in_specs=x_specs + [
            pl.BlockSpec((1, d), lambda i: (0, 0)),
            pl.BlockSpec((None, 6, d), _stream_of(n_lat_tiles)),
            pl.BlockSpec((d, n), lambda i: (0, 0)),
            pl.BlockSpec(cs.shape, lambda i: (0, 0)),
        ],
        out_specs=[
            pl.BlockSpec((2, tm, FOURIER_WIDTH), lambda i: (0, i, 0)),
            pl.BlockSpec((tm, CONV_WIDTH), lambda i: (i, 0)),
            pl.BlockSpec((tm, CONV_WIDTH), lambda i: (i, 0)),
        ],
        out_shape=[
            jax.ShapeDtypeStruct((2, t, FOURIER_WIDTH), F32),
            jax.ShapeDtypeStruct((t, CONV_WIDTH), BF16),
            jax.ShapeDtypeStruct((t, CONV_WIDTH), BF16),
        ],
        compiler_params=_cparams(("arbitrary",)),
        name="even_in_proj",
    )(*xs, g, mod, w_in, cs)


def _dft_major_kernel(m_ref, z_ref, o_ref):
    shape = z_ref.shape
    z = z_ref[...].reshape(shape[0] * shape[1] * shape[2], shape[3]).astype(BF16)
    o_ref[...] = jnp.dot(m_ref[...], z, preferred_element_type=F32).reshape(shape)


def _dft_major(m1, z4, n1):
    c = z4.shape[-1]
    cb = DFT_MAJOR_COLS
    return pl.pallas_call(
        _dft_major_kernel,
        grid=(DFT_MINOR // cb,),
        in_specs=[pl.BlockSpec(m1.shape, lambda j: (0, 0)), pl.BlockSpec((2, n1, cb, c), lambda j: (0, 0, j, 0))],
        out_specs=pl.BlockSpec((2, n1, cb, c), lambda j: (0, 0, j, 0)),
        out_shape=jax.ShapeDtypeStruct((2, n1, DFT_MINOR, c), F32),
        compiler_params=_cparams(("arbitrary",)),
        name="dft_major",
    )(m1, z4)


def _dft_minor_kernel(a_ref, g_ref, o_ref):
    for j in range(a_ref.shape[0]):
        gcat = jnp.concatenate([g_ref[0, j], g_ref[1, j]], axis=0).astype(BF16)
        o_ref[:, j, :] = jnp.dot(a_ref[j], gcat, preferred_element_type=F32)


def _dft_minor(a_tab, g4):
    n1, m, k2 = a_tab.shape
    c = g4.shape[-1]
    kb = 8
    return pl.pallas_call(
        _dft_minor_kernel,
        grid=(n1 // kb,),
        in_specs=[
            pl.BlockSpec((kb, m, k2), lambda i: (i, 0, 0)),
            pl.BlockSpec((2, kb, DFT_MINOR, c), lambda i: (0, i, 0, 0)),
        ],
        out_specs=pl.BlockSpec((DFT_MINOR, kb, c), lambda i: (0, i, 0)),
        out_shape=jax.ShapeDtypeStruct((DFT_MINOR, n1, c), F32),
        compiler_params=_cparams(("arbitrary",)),
        name="dft_minor",
    )(a_tab, g4)


def _dft_ctx_kernel(a_ref, z_ref, o_ref):
    o_ref[...] = jnp.dot(a_ref[...], z_ref[...].astype(BF16), preferred_element_type=F32)


def _dft_ctx(mc, zc):
    lc, c = mc.shape[0], zc.shape[1]
    return pl.pallas_call(
        _dft_ctx_kernel,
        grid=(1,),
        in_specs=[pl.BlockSpec(mc.shape, lambda i: (0, 0)), pl.BlockSpec(zc.shape, lambda i: (0, 0))],
        out_specs=pl.BlockSpec((lc, c), lambda i: (0, 0)),
        out_shape=jax.ShapeDtypeStruct((lc, c), F32),
        compiler_params=_cparams(("arbitrary",)),
        name="dft_context",
    )(mc, zc)


def _dft_tables(l, lc):
    gd = FOURIER_GROUP_DIM
    kk = np.arange(gd)
    ang = 2.0 * np.pi * ((kk[:, None] * kk[None, :]) % gd) / gd
    cs = np.concatenate([np.cos(ang), -np.sin(ang)], axis=1)
    n1 = l // DFT_MINOR
    k1 = np.arange(n1)
    ang1 = 2.0 * np.pi * ((k1[:, None] * k1[None, :]) % n1) / n1
    c1, s1 = np.cos(ang1), np.sin(ang1)
    m1 = np.kron(np.block([[c1, s1], [-s1, c1]]), np.eye(DFT_MAJOR_COLS))
    l2 = np.arange(DFT_MINOR)
    kfull = k1[:, None, None] + n1 * l2[None, :, None]
    ang2 = 2.0 * np.pi * ((kfull * l2[None, None, :]) % l) / l
    sc = 1.0 / math.sqrt(l * gd)
    a_tab = np.concatenate([np.cos(ang2), np.sin(ang2)], axis=2) * sc
    kc = np.arange(lc)
    angc = 2.0 * np.pi * ((kc[:, None] * kc[None, :]) % lc) / lc
    mc = np.concatenate([np.cos(angc), np.sin(angc)], axis=1) / math.sqrt(lc * gd)
    as_bf16 = lambda v: jnp.asarray(v, F32).astype(BF16)
    return as_bf16(cs), as_bf16(m1), as_bf16(a_tab), as_bf16(mc)


def _fourier_seq(z, l, m1, a_tab, mc):
    t, c = z.shape[1], z.shape[2]
    lc = t - l
    n1 = l // DFT_MINOR
    g4 = _dft_major(m1, z.reshape(2, t // DFT_MINOR, DFT_MINOR, c), n1)
    f_lat = _dft_minor(a_tab, g4).reshape(l, c)
    return f_lat, _dft_ctx(mc, z[:, l:].reshape(2 * lc, c))


ROUTER_EXPERT_ROW0 = 8


def _route_tile(x, g_ref, mod_ref, wr_ref, br_ref, tri_ref, h_ref, ri_ref, rg_ref, cnt_ref):
    h = _norm_mod(x, g_ref[...], mod_ref[3:4, :], mod_ref[4:5, :])
    h_ref[...] = h.astype(BF16)
    tm = h.shape[0]
    h_hi, h_lo = _split_bf16(h)
    wr = wr_ref[...]
    both = jnp.dot(h_hi, wr, preferred_element_type=F32)
    logits = (both[:, :LANES] + both[:, LANES:]
              + jnp.dot(h_lo, wr[:, :LANES], preferred_element_type=F32))
    logits = logits.T + br_ref[:, 0:1]
    gl = logits[0:N_GROUPS]
    gmax = jnp.max(gl, axis=0, keepdims=True)
    gi = lax.broadcasted_iota(I32, gl.shape, 0)
    g_idx = jnp.min(jnp.where(gl == gmax, gi, N_GROUPS), axis=0, keepdims=True)
    g_val = 1.0 / jnp.sum(jnp.exp(gl - gmax), axis=0, keepdims=True)
    e_in = logits[ROUTER_EXPERT_ROW0:ROUTER_EXPERT_ROW0 + EXPERTS_PER_GROUP]
    for grp in range(1, N_GROUPS):
        lo = ROUTER_EXPERT_ROW0 + grp * EXPERTS_PER_GROUP
        e_in = jnp.where(g_idx == grp, logits[lo:lo + EXPERTS_PER_GROUP], e_in)
    ei = lax.broadcasted_iota(I32, e_in.shape, 0)
    v1 = jnp.max(e_in, axis=0, keepdims=True)
    i1 = jnp.min(jnp.where(e_in == v1, ei, EXPERTS_PER_GROUP), axis=0, keepdims=True)
    rest = jnp.where(ei == i1, -jnp.inf, e_in)
    v2 = jnp.max(rest, axis=0, keepdims=True)
    i2 = jnp.min(jnp.where(rest == v2, ei, EXPERTS_PER_GROUP), axis=0, keepdims=True)
    w2 = jnp.exp(v2 - v1)
    gate1 = g_val / (1.0 + w2)
    gate2 = g_val * w2 / (1.0 + w2)
    e1 = g_idx * EXPERTS_PER_GROUP + i1
    e2 = g_idx * EXPERTS_PER_GROUP + i2
    xi = lax.broadcasted_iota(I32, (N_EXPERTS, tm), 0)
    oh1 = xi == e1
    oh2 = xi == e2
    oh = oh1.astype(F32) + oh2.astype(F32)
    before = jnp.dot(oh.astype(BF16), tri_ref[...], preferred_element_type=F32)
    rank1 = jnp.sum(jnp.where(oh1, before, 0.0), axis=0, keepdims=True)
    rank2 = jnp.sum(jnp.where(oh2, before, 0.0), axis=0, keepdims=True)
    cnt_ref[...] = jnp.broadcast_to(jnp.sum(oh, axis=1, keepdims=True), cnt_ref.shape)
    orow = lax.broadcasted_iota(I32, (8, tm), 0)
    ri_ref[...] = jnp.where(orow == 0, e1, jnp.where(orow == 1, e2, jnp.where(
        orow == 2, rank1.astype(I32), jnp.where(orow == 3, rank2.astype(I32), 0))))
    rg_ref[...] = jnp.where(orow == 0, gate1, jnp.where(orow == 1, gate2, 0.0))


def _route_specs(route, d, n_tiles):
    tm = ROW_TILE
    const = lambda a: pl.BlockSpec(a.shape, lambda i: (0,) * a.ndim)
    in_specs = [const(a) for a in route]
    out_specs = [
        pl.BlockSpec((tm, d), lambda i: (i, 0)),
        pl.BlockSpec((8, tm), lambda i: (0, i)),
        pl.BlockSpec((8, tm), lambda i: (0, i)),
        pl.BlockSpec((None, N_EXPERTS, LANES), lambda i: (i, 0, 0)),
    ]
    rows = n_tiles * tm
    out_shape = [
        jax.ShapeDtypeStruct((rows, d), BF16),
        jax.ShapeDtypeStruct((8, rows), I32),
        jax.ShapeDtypeStruct((8, rows), F32),
        jax.ShapeDtypeStruct((n_tiles, N_EXPERTS, LANES), F32),
    ]
    return in_specs, out_specs, out_shape


def _even_out_kernel(fl_ref, fc_ref, bg_ref, u_ref, up_ref, un_ref, cw_ref, w_ref, xl_ref, xc_ref, mod_ref,
                     gf_ref, wr_ref, br_ref, tri_ref, o_ref, h_ref, ri_ref, rg_ref, cnt_ref, *, n_lat_tiles, n_tiles):
    i = pl.program_id(0)
    tm = u_ref.shape[0]
    u = u_ref[...].astype(F32)
    row = lax.broadcasted_iota(I32, u.shape, 0)
    first = jnp.logical_or(i == 0, i == n_lat_tiles)
    last = jnp.logical_or(i == n_lat_tiles - 1, i == n_tiles - 1)
    hb = up_ref.shape[0]
    halo_p = up_ref[...].astype(F32)[hb - 1:hb, :] * jnp.where(first, 0.0, 1.0)
    halo_n = un_ref[...].astype(F32)[0:1, :] * jnp.where(last, 0.0, 1.0)
    u_prev = jnp.where(row == 0, halo_p, pltpu.roll(u, 1, axis=0))
    u_next = jnp.where(row == tm - 1, halo_n, pltpu.roll(u, tm - 1, axis=0))
    cw = cw_ref[...]
    y = bg_ref[...].astype(F32) * (cw[0:1, :] * u_prev + cw[1:2, :] * u + cw[2:3, :] * u_next)
    is_ctx = jnp.full(fl_ref.shape, i, I32) >= n_lat_tiles
    f = jnp.where(is_ctx, fc_ref[...], fl_ref[...])
    acc = jnp.dot(f.astype(BF16), w_ref[0:FOURIER_WIDTH, :], preferred_element_type=F32)
    acc += jnp.dot(y.astype(BF16), w_ref[FOURIER_WIDTH:, :], preferred_element_type=F32)
    x_new = _pick_rows(xl_ref, xc_ref, n_lat_tiles) + mod_ref[2:3, :] * acc
    o_ref[...] = x_new
    _route_tile(x_new, gf_ref, mod_ref, wr_ref, br_ref, tri_ref, h_ref, ri_ref, rg_ref, cnt_ref)


def _even_out(f_lat, f_ctx, bg, u, conv_w, w_out, x, mod, route, n_lat_tiles):
    t, d = u.shape[0], w_out.shape[1]
    tm = ROW_TILE
    hb = 16
    n_tiles = t // tm
    r = tm // hb
    r_in, r_out, r_shape = _route_specs(route, d, n_tiles)
    xs, x_specs = _row_sources(x, n_lat_tiles, n_tiles, d)
    return pl.pallas_call(
        functools.partial(_even_out_kernel, n_lat_tiles=n_lat_tiles, n_tiles=n_tiles),
        grid=(n_tiles,),
        in_specs=[
            pl.BlockSpec((tm, FOURIER_WIDTH), lambda i: (jnp.minimum(i, n_lat_tiles - 1), 0)),
            pl.BlockSpec((tm, FOURIER_WIDTH), lambda i: (jnp.clip(i - n_lat_tiles, 0, n_tiles - n_lat_tiles - 1), 0)),
            pl.BlockSpec((tm, CONV_WIDTH), lambda i: (i, 0)),
            pl.BlockSpec((tm, CONV_WIDTH), lambda i: (i, 0)),
            pl.BlockSpec((hb, CONV_WIDTH), lambda i: (jnp.maximum(i * r - 1, 0), 0)),
            pl.BlockSpec((hb, CONV_WIDTH), lambda i: (jnp.minimum((i + 1) * r, t // hb - 1), 0)),
            pl.BlockSpec(conv_w.shape, lambda i: (0, 0)),
            pl.BlockSpec(w_out.shape, lambda i: (0, 0)),
        ] + x_specs + [
            pl.BlockSpec((None, 6, d), _stream_of(n_lat_tiles)),
        ] + r_in,
        out_specs=[pl.BlockSpec((tm, d), lambda i: (i, 0))] + r_out,
        out_shape=[jax.ShapeDtypeStruct((t, d), F32)] + r_shape,
        compiler_params=_cparams(("arbitrary",)),
        name="even_out_proj",
    )(f_lat, f_ctx, bg, u, u, u, conv_w, w_out, *xs, mod, *route)


def _seg_rms_scale(v, seg, seg_t):
    ss = jnp.dot((v * v).astype(BF16), seg, preferred_element_type=F32)
    inv = lax.rsqrt(ss * (1.0 / HEAD_DIM) + EPS)
    inv2 = jnp.concatenate(_split_bf16(inv), axis=1)
    return jnp.dot(inv2, seg_t, preferred_element_type=F32)


def _rope_cols(v, cos, sa, sb, scale):
    cols = []
    for j in range(v.shape[1] // LANES):
        c = v[:, j * LANES:(j + 1) * LANES]
        r = c * cos + pltpu.roll(c, LANES - ROPE_PAIRS, axis=1) * sa + pltpu.roll(c, ROPE_PAIRS, axis=1) * sb
        cols.append(r * scale if scale != 1.0 else r)
    return jnp.concatenate(cols, axis=1)


def _dup_head_lanes(v):
    low = lax.broadcasted_iota(I32, (v.shape[0], LANES), 1) < HEAD_DIM
    cols = []
    for j in range(v.shape[1] // LANES):
        c = v[:, j * LANES:(j + 1) * LANES]
        r = pltpu.roll(c, HEAD_DIM, axis=1)
        cols += [jnp.where(low, c, r), jnp.where(low, r, c)]
    return jnp.concatenate(cols, axis=1)


def _odd_in_kernel(x_ref, g_ref, mod_ref, w_ref, qg_ref, kg_ref, segq_ref, segqt_ref, segk_ref, segkt_ref,
                   rope_ref, q_ref, k_ref, v_ref):
    h = _norm_mod(x_ref[...], g_ref[...], mod_ref[0:1, :], mod_ref[1:2, :])
    p = jnp.dot(h.astype(BF16), w_ref[...], preferred_element_type=F32)
    qd = N_HEADS * HEAD_DIM
    kd = N_KV_HEADS * HEAD_DIM
    cos, sa, sb = rope_ref[0], rope_ref[1], rope_ref[2]
    q = p[:, :qd]
    q = q * _seg_rms_scale(q, segq_ref[...], segqt_ref[...]) * qg_ref[...]
    q_ref[...] = _rope_cols(q, cos, sa, sb, HEAD_DIM ** -0.5 * LOG2E).astype(BF16)
    k = p[:, qd:qd + kd]
    k = k * _seg_rms_scale(k, segk_ref[...], segkt_ref[...]) * kg_ref[...]
    k_ref[...] = _dup_head_lanes(_rope_cols(k, cos, sa, sb, 1.0)).astype(BF16)
    v_ref[...] = _dup_head_lanes(p[:, qd + kd:]).astype(BF16)


def _odd_in(x, g, mod, w_qkv, qg, kg, segs, rope, n_lat_tiles, n_tiles):
    t, d = x.shape
    n = w_qkv.shape[1]
    tm = ROW_TILE
    rows = n_tiles * tm
    qd = N_HEADS * HEAD_DIM
    kd = 2 * N_KV_HEADS * HEAD_DIM
    segq, segqt, segk, segkt = segs
    const = lambda a: pl.BlockSpec(a.shape, lambda i: (0,) * a.ndim)
    return pl.pallas_call(
        _odd_in_kernel,
        grid=(n_tiles,),
        in_specs=[
            pl.BlockSpec((tm, d), lambda i: (i, 0)),
            const(g),
            pl.BlockSpec((None, 6, d), _stream_of(n_lat_tiles)),
            const(w_qkv), const(qg), const(kg), const(segq), const(segqt), const(segk), const(segkt),
            pl.BlockSpec((3, tm, LANES), lambda i: (0, i, 0)),
        ],
        out_specs=[
            pl.BlockSpec((tm, qd), lambda i: (i, 0)),
            pl.BlockSpec((tm, kd), lambda i: (i, 0)),
            pl.BlockSpec((tm, kd), lambda i: (i, 0)),
        ],
        out_shape=[
            jax.ShapeDtypeStruct((rows, qd), BF16),
            jax.ShapeDtypeStruct((rows, kd), BF16),
            jax.ShapeDtypeStruct((rows, kd), BF16),
        ],
        compiler_params=_cparams(("arbitrary",)),
        name="odd_in_proj",
    )(x, g, mod, w_qkv, qg, kg, segq, segqt, segk, segkt, rope)


def _rope_tables(l, t):
    pos = np.arange(l)
    freqs = ROPE_BASE ** (-np.arange(ROPE_PAIRS, dtype=np.float32) / ROPE_PAIRS)
    lane = np.arange(LANES) % HEAD_DIM
    axis = lane // (2 * ROPE_PAIRS)
    half = (lane % (2 * ROPE_PAIRS)) // ROPE_PAIRS
    pair = lane % ROPE_PAIRS
    p = np.where(axis[None, :] == 0, (pos // GRID_W)[:, None], (pos % GRID_W)[:, None]).astype(np.float32)
    ang = p * freqs[pair][None, :].astype(np.float32)
    cos, sin = np.cos(ang), np.sin(ang)
    sa = np.where(half[None, :] == 0, -sin, 0.0)
    sb = np.where(half[None, :] == 1, sin, 0.0)
    tab = np.zeros((3, t, LANES), np.float32)
    tab[0, :l], tab[1, :l], tab[2, :l] = cos, sa, sb
    tab[0, l:] = 1.0
    return jnp.asarray(tab)


def _segment_matrices():
    def seg(width):
        m = np.zeros((width, LANES), np.float32)
        m[np.arange(width), np.arange(width) // HEAD_DIM] = 1.0
        return m
    sq, sk = seg(N_HEADS * HEAD_DIM), seg(N_KV_HEADS * HEAD_DIM)
    b = lambda v: jnp.asarray(v).astype(BF16)
    twice = lambda m: np.concatenate([m, m], axis=0)
    return b(sq), b(twice(sq.T)), b(sk), b(twice(sk.T))


def _attn_block(sink_ref, q_ref, r0, k_refs, v_refs, kx_ref, vx_ref, bias, o_ref):
    bq = ATT_BLOCK
    kp_ref, kc_ref, kn_ref = k_refs
    vp_ref, vc_ref, vn_ref = v_refs
    rows = slice(r0, r0 + bq)
    low = lax.broadcasted_iota(I32, (bq, LANES), 1) < HEAD_DIM
    top = lax.broadcasted_iota(I32, (LANES, bq), 0) < HEAD_DIM
    nt = (((1,), (1,)), ((), ()))
    tn = (((0,), (0,)), ((), ()))
    st = []
    for g in range(N_KV_HEADS):
        ks = slice(g * LANES, (g + 1) * LANES)
        parts, sinks = [], []
        for j in range(GQA_GROUP):
            col = 2 * g + j // 2
            c = q_ref[rows, col * LANES:(col + 1) * LANES]
            keep = low if j % 2 == 0 else jnp.logical_not(low)
            parts.append(jnp.where(keep, c, jnp.zeros_like(c)))
            sinks.append(jnp.full((1, bq), sink_ref[g * GQA_GROUP + j], F32))
        qs = jnp.concatenate(parts, axis=0)
        sink = jnp.concatenate(sinks, axis=1)
        kwin = jnp.concatenate([kp_ref[:, ks], kc_ref[:, ks], kn_ref[:, ks]], axis=0)
        s_loc = lax.dot_general(kwin, qs, nt, preferred_element_type=F32) + bias
        s_ctx = lax.dot_general(kx_ref[:, ks], qs, nt, preferred_element_type=F32)
        st.append((s_loc, s_ctx, sink))
    pr = []
    for g in range(N_KV_HEADS):
        s_loc, s_ctx, sink = st[g]
        m = jnp.maximum(jnp.maximum(jnp.max(s_loc, axis=0, keepdims=True),
                                    jnp.max(s_ctx, axis=0, keepdims=True)), sink)
        p_loc = jnp.exp2(s_loc - m)
        p_ctx = jnp.exp2(s_ctx - m)
        den = (jnp.sum(p_loc, axis=0, keepdims=True) + jnp.sum(p_ctx, axis=0, keepdims=True)
               + jnp.exp2(sink - m))
        pr.append((p_loc.astype(BF16), p_ctx.astype(BF16), 1.0 / den))
    for g in range(N_KV_HEADS):
        ks = slice(g * LANES, (g + 1) * LANES)
        p_loc, p_ctx, inv = pr[g]
        vwin = jnp.concatenate([vp_ref[:, ks], vc_ref[:, ks], vn_ref[:, ks]], axis=0)
        ot = lax.dot_general(vwin, p_loc, tn, preferred_element_type=F32)
        ot += lax.dot_general(vx_ref[:, ks], p_ctx, tn, preferred_element_type=F32)
        ot = ot * inv
        t0 = jnp.where(top, ot[:, 0:bq], ot[:, bq:2 * bq])
        t1 = jnp.where(top, ot[:, 2 * bq:3 * bq], ot[:, 3 * bq:4 * bq])
        o_ref[rows, 2 * g * LANES:(2 * g + 1) * LANES] = t0.T.astype(o_ref.dtype)
        o_ref[rows, (2 * g + 1) * LANES:(2 * g + 2) * LANES] = t1.T.astype(o_ref.dtype)


def _attn_kernel(sink_ref, q_ref, k0, k1, k2, k3, v0, v1, v2, v3, kx_ref, vx_ref, bias_a_ref, bias_b_ref, o_ref):
    _attn_block(sink_ref, q_ref, 0, (k0, k1, k2), (v0, v1, v2), kx_ref, vx_ref, bias_a_ref[...], o_ref)
    _attn_block(sink_ref, q_ref, ATT_BLOCK, (k1, k2, k3), (v1, v2, v3), kx_ref, vx_ref, bias_b_ref[...], o_ref)


def _attn_bias(l):
    bq = ATT_BLOCK
    r = np.arange(GQA_GROUP * bq)[None, :] % bq
    col = np.arange(3 * bq)[:, None]
    band = np.abs(col - bq - r) <= WINDOW
    no_prev, no_next = col >= bq, col < 2 * bq
    masks = [band, band & no_prev, band & no_next, band & no_prev & no_next, np.zeros_like(band)]
    return jnp.asarray(np.where(np.stack(masks), 0.0, NEG_INF).astype(np.float32))


def _attention(q, k, v, sink, bias, l, n_q_blocks):
    t = q.shape[0]
    bq = ATT_BLOCK
    nlb = l // bq
    lc = t - l
    ctx_blk = l // lc
    kw = k.shape[1]
    assert n_q_blocks % 2 == 0 and nlb % 2 == 0
    key_block = lambda off: pl.BlockSpec((bq, kw), lambda i, s: (jnp.clip(2 * i + off, 0, nlb - 1), 0))
    keys = [key_block(off) for off in (-1, 0, 1, 2)]
    ctx = pl.BlockSpec((lc, kw), lambda i, s: (ctx_blk, 0))

    def kind(off):
        def index(i, s):
            b = 2 * i + off
            return (jnp.where(b >= nlb, 4, (b == 0).astype(I32) + 2 * (b == nlb - 1).astype(I32)), 0, 0)
        return pl.BlockSpec((None,) + bias.shape[1:], index)
    return pl.pallas_call(
        _attn_kernel,
        grid_spec=pltpu.PrefetchScalarGridSpec(
            num_scalar_prefetch=1,
            grid=(n_q_blocks // 2,),
            in_specs=[pl.BlockSpec((2 * bq, q.shape[1]), lambda i, s: (i, 0))] + keys + keys
                     + [ctx, ctx, kind(0), kind(1)],
            out_specs=pl.BlockSpec((2 * bq, q.shape[1]), lambda i, s: (i, 0)),
        ),
        out_shape=jax.ShapeDtypeStruct((n_q_blocks * bq, q.shape[1]), BF16),
        compiler_params=_cparams(("arbitrary",)),
        name="window_attention",
    )(sink, q, k, k, k, k, v, v, v, v, k, v, bias, bias)


def _odd_out_kernel(a_ref, w_ref, x_ref, mod_ref, gf_ref, wr_ref, br_ref, tri_ref, o_ref, h_ref, ri_ref, rg_ref,
                    cnt_ref):
    acc = jnp.dot(a_ref[...], w_ref[...], preferred_element_type=F32)
    x_new = x_ref[...] + mod_ref[2:3, :] * acc
    o_ref[...] = x_new
    _route_tile(x_new, gf_ref, mod_ref, wr_ref, br_ref, tri_ref, h_ref, ri_ref, rg_ref, cnt_ref)


def _odd_out(a, w_o, x, mod, route, n_lat_tiles, n_tiles):
    d = x.shape[1]
    tm = ROW_TILE
    r_in, r_out, r_shape = _route_specs(route, d, n_tiles)
    return pl.pallas_call(
        _odd_out_kernel,
        grid=(n_tiles,),
        in_specs=[
            pl.BlockSpec((tm, a.shape[1]), lambda i: (i, 0)),
            pl.BlockSpec(w_o.shape, lambda i: (0, 0)),
            pl.BlockSpec((tm, d), lambda i: (i, 0)),
            pl.BlockSpec((None, 6, d), _stream_of(n_lat_tiles)),
        ] + r_in,
        out_specs=[pl.BlockSpec((tm, d), lambda i: (i, 0))] + r_out,
        out_shape=[jax.ShapeDtypeStruct((n_tiles * tm, d), F32)] + r_shape,
        compiler_params=_cparams(("arbitrary",)),
        name="odd_out_proj",
    )(a, w_o, x, mod, *route)


CHUNK = 8
TILE_BUF = 2 * ROW_TILE + N_EXPERTS * CHUNK
MAX_TILE_CHUNKS = TILE_BUF // CHUNK
FFN_BUFS = 4


def _chunk_rows(c):
    return pl.ds(pl.multiple_of(c * CHUNK, CHUNK), CHUNK)


def _wait_rows(copy_of_rows, n_chunks):
    bit = 1
    while bit <= MAX_TILE_CHUNKS:
        @pl.when((n_chunks & bit) != 0)
        def _(bit=bit):
            copy_of_rows(bit * CHUNK).wait()
        bit *= 2


def _dispatch_kernel(tab_ref, lused_ref, fill_ref, h_ref, pos_ref, gate_ref, xb_ref, hs, zbuf, sem, fsem, *, n_tiles,
                     n_blocks):
    i = pl.program_id(0)
    slot = i % 2
    tm = h_ref.shape[0]

    def tail_copy(c):
        return pltpu.make_async_copy(zbuf.at[pl.ds(0, CHUNK), :], xb_ref.at[_chunk_rows(c), :], fsem)

    def block_copy(b):
        return pltpu.make_async_copy(zbuf, xb_ref.at[pl.ds(pl.multiple_of(b * MOE_ROWS, MOE_ROWS), MOE_ROWS), :], fsem)

    def fill(start):
        def tail(e, c):
            st, n = fill_ref[e], fill_ref[N_EXPERTS + e]

            def one(c2, cc):
                cp = tail_copy(st + c2)
                cp.start() if start else cp.wait()
                return cc
            return lax.fori_loop(0, n, one, c)
        lax.fori_loop(0, N_EXPERTS, tail, 0)

        def blk(b, c):
            cp = block_copy(b)
            cp.start() if start else cp.wait()
            return c
        lax.fori_loop(fill_ref[2 * N_EXPERTS], n_blocks, blk, 0)

    @pl.when(i == 0)
    def _():
        zbuf[...] = jnp.zeros_like(zbuf)
        fill(True)

    pos = pos_ref[...]
    r = lax.broadcasted_iota(I32, (TILE_BUF, tm), 0)
    first = r == pos[0:1, :]
    second = r == pos[1:2, :]
    onehot = jnp.where(jnp.logical_or(first, second), 1.0, 0.0).astype(BF16)
    d = h_ref.shape[1]
    hs[slot, :, 0:d] = jnp.dot(onehot, h_ref[...], preferred_element_type=F32).astype(BF16)
    gate = gate_ref[...]
    lane = lax.broadcasted_iota(I32, (tm, LANES), 1)

    def hi_lo_lanes(g):
        hi = g.astype(BF16).astype(F32)
        return jnp.where(lane == 0, hi, jnp.where(lane == 1, g - hi, 0.0)).astype(BF16)
    gs = (jnp.dot(jnp.where(first, 1.0, 0.0).astype(BF16), hi_lo_lanes(gate[:, 0:1]), preferred_element_type=F32)
          + jnp.dot(jnp.where(second, 1.0, 0.0).astype(BF16), hi_lo_lanes(gate[:, 1:2]), preferred_element_type=F32))
    hs[slot, :, d:d + LANES] = gs.astype(BF16)

    def chunk_copy(sl, src, dst):
        return pltpu.make_async_copy(hs.at[sl, _chunk_rows(src), :], xb_ref.at[_chunk_rows(dst), :], sem.at[sl])

    base = i * MAX_TILE_CHUNKS

    def per_chunk(c, cc):
        chunk_copy(slot, c, tab_ref[base + c]).start()
        return cc
    lax.fori_loop(0, lused_ref[i], per_chunk, 0)

    def wait_chunks(sl, n):
        _wait_rows(lambda rows: pltpu.make_async_copy(hs.at[sl, pl.ds(0, rows), :], xb_ref.at[pl.ds(0, rows), :],
                                                      sem.at[sl]), n)

    @pl.when(i > 0)
    def _():
        wait_chunks(1 - slot, lused_ref[jnp.maximum(i - 1, 0)])

    @pl.when(i == n_tiles - 1)
    def _():
        wait_chunks(slot, lused_ref[i])
        fill(False)


def _dispatch(h, pos_rows, gate_cols, tab, lused, fill, n_blocks):
    t, d = h.shape
    dw = d + LANES
    tm = ROW_TILE
    n_tiles = t // tm
    return pl.pallas_call(
        functools.partial(_dispatch_kernel, n_tiles=n_tiles, n_blocks=n_blocks),
        grid_spec=pltpu.PrefetchScalarGridSpec(
            num_scalar_prefetch=3,
            grid=(n_tiles,),
            in_specs=[
                pl.BlockSpec((tm, d), lambda i, *_: (i, 0)),
                pl.BlockSpec((8, tm), lambda i, *_: (0, i)),
                pl.BlockSpec((tm, 8), lambda i, *_: (i, 0)),
            ],
            out_specs=pl.BlockSpec(memory_space=pl.ANY),
            scratch_shapes=[
                pltpu.VMEM((2, TILE_BUF, dw), BF16),
                pltpu.VMEM((MOE_ROWS, dw), BF16),
                pltpu.SemaphoreType.DMA((2,)),
                pltpu.SemaphoreType.DMA(()),
            ],
        ),
        out_shape=jax.ShapeDtypeStruct((n_blocks * MOE_ROWS, dw), BF16),
        compiler_params=_cparams(("arbitrary",)),
        name="moe_dispatch",
    )(tab, lused, fill, h, pos_rows, gate_cols)


def _ffn_kernel(b0_ref, nb_ref, nv_ref, fill_ref, xb_ref, w1_ref, w3_ref, w2_ref, yb_ref,
                w1s, w3s, w2s, xbuf, ybuf, zbuf, sem_in, sem_out, fsem, *, n_blocks):
    e = pl.program_id(0)
    w1s[...] = w1_ref[...].astype(BF16)
    w3s[...] = w3_ref[...].astype(BF16)
    w2s[...] = w2_ref[...].astype(BF16)
    b0, nb = b0_ref[e], nb_ref[e]
    used = fill_ref[0]

    def rows(b):
        return pl.ds(pl.multiple_of(b * MOE_ROWS, MOE_ROWS), MOE_ROWS)

    def x_copy(g, sl):
        return pltpu.make_async_copy(xb_ref.at[rows(g), :], xbuf.at[sl], sem_in.at[sl])

    def y_copy(g, sl):
        return pltpu.make_async_copy(ybuf.at[sl], yb_ref.at[rows(g), :], sem_out.at[sl])

    @pl.when(e == 0)
    def _():
        for k in range(FFN_BUFS - 1):
            @pl.when(k < used)
            def _(k=k):
                x_copy(k, k).start(priority=1)

    def block(j, c):
        g = b0 + j
        sl = g % FFN_BUFS
        x_copy(g, sl).wait()
        ahead = g + (FFN_BUFS - 1)

        @pl.when(ahead < used)
        def _():
            x_copy(ahead, ahead % FFN_BUFS).start(priority=1)

        @pl.when(g >= FFN_BUFS)
        def _():
            y_copy(g - FFN_BUFS, sl).wait()

        d = w1s.shape[0]
        x = xbuf[sl, :, 0:d]
        row = lax.broadcasted_iota(I32, x.shape, 0)
        xb = jnp.where(row < nv_ref[g], x, jnp.zeros_like(x))
        a = jnp.dot(xb, w1s[...], preferred_element_type=F32)
        b = jnp.dot(xb, w3s[...], preferred_element_type=F32)
        hid = (a * jax.nn.sigmoid(a) * b).astype(BF16)
        gl = xbuf[sl, :, d:d + LANES].astype(F32)
        gate = gl[:, 0:1] + gl[:, 1:2]
        ybuf[sl] = (jnp.dot(hid, w2s[...], preferred_element_type=F32) * gate).astype(BF16)
        y_copy(g, sl).start(priority=1)
        return c
    lax.fori_loop(0, nb, block, 0)

    @pl.when(e == N_EXPERTS - 1)
    def _():
        for k in range(FFN_BUFS):
            @pl.when(used > k)
            def _(k=k):
                y_copy(used - 1 - k, (used - 1 - k) % FFN_BUFS).wait()
        zbuf[...] = jnp.zeros_like(zbuf)

        def z_copy(b):
            return pltpu.make_async_copy(zbuf, yb_ref.at[rows(b), :], fsem)

        def start(b, c):
            z_copy(b).start()
            return c

        def wait(b, c):
            z_copy(b).wait()
            return c
        lax.fori_loop(fill_ref[0], n_blocks, start, 0)
        lax.fori_loop(fill_ref[0], n_blocks, wait, 0)


def _expert_ffn(xb, blk_start, blk_count, n_valid, used_blocks, w1, w3, w2, layer):
    r = xb.shape[0]
    d, f = w1.shape[2], w1.shape[3]
    bm = MOE_ROWS
    return pl.pallas_call(
        functools.partial(_ffn_kernel, n_blocks=r // bm),
        grid_spec=pltpu.PrefetchScalarGridSpec(
            num_scalar_prefetch=4,
            grid=(N_EXPERTS,),
            in_specs=[
                pl.BlockSpec(memory_space=pl.ANY),
                pl.BlockSpec((None, None, d, f), lambda e, *_: (layer, e, 0, 0)),
                pl.BlockSpec((None, None, d, f), lambda e, *_: (layer, e, 0, 0)),
                pl.BlockSpec((None, None, f, d), lambda e, *_: (layer, e, 0, 0)),
            ],
            out_specs=pl.BlockSpec(memory_space=pl.ANY),
            scratch_shapes=[
                pltpu.VMEM((d, f), BF16), pltpu.VMEM((d, f), BF16), pltpu.VMEM((f, d), BF16),
                pltpu.VMEM((FFN_BUFS, bm, d + LANES), BF16), pltpu.VMEM((FFN_BUFS, bm, d), BF16),
                pltpu.VMEM((bm, d), BF16),
                pltpu.SemaphoreType.DMA((FFN_BUFS,)), pltpu.SemaphoreType.DMA((FFN_BUFS,)), pltpu.SemaphoreType.DMA(()),
            ],
        ),
        out_shape=jax.ShapeDtypeStruct((r, d), BF16),
        compiler_params=_cparams(("arbitrary",)),
        name="moe_expert_mlp",
    )(blk_start, blk_count, n_valid, used_blocks, xb, w1, w3, w2)


def _combine_kernel(tab_ref, lused_ref, yb_ref, pos_ref, x_ref, mod_ref, o_ref, ys, sem, *, n_tiles):
    i = pl.program_id(0)
    slot = i % 2
    tm = x_ref.shape[0]

    def chunk_copy(sl, src, dst):
        return pltpu.make_async_copy(yb_ref.at[_chunk_rows(src), :], ys.at[sl, _chunk_rows(dst), :], sem.at[sl])

    def fetch(tile, sl):
        base = tile * MAX_TILE_CHUNKS

        def per_chunk(c, cc):
            chunk_copy(sl, tab_ref[base + c], c).start()
            return cc
        lax.fori_loop(0, lused_ref[tile], per_chunk, 0)

    @pl.when(i == 0)
    def _():
        fetch(0, 0)

    @pl.when(i + 1 < n_tiles)
    def _():
        fetch(jnp.minimum(i + 1, n_tiles - 1), 1 - slot)

    _wait_rows(lambda rows: pltpu.make_async_copy(yb_ref.at[pl.ds(0, rows), :], ys.at[slot, pl.ds(0, rows), :],
                                                  sem.at[slot]), lused_ref[i])

    y = ys[slot]
    used = lused_ref[i] * CHUNK
    rowi = lax.broadcasted_iota(I32, y.shape, 0)
    y16 = jnp.where(rowi < used, y, jnp.zeros_like(y))
    pos = pos_ref[...]
    lane = lax.broadcasted_iota(I32, (tm, TILE_BUF), 1)
    pick = jnp.where(jnp.logical_or(lane == pos[:, 0:1], lane == pos[:, 1:2]), 1.0, 0.0).astype(BF16)
    mix = jnp.dot(pick, y16, preferred_element_type=F32)
    o_ref[...] = x_ref[...] + mod_ref[5:6, :] * mix


def _combine(yb, pos_cols, tab, lused, x, mod, n_lat_tiles, n_tiles):
    d = x.shape[1]
    tm = ROW_TILE
    return pl.pallas_call(
        functools.partial(_combine_kernel, n_tiles=n_tiles),
        grid_spec=pltpu.PrefetchScalarGridSpec(
            num_scalar_prefetch=2,
            grid=(n_tiles,),
            in_specs=[
                pl.BlockSpec(memory_space=pl.ANY),
                pl.BlockSpec((tm, 8), lambda i, *_: (i, 0)),
                pl.BlockSpec((tm, d), lambda i, *_: (i, 0)),
                pl.BlockSpec((None, 6, d), lambda i, *_: (jnp.where(i >= n_lat_tiles, 1, 0), 0, 0)),
            ],
            out_specs=pl.BlockSpec((tm, d), lambda i, *_: (i, 0)),
            scratch_shapes=[pltpu.VMEM((2, TILE_BUF, d), BF16), pltpu.SemaphoreType.DMA((2,))],
        ),
        out_shape=jax.ShapeDtypeStruct((n_tiles * tm, d), F32),
        compiler_params=_cparams(("arbitrary",)),
        name="moe_combine",
    )(tab, lused, yb, pos_cols, x, mod)


def _moe_layer(x, routing, mod, w1, w3, w2, layer, n_lat_tiles, n_tiles):
    tm = ROW_TILE
    rows = n_tiles * tm
    cpb = MOE_ROWS // CHUNK
    h, ri, rg, cnt3 = routing
    cnt = cnt3[:, :, 0].astype(I32)
    nch = (cnt + CHUNK - 1) // CHUNK
    lbase = jnp.cumsum(nch, axis=1) - nch
    lused = jnp.sum(nch, axis=1).astype(I32)
    tot = jnp.sum(nch, axis=0)
    reg = (tot + cpb - 1) // cpb * cpb
    gend = jnp.cumsum(reg)
    gstart = gend - reg
    gpos = gstart[None, :] + jnp.cumsum(nch, axis=0) - nch
    rows_max = 2 * rows + n_tiles * N_EXPERTS * (CHUNK - 1) + N_EXPERTS * (MOE_ROWS - CHUNK)
    n_blocks = -(-rows_max // MOE_ROWS)
    ex = jnp.arange(N_EXPERTS, dtype=I32)
    blk0 = jnp.arange(n_blocks, dtype=I32) * cpb
    block_exp = jnp.minimum(jnp.sum((gend[None, :] <= blk0[:, None]).astype(I32), axis=1), N_EXPERTS - 1)
    sel = block_exp[:, None] == ex[None, :]
    tot_b = jnp.sum(jnp.where(sel, tot[None, :], 0), axis=1)
    st_b = jnp.sum(jnp.where(sel, gstart[None, :], 0), axis=1)
    n_valid = jnp.clip((tot_b - (blk0 - st_b)) * CHUNK, 0, MOE_ROWS).astype(I32)
    lb_tok = jnp.repeat(lbase, tm, axis=0)
    at = lambda e: jnp.sum(jnp.where(e[:, None] == ex[None, :], lb_tok, 0), axis=1)
    pos1 = CHUNK * at(ri[0]) + ri[2]
    pos2 = CHUNK * at(ri[1]) + ri[3]
    zero = jnp.zeros_like(pos1)
    pos_rows = jnp.stack([pos1, pos2] + [zero] * 6, axis=0).astype(I32)
    slot_id = jnp.arange(MAX_TILE_CHUNKS, dtype=I32)
    owner = jnp.sum((lbase + nch)[:, None, :] <= slot_id[None, :, None], axis=2)
    own = jnp.minimum(owner, N_EXPERTS - 1)[:, :, None] == ex[None, None, :]
    tab = jnp.sum(jnp.where(own, (gpos - lbase)[:, None, :], 0), axis=2) + slot_id[None, :]
    tab = tab.astype(I32).reshape(-1)
    fill = jnp.concatenate([gstart + tot, reg - tot, gend[-1:] // cpb]).astype(I32)
    xb = _dispatch(h, pos_rows, rg.T, tab, lused, fill, n_blocks)
    yb = _expert_ffn(xb, (gstart // cpb).astype(I32), (reg // cpb).astype(I32), n_valid,
                     (gend[-1:] // cpb).astype(I32), w1, w3, w2, layer)
    return _combine(yb, pos_rows.T, tab, lused, x, mod, n_lat_tiles, n_tiles)


def _router_matrix(w_rg, b_rg, w_re, b_re):
    d = w_rg.shape[0]
    wr = jnp.zeros((d, LANES), F32)
    wr = wr.at[:, 0:N_GROUPS].set(w_rg.astype(F32))
    wr = wr.at[:, ROUTER_EXPERT_ROW0:ROUTER_EXPERT_ROW0 + N_EXPERTS].set(w_re.astype(F32))
    wr = jnp.concatenate(_split_bf16(wr), axis=1)
    br = jnp.zeros((LANES,), F32)
    br = br.at[0:N_GROUPS].set(b_rg.astype(F32))
    br = br.at[ROUTER_EXPERT_ROW0:ROUTER_EXPERT_ROW0 + N_EXPERTS].set(b_re.astype(F32))
    return wr, jnp.broadcast_to(br[:, None], (LANES, LANES))


def kernel(x, c, ctx, c_ctx, w_mod, b_mod, norm_mix_g, norm_ffn_g, w_in_even, conv_w, w_out_even, w_qkv, q_norm_g,
           k_norm_g, sink_logit, w_o, w_router_g, b_router_g, w_router_e, b_router_e, w1, w3, w2):
    bsz, l, d = x.shape
    lc = ctx.shape[1]
    assert bsz == 1, "one sample per call"
    tm = ROW_TILE
    assert l % tm == 0 and lc % tm == 0 and l % lc == 0 and l % (DFT_MINOR * 8) == 0
    depth = w_mod.shape[0]
    t = l + lc
    assert t % DFT_MINOR == 0
    nl, nt = l // tm, t // tm

    xs = (x.reshape(l, d), ctx.reshape(lc, d))
    mod_all = _modulation(c, c_ctx, w_mod, b_mod).reshape(depth, 2, 6, d)
    cs, m1, a_tab, mc = _dft_tables(l, lc)
    rope = _rope_tables(l, t)
    segs = _segment_matrices()
    attn_bias = _attn_bias(l)
    tri = jnp.asarray(np.triu(np.ones((tm, tm), np.float32), 1)).astype(BF16)
    qd = N_HEADS * HEAD_DIM

    for layer in range(depth):
        last = layer == depth - 1
        j = layer // 2
        mod = mod_all[layer]
        g_mix = norm_mix_g[layer].reshape(1, d)
        g_ffn = norm_ffn_g[layer].reshape(1, d)
        wr, br = _router_matrix(w_router_g[layer], b_router_g[layer], w_router_e[layer], b_router_e[layer])
        route = (g_ffn, wr, br, tri)
        if layer % 2 == 0:
            z, bg, u = _even_in(xs, g_mix, mod, w_in_even[j].astype(BF16), cs, nl, nt)
            f_lat, f_ctx = _fourier_seq(z, l, m1, a_tab, mc)
            xs, *routing = _even_out(f_lat, f_ctx, bg, u, conv_w[j], w_out_even[j].astype(BF16), xs, mod, route, nl)
        else:
            wq = w_qkv[j]
            w_all = wq.astype(BF16)
            qg = jnp.tile(q_norm_g[j], N_HEADS).reshape(1, qd)
            kg = jnp.tile(k_norm_g[j], N_KV_HEADS).reshape(1, N_KV_HEADS * HEAD_DIM)
            q, k, v = _odd_in(xs, g_mix, mod, w_all, qg, kg, segs, rope, nl, nt)
            n_out = nl if last else nt
            att = _attention(q, k, v, sink_logit[j].astype(F32) * LOG2E, attn_bias, l,
                             n_out * (tm // ATT_BLOCK))
            xs, *routing = _odd_out(att, w_o[j].astype(BF16), xs, mod, route, nl, n_out)
        n_moe = nl if last else nt
        xs = _moe_layer(xs, routing, mod, w1, w3, w2, layer, nl, n_moe)
    return xs[:l].reshape(bsz, l, d)
```

```python
import functools
import math

import numpy as np
import jax
import jax.numpy as jnp
from jax import lax
from jax.experimental import pallas as pl
from jax.experimental.pallas import tpu as pltpu

F32 = jnp.float32
BF16 = jnp.bfloat16
I32 = jnp.int32

EPS = 1e-6
NEG_INF = -1e30

GRID_W = 64
FOURIER_GROUPS = 4
FOURIER_GROUP_DIM = 128
FOURIER_WIDTH = FOURIER_GROUPS * FOURIER_GROUP_DIM
CONV_WIDTH = 512
N_HEADS = 16
N_KV_HEADS = 4
GQA_GROUP = N_HEADS // N_KV_HEADS
HEAD_DIM = 64
WINDOW = 128
ROPE_BASE = 10000.0
ROPE_PAIRS = HEAD_DIM // 4
N_GROUPS = 4
EXPERTS_PER_GROUP = 8
N_EXPERTS = N_GROUPS * EXPERTS_PER_GROUP

LANES = 128
ROW_TILE = 256
ATT_BLOCK = 128
MOE_ROWS = 256
DFT_MINOR = 128
DFT_MAJOR_COLS = 8
VMEM_LIMIT = 48 * 1024 * 1024

LOG2E = math.log2(math.e)


def _cparams(sem):
    return pltpu.CompilerParams(dimension_semantics=sem, vmem_limit_bytes=VMEM_LIMIT)


def _split_bf16(x):
    hi = x.astype(BF16)
    lo = (x - hi.astype(F32)).astype(BF16)
    return hi, lo


def _mod_kernel(ct_ref, w_ref, b_ref, o_ref):
    ct = ct_ref[...]
    s = ct * jax.nn.sigmoid(ct)
    w = w_ref[...]
    r0 = jnp.sum(w * s[:, 0:1], axis=0, keepdims=True)
    r1 = jnp.sum(w * s[:, 1:2], axis=0, keepdims=True)
    o_ref[...] = jnp.concatenate([r0, r1], axis=0) + b_ref[...]


def _modulation(c, c_ctx, w_mod, b_mod):
    depth, d, n = w_mod.shape
    tn = 512
    ct = jnp.stack([c.reshape(d), c_ctx.reshape(d)], axis=1)
    return pl.pallas_call(
        _mod_kernel,
        grid=(depth, n // tn),
        in_specs=[
            pl.BlockSpec((d, 2), lambda l, j: (0, 0)),
            pl.BlockSpec((None, d, tn), lambda l, j: (l, 0, j)),
            pl.BlockSpec((None, 1, tn), lambda l, j: (l, 0, j)),
        ],
        out_specs=pl.BlockSpec((None, 2, tn), lambda l, j: (l, 0, j)),
        out_shape=jax.ShapeDtypeStruct((depth, 2, n), F32),
        compiler_params=_cparams(("arbitrary", "arbitrary")),
        name="modulation",
    )(ct, w_mod, b_mod.reshape(depth, 1, n))


def _norm_mod(x, g, shift, scale):
    ms = jnp.mean(x * x, axis=-1, keepdims=True)
    y = x * lax.rsqrt(ms + EPS) * g
    return y * (1.0 + scale) + shift


def _stream_of(n_lat_tiles):
    return lambda i: (jnp.where(i >= n_lat_tiles, 1, 0), 0, 0)


def _row_sources(x, n_lat_tiles, n_tiles, d):
    tm = ROW_TILE
    lat = pl.BlockSpec((tm, d), lambda i: (jnp.minimum(i, n_lat_tiles - 1), 0))
    if isinstance(x, tuple):
        ctx = pl.BlockSpec((tm, d), lambda i: (jnp.clip(i - n_lat_tiles, 0, n_tiles - n_lat_tiles - 1), 0))
        return x, [lat, ctx]
    return (x, x), [lat, pl.BlockSpec((tm, d), lambda i: (jnp.clip(i, n_lat_tiles, n_tiles - 1), 0))]


def _pick_rows(xl_ref, xc_ref, n_lat_tiles):
    is_ctx = jnp.full(xl_ref.shape, pl.program_id(0), I32) >= n_lat_tiles
    return jnp.where(is_ctx, xc_ref[...], xl_ref[...])


def _even_in_kernel(xl_ref, xc_ref, g_ref, mod_ref, w_ref, cs_ref, z_ref, bg_ref, u_ref, *, n_lat_tiles):
    h = _norm_mod(_pick_rows(xl_ref, xc_ref, n_lat_tiles), g_ref[...], mod_ref[0:1, :], mod_ref[1:2, :])
    p = jnp.dot(h.astype(BF16), w_ref[...], preferred_element_type=F32)
    cs = cs_ref[...]
    for grp in range(FOURIER_GROUPS):
        lo = grp * FOURIER_GROUP_DIM
        a = p[:, lo:lo + FOURIER_GROUP_DIM].astype(BF16)
        z = jnp.dot(a, cs, preferred_element_type=F32)
        z_ref[0, :, lo:lo + FOURIER_GROUP_DIM] = z[:, :FOURIER_GROUP_DIM]
        z_ref[1, :, lo:lo + FOURIER_GROUP_DIM] = z[:, FOURIER_GROUP_DIM:]
    o = FOURIER_WIDTH
    bg_ref[...] = p[:, o:o + CONV_WIDTH].astype(BF16)
    u_ref[...] = (p[:, o + CONV_WIDTH:o + 2 * CONV_WIDTH] * p[:, o + 2 * CONV_WIDTH:]).astype(BF16)


def _even_in(x, g, mod, w_in, cs, n_lat_tiles, n_tiles):
    d, n = w_in.shape
    tm = ROW_TILE
    t = n_tiles * tm
    xs, x_specs = _row_sources(x, n_lat_tiles, n_tiles, d)
    return pl.pallas_call(
        functools.partial(_even_in_kernel, n_lat_tiles=n_lat_tiles),
        grid=(n_tiles,),
        in_specs=x_specs + [
            pl.BlockSpec((1, d), lambda i: (0, 0)),
            pl.BlockSpec((None, 6, d), _stream_of(n_lat_tiles)),
            pl.BlockSpec((d, n), lambda i: (0, 0)),
            pl.BlockSpec(cs.shape, lambda i: (0, 0)),
        ],
        out_specs=[
            pl.BlockSpec((2, tm, FOURIER_WIDTH), lambda i: (0, i, 0)),
            pl.BlockSpec((tm, CONV_WIDTH), lambda i: (i, 0)),
            pl.BlockSpec((tm, CONV_WIDTH), lambda i: (i, 0)),
        ],
        out_shape=[
            jax.ShapeDtypeStruct((2, t, FOURIER_WIDTH), F32),
            jax.ShapeDtypeStruct((t, CONV_WIDTH), BF16),
            jax.ShapeDtypeStruct((t, CONV_WIDTH), BF16),
        ],
        compiler_params=_cparams(("arbitrary",)),
        name="even_in_proj",
    )(*xs, g, mod, w_in, cs)


def _dft_major_kernel(m_ref, z_ref, o_ref):
    shape = z_ref.shape
    z = z_ref[...].reshape(shape[0] * shape[1] * shape[2], shape[3]).astype(BF16)
    o_ref[...] = jnp.dot(m_ref[...], z, preferred_element_type=F32).reshape(shape)


def _dft_major(m1, z4, n1):
    c = z4.shape[-1]
    cb = DFT_MAJOR_COLS
    return pl.pallas_call(
        _dft_major_kernel,
        grid=(DFT_MINOR // cb,),
        in_specs=[pl.BlockSpec(m1.shape, lambda j: (0, 0)), pl.BlockSpec((2, n1, cb, c), lambda j: (0, 0, j, 0))],
        out_specs=pl.BlockSpec((2, n1, cb, c), lambda j: (0, 0, j, 0)),
        out_shape=jax.ShapeDtypeStruct((2, n1, DFT_MINOR, c), F32),
        compiler_params=_cparams(("arbitrary",)),
        name="dft_major",
    )(m1, z4)


def _dft_minor_kernel(a_ref, g_ref, o_ref):
    for j in range(a_ref.shape[0]):
        gcat = jnp.concatenate([g_ref[0, j], g_ref[1, j]], axis=0).astype(BF16)
        o_ref[:, j, :] = jnp.dot(a_ref[j], gcat, preferred_element_type=F32)


def _dft_minor(a_tab, g4):
    n1, m, k2 = a_tab.shape
    c = g4.shape[-1]
    kb = 8
    return pl.pallas_call(
        _dft_minor_kernel,
        grid=(n1 // kb,),
        in_specs=[
            pl.BlockSpec((kb, m, k2), lambda i: (i, 0, 0)),
            pl.BlockSpec((2, kb, DFT_MINOR, c), lambda i: (0, i, 0, 0)),
        ],
        out_specs=pl.BlockSpec((DFT_MINOR, kb, c), lambda i: (0, i, 0)),
        out_shape=jax.ShapeDtypeStruct((DFT_MINOR, n1, c), F32),
        compiler_params=_cparams(("arbitrary",)),
        name="dft_minor",
    )(a_tab, g4)


def _dft_ctx_kernel(a_ref, z_ref, o_ref):
    o_ref[...] = jnp.dot(a_ref[...], z_ref[...].astype(BF16), preferred_element_type=F32)


def _dft_ctx(mc, zc):
    lc, c = mc.shape[0], zc.shape[1]
    return pl.pallas_call(
        _dft_ctx_kernel,
        grid=(1,),
        in_specs=[pl.BlockSpec(mc.shape, lambda i: (0, 0)), pl.BlockSpec(zc.shape, lambda i: (0, 0))],
        out_specs=pl.BlockSpec((lc, c), lambda i: (0, 0)),
        out_shape=jax.ShapeDtypeStruct((lc, c), F32),
        compiler_params=_cparams(("arbitrary",)),
        name="dft_context",
    )(mc, zc)


def _dft_tables(l, lc):
    gd = FOURIER_GROUP_DIM
    kk = np.arange(gd)
    ang = 2.0 * np.pi * ((kk[:, None] * kk[None, :]) % gd) / gd
    cs = np.concatenate([np.cos(ang), -np.sin(ang)], axis=1)
    n1 = l // DFT_MINOR
    k1 = np.arange(n1)
    ang1 = 2.0 * np.pi * ((k1[:, None] * k1[None, :]) % n1) / n1
    c1, s1 = np.cos(ang1), np.sin(ang1)
    m1 = np.kron(np.block([[c1, s1], [-s1, c1]]), np.eye(DFT_MAJOR_COLS))
    l2 = np.arange(DFT_MINOR)
    kfull = k1[:, None, None] + n1 * l2[None, :, None]
    ang2 = 2.0 * np.pi * ((kfull * l2[None, None, :]) % l) / l
    sc = 1.0 / math.sqrt(l * gd)
    a_tab = np.concatenate([np.cos(ang2), np.sin(ang2)], axis=2) * sc
    kc = np.arange(lc)
    angc = 2.0 * np.pi * ((kc[:, None] * kc[None, :]) % lc) / lc
    mc = np.concatenate([np.cos(angc), np.sin(angc)], axis=1) / math.sqrt(lc * gd)
    as_bf16 = lambda v: jnp.asarray(v, F32).astype(BF16)
    return as_bf16(cs), as_bf16(m1), as_bf16(a_tab), as_bf16(mc)


def _fourier_seq(z, l, m1, a_tab, mc):
    t, c = z.shape[1], z.shape[2]
    lc = t - l
    n1 = l // DFT_MINOR
    g4 = _dft_major(m1, z.reshape(2, t // DFT_MINOR, DFT_MINOR, c), n1)
    f_lat = _dft_minor(a_tab, g4).reshape(l, c)
    return f_lat, _dft_ctx(mc, z[:, l:].reshape(2 * lc, c))


ROUTER_EXPERT_ROW0 = 8


def _route_tile(x, g_ref, mod_ref, wr_ref, br_ref, tri_ref, h_ref, ri_ref, rg_ref, cnt_ref):
    h = _norm_mod(x, g_ref[...], mod_ref[3:4, :], mod_ref[4:5, :])
    h_ref[...] = h.astype(BF16)
    tm = h.shape[0]
    h_hi, h_lo = _split_bf16(h)
    wr = wr_ref[...]
    both = jnp.dot(h_hi, wr, preferred_element_type=F32)
    logits = (both[:, :LANES] + both[:, LANES:]
              + jnp.dot(h_lo, wr[:, :LANES], preferred_element_type=F32))
    logits = logits.T + br_ref[:, 0:1]
    gl = logits[0:N_GROUPS]
    gmax = jnp.max(gl, axis=0, keepdims=True)
    gi = lax.broadcasted_iota(I32, gl.shape, 0)
    g_idx = jnp.min(jnp.where(gl == gmax, gi, N_GROUPS), axis=0, keepdims=True)
    g_val = 1.0 / jnp.sum(jnp.exp(gl - gmax), axis=0, keepdims=True)
    e_in = logits[ROUTER_EXPERT_ROW0:ROUTER_EXPERT_ROW0 + EXPERTS_PER_GROUP]
    for grp in range(1, N_GROUPS):
        lo = ROUTER_EXPERT_ROW0 + grp * EXPERTS_PER_GROUP
        e_in = jnp.where(g_idx == grp, logits[lo:lo + EXPERTS_PER_GROUP], e_in)
    ei = lax.broadcasted_iota(I32, e_in.shape, 0)
    v1 = jnp.max(e_in, axis=0, keepdims=True)
    i1 = jnp.min(jnp.where(e_in == v1, ei, EXPERTS_PER_GROUP), axis=0, keepdims=True)
    rest = jnp.where(ei == i1, -jnp.inf, e_in)
    v2 = jnp.max(rest, axis=0, keepdims=True)
    i2 = jnp.min(jnp.where(rest == v2, ei, EXPERTS_PER_GROUP), axis=0, keepdims=True)
    w2 = jnp.exp(v2 - v1)
    gate1 = g_val / (1.0 + w2)
    gate2 = g_val * w2 / (1.0 + w2)
    e1 = g_idx * EXPERTS_PER_GROUP + i1
    e2 = g_idx * EXPERTS_PER_GROUP + i2
    xi = lax.broadcasted_iota(I32, (N_EXPERTS, tm), 0)
    oh1 = xi == e1
    oh2 = xi == e2
    oh = oh1.astype(F32) + oh2.astype(F32)
    before = jnp.dot(oh.astype(BF16), tri_ref[...], preferred_element_type=F32)
    rank1 = jnp.sum(jnp.where(oh1, before, 0.0), axis=0, keepdims=True)
    rank2 = jnp.sum(jnp.where(oh2, before, 0.0), axis=0, keepdims=True)
    cnt_ref[...] = jnp.broadcast_to(jnp.sum(oh, axis=1, keepdims=True), cnt_ref.shape)
    orow = lax.broadcasted_iota(I32, (8, tm), 0)
    ri_ref[...] = jnp.where(orow == 0, e1, jnp.where(orow == 1, e2, jnp.where(
        orow == 2, rank1.astype(I32), jnp.where(orow == 3, rank2.astype(I32), 0))))
    rg_ref[...] = jnp.where(orow == 0, gate1, jnp.where(orow == 1, gate2, 0.0))


def _route_specs(route, d, n_tiles):
    tm = ROW_TILE
    const = lambda a: pl.BlockSpec(a.shape, lambda i: (0,) * a.ndim)
    in_specs = [const(a) for a in route]
    out_specs = [
        pl.BlockSpec((tm, d), lambda i: (i, 0)),
        pl.BlockSpec((8, tm), lambda i: (0, i)),
        pl.BlockSpec((8, tm), lambda i: (0, i)),
        pl.BlockSpec((None, N_EXPERTS, LANES), lambda i: (i, 0, 0)),
    ]
    rows = n_tiles * tm
    out_shape = [
        jax.ShapeDtypeStruct((rows, d), BF16),
        jax.ShapeDtypeStruct((8, rows), I32),
        jax.ShapeDtypeStruct((8, rows), F32),
        jax.ShapeDtypeStruct((n_tiles, N_EXPERTS, LANES), F32),
    ]
    return in_specs, out_specs, out_shape


def _even_out_kernel(fl_ref, fc_ref, bg_ref, u_ref, up_ref, un_ref, cw_ref, w_ref, xl_ref, xc_ref, mod_ref,
                     gf_ref, wr_ref, br_ref, tri_ref, o_ref, h_ref, ri_ref, rg_ref, cnt_ref, *, n_lat_tiles, n_tiles):
    i = pl.program_id(0)
    tm = u_ref.shape[0]
    u = u_ref[...].astype(F32)
    row = lax.broadcasted_iota(I32, u.shape, 0)
    first = jnp.logical_or(i == 0, i == n_lat_tiles)
    last = jnp.logical_or(i == n_lat_tiles - 1, i == n_tiles - 1)
    hb = up_ref.shape[0]
    halo_p = up_ref[...].astype(F32)[hb - 1:hb, :] * jnp.where(first, 0.0, 1.0)
    halo_n = un_ref[...].astype(F32)[0:1, :] * jnp.where(last, 0.0, 1.0)
    u_prev = jnp.where(row == 0, halo_p, pltpu.roll(u, 1, axis=0))
    u_next = jnp.where(row == tm - 1, halo_n, pltpu.roll(u, tm - 1, axis=0))
    cw = cw_ref[...]
    y = bg_ref[...].astype(F32) * (cw[0:1, :] * u_prev + cw[1:2, :] * u + cw[2:3, :] * u_next)
    is_ctx = jnp.full(fl_ref.shape, i, I32) >= n_lat_tiles
    f = jnp.where(is_ctx, fc_ref[...], fl_ref[...])
    acc = jnp.dot(f.astype(BF16), w_ref[0:FOURIER_WIDTH, :], preferred_element_type=F32)
    acc += jnp.dot(y.astype(BF16), w_ref[FOURIER_WIDTH:, :], preferred_element_type=F32)
    x_new = _pick_rows(xl_ref, xc_ref, n_lat_tiles) + mod_ref[2:3, :] * acc
    o_ref[...] = x_new
    _route_tile(x_new, gf_ref, mod_ref, wr_ref, br_ref, tri_ref, h_ref, ri_ref, rg_ref, cnt_ref)


def _even_out(f_lat, f_ctx, bg, u, conv_w, w_out, x, mod, route, n_lat_tiles):
    t, d = u.shape[0], w_out.shape[1]
    tm = ROW_TILE
    hb = 16
    n_tiles = t // tm
    r = tm // hb
    r_in, r_out, r_shape = _route_specs(route, d, n_tiles)
    xs, x_specs = _row_sources(x, n_lat_tiles, n_tiles, d)
    return pl.pallas_call(
        functools.partial(_even_out_kernel, n_lat_tiles=n_lat_tiles, n_tiles=n_tiles),
        grid=(n_tiles,),
        in_specs=[
            pl.BlockSpec((tm, FOURIER_WIDTH), lambda i: (jnp.minimum(i, n_lat_tiles - 1), 0)),
            pl.BlockSpec((tm, FOURIER_WIDTH), lambda i: (jnp.clip(i - n_lat_tiles, 0, n_tiles - n_lat_tiles - 1), 0)),
            pl.BlockSpec((tm, CONV_WIDTH), lambda i: (i, 0)),
            pl.BlockSpec((tm, CONV_WIDTH), lambda i: (i, 0)),
            pl.BlockSpec((hb, CONV_WIDTH), lambda i: (jnp.maximum(i * r - 1, 0), 0)),
            pl.BlockSpec((hb, CONV_WIDTH), lambda i: (jnp.minimum((i + 1) * r, t // hb - 1), 0)),
            pl.BlockSpec(conv_w.shape, lambda i: (0, 0)),
            pl.BlockSpec(w_out.shape, lambda i: (0, 0)),
        ] + x_specs + [
            pl.BlockSpec((None, 6, d), _stream_of(n_lat_tiles)),
        ] + r_in,
        out_specs=[pl.BlockSpec((tm, d), lambda i: (i, 0))] + r_out,
        out_shape=[jax.ShapeDtypeStruct((t, d), F32)] + r_shape,
        compiler_params=_cparams(("arbitrary",)),
        name="even_out_proj",
    )(f_lat, f_ctx, bg, u, u, u, conv_w, w_out, *xs, mod, *route)


def _seg_rms_scale(v, seg, seg_t):
    ss = jnp.dot((v * v).astype(BF16), seg, preferred_element_type=F32)
    inv = lax.rsqrt(ss * (1.0 / HEAD_DIM) + EPS)
    inv2 = jnp.concatenate(_split_bf16(inv), axis=1)
    return jnp.dot(inv2, seg_t, preferred_element_type=F32)


def _rope_cols(v, cos, sa, sb, scale):
    cols = []
    for j in range(v.shape[1] // LANES):
        c = v[:, j * LANES:(j + 1) * LANES]
        r = c * cos + pltpu.roll(c, LANES - ROPE_PAIRS, axis=1) * sa + pltpu.roll(c, ROPE_PAIRS, axis=1) * sb
        cols.append(r * scale if scale != 1.0 else r)
    return jnp.concatenate(cols, axis=1)


def _dup_head_lanes(v):
    low = lax.broadcasted_iota(I32, (v.shape[0], LANES), 1) < HEAD_DIM
    cols = []
    for j in range(v.shape[1] // LANES):
        c = v[:, j * LANES:(j + 1) * LANES]
        r = pltpu.roll(c, HEAD_DIM, axis=1)
        cols += [jnp.where(low, c, r), jnp.where(low, r, c)]
    return jnp.concatenate(cols, axis=1)


def _odd_in_kernel(x_ref, g_ref, mod_ref, w_ref, qg_ref, kg_ref, segq_ref, segqt_ref, segk_ref, segkt_ref,
                   rope_ref, q_ref, k_ref, v_ref):
    h = _norm_mod(x_ref[...], g_ref[...], mod_ref[0:1, :], mod_ref[1:2, :])
    p = jnp.dot(h.astype(BF16), w_ref[...], preferred_element_type=F32)
    qd = N_HEADS * HEAD_DIM
    kd = N_KV_HEADS * HEAD_DIM
    cos, sa, sb = rope_ref[0], rope_ref[1], rope_ref[2]
    q = p[:, :qd]
    q = q * _seg_rms_scale(q, segq_ref[...], segqt_ref[...]) * qg_ref[...]
    q_ref[...] = _rope_cols(q, cos, sa, sb, HEAD_DIM ** -0.5 * LOG2E).astype(BF16)
    k = p[:, qd:qd + kd]
    k = k * _seg_rms_scale(k, segk_ref[...], segkt_ref[...]) * kg_ref[...]
    k_ref[...] = _dup_head_lanes(_rope_cols(k, cos, sa, sb, 1.0)).astype(BF16)
    v_ref[...] = _dup_head_lanes(p[:, qd + kd:]).astype(BF16)


def _odd_in(x, g, mod, w_qkv, qg, kg, segs, rope, n_lat_tiles, n_tiles):
    t, d = x.shape
    n = w_qkv.shape[1]
    tm = ROW_TILE
    rows = n_tiles * tm
    qd = N_HEADS * HEAD_DIM
    kd = 2 * N_KV_HEADS * HEAD_DIM
    segq, segqt, segk, segkt = segs
    const = lambda a: pl.BlockSpec(a.shape, lambda i: (0,) * a.ndim)
    return pl.pallas_call(
        _odd_in_kernel,
        grid=(n_tiles,),
        in_specs=[
            pl.BlockSpec((tm, d), lambda i: (i, 0)),
            const(g),
            pl.BlockSpec((None, 6, d), _stream_of(n_lat_tiles)),
            const(w_qkv), const(qg), const(kg), const(segq), const(segqt), const(segk), const(segkt),
            pl.BlockSpec((3, tm, LANES), lambda i: (0, i, 0)),
        ],
        out_specs=[
            pl.BlockSpec((tm, qd), lambda i: (i, 0)),
            pl.BlockSpec((tm, kd), lambda i: (i, 0)),
            pl.BlockSpec((tm, kd), lambda i: (i, 0)),
        ],
        out_shape=[
            jax.ShapeDtypeStruct((rows, qd), BF16),
            jax.ShapeDtypeStruct((rows, kd), BF16),
            jax.ShapeDtypeStruct((rows, kd), BF16),
        ],
        compiler_params=_cparams(("arbitrary",)),
        name="odd_in_proj",
    )(x, g, mod, w_qkv, qg, kg, segq, segqt, segk, segkt, rope)


def _rope_tables(l, t):
    pos = np.arange(l)
    freqs = ROPE_BASE ** (-np.arange(ROPE_PAIRS, dtype=np.float32) / ROPE_PAIRS)
    lane = np.arange(LANES) % HEAD_DIM
    axis = lane // (2 * ROPE_PAIRS)
    half = (lane % (2 * ROPE_PAIRS)) // ROPE_PAIRS
    pair = lane % ROPE_PAIRS
    p = np.where(axis[None, :] == 0, (pos // GRID_W)[:, None], (pos % GRID_W)[:, None]).astype(np.float32)
    ang = p * freqs[pair][None, :].astype(np.float32)
    cos, sin = np.cos(ang), np.sin(ang)
    sa = np.where(half[None, :] == 0, -sin, 0.0)
    sb = np.where(half[None, :] == 1, sin, 0.0)
    tab = np.zeros((3, t, LANES), np.float32)
    tab[0, :l], tab[1, :l], tab[2, :l] = cos, sa, sb
    tab[0, l:] = 1.0
    return jnp.asarray(tab)


def _segment_matrices():
    def seg(width):
        m = np.zeros((width, LANES), np.float32)
        m[np.arange(width), np.arange(width) // HEAD_DIM] = 1.0
        return m
    sq, sk = seg(N_HEADS * HEAD_DIM), seg(N_KV_HEADS * HEAD_DIM)
    b = lambda v: jnp.asarray(v).astype(BF16)
    twice = lambda m: np.concatenate([m, m], axis=0)
    return b(sq), b(twice(sq.T)), b(sk), b(twice(sk.T))


def _attn_kernel(sink_ref, q_ref, k0, k1, k2, k3, v0, v1, v2, v3, kx_ref, vx_ref, bias_a_ref, bias_b_ref, o_ref):
    bq = ATT_BLOCK
    low = lax.broadcasted_iota(I32, (bq, LANES), 1) < HEAD_DIM
    top = lax.broadcasted_iota(I32, (LANES, bq), 0) < HEAD_DIM
    nt = (((1,), (1,)), ((), ()))
    tn = (((0,), (0,)), ((), ()))
    blocks = ((0, (k0, k1, k2), (v0, v1, v2), bias_a_ref), (bq, (k1, k2, k3), (v1, v2, v3), bias_b_ref))
    jobs = [(blk, g) for blk in blocks for g in range(N_KV_HEADS)]
    st = []
    for (r0, k_refs, _, bias_ref), g in jobs:
        rows = slice(r0, r0 + bq)
        ks = slice(g * LANES, (g + 1) * LANES)
        parts, sinks = [], []
        for j in range(GQA_GROUP):
            col = 2 * g + j // 2
            c = q_ref[rows, col * LANES:(col + 1) * LANES]
            keep = low if j % 2 == 0 else jnp.logical_not(low)
            parts.append(jnp.where(keep, c, jnp.zeros_like(c)))
            sinks.append(jnp.full((1, bq), sink_ref[g * GQA_GROUP + j], F32))
        qs = jnp.concatenate(parts, axis=0)
        sink = jnp.concatenate(sinks, axis=1)
        kwin = jnp.concatenate([r[:, ks] for r in k_refs], axis=0)
        s_loc = lax.dot_general(kwin, qs, nt, preferred_element_type=F32) + bias_ref[...]
        s_ctx = lax.dot_general(kx_ref[:, ks], qs, nt, preferred_element_type=F32)
        st.append((s_loc, s_ctx, sink))
    pr = []
    for s_loc, s_ctx, sink in st:
        m = jnp.maximum(jnp.maximum(jnp.max(s_loc, axis=0, keepdims=True),
                                    jnp.max(s_ctx, axis=0, keepdims=True)), sink)
        p_loc = jnp.exp2(s_loc - m)
        p_ctx = jnp.exp2(s_ctx - m)
        den = (jnp.sum(p_loc, axis=0, keepdims=True) + jnp.sum(p_ctx, axis=0, keepdims=True)
               + jnp.exp2(sink - m))
        pr.append((p_loc.astype(BF16), p_ctx.astype(BF16), 1.0 / den))
    for ((r0, _, v_refs, _), g), (p_loc, p_ctx, inv) in zip(jobs, pr):
        rows = slice(r0, r0 + bq)
        ks = slice(g * LANES, (g + 1) * LANES)
        vwin = jnp.concatenate([r[:, ks] for r in v_refs], axis=0)
        ot = lax.dot_general(vwin, p_loc, tn, preferred_element_type=F32)
        ot += lax.dot_general(vx_ref[:, ks], p_ctx, tn, preferred_element_type=F32)
        ot = ot * inv
        t0 = jnp.where(top, ot[:, 0:bq], ot[:, bq:2 * bq])
        t1 = jnp.where(top, ot[:, 2 * bq:3 * bq], ot[:, 3 * bq:4 * bq])
        o_ref[rows, 2 * g * LANES:(2 * g + 1) * LANES] = t0.T.astype(o_ref.dtype)
        o_ref[rows, (2 * g + 1) * LANES:(2 * g + 2) * LANES] = t1.T.astype(o_ref.dtype)


def _attn_bias(l):
    bq = ATT_BLOCK
    r = np.arange(GQA_GROUP * bq)[None, :] % bq
    col = np.arange(3 * bq)[:, None]
    band = np.abs(col - bq - r) <= WINDOW
    no_prev, no_next = col >= bq, col < 2 * bq
    masks = [band, band & no_prev, band & no_next, band & no_prev & no_next, np.zeros_like(band)]
    return jnp.asarray(np.where(np.stack(masks), 0.0, NEG_INF).astype(np.float32))


def _attention(q, k, v, sink, bias, l, n_q_blocks):
    t = q.shape[0]
    bq = ATT_BLOCK
    nlb = l // bq
    lc = t - l
    ctx_blk = l // lc
    kw = k.shape[1]
    assert n_q_blocks % 2 == 0 and nlb % 2 == 0
    key_block = lambda off: pl.BlockSpec((bq, kw), lambda i, s: (jnp.clip(2 * i + off, 0, nlb - 1), 0))
    keys = [key_block(off) for off in (-1, 0, 1, 2)]
    ctx = pl.BlockSpec((lc, kw), lambda i, s: (ctx_blk, 0))

    def kind(off):
        def index(i, s):
            b = 2 * i + off
            return (jnp.where(b >= nlb, 4, (b == 0).astype(I32) + 2 * (b == nlb - 1).astype(I32)), 0, 0)
        return pl.BlockSpec((None,) + bias.shape[1:], index)
    return pl.pallas_call(
        _attn_kernel,
        grid_spec=pltpu.PrefetchScalarGridSpec(
            num_scalar_prefetch=1,
            grid=(n_q_blocks // 2,),
            in_specs=[pl.BlockSpec((2 * bq, q.shape[1]), lambda i, s: (i, 0))] + keys + keys
                     + [ctx, ctx, kind(0), kind(1)],
            out_specs=pl.BlockSpec((2 * bq, q.shape[1]), lambda i, s: (i, 0)),
        ),
        out_shape=jax.ShapeDtypeStruct((n_q_blocks * bq, q.shape[1]), BF16),
        compiler_params=_cparams(("arbitrary",)),
        name="window_attention",
    )(sink, q, k, k, k, k, v, v, v, v, k, v, bias, bias)


def _odd_out_kernel(a_ref, w_ref, x_ref, mod_ref, gf_ref, wr_ref, br_ref, tri_ref, o_ref, h_ref, ri_ref, rg_ref,
                    cnt_ref):
    acc = jnp.dot(a_ref[...], w_ref[...], preferred_element_type=F32)
    x_new = x_ref[...] + mod_ref[2:3, :] * acc
    o_ref[...] = x_new
    _route_tile(x_new, gf_ref, mod_ref, wr_ref, br_ref, tri_ref, h_ref, ri_ref, rg_ref, cnt_ref)


def _odd_out(a, w_o, x, mod, route, n_lat_tiles, n_tiles):
    d = x.shape[1]
    tm = ROW_TILE
    r_in, r_out, r_shape = _route_specs(route, d, n_tiles)
    return pl.pallas_call(
        _odd_out_kernel,
        grid=(n_tiles,),
        in_specs=[
            pl.BlockSpec((tm, a.shape[1]), lambda i: (i, 0)),
            pl.BlockSpec(w_o.shape, lambda i: (0, 0)),
            pl.BlockSpec((tm, d), lambda i: (i, 0)),
            pl.BlockSpec((None, 6, d), _stream_of(n_lat_tiles)),
        ] + r_in,
        out_specs=[pl.BlockSpec((tm, d), lambda i: (i, 0))] + r_out,
        out_shape=[jax.ShapeDtypeStruct((n_tiles * tm, d), F32)] + r_shape,
        compiler_params=_cparams(("arbitrary",)),
        name="odd_out_proj",
    )(a, w_o, x, mod, *route)


CHUNK = 8
TILE_BUF = 2 * ROW_TILE + N_EXPERTS * CHUNK
MAX_TILE_CHUNKS = TILE_BUF // CHUNK
FFN_BUFS = 4
COMBINE_BUFS = 3


def _chunk_rows(c):
    return pl.ds(pl.multiple_of(c * CHUNK, CHUNK), CHUNK)


def _wait_rows(copy_of_rows, n_chunks):
    bit = 1
    while bit <= MAX_TILE_CHUNKS:
        @pl.when((n_chunks & bit) != 0)
        def _(bit=bit):
            copy_of_rows(bit * CHUNK).wait()
        bit *= 2


def _dispatch_kernel(tab_ref, lused_ref, fill_ref, h_ref, pos_ref, gate_ref, xb_ref, hs, zbuf, sem, fsem, *, n_tiles,
                     n_blocks):
    i = pl.program_id(0)
    slot = i % 2
    tm = h_ref.shape[0]

    def tail_copy(c):
        return pltpu.make_async_copy(zbuf.at[pl.ds(0, CHUNK), :], xb_ref.at[_chunk_rows(c), :], fsem)

    def block_copy(b):
        return pltpu.make_async_copy(zbuf, xb_ref.at[pl.ds(pl.multiple_of(b * MOE_ROWS, MOE_ROWS), MOE_ROWS), :], fsem)

    def fill(start):
        def tail(e, c):
            st, n = fill_ref[e], fill_ref[N_EXPERTS + e]

            def one(c2, cc):
                cp = tail_copy(st + c2)
                cp.start() if start else cp.wait()
                return cc
            return lax.fori_loop(0, n, one, c)
        lax.fori_loop(0, N_EXPERTS, tail, 0)

        def blk(b, c):
            cp = block_copy(b)
            cp.start() if start else cp.wait()
            return c
        lax.fori_loop(fill_ref[2 * N_EXPERTS], n_blocks, blk, 0)

    @pl.when(i == 0)
    def _():
        zbuf[...] = jnp.zeros_like(zbuf)
        fill(True)

    pos = pos_ref[...]
    r = lax.broadcasted_iota(I32, (TILE_BUF, tm), 0)
    first = r == pos[0:1, :]
    second = r == pos[1:2, :]
    onehot = jnp.where(jnp.logical_or(first, second), 1.0, 0.0).astype(BF16)
    d = h_ref.shape[1]
    hs[slot, :, 0:d] = jnp.dot(onehot, h_ref[...], preferred_element_type=F32).astype(BF16)
    gate = gate_ref[...]
    lane = lax.broadcasted_iota(I32, (tm, LANES), 1)

    def hi_lo_lanes(g):
        hi = g.astype(BF16).astype(F32)
        return jnp.where(lane == 0, hi, jnp.where(lane == 1, g - hi, 0.0)).astype(BF16)
    gs = (jnp.dot(jnp.where(first, 1.0, 0.0).astype(BF16), hi_lo_lanes(gate[:, 0:1]), preferred_element_type=F32)
          + jnp.dot(jnp.where(second, 1.0, 0.0).astype(BF16), hi_lo_lanes(gate[:, 1:2]), preferred_element_type=F32))
    hs[slot, :, d:d + LANES] = gs.astype(BF16)

    def chunk_copy(sl, src, dst):
        return pltpu.make_async_copy(hs.at[sl, _chunk_rows(src), :], xb_ref.at[_chunk_rows(dst), :], sem.at[sl])

    base = i * MAX_TILE_CHUNKS

    def per_chunk(c, cc):
        chunk_copy(slot, c, tab_ref[base + c]).start()
        return cc
    lax.fori_loop(0, lused_ref[i], per_chunk, 0)

    def wait_chunks(sl, n):
        _wait_rows(lambda rows: pltpu.make_async_copy(hs.at[sl, pl.ds(0, rows), :], xb_ref.at[pl.ds(0, rows), :],
                                                      sem.at[sl]), n)

    @pl.when(i > 0)
    def _():
        wait_chunks(1 - slot, lused_ref[jnp.maximum(i - 1, 0)])

    @pl.when(i == n_tiles - 1)
    def _():
        wait_chunks(slot, lused_ref[i])
        fill(False)


def _dispatch(h, pos_rows, gate_cols, tab, lused, fill, n_blocks):
    t, d = h.shape
    dw = d + LANES
    tm = ROW_TILE
    n_tiles = t // tm
    return pl.pallas_call(
        functools.partial(_dispatch_kernel, n_tiles=n_tiles, n_blocks=n_blocks),
        grid_spec=pltpu.PrefetchScalarGridSpec(
            num_scalar_prefetch=3,
            grid=(n_tiles,),
            in_specs=[
                pl.BlockSpec((tm, d), lambda i, *_: (i, 0)),
                pl.BlockSpec((8, tm), lambda i, *_: (0, i)),
                pl.BlockSpec((tm, 8), lambda i, *_: (i, 0)),
            ],
            out_specs=pl.BlockSpec(memory_space=pl.ANY),
            scratch_shapes=[
                pltpu.VMEM((2, TILE_BUF, dw), BF16),
                pltpu.VMEM((MOE_ROWS, dw), BF16),
                pltpu.SemaphoreType.DMA((2,)),
                pltpu.SemaphoreType.DMA(()),
            ],
        ),
        out_shape=jax.ShapeDtypeStruct((n_blocks * MOE_ROWS, dw), BF16),
        compiler_params=_cparams(("arbitrary",)),
        name="moe_dispatch",
    )(tab, lused, fill, h, pos_rows, gate_cols)


def _ffn_kernel(b0_ref, nb_ref, nv_ref, fill_ref, xb_ref, w1_ref, w3_ref, w2_ref, yb_ref,
                w1s, w3s, w2s, xbuf, ybuf, zbuf, sem_in, sem_out, fsem, *, n_blocks):
    e = pl.program_id(0)
    w1s[...] = w1_ref[...].astype(BF16)
    w3s[...] = w3_ref[...].astype(BF16)
    w2s[...] = w2_ref[...].astype(BF16)
    b0, nb = b0_ref[e], nb_ref[e]
    used = fill_ref[0]

    def rows(b):
        return pl.ds(pl.multiple_of(b * MOE_ROWS, MOE_ROWS), MOE_ROWS)

    def x_copy(g, sl):
        return pltpu.make_async_copy(xb_ref.at[rows(g), :], xbuf.at[sl], sem_in.at[sl])

    def y_copy(g, sl):
        return pltpu.make_async_copy(ybuf.at[sl], yb_ref.at[rows(g), :], sem_out.at[sl])

    @pl.when(e == 0)
    def _():
        for k in range(FFN_BUFS - 1):
            @pl.when(k < used)
            def _(k=k):
                x_copy(k, k).start(priority=1)

    def block(j, c):
        g = b0 + j
        sl = g % FFN_BUFS
        x_copy(g, sl).wait()
        ahead = g + (FFN_BUFS - 1)

        @pl.when(ahead < used)
        def _():
            x_copy(ahead, ahead % FFN_BUFS).start(priority=1)

        @pl.when(g >= FFN_BUFS)
        def _():
            y_copy(g - FFN_BUFS, sl).wait()

        d = w1s.shape[0]
        x = xbuf[sl, :, 0:d]
        row = lax.broadcasted_iota(I32, x.shape, 0)
        xb = jnp.where(row < nv_ref[g], x, jnp.zeros_like(x))
        a = jnp.dot(xb, w1s[...], preferred_element_type=F32)
        b = jnp.dot(xb, w3s[...], preferred_element_type=F32)
        hid = (a * jax.nn.sigmoid(a) * b).astype(BF16)
        gl = xbuf[sl, :, d:d + LANES].astype(F32)
        gate = gl[:, 0:1] + gl[:, 1:2]
        ybuf[sl] = (jnp.dot(hid, w2s[...], preferred_element_type=F32) * gate).astype(BF16)
        y_copy(g, sl).start(priority=1)
        return c
    lax.fori_loop(0, nb, block, 0)

    @pl.when(e == N_EXPERTS - 1)
    def _():
        for k in range(FFN_BUFS):
            @pl.when(used > k)
            def _(k=k):
                y_copy(used - 1 - k, (used - 1 - k) % FFN_BUFS).wait()
        zbuf[...] = jnp.zeros_like(zbuf)

        def z_copy(b):
            return pltpu.make_async_copy(zbuf, yb_ref.at[rows(b), :], fsem)

        def start(b, c):
            z_copy(b).start()
            return c

        def wait(b, c):
            z_copy(b).wait()
            return c
        lax.fori_loop(fill_ref[0], n_blocks, start, 0)
        lax.fori_loop(fill_ref[0], n_blocks, wait, 0)


def _expert_ffn(xb, blk_start, blk_count, n_valid, used_blocks, w1, w3, w2, layer):
    r = xb.shape[0]
    d, f = w1.shape[2], w1.shape[3]
    bm = MOE_ROWS
    return pl.pallas_call(
        functools.partial(_ffn_kernel, n_blocks=r // bm),
        grid_spec=pltpu.PrefetchScalarGridSpec(
            num_scalar_prefetch=4,
            grid=(N_EXPERTS,),
            in_specs=[
                pl.BlockSpec(memory_space=pl.ANY),
                pl.BlockSpec((None, None, d, f), lambda e, *_: (layer, e, 0, 0)),
                pl.BlockSpec((None, None, d, f), lambda e, *_: (layer, e, 0, 0)),
                pl.BlockSpec((None, None, f, d), lambda e, *_: (layer, e, 0, 0)),
            ],
            out_specs=pl.BlockSpec(memory_space=pl.ANY),
            scratch_shapes=[
                pltpu.VMEM((d, f), BF16), pltpu.VMEM((d, f), BF16), pltpu.VMEM((f, d), BF16),
                pltpu.VMEM((FFN_BUFS, bm, d + LANES), BF16), pltpu.VMEM((FFN_BUFS, bm, d), BF16),
                pltpu.VMEM((bm, d), BF16),
                pltpu.SemaphoreType.DMA((FFN_BUFS,)), pltpu.SemaphoreType.DMA((FFN_BUFS,)), pltpu.SemaphoreType.DMA(()),
            ],
        ),
        out_shape=jax.ShapeDtypeStruct((r, d), BF16),
        compiler_params=_cparams(("arbitrary",)),
        name="moe_expert_mlp",
    )(blk_start, blk_count, n_valid, used_blocks, xb, w1, w3, w2)


def _combine_kernel(tab_ref, lused_ref, yb_ref, pos_ref, x_ref, mod_ref, o_ref, ys, sem, *, n_tiles):
    i = pl.program_id(0)
    slot = i % COMBINE_BUFS
    tm = x_ref.shape[0]

    def chunk_copy(sl, src, dst):
        return pltpu.make_async_copy(yb_ref.at[_chunk_rows(src), :], ys.at[sl, _chunk_rows(dst), :], sem.at[sl])

    def fetch(tile, sl):
        base = tile * MAX_TILE_CHUNKS

        def per_chunk(c, cc):
            chunk_copy(sl, tab_ref[base + c], c).start()
            return cc
        lax.fori_loop(0, lused_ref[tile], per_chunk, 0)

    @pl.when(i == 0)
    def _():
        for k in range(COMBINE_BUFS - 1):
            if k < n_tiles:
                fetch(k, k)

    ahead = i + (COMBINE_BUFS - 1)

    @pl.when(ahead < n_tiles)
    def _():
        fetch(jnp.minimum(ahead, n_tiles - 1), ahead % COMBINE_BUFS)

    _wait_rows(lambda rows: pltpu.make_async_copy(yb_ref.at[pl.ds(0, rows), :], ys.at[slot, pl.ds(0, rows), :],
                                                  sem.at[slot]), lused_ref[i])

    y = ys[slot]
    used = lused_ref[i] * CHUNK
    rowi = lax.broadcasted_iota(I32, y.shape, 0)
    y16 = jnp.where(rowi < used, y, jnp.zeros_like(y))
    pos = pos_ref[...]
    lane = lax.broadcasted_iota(I32, (tm, TILE_BUF), 1)
    pick = jnp.where(jnp.logical_or(lane == pos[:, 0:1], lane == pos[:, 1:2]), 1.0, 0.0).astype(BF16)
    mix = jnp.dot(pick, y16, preferred_element_type=F32)
    o_ref[...] = x_ref[...] + mod_ref[5:6, :] * mix


def _combine(yb, pos_cols, tab, lused, x, mod, n_lat_tiles, n_tiles):
    d = x.shape[1]
    tm = ROW_TILE
    return pl.pallas_call(
        functools.partial(_combine_kernel, n_tiles=n_tiles),
        grid_spec=pltpu.PrefetchScalarGridSpec(
            num_scalar_prefetch=2,
            grid=(n_tiles,),
            in_specs=[
                pl.BlockSpec(memory_space=pl.ANY),
                pl.BlockSpec((tm, 8), lambda i, *_: (i, 0)),
                pl.BlockSpec((tm, d), lambda i, *_: (i, 0)),
                pl.BlockSpec((None, 6, d), lambda i, *_: (jnp.where(i >= n_lat_tiles, 1, 0), 0, 0)),
            ],
            out_specs=pl.BlockSpec((tm, d), lambda i, *_: (i, 0)),
            scratch_shapes=[pltpu.VMEM((COMBINE_BUFS, TILE_BUF, d), BF16), pltpu.SemaphoreType.DMA((COMBINE_BUFS,))],
        ),
        out_shape=jax.ShapeDtypeStruct((n_tiles * tm, d), F32),
        compiler_params=_cparams(("arbitrary",)),
        name="moe_combine",
    )(tab, lused, yb, pos_cols, x, mod)


def _moe_layer(x, routing, mod, w1, w3, w2, layer, n_lat_tiles, n_tiles):
    tm = ROW_TILE
    rows = n_tiles * tm
    cpb = MOE_ROWS // CHUNK
    h, ri, rg, cnt3 = routing
    cnt = cnt3[:, :, 0].astype(I32)
    nch = (cnt + CHUNK - 1) // CHUNK
    lbase = jnp.cumsum(nch, axis=1) - nch
    lused = jnp.sum(nch, axis=1).astype(I32)
    tot = jnp.sum(nch, axis=0)
    reg = (tot + cpb - 1) // cpb * cpb
    gend = jnp.cumsum(reg)
    gstart = gend - reg
    gpos = gstart[None, :] + jnp.cumsum(nch, axis=0) - nch
    rows_max = 2 * rows + n_tiles * N_EXPERTS * (CHUNK - 1) + N_EXPERTS * (MOE_ROWS - CHUNK)
    n_blocks = -(-rows_max // MOE_ROWS)
    ex = jnp.arange(N_EXPERTS, dtype=I32)
    blk0 = jnp.arange(n_blocks, dtype=I32) * cpb
    block_exp = jnp.minimum(jnp.sum((gend[None, :] <= blk0[:, None]).astype(I32), axis=1), N_EXPERTS - 1)
    sel = block_exp[:, None] == ex[None, :]
    tot_b = jnp.sum(jnp.where(sel, tot[None, :], 0), axis=1)
    st_b = jnp.sum(jnp.where(sel, gstart[None, :], 0), axis=1)
    n_valid = jnp.clip((tot_b - (blk0 - st_b)) * CHUNK, 0, MOE_ROWS).astype(I32)
    lb_tok = jnp.repeat(lbase, tm, axis=0)
    at = lambda e: jnp.sum(jnp.where(e[:, None] == ex[None, :], lb_tok, 0), axis=1)
    pos1 = CHUNK * at(ri[0]) + ri[2]
    pos2 = CHUNK * at(ri[1]) + ri[3]
    zero = jnp.zeros_like(pos1)
    pos_rows = jnp.stack([pos1, pos2] + [zero] * 6, axis=0).astype(I32)
    slot_id = jnp.arange(MAX_TILE_CHUNKS, dtype=I32)
    owner = jnp.sum((lbase + nch)[:, None, :] <= slot_id[None, :, None], axis=2)
    own = jnp.minimum(owner, N_EXPERTS - 1)[:, :, None] == ex[None, None, :]
    tab = jnp.sum(jnp.where(own, (gpos - lbase)[:, None, :], 0), axis=2) + slot_id[None, :]
    tab = tab.astype(I32).reshape(-1)
    fill = jnp.concatenate([gstart + tot, reg - tot, gend[-1:] // cpb]).astype(I32)
    xb = _dispatch(h, pos_rows, rg.T, tab, lused, fill, n_blocks)
    yb = _expert_ffn(xb, (gstart // cpb).astype(I32), (reg // cpb).astype(I32), n_valid,
                     (gend[-1:] // cpb).astype(I32), w1, w3, w2, layer)
    return _combine(yb, pos_rows.T, tab, lused, x, mod, n_lat_tiles, n_tiles)


def _router_matrix(w_rg, b_rg, w_re, b_re):
    d = w_rg.shape[0]
    wr = jnp.zeros((d, LANES), F32)
    wr = wr.at[:, 0:N_GROUPS].set(w_rg.astype(F32))
    wr = wr.at[:, ROUTER_EXPERT_ROW0:ROUTER_EXPERT_ROW0 + N_EXPERTS].set(w_re.astype(F32))
    wr = jnp.concatenate(_split_bf16(wr), axis=1)
    br = jnp.zeros((LANES,), F32)
    br = br.at[0:N_GROUPS].set(b_rg.astype(F32))
    br = br.at[ROUTER_EXPERT_ROW0:ROUTER_EXPERT_ROW0 + N_EXPERTS].set(b_re.astype(F32))
    return wr, jnp.broadcast_to(br[:, None], (LANES, LANES))


def kernel(x, c, ctx, c_ctx, w_mod, b_mod, norm_mix_g, norm_ffn_g, w_in_even, conv_w, w_out_even, w_qkv, q_norm_g,
           k_norm_g, sink_logit, w_o, w_router_g, b_router_g, w_router_e, b_router_e, w1, w3, w2):
    bsz, l, d = x.shape
    lc = ctx.shape[1]
    assert bsz == 1, "one sample per call"
    tm = ROW_TILE
    assert l % tm == 0 and lc % tm == 0 and l % lc == 0 and l % (DFT_MINOR * 8) == 0
    depth = w_mod.shape[0]
    t = l + lc
    assert t % DFT_MINOR == 0
    nl, nt = l // tm, t // tm

    xs = (x.reshape(l, d), ctx.reshape(lc, d))
    mod_all = _modulation(c, c_ctx, w_mod, b_mod).reshape(depth, 2, 6, d)
    cs, m1, a_tab, mc = _dft_tables(l, lc)
    rope = _rope_tables(l, t)
    segs = _segment_matrices()
    attn_bias = _attn_bias(l)
    tri = jnp.asarray(np.triu(np.ones((tm, tm), np.float32), 1)).astype(BF16)
    qd = N_HEADS * HEAD_DIM

    for layer in range(depth):
        last = layer == depth - 1
        j = layer // 2
        mod = mod_all[layer]
        g_mix = norm_mix_g[layer].reshape(1, d)
        g_ffn = norm_ffn_g[layer].reshape(1, d)
        wr, br = _router_matrix(w_router_g[layer], b_router_g[layer], w_router_e[layer], b_router_e[layer])
        route = (g_ffn, wr, br, tri)
        if layer % 2 == 0:
            z, bg, u = _even_in(xs, g_mix, mod, w_in_even[j].astype(BF16), cs, nl, nt)
            f_lat, f_ctx = _fourier_seq(z, l, m1, a_tab, mc)
            xs, *routing = _even_out(f_lat, f_ctx, bg, u, conv_w[j], w_out_even[j].astype(BF16), xs, mod, route, nl)
        else:
            wq = w_qkv[j]
            w_all = wq.astype(BF16)
            qg = jnp.tile(q_norm_g[j], N_HEADS).reshape(1, qd)
            kg = jnp.tile(k_norm_g[j], N_KV_HEADS).reshape(1, N_KV_HEADS * HEAD_DIM)
            q, k, v = _odd_in(xs, g_mix, mod, w_all, qg, kg, segs, rope, nl, nt)
            n_out = nl if last else nt
            att = _attention(q, k, v, sink_logit[j].astype(F32) * LOG2E, attn_bias, l,
                             n_out * (tm // ATT_BLOCK))
            xs, *routing = _odd_out(att, w_o[j].astype(BF16), xs, mod, route, nl, n_out)
        n_moe = nl if last else nt
        xs = _moe_layer(xs, routing, mod, w1, w3, w2, layer, nl, n_moe)
    return xs[:l].reshape(bsz, l, d)
```

```python
import functools
import math

import numpy as np
import jax
import jax.numpy as jnp
from jax import lax
from jax.experimental import pallas as pl
from jax.experimental.pallas import tpu as pltpu

F32 = jnp.float32
BF16 = jnp.bfloat16
I32 = jnp.int32

EPS = 1e-6
NEG_INF = -1e30

GRID_W = 64
FOURIER_GROUPS = 4
FOURIER_GROUP_DIM = 128
FOURIER_WIDTH = FOURIER_GROUPS * FOURIER_GROUP_DIM
CONV_WIDTH = 512
N_HEADS = 16
N_KV_HEADS = 4
GQA_GROUP = N_HEADS // N_KV_HEADS
HEAD_DIM = 64
WINDOW = 128
ROPE_BASE = 10000.0
ROPE_PAIRS = HEAD_DIM // 4
N_GROUPS = 4
EXPERTS_PER_GROUP = 8
N_EXPERTS = N_GROUPS * EXPERTS_PER_GROUP

LANES = 128
ROW_TILE = 256
ATT_BLOCK = 128
MOE_ROWS = 256
DFT_MINOR = 128
DFT_MAJOR_COLS = 8
VMEM_LIMIT = 48 * 1024 * 1024

LOG2E = math.log2(math.e)


def _cparams(sem):
    return pltpu.CompilerParams(dimension_semantics=sem, vmem_limit_bytes=VMEM_LIMIT)


def _split_bf16(x):
    hi = x.astype(BF16)
    lo = (x - hi.astype(F32)).astype(BF16)
    return hi, lo


def _mod_kernel(ct_ref, w_ref, b_ref, o_ref):
    ct = ct_ref[...]
    s = ct * jax.nn.sigmoid(ct)
    w = w_ref[...]
    r0 = jnp.sum(w * s[:, 0:1], axis=0, keepdims=True)
    r1 = jnp.sum(w * s[:, 1:2], axis=0, keepdims=True)
    o_ref[...] = jnp.concatenate([r0, r1], axis=0) + b_ref[...]


def _modulation(c, c_ctx, w_mod, b_mod):
    depth, d, n = w_mod.shape
    tn = 1024
    ct = jnp.stack([c.reshape(d), c_ctx.reshape(d)], axis=1)
    return pl.pallas_call(
        _mod_kernel,
        grid=(depth, n // tn),
        in_specs=[
            pl.BlockSpec((d, 2), lambda l, j: (0, 0)),
            pl.BlockSpec((None, d, tn), lambda l, j: (l, 0, j)),
            pl.BlockSpec((None, 1, tn), lambda l, j: (l, 0, j)),
        ],
        out_specs=pl.BlockSpec((None, 2, tn), lambda l, j: (l, 0, j)),
        out_shape=jax.ShapeDtypeStruct((depth, 2, n), F32),
        compiler_params=_cparams(("arbitrary", "arbitrary")),
        name="modulation",
    )(ct, w_mod, b_mod.reshape(depth, 1, n))


def _norm_mod(x, g, shift, scale):
    ms = jnp.mean(x * x, axis=-1, keepdims=True)
    y = x * lax.rsqrt(ms + EPS) * g
    return y * (1.0 + scale) + shift


def _stream_of(n_lat_tiles):
    return lambda i: (jnp.where(i >= n_lat_tiles, 1, 0), 0, 0)


def _row_sources(x, n_lat_tiles, n_tiles, d):
    tm = ROW_TILE
    lat = pl.BlockSpec((tm, d), lambda i: (jnp.minimum(i, n_lat_tiles - 1), 0))
    if isinstance(x, tuple):
        ctx = pl.BlockSpec((tm, d), lambda i: (jnp.clip(i - n_lat_tiles, 0, n_tiles - n_lat_tiles - 1), 0))
        return x, [lat, ctx]
    return (x, x), [lat, pl.BlockSpec((tm, d), lambda i: (jnp.clip(i, n_lat_tiles, n_tiles - 1), 0))]


def _pick_rows(xl_ref, xc_ref, n_lat_tiles):
    is_ctx = jnp.full(xl_ref.shape, pl.program_id(0), I32) >= n_lat_tiles
    return jnp.where(is_ctx, xc_ref[...], xl_ref[...])


def _even_in_kernel(xl_ref, xc_ref, g_ref, mod_ref, w_ref, cs_ref, z_ref, bg_ref, u_ref, *, n_lat_tiles):
    h = _norm_mod(_pick_rows(xl_ref, xc_ref, n_lat_tiles), g_ref[...], mod_ref[0:1, :], mod_ref[1:2, :])
    p = jnp.dot(h.astype(BF16), w_ref[...], preferred_element_type=F32)
    cs = cs_ref[...]
    for grp in range(FOURIER_GROUPS):
        lo = grp * FOURIER_GROUP_DIM
        a = p[:, lo:lo + FOURIER_GROUP_DIM].astype(BF16)
        z = jnp.dot(a, cs, preferred_element_type=F32)
        z_ref[0, :, lo:lo + FOURIER_GROUP_DIM] = z[:, :FOURIER_GROUP_DIM]
        z_ref[1, :, lo:lo + FOURIER_GROUP_DIM] = z[:, FOURIER_GROUP_DIM:]
    o = FOURIER_WIDTH
    bg_ref[...] = p[:, o:o + CONV_WIDTH].astype(BF16)
    u_ref[...] = (p[:, o + CONV_WIDTH:o + 2 * CONV_WIDTH] * p[:, o + 2 * CONV_WIDTH:]).astype(BF16)


def _even_in(x, g, mod, w_in, cs, n_lat_tiles, n_tiles):
    d, n = w_in.shape
    tm = ROW_TILE
    t = n_tiles * tm
    xs, x_specs = _row_sources(x, n_lat_tiles, n_tiles, d)
    return pl.pallas_call(
        functools.partial(_even_in_kernel, n_lat_tiles=n_lat_tiles),
        grid=(n_tiles,),
        in_specs=x_specs + [
            pl.BlockSpec((1, d), lambda i: (0, 0)),
            pl.BlockSpec((None, 6, d), _stream_of(n_lat_tiles)),
            pl.BlockSpec((d, n), lambda i: (0, 0)),
            pl.BlockSpec(cs.shape, lambda i: (0, 0)),
        ],
        out_specs=[
            pl.BlockSpec((2, tm, FOURIER_WIDTH), lambda i: (0, i, 0)),
            pl.BlockSpec((tm, CONV_WIDTH), lambda i: (i, 0)),
            pl.BlockSpec((tm, CONV_WIDTH), lambda i: (i, 0)),
        ],
        out_shape=[
            jax.ShapeDtypeStruct((2, t, FOURIER_WIDTH), F32),
            jax.ShapeDtypeStruct((t, CONV_WIDTH), BF16),
            jax.ShapeDtypeStruct((t, CONV_WIDTH), BF16),
        ],
        compiler_params=_cparams(("arbitrary",)),
        name="even_in_proj",
    )(*xs, g, mod, w_in, cs)


def _dft_major_kernel(m_ref, z_ref, o_ref):
    shape = z_ref.shape
    z = z_ref[...].reshape(shape[0] * shape[1] * shape[2], shape[3]).astype(BF16)
    o_ref[...] = jnp.dot(m_ref[...], z, preferred_element_type=F32).reshape(shape)


def _dft_major(m1, z4, n1):
    c = z4.shape[-1]
    cb = DFT_MAJOR_COLS
    return pl.pallas_call(
        _dft_major_kernel,
        grid=(DFT_MINOR // cb,),
        in_specs=[pl.BlockSpec(m1.shape, lambda j: (0, 0)), pl.BlockSpec((2, n1, cb, c), lambda j: (0, 0, j, 0))],
        out_specs=pl.BlockSpec((2, n1, cb, c), lambda j: (0, 0, j, 0)),
        out_shape=jax.ShapeDtypeStruct((2, n1, DFT_MINOR, c), F32),
        compiler_params=_cparams(("arbitrary",)),
        name="dft_major",
    )(m1, z4)


def _dft_minor_kernel(a_ref, g_ref, o_ref):
    for j in range(a_ref.shape[0]):
        gcat = jnp.concatenate([g_ref[0, j], g_ref[1, j]], axis=0).astype(BF16)
        o_ref[:, j, :] = jnp.dot(a_ref[j], gcat, preferred_element_type=F32)


def _dft_minor(a_tab, g4):
    n1, m, k2 = a_tab.shape
    c = g4.shape[-1]
    kb = 8
    return pl.pallas_call(
        _dft_minor_kernel,
        grid=(n1 // kb,),
        in_specs=[
            pl.BlockSpec((kb, m, k2), lambda i: (i, 0, 0)),
            pl.BlockSpec((2, kb, DFT_MINOR, c), lambda i: (0, i, 0, 0)),
        ],
        out_specs=pl.BlockSpec((DFT_MINOR, kb, c), lambda i: (0, i, 0)),
        out_shape=jax.ShapeDtypeStruct((DFT_MINOR, n1, c), F32),
        compiler_params=_cparams(("arbitrary",)),
        name="dft_minor",
    )(a_tab, g4)


def _dft_ctx_kernel(a_ref, z_ref, o_ref):
    o_ref[...] = jnp.dot(a_ref[...], z_ref[...].astype(BF16), preferred_element_type=F32)


def _dft_ctx(mc, zc):
    lc, c = mc.shape[0], zc.shape[1]
    return pl.pallas_call(
        _dft_ctx_kernel,
        grid=(1,),
        in_specs=[pl.BlockSpec(mc.shape, lambda i: (0, 0)), pl.BlockSpec(zc.shape, lambda i: (0, 0))],
        out_specs=pl.BlockSpec((lc, c), lambda i: (0, 0)),
        out_shape=jax.ShapeDtypeStruct((lc, c), F32),
        compiler_params=_cparams(("arbitrary",)),
        name="dft_context",
    )(mc, zc)


def _dft_tables(l, lc):
    gd = FOURIER_GROUP_DIM
    kk = np.arange(gd)
    ang = 2.0 * np.pi * ((kk[:, None] * kk[None, :]) % gd) / gd
    cs = np.concatenate([np.cos(ang), -np.sin(ang)], axis=1)
    n1 = l // DFT_MINOR
    k1 = np.arange(n1)
    ang1 = 2.0 * np.pi * ((k1[:, None] * k1[None, :]) % n1) / n1
    c1, s1 = np.cos(ang1), np.sin(ang1)
    m1 = np.kron(np.block([[c1, s1], [-s1, c1]]), np.eye(DFT_MAJOR_COLS))
    l2 = np.arange(DFT_MINOR)
    kfull = k1[:, None, None] + n1 * l2[None, :, None]
    ang2 = 2.0 * np.pi * ((kfull * l2[None, None, :]) % l) / l
    sc = 1.0 / math.sqrt(l * gd)
    a_tab = np.concatenate([np.cos(ang2), np.sin(ang2)], axis=2) * sc
    kc = np.arange(lc)
    angc = 2.0 * np.pi * ((kc[:, None] * kc[None, :]) % lc) / lc
    mc = np.concatenate([np.cos(angc), np.sin(angc)], axis=1) / math.sqrt(lc * gd)
    as_bf16 = lambda v: jnp.asarray(v, F32).astype(BF16)
    return as_bf16(cs), as_bf16(m1), as_bf16(a_tab), as_bf16(mc)


def _fourier_seq(z, l, m1, a_tab, mc):
    t, c = z.shape[1], z.shape[2]
    lc = t - l
    n1 = l // DFT_MINOR
    g4 = _dft_major(m1, z.reshape(2, t // DFT_MINOR, DFT_MINOR, c), n1)
    f_lat = _dft_minor(a_tab, g4).reshape(l, c)
    return f_lat, _dft_ctx(mc, z[:, l:].reshape(2 * lc, c))


ROUTER_EXPERT_ROW0 = 8


def _route_tile(x, g_ref, mod_ref, wr_ref, br_ref, tri_ref, h_ref, ri_ref, rg_ref, cnt_ref):
    h = _norm_mod(x, g_ref[...], mod_ref[3:4, :], mod_ref[4:5, :])
    h_ref[...] = h.astype(BF16)
    tm = h.shape[0]
    h_hi, h_lo = _split_bf16(h)
    wr = wr_ref[...]
    both = jnp.dot(h_hi, wr, preferred_element_type=F32)
    logits = (both[:, :LANES] + both[:, LANES:]
              + jnp.dot(h_lo, wr[:, :LANES], preferred_element_type=F32))
    logits = logits.T + br_ref[:, 0:1]
    gl = logits[0:N_GROUPS]
    gmax = jnp.max(gl, axis=0, keepdims=True)
    gi = lax.broadcasted_iota(I32, gl.shape, 0)
    g_idx = jnp.min(jnp.where(gl == gmax, gi, N_GROUPS), axis=0, keepdims=True)
    g_val = 1.0 / jnp.sum(jnp.exp(gl - gmax), axis=0, keepdims=True)
    e_in = logits[ROUTER_EXPERT_ROW0:ROUTER_EXPERT_ROW0 + EXPERTS_PER_GROUP]
    for grp in range(1, N_GROUPS):
        lo = ROUTER_EXPERT_ROW0 + grp * EXPERTS_PER_GROUP
        e_in = jnp.where(g_idx == grp, logits[lo:lo + EXPERTS_PER_GROUP], e_in)
    ei = lax.broadcasted_iota(I32, e_in.shape, 0)
    v1 = jnp.max(e_in, axis=0, keepdims=True)
    i1 = jnp.min(jnp.where(e_in == v1, ei, EXPERTS_PER_GROUP), axis=0, keepdims=True)
    rest = jnp.where(ei == i1, -jnp.inf, e_in)
    v2 = jnp.max(rest, axis=0, keepdims=True)
    i2 = jnp.min(jnp.where(rest == v2, ei, EXPERTS_PER_GROUP), axis=0, keepdims=True)
    w2 = jnp.exp(v2 - v1)
    gate1 = g_val / (1.0 + w2)
    gate2 = g_val * w2 / (1.0 + w2)
    e1 = g_idx * EXPERTS_PER_GROUP + i1
    e2 = g_idx * EXPERTS_PER_GROUP + i2
    xi = lax.broadcasted_iota(I32, (N_EXPERTS, tm), 0)
    oh1 = xi == e1
    oh2 = xi == e2
    oh = oh1.astype(F32) + oh2.astype(F32)
    before = jnp.dot(oh.astype(BF16), tri_ref[...], preferred_element_type=F32)
    rank1 = jnp.sum(jnp.where(oh1, before, 0.0), axis=0, keepdims=True)
    rank2 = jnp.sum(jnp.where(oh2, before, 0.0), axis=0, keepdims=True)
    cnt_ref[...] = jnp.broadcast_to(jnp.sum(oh, axis=1, keepdims=True), cnt_ref.shape)
    orow = lax.broadcasted_iota(I32, (8, tm), 0)
    ri_ref[...] = jnp.where(orow == 0, e1, jnp.where(orow == 1, e2, jnp.where(
        orow == 2, rank1.astype(I32), jnp.where(orow == 3, rank2.astype(I32), 0))))
    rg_ref[...] = jnp.where(orow == 0, gate1, jnp.where(orow == 1, gate2, 0.0))


def _route_specs(route, d, n_tiles):
    tm = ROW_TILE
    const = lambda a: pl.BlockSpec(a.shape, lambda i: (0,) * a.ndim)
    in_specs = [const(a) for a in route]
    out_specs = [
        pl.BlockSpec((tm, d), lambda i: (i, 0)),
        pl.BlockSpec((8, tm), lambda i: (0, i)),
        pl.BlockSpec((8, tm), lambda i: (0, i)),
        pl.BlockSpec((None, N_EXPERTS, LANES), lambda i: (i, 0, 0)),
    ]
    rows = n_tiles * tm
    out_shape = [
        jax.ShapeDtypeStruct((rows, d), BF16),
        jax.ShapeDtypeStruct((8, rows), I32),
        jax.ShapeDtypeStruct((8, rows), F32),
        jax.ShapeDtypeStruct((n_tiles, N_EXPERTS, LANES), F32),
    ]
    return in_specs, out_specs, out_shape


def _even_out_kernel(fl_ref, fc_ref, bg_ref, u_ref, up_ref, un_ref, cw_ref, w_ref, xl_ref, xc_ref, mod_ref,
                     gf_ref, wr_ref, br_ref, tri_ref, o_ref, h_ref, ri_ref, rg_ref, cnt_ref, *, n_lat_tiles, n_tiles):
    i = pl.program_id(0)
    tm = u_ref.shape[0]
    u = u_ref[...].astype(F32)
    row = lax.broadcasted_iota(I32, u.shape, 0)
    first = jnp.logical_or(i == 0, i == n_lat_tiles)
    last = jnp.logical_or(i == n_lat_tiles - 1, i == n_tiles - 1)
    hb = up_ref.shape[0]
    halo_p = up_ref[...].astype(F32)[hb - 1:hb, :] * jnp.where(first, 0.0, 1.0)
    halo_n = un_ref[...].astype(F32)[0:1, :] * jnp.where(last, 0.0, 1.0)
    u_prev = jnp.where(row == 0, halo_p, pltpu.roll(u, 1, axis=0))
    u_next = jnp.where(row == tm - 1, halo_n, pltpu.roll(u, tm - 1, axis=0))
    cw = cw_ref[...]
    y = bg_ref[...].astype(F32) * (cw[0:1, :] * u_prev + cw[1:2, :] * u + cw[2:3, :] * u_next)
    is_ctx = jnp.full(fl_ref.shape, i, I32) >= n_lat_tiles
    f = jnp.where(is_ctx, fc_ref[...], fl_ref[...])
    acc = jnp.dot(f.astype(BF16), w_ref[0:FOURIER_WIDTH, :], preferred_element_type=F32)
    acc += jnp.dot(y.astype(BF16), w_ref[FOURIER_WIDTH:, :], preferred_element_type=F32)
    x_new = _pick_rows(xl_ref, xc_ref, n_lat_tiles) + mod_ref[2:3, :] * acc
    o_ref[...] = x_new
    _route_tile(x_new, gf_ref, mod_ref, wr_ref, br_ref, tri_ref, h_ref, ri_ref, rg_ref, cnt_ref)


def _even_out(f_lat, f_ctx, bg, u, conv_w, w_out, x, mod, route, n_lat_tiles):
    t, d = u.shape[0], w_out.shape[1]
    tm = ROW_TILE
    hb = 16
    n_tiles = t // tm
    r = tm // hb
    r_in, r_out, r_shape = _route_specs(route, d, n_tiles)
    xs, x_specs = _row_sources(x, n_lat_tiles, n_tiles, d)
    return pl.pallas_call(
        functools.partial(_even_out_kernel, n_lat_tiles=n_lat_tiles, n_tiles=n_tiles),
        grid=(n_tiles,),
        in_specs=[
            pl.BlockSpec((tm, FOURIER_WIDTH), lambda i: (jnp.minimum(i, n_lat_tiles - 1), 0)),
            pl.BlockSpec((tm, FOURIER_WIDTH), lambda i: (jnp.clip(i - n_lat_tiles, 0, n_tiles - n_lat_tiles - 1), 0)),
            pl.BlockSpec((tm, CONV_WIDTH), lambda i: (i, 0)),
            pl.BlockSpec((tm, CONV_WIDTH), lambda i: (i, 0)),
            pl.BlockSpec((hb, CONV_WIDTH), lambda i: (jnp.maximum(i * r - 1, 0), 0)),
            pl.BlockSpec((hb, CONV_WIDTH), lambda i: (jnp.minimum((i + 1) * r, t // hb - 1), 0)),
            pl.BlockSpec(conv_w.shape, lambda i: (0, 0)),
            pl.BlockSpec(w_out.shape, lambda i: (0, 0)),
        ] + x_specs + [
            pl.BlockSpec((None, 6, d), _stream_of(n_lat_tiles)),
        ] + r_in,
        out_specs=[pl.BlockSpec((tm, d), lambda i: (i, 0))] + r_out,
        out_shape=[jax.ShapeDtypeStruct((t, d), F32)] + r_shape,
        compiler_params=_cparams(("arbitrary",)),
        name="even_out_proj",
    )(f_lat, f_ctx, bg, u, u, u, conv_w, w_out, *xs, mod, *route)


def _seg_rms_scale(v, seg, seg_t):
    ss = jnp.dot((v * v).astype(BF16), seg, preferred_element_type=F32)
    inv = lax.rsqrt(ss * (1.0 / HEAD_DIM) + EPS)
    inv2 = jnp.concatenate(_split_bf16(inv), axis=1)
    return jnp.dot(inv2, seg_t, preferred_element_type=F32)


def _rope_cols(v, cos, sa, sb, scale):
    cols = []
    for j in range(v.shape[1] // LANES):
        c = v[:, j * LANES:(j + 1) * LANES]
        r = c * cos + pltpu.roll(c, LANES - ROPE_PAIRS, axis=1) * sa + pltpu.roll(c, ROPE_PAIRS, axis=1) * sb
        cols.append(r * scale if scale != 1.0 else r)
    return jnp.concatenate(cols, axis=1)


def _dup_head_lanes(v):
    low = lax.broadcasted_iota(I32, (v.shape[0], LANES), 1) < HEAD_DIM
    cols = []
    for j in range(v.shape[1] // LANES):
        c = v[:, j * LANES:(j + 1) * LANES]
        r = pltpu.roll(c, HEAD_DIM, axis=1)
        cols += [jnp.where(low, c, r), jnp.where(low, r, c)]
    return jnp.concatenate(cols, axis=1)


def _odd_in_kernel(x_ref, g_ref, mod_ref, w_ref, qg_ref, kg_ref, segq_ref, segqt_ref, segk_ref, segkt_ref,
                   rope_ref, q_ref, k_ref, v_ref):
    h = _norm_mod(x_ref[...], g_ref[...], mod_ref[0:1, :], mod_ref[1:2, :])
    p = jnp.dot(h.astype(BF16), w_ref[...], preferred_element_type=F32)
    qd = N_HEADS * HEAD_DIM
    kd = N_KV_HEADS * HEAD_DIM
    cos, sa, sb = rope_ref[0], rope_ref[1], rope_ref[2]
    q = p[:, :qd]
    q = q * _seg_rms_scale(q, segq_ref[...], segqt_ref[...]) * qg_ref[...]
    q_ref[...] = _rope_cols(q, cos, sa, sb, HEAD_DIM ** -0.5 * LOG2E).astype(BF16)
    k = p[:, qd:qd + kd]
    k = k * _seg_rms_scale(k, segk_ref[...], segkt_ref[...]) * kg_ref[...]
    k_ref[...] = _dup_head_lanes(_rope_cols(k, cos, sa, sb, 1.0)).astype(BF16)
    v_ref[...] = _dup_head_lanes(p[:, qd + kd:]).astype(BF16)


def _odd_in(x, g, mod, w_qkv, qg, kg, segs, rope, n_lat_tiles, n_tiles):
    t, d = x.shape
    n = w_qkv.shape[1]
    tm = ROW_TILE
    rows = n_tiles * tm
    qd = N_HEADS * HEAD_DIM
    kd = 2 * N_KV_HEADS * HEAD_DIM
    segq, segqt, segk, segkt = segs
    const = lambda a: pl.BlockSpec(a.shape, lambda i: (0,) * a.ndim)
    return pl.pallas_call(
        _odd_in_kernel,
        grid=(n_tiles,),
        in_specs=[
            pl.BlockSpec((tm, d), lambda i: (i, 0)),
            const(g),
            pl.BlockSpec((None, 6, d), _stream_of(n_lat_tiles)),
            const(w_qkv), const(qg), const(kg), const(segq), const(segqt), const(segk), const(segkt),
            pl.BlockSpec((3, tm, LANES), lambda i: (0, i, 0)),
        ],
        out_specs=[
            pl.BlockSpec((tm, qd), lambda i: (i, 0)),
            pl.BlockSpec((tm, kd), lambda i: (i, 0)),
            pl.BlockSpec((tm, kd), lambda i: (i, 0)),
        ],
        out_shape=[
            jax.ShapeDtypeStruct((rows, qd), BF16),
            jax.ShapeDtypeStruct((rows, kd), BF16),
            jax.ShapeDtypeStruct((rows, kd), BF16),
        ],
        compiler_params=_cparams(("arbitrary",)),
        name="odd_in_proj",
    )(x, g, mod, w_qkv, qg, kg, segq, segqt, segk, segkt, rope)


def _rope_tables(l, t):
    pos = np.arange(l)
    freqs = ROPE_BASE ** (-np.arange(ROPE_PAIRS, dtype=np.float32) / ROPE_PAIRS)
    lane = np.arange(LANES) % HEAD_DIM
    axis = lane // (2 * ROPE_PAIRS)
    half = (lane % (2 * ROPE_PAIRS)) // ROPE_PAIRS
    pair = lane % ROPE_PAIRS
    p = np.where(axis[None, :] == 0, (pos // GRID_W)[:, None], (pos % GRID_W)[:, None]).astype(np.float32)
    ang = p * freqs[pair][None, :].astype(np.float32)
    cos, sin = np.cos(ang), np.sin(ang)
    sa = np.where(half[None, :] == 0, -sin, 0.0)
    sb = np.where(half[None, :] == 1, sin, 0.0)
    tab = np.zeros((3, t, LANES), np.float32)
    tab[0, :l], tab[1, :l], tab[2, :l] = cos, sa, sb
    tab[0, l:] = 1.0
    return jnp.asarray(tab)


def _segment_matrices():
    def seg(width):
        m = np.zeros((width, LANES), np.float32)
        m[np.arange(width), np.arange(width) // HEAD_DIM] = 1.0
        return m
    sq, sk = seg(N_HEADS * HEAD_DIM), seg(N_KV_HEADS * HEAD_DIM)
    b = lambda v: jnp.asarray(v).astype(BF16)
    twice = lambda m: np.concatenate([m, m], axis=0)
    return b(sq), b(twice(sq.T)), b(sk), b(twice(sk.T))


def _attn_kernel(sink_ref, q_ref, k0, k1, k2, k3, v0, v1, v2, v3, kx_ref, vx_ref, bias_a_ref, bias_b_ref, o_ref):
    bq = ATT_BLOCK
    low = lax.broadcasted_iota(I32, (bq, LANES), 1) < HEAD_DIM
    top = lax.broadcasted_iota(I32, (LANES, bq), 0) < HEAD_DIM
    nt = (((1,), (1,)), ((), ()))
    tn = (((0,), (0,)), ((), ()))
    blocks = ((0, (k0, k1, k2), (v0, v1, v2), bias_a_ref), (bq, (k1, k2, k3), (v1, v2, v3), bias_b_ref))
    jobs = [(blk, g) for blk in blocks for g in range(N_KV_HEADS)]
    st = []
    for (r0, k_refs, _, bias_ref), g in jobs:
        rows = slice(r0, r0 + bq)
        ks = slice(g * LANES, (g + 1) * LANES)
        parts, sinks = [], []
        for j in range(GQA_GROUP):
            col = 2 * g + j // 2
            c = q_ref[rows, col * LANES:(col + 1) * LANES]
            keep = low if j % 2 == 0 else jnp.logical_not(low)
            parts.append(jnp.where(keep, c, jnp.zeros_like(c)))
            sinks.append(jnp.full((1, bq), sink_ref[g * GQA_GROUP + j], F32))
        qs = jnp.concatenate(parts, axis=0)
        sink = jnp.concatenate(sinks, axis=1)
        kwin = jnp.concatenate([r[:, ks] for r in k_refs], axis=0)
        s_loc = lax.dot_general(kwin, qs, nt, preferred_element_type=F32) + bias_ref[...]
        s_ctx = lax.dot_general(kx_ref[:, ks], qs, nt, preferred_element_type=F32)
        st.append((s_loc, s_ctx, sink))
    pr = []
    for s_loc, s_ctx, sink in st:
        m = jnp.maximum(jnp.maximum(jnp.max(s_loc, axis=0, keepdims=True),
                                    jnp.max(s_ctx, axis=0, keepdims=True)), sink)
        p_loc = jnp.exp2(s_loc - m)
        p_ctx = jnp.exp2(s_ctx - m)
        den = (jnp.sum(p_loc, axis=0, keepdims=True) + jnp.sum(p_ctx, axis=0, keepdims=True)
               + jnp.exp2(sink - m))
        pr.append((p_loc.astype(BF16), p_ctx.astype(BF16), 1.0 / den))
    for ((r0, _, v_refs, _), g), (p_loc, p_ctx, inv) in zip(jobs, pr):
        rows = slice(r0, r0 + bq)
        ks = slice(g * LANES, (g + 1) * LANES)
        vwin = jnp.concatenate([r[:, ks] for r in v_refs], axis=0)
        ot = lax.dot_general(vwin, p_loc, tn, preferred_element_type=F32)
        ot += lax.dot_general(vx_ref[:, ks], p_ctx, tn, preferred_element_type=F32)
        ot = ot * inv
        t0 = jnp.where(top, ot[:, 0:bq], ot[:, bq:2 * bq])
        t1 = jnp.where(top, ot[:, 2 * bq:3 * bq], ot[:, 3 * bq:4 * bq])
        o_ref[rows, 2 * g * LANES:(2 * g + 1) * LANES] = t0.T.astype(o_ref.dtype)
        o_ref[rows, (2 * g + 1) * LANES:(2 * g + 2) * LANES] = t1.T.astype(o_ref.dtype)


def _attn_bias(l):
    bq = ATT_BLOCK
    r = np.arange(GQA_GROUP * bq)[None, :] % bq
    col = np.arange(3 * bq)[:, None]
    band = np.abs(col - bq - r) <= WINDOW
    no_prev, no_next = col >= bq, col < 2 * bq
    masks = [band, band & no_prev, band & no_next, band & no_prev & no_next, np.zeros_like(band)]
    return jnp.asarray(np.where(np.stack(masks), 0.0, NEG_INF).astype(np.float32))


def _attention(q, k, v, sink, bias, l, n_q_blocks):
    t = q.shape[0]
    bq = ATT_BLOCK
    nlb = l // bq
    lc = t - l
    ctx_blk = l // lc
    kw = k.shape[1]
    assert n_q_blocks % 2 == 0 and nlb % 2 == 0
    key_block = lambda off: pl.BlockSpec((bq, kw), lambda i, s: (jnp.clip(2 * i + off, 0, nlb - 1), 0))
    keys = [key_block(off) for off in (-1, 0, 1, 2)]
    ctx = pl.BlockSpec((lc, kw), lambda i, s: (ctx_blk, 0))

    def kind(off):
        def index(i, s):
            b = 2 * i + off
            return (jnp.where(b >= nlb, 4, (b == 0).astype(I32) + 2 * (b == nlb - 1).astype(I32)), 0, 0)
        return pl.BlockSpec((None,) + bias.shape[1:], index)
    return pl.pallas_call(
        _attn_kernel,
        grid_spec=pltpu.PrefetchScalarGridSpec(
            num_scalar_prefetch=1,
            grid=(n_q_blocks // 2,),
            in_specs=[pl.BlockSpec((2 * bq, q.shape[1]), lambda i, s: (i, 0))] + keys + keys
                     + [ctx, ctx, kind(0), kind(1)],
            out_specs=pl.BlockSpec((2 * bq, q.shape[1]), lambda i, s: (i, 0)),
        ),
        out_shape=jax.ShapeDtypeStruct((n_q_blocks * bq, q.shape[1]), BF16),
        compiler_params=_cparams(("arbitrary",)),
        name="window_attention",
    )(sink, q, k, k, k, k, v, v, v, v, k, v, bias, bias)


def _odd_out_kernel(a_ref, w_ref, x_ref, mod_ref, gf_ref, wr_ref, br_ref, tri_ref, o_ref, h_ref, ri_ref, rg_ref,
                    cnt_ref):
    acc = jnp.dot(a_ref[...], w_ref[...], preferred_element_type=F32)
    x_new = x_ref[...] + mod_ref[2:3, :] * acc
    o_ref[...] = x_new
    _route_tile(x_new, gf_ref, mod_ref, wr_ref, br_ref, tri_ref, h_ref, ri_ref, rg_ref, cnt_ref)


def _odd_out(a, w_o, x, mod, route, n_lat_tiles, n_tiles):
    d = x.shape[1]
    tm = ROW_TILE
    r_in, r_out, r_shape = _route_specs(route, d, n_tiles)
    return pl.pallas_call(
        _odd_out_kernel,
        grid=(n_tiles,),
        in_specs=[
            pl.BlockSpec((tm, a.shape[1]), lambda i: (i, 0)),
            pl.BlockSpec(w_o.shape, lambda i: (0, 0)),
            pl.BlockSpec((tm, d), lambda i: (i, 0)),
            pl.BlockSpec((None, 6, d), _stream_of(n_lat_tiles)),
        ] + r_in,
        out_specs=[pl.BlockSpec((tm, d), lambda i: (i, 0))] + r_out,
        out_shape=[jax.ShapeDtypeStruct((n_tiles * tm, d), F32)] + r_shape,
        compiler_params=_cparams(("arbitrary",)),
        name="odd_out_proj",
    )(a, w_o, x, mod, *route)


CHUNK = 8
TILE_BUF = 2 * ROW_TILE + N_EXPERTS * CHUNK
MAX_TILE_CHUNKS = TILE_BUF // CHUNK
FFN_BUFS = 4
COMBINE_BUFS = 3


def _chunk_rows(c):
    return pl.ds(pl.multiple_of(c * CHUNK, CHUNK), CHUNK)


def _wait_rows(copy_of_rows, n_chunks):
    bit = 1
    while bit <= MAX_TILE_CHUNKS:
        @pl.when((n_chunks & bit) != 0)
        def _(bit=bit):
            copy_of_rows(bit * CHUNK).wait()
        bit *= 2


def _dispatch_kernel(tab_ref, lused_ref, fill_ref, h_ref, pos_ref, gate_ref, xb_ref, hs, zbuf, sem, fsem, *, n_tiles,
                     n_blocks):
    i = pl.program_id(0)
    slot = i % 2
    tm = h_ref.shape[0]

    def tail_copy(c):
        return pltpu.make_async_copy(zbuf.at[pl.ds(0, CHUNK), :], xb_ref.at[_chunk_rows(c), :], fsem)

    def block_copy(b):
        return pltpu.make_async_copy(zbuf, xb_ref.at[pl.ds(pl.multiple_of(b * MOE_ROWS, MOE_ROWS), MOE_ROWS), :], fsem)

    def fill(start):
        def tail(e, c):
            st, n = fill_ref[e], fill_ref[N_EXPERTS + e]

            def one(c2, cc):
                cp = tail_copy(st + c2)
                cp.start() if start else cp.wait()
                return cc
            return lax.fori_loop(0, n, one, c)
        lax.fori_loop(0, N_EXPERTS, tail, 0)

        def blk(b, c):
            cp = block_copy(b)
            cp.start() if start else cp.wait()
            return c
        lax.fori_loop(fill_ref[2 * N_EXPERTS], n_blocks, blk, 0)

    @pl.when(i == 0)
    def _():
        zbuf[...] = jnp.zeros_like(zbuf)
        fill(True)

    pos = pos_ref[...]
    r = lax.broadcasted_iota(I32, (TILE_BUF, tm), 0)
    first = r == pos[0:1, :]
    second = r == pos[1:2, :]
    onehot = jnp.where(jnp.logical_or(first, second), 1.0, 0.0).astype(BF16)
    d = h_ref.shape[1]
    hs[slot, :, 0:d] = jnp.dot(onehot, h_ref[...], preferred_element_type=F32).astype(BF16)
    gate = gate_ref[...]
    lane = lax.broadcasted_iota(I32, (tm, LANES), 1)

    def hi_lo_lanes(g):
        hi = g.astype(BF16).astype(F32)
        return jnp.where(lane == 0, hi, jnp.where(lane == 1, g - hi, 0.0)).astype(BF16)
    gs = (jnp.dot(jnp.where(first, 1.0, 0.0).astype(BF16), hi_lo_lanes(gate[:, 0:1]), preferred_element_type=F32)
          + jnp.dot(jnp.where(second, 1.0, 0.0).astype(BF16), hi_lo_lanes(gate[:, 1:2]), preferred_element_type=F32))
    hs[slot, :, d:d + LANES] = gs.astype(BF16)

    def chunk_copy(sl, src, dst):
        return pltpu.make_async_copy(hs.at[sl, _chunk_rows(src), :], xb_ref.at[_chunk_rows(dst), :], sem.at[sl])

    base = i * MAX_TILE_CHUNKS

    def per_chunk(c, cc):
        chunk_copy(slot, c, tab_ref[base + c]).start()
        return cc
    lax.fori_loop(0, lused_ref[i], per_chunk, 0)

    def wait_chunks(sl, n):
        _wait_rows(lambda rows: pltpu.make_async_copy(hs.at[sl, pl.ds(0, rows), :], xb_ref.at[pl.ds(0, rows), :],
                                                      sem.at[sl]), n)

    @pl.when(i > 0)
    def _():
        wait_chunks(1 - slot, lused_ref[jnp.maximum(i - 1, 0)])

    @pl.when(i == n_tiles - 1)
    def _():
        wait_chunks(slot, lused_ref[i])
        fill(False)


def _dispatch(h, pos_rows, gate_cols, tab, lused, fill, n_blocks):
    t, d = h.shape
    dw = d + LANES
    tm = ROW_TILE
    n_tiles = t // tm
    return pl.pallas_call(
        functools.partial(_dispatch_kernel, n_tiles=n_tiles, n_blocks=n_blocks),
        grid_spec=pltpu.PrefetchScalarGridSpec(
            num_scalar_prefetch=3,
            grid=(n_tiles,),
            in_specs=[
                pl.BlockSpec((tm, d), lambda i, *_: (i, 0)),
                pl.BlockSpec((8, tm), lambda i, *_: (0, i)),
                pl.BlockSpec((tm, 8), lambda i, *_: (i, 0)),
            ],
            out_specs=pl.BlockSpec(memory_space=pl.ANY),
            scratch_shapes=[
                pltpu.VMEM((2, TILE_BUF, dw), BF16),
                pltpu.VMEM((MOE_ROWS, dw), BF16),
                pltpu.SemaphoreType.DMA((2,)),
                pltpu.SemaphoreType.DMA(()),
            ],
        ),
        out_shape=jax.ShapeDtypeStruct((n_blocks * MOE_ROWS, dw), BF16),
        compiler_params=_cparams(("arbitrary",)),
        name="moe_dispatch",
    )(tab, lused, fill, h, pos_rows, gate_cols)


def _ffn_kernel(b0_ref, nb_ref, nv_ref, fill_ref, xb_ref, w1_ref, w3_ref, w2_ref, yb_ref,
                w1s, w3s, w2s, xbuf, ybuf, zbuf, sem_in, sem_out, fsem, *, n_blocks):
    e = pl.program_id(0)
    w1s[...] = w1_ref[...].astype(BF16)
    w3s[...] = w3_ref[...].astype(BF16)
    w2s[...] = w2_ref[...].astype(BF16)
    b0, nb = b0_ref[e], nb_ref[e]
    used = fill_ref[0]

    def rows(b):
        return pl.ds(pl.multiple_of(b * MOE_ROWS, MOE_ROWS), MOE_ROWS)

    def x_copy(g, sl):
        return pltpu.make_async_copy(xb_ref.at[rows(g), :], xbuf.at[sl], sem_in.at[sl])

    def y_copy(g, sl):
        return pltpu.make_async_copy(ybuf.at[sl], yb_ref.at[rows(g), :], sem_out.at[sl])

    @pl.when(e == 0)
    def _():
        for k in range(FFN_BUFS - 1):
            @pl.when(k < used)
            def _(k=k):
                x_copy(k, k).start(priority=1)

    def block(j, c):
        g = b0 + j
        sl = g % FFN_BUFS
        x_copy(g, sl).wait()
        ahead = g + (FFN_BUFS - 1)

        @pl.when(ahead < used)
        def _():
            x_copy(ahead, ahead % FFN_BUFS).start(priority=1)

        @pl.when(g >= FFN_BUFS)
        def _():
            y_copy(g - FFN_BUFS, sl).wait()

        d = w1s.shape[0]
        x = xbuf[sl, :, 0:d]
        row = lax.broadcasted_iota(I32, x.shape, 0)
        xb = jnp.where(row < nv_ref[g], x, jnp.zeros_like(x))
        a = jnp.dot(xb, w1s[...], preferred_element_type=F32)
        b = jnp.dot(xb, w3s[...], preferred_element_type=F32)
        hid = (a * jax.nn.sigmoid(a) * b).astype(BF16)
        gl = xbuf[sl, :, d:d + LANES].astype(F32)
        gate = gl[:, 0:1] + gl[:, 1:2]
        ybuf[sl] = (jnp.dot(hid, w2s[...], preferred_element_type=F32) * gate).astype(BF16)
        y_copy(g, sl).start(priority=1)
        return c
    lax.fori_loop(0, nb, block, 0)

    @pl.when(e == N_EXPERTS - 1)
    def _():
        for k in range(FFN_BUFS):
            @pl.when(used > k)
            def _(k=k):
                y_copy(used - 1 - k, (used - 1 - k) % FFN_BUFS).wait()
        zbuf[...] = jnp.zeros_like(zbuf)

        def z_copy(b):
            return pltpu.make_async_copy(zbuf, yb_ref.at[rows(b), :], fsem)

        def start(b, c):
            z_copy(b).start()
            return c

        def wait(b, c):
            z_copy(b).wait()
            return c
        lax.fori_loop(fill_ref[0], n_blocks, start, 0)
        lax.fori_loop(fill_ref[0], n_blocks, wait, 0)


def _expert_ffn(xb, blk_start, blk_count, n_valid, used_blocks, w1, w3, w2, layer):
    r = xb.shape[0]
    d, f = w1.shape[2], w1.shape[3]
    bm = MOE_ROWS
    return pl.pallas_call(
        functools.partial(_ffn_kernel, n_blocks=r // bm),
        grid_spec=pltpu.PrefetchScalarGridSpec(
            num_scalar_prefetch=4,
            grid=(N_EXPERTS,),
            in_specs=[
                pl.BlockSpec(memory_space=pl.ANY),
                pl.BlockSpec((None, None, d, f), lambda e, *_: (layer, e, 0, 0)),
                pl.BlockSpec((None, None, d, f), lambda e, *_: (layer, e, 0, 0)),
                pl.BlockSpec((None, None, f, d), lambda e, *_: (layer, e, 0, 0)),
            ],
            out_specs=pl.BlockSpec(memory_space=pl.ANY),
            scratch_shapes=[
                pltpu.VMEM((d, f), BF16), pltpu.VMEM((d, f), BF16), pltpu.VMEM((f, d), BF16),
                pltpu.VMEM((FFN_BUFS, bm, d + LANES), BF16), pltpu.VMEM((FFN_BUFS, bm, d), BF16),
                pltpu.VMEM((bm, d), BF16),
                pltpu.SemaphoreType.DMA((FFN_BUFS,)), pltpu.SemaphoreType.DMA((FFN_BUFS,)), pltpu.SemaphoreType.DMA(()),
            ],
        ),
        out_shape=jax.ShapeDtypeStruct((r, d), BF16),
        compiler_params=_cparams(("arbitrary",)),
        name="moe_expert_mlp",
    )(blk_start, blk_count, n_valid, used_blocks, xb, w1, w3, w2)


def _combine_kernel(tab_ref, lused_ref, yb_ref, pos_ref, x_ref, mod_ref, o_ref, ys, sem, *, n_tiles):
    i = pl.program_id(0)
    slot = i % COMBINE_BUFS
    tm = x_ref.shape[0]

    def chunk_copy(sl, src, dst):
        return pltpu.make_async_copy(yb_ref.at[_chunk_rows(src), :], ys.at[sl, _chunk_rows(dst), :], sem.at[sl])

    def fetch(tile, sl):
        base = tile * MAX_TILE_CHUNKS

        def per_chunk(c, cc):
            chunk_copy(sl, tab_ref[base + c], c).start()
            return cc
        lax.fori_loop(0, lused_ref[tile], per_chunk, 0)

    @pl.when(i == 0)
    def _():
        for k in range(COMBINE_BUFS - 1):
            if k < n_tiles:
                fetch(k, k)

    ahead = i + (COMBINE_BUFS - 1)

    @pl.when(ahead < n_tiles)
    def _():
        fetch(jnp.minimum(ahead, n_tiles - 1), ahead % COMBINE_BUFS)

    _wait_rows(lambda rows: pltpu.make_async_copy(yb_ref.at[pl.ds(0, rows), :], ys.at[slot, pl.ds(0, rows), :],
                                                  sem.at[slot]), lused_ref[i])

    y = ys[slot]
    used = lused_ref[i] * CHUNK
    rowi = lax.broadcasted_iota(I32, y.shape, 0)
    y16 = jnp.where(rowi < used, y, jnp.zeros_like(y))
    pos = pos_ref[...]
    lane = lax.broadcasted_iota(I32, (tm, TILE_BUF), 1)
    pick = jnp.where(jnp.logical_or(lane == pos[:, 0:1], lane == pos[:, 1:2]), 1.0, 0.0).astype(BF16)
    mix = jnp.dot(pick, y16, preferred_element_type=F32)
    o_ref[...] = x_ref[...] + mod_ref[5:6, :] * mix


def _combine(yb, pos_cols, tab, lused, x, mod, n_lat_tiles, n_tiles):
    d = x.shape[1]
    tm = ROW_TILE
    return pl.pallas_call(
        functools.partial(_combine_kernel, n_tiles=n_tiles),
        grid_spec=pltpu.PrefetchScalarGridSpec(
            num_scalar_prefetch=2,
            grid=(n_tiles,),
            in_specs=[
                pl.BlockSpec(memory_space=pl.ANY),
                pl.BlockSpec((tm, 8), lambda i, *_: (i, 0)),
                pl.BlockSpec((tm, d), lambda i, *_: (i, 0)),
                pl.BlockSpec((None, 6, d), lambda i, *_: (jnp.where(i >= n_lat_tiles, 1, 0), 0, 0)),
            ],
            out_specs=pl.BlockSpec((tm, d), lambda i, *_: (i, 0)),
            scratch_shapes=[pltpu.VMEM((COMBINE_BUFS, TILE_BUF, d), BF16), pltpu.SemaphoreType.DMA((COMBINE_BUFS,))],
        ),
        out_shape=jax.ShapeDtypeStruct((n_tiles * tm, d), F32),
        compiler_params=_cparams(("arbitrary",)),
        name="moe_combine",
    )(tab, lused, yb, pos_cols, x, mod)


def _moe_layer(x, routing, mod, w1, w3, w2, layer, n_lat_tiles, n_tiles):
    tm = ROW_TILE
    rows = n_tiles * tm
    cpb = MOE_ROWS // CHUNK
    h, ri, rg, cnt3 = routing
    cnt = cnt3[:, :, 0].astype(I32)
    nch = (cnt + CHUNK - 1) // CHUNK
    lbase = jnp.cumsum(nch, axis=1) - nch
    lused = jnp.sum(nch, axis=1).astype(I32)
    tot = jnp.sum(nch, axis=0)
    reg = (tot + cpb - 1) // cpb * cpb
    gend = jnp.cumsum(reg)
    gstart = gend - reg
    gpos = gstart[None, :] + jnp.cumsum(nch, axis=0) - nch
    rows_max = 2 * rows + n_tiles * N_EXPERTS * (CHUNK - 1) + N_EXPERTS * (MOE_ROWS - CHUNK)
    n_blocks = -(-rows_max // MOE_ROWS)
    ex = jnp.arange(N_EXPERTS, dtype=I32)
    blk0 = jnp.arange(n_blocks, dtype=I32) * cpb
    block_exp = jnp.minimum(jnp.sum((gend[None, :] <= blk0[:, None]).astype(I32), axis=1), N_EXPERTS - 1)
    sel = block_exp[:, None] == ex[None, :]
    tot_b = jnp.sum(jnp.where(sel, tot[None, :], 0), axis=1)
    st_b = jnp.sum(jnp.where(sel, gstart[None, :], 0), axis=1)
    n_valid = jnp.clip((tot_b - (blk0 - st_b)) * CHUNK, 0, MOE_ROWS).astype(I32)
    lb_tok = jnp.repeat(lbase, tm, axis=0)
    at = lambda e: jnp.sum(jnp.where(e[:, None] == ex[None, :], lb_tok, 0), axis=1)
    pos1 = CHUNK * at(ri[0]) + ri[2]
    pos2 = CHUNK * at(ri[1]) + ri[3]
    zero = jnp.zeros_like(pos1)
    pos_rows = jnp.stack([pos1, pos2] + [zero] * 6, axis=0).astype(I32)
    slot_id = jnp.arange(MAX_TILE_CHUNKS, dtype=I32)
    owner = jnp.sum((lbase + nch)[:, None, :] <= slot_id[None, :, None], axis=2)
    own = jnp.minimum(owner, N_EXPERTS - 1)[:, :, None] == ex[None, None, :]
    tab = jnp.sum(jnp.where(own, (gpos - lbase)[:, None, :], 0), axis=2) + slot_id[None, :]
    tab = tab.astype(I32).reshape(-1)
    fill = jnp.concatenate([gstart + tot, reg - tot, gend[-1:] // cpb]).astype(I32)
    xb = _dispatch(h, pos_rows, rg.T, tab, lused, fill, n_blocks)
    yb = _expert_ffn(xb, (gstart // cpb).astype(I32), (reg // cpb).astype(I32), n_valid,
                     (gend[-1:] // cpb).astype(I32), w1, w3, w2, layer)
    return _combine(yb, pos_rows.T, tab, lused, x, mod, n_lat_tiles, n_tiles)


def _router_matrix(w_rg, b_rg, w_re, b_re):
    d = w_rg.shape[0]
    wr = jnp.zeros((d, LANES), F32)
    wr = wr.at[:, 0:N_GROUPS].set(w_rg.astype(F32))
    wr = wr.at[:, ROUTER_EXPERT_ROW0:ROUTER_EXPERT_ROW0 + N_EXPERTS].set(w_re.astype(F32))
    wr = jnp.concatenate(_split_bf16(wr), axis=1)
    br = jnp.zeros((LANES,), F32)
    br = br.at[0:N_GROUPS].set(b_rg.astype(F32))
    br = br.at[ROUTER_EXPERT_ROW0:ROUTER_EXPERT_ROW0 + N_EXPERTS].set(b_re.astype(F32))
    return wr, jnp.broadcast_to(br[:, None], (LANES, LANES))


def kernel(x, c, ctx, c_ctx, w_mod, b_mod, norm_mix_g, norm_ffn_g, w_in_even, conv_w, w_out_even, w_qkv, q_norm_g,
           k_norm_g, sink_logit, w_o, w_router_g, b_router_g, w_router_e, b_router_e, w1, w3, w2):
    bsz, l, d = x.shape
    lc = ctx.shape[1]
    assert bsz == 1, "one sample per call"
    tm = ROW_TILE
    assert l % tm == 0 and lc % tm == 0 and l % lc == 0 and l % (DFT_MINOR * 8) == 0
    depth = w_mod.shape[0]
    t = l + lc
    assert t % DFT_MINOR == 0
    nl, nt = l // tm, t // tm

    xs = (x.reshape(l, d), ctx.reshape(lc, d))
    mod_all = _modulation(c, c_ctx, w_mod, b_mod).reshape(depth, 2, 6, d)
    cs, m1, a_tab, mc = _dft_tables(l, lc)
    rope = _rope_tables(l, t)
    segs = _segment_matrices()
    attn_bias = _attn_bias(l)
    tri = jnp.asarray(np.triu(np.ones((tm, tm), np.float32), 1)).astype(BF16)
    qd = N_HEADS * HEAD_DIM

    for layer in range(depth):
        last = layer == depth - 1
        j = layer // 2
        mod = mod_all[layer]
        g_mix = norm_mix_g[layer].reshape(1, d)
        g_ffn = norm_ffn_g[layer].reshape(1, d)
        wr, br = _router_matrix(w_router_g[layer], b_router_g[layer], w_router_e[layer], b_router_e[layer])
        route = (g_ffn, wr, br, tri)
        if layer % 2 == 0:
            z, bg, u = _even_in(xs, g_mix, mod, w_in_even[j].astype(BF16), cs, nl, nt)
            f_lat, f_ctx = _fourier_seq(z, l, m1, a_tab, mc)
            xs, *routing = _even_out(f_lat, f_ctx, bg, u, conv_w[j], w_out_even[j].astype(BF16), xs, mod, route, nl)
        else:
            wq = w_qkv[j]
            w_all = wq.astype(BF16)
            qg = jnp.tile(q_norm_g[j], N_HEADS).reshape(1, qd)
            kg = jnp.tile(k_norm_g[j], N_KV_HEADS).reshape(1, N_KV_HEADS * HEAD_DIM)
            q, k, v = _odd_in(xs, g_mix, mod, w_all, qg, kg, segs, rope, nl, nt)
            n_out = nl if last else nt
            att = _attention(q, k, v, sink_logit[j].astype(F32) * LOG2E, attn_bias, l,
                             n_out * (tm // ATT_BLOCK))
            xs, *routing = _odd_out(att, w_o[j].astype(BF16), xs, mod, route, nl, n_out)
        n_moe = nl if last else nt
        xs = _moe_layer(xs, routing, mod, w1, w3, w2, layer, nl, n_moe)
    return xs[:l].reshape(bsz, l, d)
```
